```python
import math
import jax
import jax.numpy as jnp
from jax import lax
import numpy as np

D_MODEL = 2048
BATCH = 16
SEQ = 256
DEPTH = 2
DEC_BATCH = 2
DEC_SEQ = 4096
PAST_LEN = 256

GRID_W = 64
POS_BASE = 10000.0
EPS = 1e-6
SSM_WIDTH = D_MODEL // 2
SSM_GROUP = 16
SSM_GROUPS = SSM_WIDTH // SSM_GROUP
SSM_STATE = 64
GM_WIDTH = D_MODEL // 2
GM_CHUNK = 128
GM_GROUPS = 8
GM_GROUP_DIM = GM_WIDTH // GM_GROUPS
IN_WIDTH = SSM_WIDTH + 2 * GM_WIDTH + 2 * D_MODEL
N_EXPERTS = 32
N_EXPERT_GROUPS = 8
EXPERTS_PER_GROUP = N_EXPERTS // N_EXPERT_GROUPS
TOP_K = 2
EXPERT_FF = D_MODEL // 4
MOE_BLOCK = 128
N_MOD = 6

kernel_name = "hybrid_s5_gmlp_moe_diffusion_step"


def rmsnorm(x, g):
    xf = x.astype(jnp.float32)
    y = xf * lax.rsqrt(jnp.mean(xf * xf, axis=-1, keepdims=True) + EPS)
    return (y * g.astype(jnp.float32)).astype(x.dtype)


def grid_pos_embed(rows, dtype):
    quarter = D_MODEL // 4
    freqs = 1.0 / (POS_BASE ** (jnp.arange(quarter, dtype=jnp.float32) / quarter))
    er = jnp.arange(rows, dtype=jnp.float32)[:, None] * freqs
    ec = jnp.arange(GRID_W, dtype=jnp.float32)[:, None] * freqs
    row_emb = jnp.concatenate([jnp.sin(er), jnp.cos(er)], axis=-1)
    col_emb = jnp.concatenate([jnp.sin(ec), jnp.cos(ec)], axis=-1)
    pe = jnp.concatenate([
        jnp.broadcast_to(row_emb[:, None, :], (rows, GRID_W, D_MODEL // 2)),
        jnp.broadcast_to(col_emb[None, :, :], (rows, GRID_W, D_MODEL // 2))], axis=-1)
    return pe.reshape(rows * GRID_W, D_MODEL).astype(dtype)


def _combine(left, right):
    a1r, a1i, b1r, b1i = left
    a2r, a2i, b2r, b2i = right
    return (a2r * a1r - a2i * a1i,
            a2r * a1i + a2i * a1r,
            a2r * b1r - a2i * b1i + b2r,
            a2r * b1i + a2i * b1r + b2i)


def _ssm_scan(ar, ai, bu_r, bu_i, h0r, h0i):
    if h0r is not None:
        bu_r = bu_r.at[:, 0].add(ar * h0r - ai * h0i)
        bu_i = bu_i.at[:, 0].add(ar * h0i + ai * h0r)
    a_r = jnp.broadcast_to(ar, bu_r.shape)
    a_i = jnp.broadcast_to(ai, bu_r.shape)
    _, _, xr, xi = lax.associative_scan(_combine, (a_r, a_i, bu_r, bu_i), axis=1)
    return xr, xi


def s5_mixer(u, lam_re, lam_im, log_step, b_re, b_im, c_re, c_im, d_skip, w_glu, b_glu,
             h0_re=None, h0_im=None):
    f32 = jnp.float32
    bsz, seqlen, _ = u.shape
    uf = u.astype(f32)
    ug = uf.reshape(bsz, seqlen, SSM_GROUPS, SSM_GROUP)
    lr = lam_re.astype(f32)
    li = lam_im.astype(f32)
    dt = jnp.exp(log_step.astype(f32))[..., None]
    mag = jnp.exp(lr * dt)
    ar = mag * jnp.cos(li * dt)
    ai = mag * jnp.sin(li * dt)
    den = lr * lr + li * li
    qr = ((ar - 1.0) * lr + ai * li) / den
    qi = (ai * lr - (ar - 1.0) * li) / den
    br = b_re.astype(f32)
    bi = b_im.astype(f32)
    bbr = qr[..., None] * br - qi[..., None] * bi
    bbi = qr[..., None] * bi + qi[..., None] * br
    bur = jnp.einsum('blgh,kgph->kblgp', ug, bbr)
    bui = jnp.einsum('blgh,kgph->kblgp', ug, bbi)
    if h0_re is not None:
        h0r = h0_re.astype(f32)
        h0i = h0_im.astype(f32)
        hf_r, hf_i, hb_r, hb_i = h0r[:, 0], h0i[:, 0], h0r[:, 1], h0i[:, 1]
    else:
        hf_r = hf_i = hb_r = hb_i = None
    xfr, xfi = _ssm_scan(ar[0], ai[0], bur[0], bui[0], hf_r, hf_i)
    xbr, xbi = _ssm_scan(ar[1], ai[1], jnp.flip(bur[1], axis=1), jnp.flip(bui[1], axis=1), hb_r, hb_i)
    final_re = jnp.stack([xfr[:, -1], xbr[:, -1]], axis=1)
    final_im = jnp.stack([xfi[:, -1], xbi[:, -1]], axis=1)
    xbr = jnp.flip(xbr, axis=1)
    xbi = jnp.flip(xbi, axis=1)
    cr = c_re.astype(f32)
    ci = c_im.astype(f32)
    y = (jnp.einsum('blgp,ghp->blgh', xfr, cr[0]) - jnp.einsum('blgp,ghp->blgh', xfi, ci[0])
         + jnp.einsum('blgp,ghp->blgh', xbr, cr[1]) - jnp.einsum('blgp,ghp->blgh', xbi, ci[1]))
    y = y.reshape(bsz, seqlen, SSM_WIDTH) + d_skip.astype(f32) * uf
    y = jax.nn.gelu(y)
    y = y * jax.nn.sigmoid(y @ w_glu.astype(f32) + b_glu.astype(f32))
    return y.astype(u.dtype), final_re, final_im


def chunk_gmlp(u, v, ln_g, w_s, b_s):
    bsz, seqlen, _ = v.shape
    vf = v.astype(jnp.float32)
    mu = jnp.mean(vf, axis=-1, keepdims=True)
    var = jnp.mean(jnp.square(vf - mu), axis=-1, keepdims=True)
    vn = ((vf - mu) * lax.rsqrt(var + EPS) * ln_g.astype(jnp.float32)).astype(v.dtype)
    vc = vn.reshape(bsz, seqlen // GM_CHUNK, GM_CHUNK, GM_GROUPS, GM_GROUP_DIM)
    mixed = jnp.einsum('gij,bcjgd->bcigd', w_s, vc) + jnp.transpose(b_s)[:, :, None]
    return u * mixed.reshape(bsz, seqlen, GM_WIDTH)


def route(x, w_router, b_router):
    scores = jax.nn.sigmoid(x.astype(jnp.float32) @ w_router.astype(jnp.float32))
    sel = scores + b_router.astype(jnp.float32)
    grp = sel.reshape(x.shape[0], N_EXPERT_GROUPS, EXPERTS_PER_GROUP)
    grp_score = jnp.sum(lax.top_k(grp, 2)[0], axis=-1)
    g_idx = jnp.argmax(grp_score, axis=-1)
    in_grp = jnp.take_along_axis(grp, g_idx[:, None, None], axis=1)[:, 0]
    _, local = lax.top_k(in_grp, TOP_K)
    expert_idx = (g_idx[:, None] * EXPERTS_PER_GROUP + local).astype(jnp.int32)
    w = jnp.take_along_axis(scores, expert_idx, axis=1)
    w = w / jnp.sum(w, axis=-1, keepdims=True)
    return expert_idx, w


def moe_ffn(h, w_router, b_router, e_gate, e_up, e_down):
    bsz, seqlen, dm = h.shape
    n_tok = bsz * seqlen
    x = h.reshape(n_tok, dm)
    e_idx, e_w = route(x, w_router, b_router)
    n_assign = n_tok * TOP_K
    flat_e = e_idx.reshape(n_assign)
    order = jnp.argsort(flat_e)
    sorted_e = flat_e[order]
    tok = order // TOP_K
    counts = jnp.bincount(flat_e, length=N_EXPERTS)
    padded = (counts + MOE_BLOCK - 1) // MOE_BLOCK * MOE_BLOCK
    pend = jnp.cumsum(padded)
    pstart = pend - padded
    start = jnp.cumsum(counts) - counts
    dest = pstart[sorted_e] + jnp.arange(n_assign, dtype=jnp.int32) - start[sorted_e]
    n_blocks = -(-n_assign // MOE_BLOCK) + N_EXPERTS
    slot_tok = jnp.full((n_blocks * MOE_BLOCK,), n_tok, dtype=jnp.int32).at[dest].set(tok.astype(jnp.int32))
    blk_e = jnp.minimum(jnp.searchsorted(pend, jnp.arange(n_blocks) * MOE_BLOCK, side='right'), N_EXPERTS - 1)
    x_pad = jnp.concatenate([x, jnp.zeros((1, dm), x.dtype)], axis=0)

    def run_block(args):
        idx, e = args
        xb = x_pad[idx]
        hid = jax.nn.silu(xb @ e_gate[e]) * (xb @ e_up[e])
        return hid @ e_down[e]

    yb = lax.map(run_block, (slot_tok.reshape(n_blocks, MOE_BLOCK), blk_e)).reshape(n_blocks * MOE_BLOCK, dm)
    w_sorted = e_w.reshape(n_assign)[order].astype(x.dtype)
    y = jnp.zeros((n_tok, dm), x.dtype).at[tok].add(yb[dest] * w_sorted[:, None])
    return y.reshape(bsz, seqlen, dm)


def token_mixer(h, p, h0_re, h0_im):
    z = h @ p['w_in']
    s1 = SSM_WIDTH
    s2 = s1 + GM_WIDTH
    s3 = s2 + GM_WIDTH
    s4 = s3 + D_MODEL
    s_in, gu, gv, ga, gb = jnp.split(z, [s1, s2, s3, s4], axis=-1)
    ya, fr, fi = s5_mixer(s_in, p['lam_re'], p['lam_im'], p['log_step'], p['b_re'], p['b_im'],
                          p['c_re'], p['c_im'], p['d'], p['w_glu'], p['b_glu'], h0_re, h0_im)
    yb = chunk_gmlp(jax.nn.gelu(gu), jax.nn.gelu(gv), p['ln_g'], p['w_s'], p['b_s'])
    merged = jax.nn.sigmoid(ga) * (ya @ p['w_pa']) + jax.nn.sigmoid(gb) * (yb @ p['w_pb'])
    return merged @ p['w_o'], fr, fi


def trunk_layer(x, mod, p, w_router, b_router, h0_re=None, h0_im=None):
    sh1, sc1, g1, sh2, sc2, g2 = jnp.split(mod, N_MOD, axis=-1)
    h = rmsnorm(x, p['norm1']) * (1.0 + sc1) + sh1
    mix, fr, fi = token_mixer(h, p, h0_re, h0_im)
    x = x + g1 * mix
    h = rmsnorm(x, p['norm2']) * (1.0 + sc2) + sh2
    x = x + g2 * moe_ffn(h, w_router, b_router, p['e_gate'], p['e_up'], p['e_down'])
    return x, fr, fi


def setup_inputs(seed: int = 0) -> dict:
    key = jax.random.key(seed)
    ks = jax.random.split(key, 33)
    f32 = jnp.float32

    def nrm(k, shape, scale=1.0):
        return scale * jax.random.normal(k, shape, dtype=f32)

    ssm_shape = (DEPTH, 2, SSM_GROUPS, SSM_STATE)
    return {
        'x_prompt': nrm(ks[0], (BATCH, SEQ, D_MODEL)),
        'x_sample': nrm(ks[1], (DEC_BATCH, DEC_SEQ, D_MODEL)),
        'state_ssm_re': nrm(ks[2], (DEC_BATCH, DEPTH, 2, SSM_GROUPS, SSM_STATE), 0.3),
        'state_ssm_im': nrm(ks[3], (DEC_BATCH, DEPTH, 2, SSM_GROUPS, SSM_STATE), 0.3),
        'c': nrm(ks[4], (DEC_BATCH, D_MODEL)),
        'c_ctx': nrm(ks[5], (D_MODEL,)),
        'norm1_g': 1.0 + nrm(ks[6], (DEPTH, D_MODEL), 0.02),
        'norm2_g': 1.0 + nrm(ks[7], (DEPTH, D_MODEL), 0.02),
        'w_mod': nrm(ks[8], (DEPTH, D_MODEL, N_MOD * D_MODEL), 0.5 * D_MODEL ** -0.5),
        'b_mod': nrm(ks[9], (DEPTH, N_MOD * D_MODEL), 0.01),
        'w_in': nrm(ks[10], (DEPTH, D_MODEL, IN_WIDTH), D_MODEL ** -0.5),
        'ssm_lam_re': -0.5 + nrm(ks[11], ssm_shape, 0.01),
        'ssm_lam_im': math.pi * jnp.arange(SSM_STATE, dtype=f32) + nrm(ks[12], ssm_shape, 0.01),
        'ssm_log_step': jax.random.uniform(ks[13], (DEPTH, 2, SSM_GROUPS), dtype=f32,
                                           minval=math.log(1e-3), maxval=math.log(1e-1)),
        'ssm_b_re': nrm(ks[14], (DEPTH, 2, SSM_GROUPS, SSM_STATE, SSM_GROUP), (2 * SSM_GROUP) ** -0.5),
        'ssm_b_im': nrm(ks[15], (DEPTH, 2, SSM_GROUPS, SSM_STATE, SSM_GROUP), (2 * SSM_GROUP) ** -0.5),
        'ssm_c_re': nrm(ks[16], (DEPTH, 2, SSM_GROUPS, SSM_GROUP, SSM_STATE), SSM_STATE ** -0.5),
        'ssm_c_im': nrm(ks[17], (DEPTH, 2, SSM_GROUPS, SSM_GROUP, SSM_STATE), SSM_STATE ** -0.5),
        'ssm_d': nrm(ks[18], (DEPTH, SSM_WIDTH)),
        'w_glu': nrm(ks[19], (DEPTH, SSM_WIDTH, SSM_WIDTH), SSM_WIDTH ** -0.5),
        'b_glu': nrm(ks[20], (DEPTH, SSM_WIDTH), 0.01),
        'gm_ln_g': 1.0 + nrm(ks[21], (DEPTH, GM_WIDTH), 0.02),
        'gm_w_s': nrm(ks[22], (DEPTH, GM_GROUPS, GM_CHUNK, GM_CHUNK), GM_CHUNK ** -0.5),
        'gm_b_s': 1.0 + nrm(ks[23], (DEPTH, GM_GROUPS, GM_CHUNK), 0.02),
        'w_pa': nrm(ks[24], (DEPTH, SSM_WIDTH, D_MODEL), SSM_WIDTH ** -0.5),
        'w_pb': nrm(ks[25], (DEPTH, GM_WIDTH, D_MODEL), GM_WIDTH ** -0.5),
        'w_o': nrm(ks[26], (DEPTH, D_MODEL, D_MODEL), D_MODEL ** -0.5),
        'w_router': nrm(ks[27], (D_MODEL, N_EXPERTS), D_MODEL ** -0.5),
        'b_router': nrm(ks[28], (N_EXPERTS,), 0.01),
        'e_gate': nrm(ks[29], (DEPTH, N_EXPERTS, D_MODEL, EXPERT_FF), D_MODEL ** -0.5),
        'e_up': nrm(ks[30], (DEPTH, N_EXPERTS, D_MODEL, EXPERT_FF), D_MODEL ** -0.5),
        'e_down': nrm(ks[31], (DEPTH, N_EXPERTS, EXPERT_FF, D_MODEL), EXPERT_FF ** -0.5),
        'final_g': 1.0 + nrm(ks[32], (D_MODEL,), 0.02),
    }


def reference(x_prompt, x_sample, state_ssm_re, state_ssm_im, c, c_ctx, norm1_g, norm2_g, w_mod, b_mod,
              w_in, ssm_lam_re, ssm_lam_im, ssm_log_step, ssm_b_re, ssm_b_im, ssm_c_re, ssm_c_im, ssm_d,
              w_glu, b_glu, gm_ln_g, gm_w_s, gm_b_s, w_pa, w_pb, w_o, w_router, b_router,
              e_gate, e_up, e_down, final_g):
    rows = x_sample.shape[1] // GRID_W
    xs = x_sample + grid_pos_embed(rows, x_sample.dtype)[None]
    xp = x_prompt
    new_re = []
    new_im = []
    for l in range(DEPTH):
        p = {
            'norm1': norm1_g[l], 'norm2': norm2_g[l], 'w_in': w_in[l],
            'lam_re': ssm_lam_re[l], 'lam_im': ssm_lam_im[l], 'log_step': ssm_log_step[l],
            'b_re': ssm_b_re[l], 'b_im': ssm_b_im[l], 'c_re': ssm_c_re[l], 'c_im': ssm_c_im[l],
            'd': ssm_d[l], 'w_glu': w_glu[l], 'b_glu': b_glu[l],
            'ln_g': gm_ln_g[l], 'w_s': gm_w_s[l], 'b_s': gm_b_s[l],
            'w_pa': w_pa[l], 'w_pb': w_pb[l], 'w_o': w_o[l],
            'e_gate': e_gate[l], 'e_up': e_up[l], 'e_down': e_down[l],
        }
        mod_ctx = (jax.nn.silu(c_ctx) @ w_mod[l] + b_mod[l])[None, None, :]
        mod_lat = (jax.nn.silu(c) @ w_mod[l] + b_mod[l])[:, None, :]
        xp, fr, fi = trunk_layer(xp, mod_ctx, p, w_router, b_router)
        new_re.append(fr)
        new_im.append(fi)
        xs, _, _ = trunk_layer(xs, mod_lat, p, w_router, b_router, state_ssm_re[:, l], state_ssm_im[:, l])
    y_prompt = rmsnorm(xp, final_g)
    y_sample = rmsnorm(xs, final_g)
    new_state_re = jnp.stack(new_re, axis=1).astype(x_prompt.dtype)
    new_state_im = jnp.stack(new_im, axis=1).astype(x_prompt.dtype)
    return (y_prompt, y_sample, new_state_re, new_state_im)
```

```python
import functools
import math

import jax
import jax.numpy as jnp
from jax import lax
from jax.experimental import pallas as pl
from jax.experimental.pallas import tpu as pltpu

F32 = jnp.float32
BF16 = jnp.bfloat16
HIGHEST = lax.Precision.HIGHEST

D_MODEL = 2048
BATCH = 16
SEQ = 256
DEPTH = 2
DEC_BATCH = 2
DEC_SEQ = 4096
GRID_W = 64
POS_BASE = 10000.0
EPS = 1e-6
SSM_WIDTH = D_MODEL // 2
SSM_GROUP = 16
SSM_GROUPS = SSM_WIDTH // SSM_GROUP
SSM_STATE = 64
GM_WIDTH = D_MODEL // 2
GM_CHUNK = 128
GM_GROUPS = 8
GM_GROUP_DIM = GM_WIDTH // GM_GROUPS
IN_WIDTH = SSM_WIDTH + 2 * GM_WIDTH + 2 * D_MODEL
N_EXPERTS = 32
N_EXPERT_GROUPS = 8
EXPERTS_PER_GROUP = N_EXPERTS // N_EXPERT_GROUPS
EXPERT_FF = D_MODEL // 4
N_MOD = 6

N_PROMPT = BATCH * SEQ
N_SAMPLE = DEC_BATCH * DEC_SEQ
N_TOK = N_PROMPT + N_SAMPLE
SEG_TOK = 4096
N_SEG = N_TOK // SEG_TOK
MOD_ROWS = 8

SCAN_T = 16
SCAN_W = SCAN_T * SSM_GROUP
SCAN_ROWS = N_TOK // SCAN_T
SCAN_BLK = 256
N_SCAN_BLK = SCAN_ROWS // SCAN_BLK
G_OCT = 8
STATE_W = 4 * SSM_STATE
HALF_W = 2 * SSM_STATE

MOE_BLK = 256
MOE_SLOTS = N_TOK + N_EXPERT_GROUPS * MOE_BLK
MOE_NBLK = MOE_SLOTS // MOE_BLK
GROUP_FF = EXPERTS_PER_GROUP * EXPERT_FF

VMEM_LIMIT = 56 * 1024 * 1024


def _cparams(sem):
    return pltpu.CompilerParams(dimension_semantics=sem, vmem_limit_bytes=VMEM_LIMIT)


MOD_TN = 1024


def _mod_kernel(c_ref, w_ref, b_ref, o_ref):
    c = c_ref[...]
    s = c * jax.nn.sigmoid(c)
    o_ref[...] = jnp.dot(s, w_ref[...], precision=HIGHEST, preferred_element_type=F32) + b_ref[...]


def _modulation(cvec, w_mod, b_mod):
    width = N_MOD * D_MODEL
    return pl.pallas_call(
        _mod_kernel,
        grid=(DEPTH, width // MOD_TN),
        in_specs=[
            pl.BlockSpec((MOD_ROWS, D_MODEL), lambda l, n: (0, 0)),
            pl.BlockSpec((None, D_MODEL, MOD_TN), lambda l, n: (l, 0, n)),
            pl.BlockSpec((None, 1, MOD_TN), lambda l, n: (l, 0, n)),
        ],
        out_specs=pl.BlockSpec((None, MOD_ROWS, MOD_TN), lambda l, n: (l, 0, n)),
        out_shape=jax.ShapeDtypeStruct((DEPTH, MOD_ROWS, width), F32),
        compiler_params=_cparams(("arbitrary", "arbitrary")),
        name="adaln_mod",
    )(cvec, w_mod, b_mod.reshape(DEPTH, 1, width))


INP_TM = 512
INP_TN = 1024
INP_NJ = IN_WIDTH // INP_TN


def _inproj_kernel(x_ref, add_ref, gain_ref, sc_ref, sh_ref, g_ref, ln_ref, w_ref,
                   xres_ref, z_ref, h_scr):
    i = pl.program_id(0)
    j = pl.program_id(1)
    seg = i // (SEG_TOK // INP_TM)

    @pl.when(j == 0)
    def _():
        x = x_ref[...] + gain_ref[pl.ds(seg, 1), :] * add_ref[...]
        xres_ref[...] = x
        ms = jnp.mean(x * x, axis=-1, keepdims=True)
        y = x * lax.rsqrt(ms + EPS) * g_ref[...]
        h = y * (1.0 + sc_ref[pl.ds(seg, 1), :]) + sh_ref[pl.ds(seg, 1), :]
        h_scr[...] = h.astype(BF16)

    acc = jnp.dot(h_scr[...], w_ref[...], preferred_element_type=F32)

    @pl.when(j == 0)
    def _():
        z_ref[...] = acc.astype(BF16)

    @pl.when(j == 1)
    def _():
        z_ref[...] = jax.nn.gelu(acc).astype(BF16)

    @pl.when(j == 2)
    def _():
        v = jax.nn.gelu(acc)
        mu = jnp.mean(v, axis=-1, keepdims=True)
        var = jnp.mean(jnp.square(v - mu), axis=-1, keepdims=True)
        z_ref[...] = ((v - mu) * lax.rsqrt(var + EPS) * ln_ref[...]).astype(BF16)

    @pl.when(j >= 3)
    def _():
        z_ref[...] = jax.nn.sigmoid(acc).astype(BF16)


def _inproj(x, add, add_map, gain, mod_l, norm_g, ln_g, w_in_bf16):
    mod_spec = lambda k: pl.BlockSpec((MOD_ROWS, D_MODEL), lambda i, j: (0, k))
    return pl.pallas_call(
        _inproj_kernel,
        grid=(N_TOK // INP_TM, INP_NJ),
        in_specs=[
            pl.BlockSpec((INP_TM, D_MODEL), lambda i, j: (i, 0)),
            pl.BlockSpec((INP_TM, D_MODEL), add_map),
            pl.BlockSpec((MOD_ROWS, D_MODEL), lambda i, j: (0, 0)),
            mod_spec(1),
            mod_spec(0),
            pl.BlockSpec((1, D_MODEL), lambda i, j: (0, 0)),
            pl.BlockSpec((1, GM_WIDTH), lambda i, j: (0, 0)),
            pl.BlockSpec((D_MODEL, INP_TN), lambda i, j: (0, j)),
        ],
        out_specs=[
            pl.BlockSpec((INP_TM, D_MODEL), lambda i, j: (i, 0)),
            pl.BlockSpec((INP_TM, INP_TN), lambda i, j: (i, j)),
        ],
        out_shape=[
            jax.ShapeDtypeStruct((N_TOK, D_MODEL), F32),
            jax.ShapeDtypeStruct((N_TOK, IN_WIDTH), BF16),
        ],
        scratch_shapes=[pltpu.VMEM((INP_TM, D_MODEL), BF16)],
        compiler_params=_cparams(("arbitrary", "arbitrary")),
        name="in_proj",
    )(x, add, gain, mod_l, mod_l, norm_g.reshape(1, D_MODEL), ln_g.reshape(1, GM_WIDTH), w_in_bf16)


def _s5_prep(lam_re, lam_im, log_step, b_re, b_im, c_re, c_im):
    lr = lam_re.astype(F32)
    li = lam_im.astype(F32)
    dt = jnp.exp(log_step.astype(F32))[..., None]
    mag = jnp.exp(lr * dt)
    ar = mag * jnp.cos(li * dt)
    ai = mag * jnp.sin(li * dt)
    den = lr * lr + li * li
    qr = ((ar - 1.0) * lr + ai * li) / den
    qi = (ai * lr - (ar - 1.0) * li) / den
    br = b_re.astype(F32)
    bi = b_im.astype(F32)
    bbr = qr[..., None] * br - qi[..., None] * bi
    bbi = qr[..., None] * bi + qi[..., None] * br
    k = jnp.arange(SCAN_T + 1, dtype=F32)[:, None, None, None]
    pmag = jnp.exp(k * (lr * dt))
    pr = pmag * jnp.cos(k * (li * dt))
    pi = pmag * jnp.sin(k * (li * dt))

    pf_r, pf_i = pr[:SCAN_T, 0][::-1], pi[:SCAN_T, 0][::-1]
    pb_r, pb_i = pr[:SCAN_T, 1], pi[:SCAN_T, 1]

    def lam_b(p_r, p_i, d):
        re = p_r[:, :, :, None] * bbr[d][None] - p_i[:, :, :, None] * bbi[d][None]
        im = p_r[:, :, :, None] * bbi[d][None] + p_i[:, :, :, None] * bbr[d][None]
        to = lambda a: jnp.transpose(a, (1, 0, 3, 2)).reshape(SSM_GROUPS, SCAN_W, SSM_STATE)
        return to(re), to(im)

    f_re, f_im = lam_b(pf_r, pf_i, 0)
    b_re_, b_im_ = lam_b(pb_r, pb_i, 1)
    w1 = jnp.concatenate([f_re, b_re_, f_im, b_im_], axis=-1)

    cr = c_re.astype(F32)
    ci = c_im.astype(F32)
    clr = cr[None] * pr[:, :, :, None, :] - ci[None] * pi[:, :, :, None, :]
    cli = cr[None] * pi[:, :, :, None, :] + ci[None] * pr[:, :, :, None, :]
    kern = (jnp.einsum('kdghp,dgpj->dgkhj', clr, bbr, precision=HIGHEST)
            - jnp.einsum('kdghp,dgpj->dgkhj', cli, bbi, precision=HIGHEST))
    s_idx = jnp.arange(SCAN_T)[:, None]
    t_idx = jnp.arange(SCAN_T)[None, :]
    lag_f = jnp.clip(t_idx - s_idx, 0, SCAN_T)
    lag_b = jnp.clip(s_idx - t_idx, 0, SCAN_T)
    mf = (t_idx >= s_idx).astype(F32)[None, :, :, None, None]
    mb = (s_idx >= t_idx).astype(F32)[None, :, :, None, None]
    toep = kern[0][:, lag_f] * mf + kern[1][:, lag_b] * mb
    m = jnp.transpose(toep, (0, 1, 4, 2, 3)).reshape(SSM_GROUPS, SCAN_W, SCAN_W)

    def carry_rows(a):
        return jnp.transpose(a, (1, 3, 0, 2)).reshape(SSM_GROUPS, SSM_STATE, SCAN_W)

    x_re = carry_rows(clr[1:SCAN_T + 1, 0])
    x_im = carry_rows(-cli[1:SCAN_T + 1, 0])
    y_re = carry_rows(clr[1:SCAN_T + 1, 1][::-1])
    y_im = carry_rows(-cli[1:SCAN_T + 1, 1][::-1])
    w2 = jnp.concatenate([m, x_re, y_re, x_im, y_im], axis=1)

    a16 = jnp.concatenate([pr[SCAN_T, 0], pr[SCAN_T, 1], pi[SCAN_T, 0], pi[SCAN_T, 1]], axis=-1)
    return w1.astype(BF16), w2.astype(BF16), a16


def _s5_kernel(u_ref, w1_ref, w2_ref, a_ref, h0_ref, y_ref, fs_ref,
               vr_scr, vi_scr, cr_scr, ci_scr, fr_scr, fi_scr):
    blk = pl.program_id(1)
    seq_rows = jnp.where(blk == 0, SEQ // SCAN_T, DEC_SEQ // SCAN_T)

    for g in range(G_OCT):
        v = jnp.dot(u_ref[g], w1_ref[g], preferred_element_type=F32)
        vr_scr[pl.ds(g, SCAN_BLK, stride=G_OCT), :] = v[:, 0:HALF_W]
        vi_scr[pl.ds(g, SCAN_BLK, stride=G_OCT), :] = v[:, HALF_W:STATE_W]

    a_r = a_ref[:, 0:HALF_W]
    a_i = a_ref[:, HALF_W:STATE_W]
    h0_r = h0_ref[:, 0:HALF_W]
    h0_i = h0_ref[:, HALF_W:STATE_W]
    fwd_lanes = lax.broadcasted_iota(jnp.int32, (G_OCT, HALF_W), 1) < SSM_STATE
    bwd_lanes = jnp.logical_not(fwd_lanes)

    def step(k, carry):
        s_r, s_i = carry
        rf = pl.ds(pl.multiple_of(k * G_OCT, G_OCT), G_OCT)
        rb = pl.ds(pl.multiple_of((SCAN_BLK - 1 - k) * G_OCT, G_OCT), G_OCT)
        restart = (k & (seq_rows - 1)) == 0
        s_r = jnp.where(restart, h0_r, s_r)
        s_i = jnp.where(restart, h0_i, s_i)
        pltpu.store(cr_scr.at[rf, :], s_r, mask=fwd_lanes)
        pltpu.store(cr_scr.at[rb, :], s_r, mask=bwd_lanes)
        pltpu.store(ci_scr.at[rf, :], s_i, mask=fwd_lanes)
        pltpu.store(ci_scr.at[rb, :], s_i, mask=bwd_lanes)
        v_r = jnp.where(fwd_lanes, vr_scr[rf, :], vr_scr[rb, :])
        v_i = jnp.where(fwd_lanes, vi_scr[rf, :], vi_scr[rb, :])
        n_r = a_r * s_r - a_i * s_i + v_r
        n_i = a_r * s_i + a_i * s_r + v_i
        fr_scr[rf, :] = n_r
        fi_scr[rf, :] = n_i
        return n_r, n_i

    zero = jnp.zeros((G_OCT, HALF_W), F32)
    lax.fori_loop(0, SCAN_BLK, step, (zero, zero))

    for g in range(G_OCT):
        c_r = cr_scr[pl.ds(g, SCAN_BLK, stride=G_OCT), :].astype(BF16)
        c_i = ci_scr[pl.ds(g, SCAN_BLK, stride=G_OCT), :].astype(BF16)
        y = jnp.dot(u_ref[g], w2_ref[g, 0:SCAN_W, :], preferred_element_type=F32)
        y = y + jnp.dot(c_r, w2_ref[g, SCAN_W:SCAN_W + HALF_W, :], preferred_element_type=F32)
        y = y + jnp.dot(c_i, w2_ref[g, SCAN_W + HALF_W:SCAN_W + STATE_W, :], preferred_element_type=F32)
        y_ref[g] = y

    rows_per_seq = SEQ // SCAN_T
    for q in range(SCAN_BLK // rows_per_seq):
        last = pl.ds((q * rows_per_seq + rows_per_seq - 1) * G_OCT, G_OCT)
        fs_ref[q, :, 0:HALF_W] = fr_scr[last, :]
        fs_ref[q, :, HALF_W:STATE_W] = fi_scr[last, :]


def _s5_scan(u, w1, w2, a16, h0):
    n_oct = SSM_GROUPS // G_OCT
    return pl.pallas_call(
        _s5_kernel,
        grid=(n_oct, N_SCAN_BLK),
        in_specs=[
            pl.BlockSpec((G_OCT, SCAN_BLK, SCAN_W), lambda o, b: (o, b, 0)),
            pl.BlockSpec((G_OCT, SCAN_W, STATE_W), lambda o, b: (o, 0, 0)),
            pl.BlockSpec((G_OCT, SCAN_W + STATE_W, SCAN_W), lambda o, b: (o, 0, 0)),
            pl.BlockSpec((G_OCT, STATE_W), lambda o, b: (o, 0)),
            pl.BlockSpec((None, G_OCT, STATE_W), lambda o, b: (b, o, 0)),
        ],
        out_specs=[
            pl.BlockSpec((G_OCT, SCAN_BLK, SCAN_W), lambda o, b: (o, b, 0)),
            pl.BlockSpec((None, SCAN_BLK // (SEQ // SCAN_T), G_OCT, STATE_W), lambda o, b: (b, 0, o, 0)),
        ],
        out_shape=[
            jax.ShapeDtypeStruct((SSM_GROUPS, SCAN_ROWS, SCAN_W), F32),
            jax.ShapeDtypeStruct((N_SCAN_BLK, SCAN_BLK // (SEQ // SCAN_T), SSM_GROUPS, STATE_W), F32),
        ],
        scratch_shapes=[pltpu.VMEM((SCAN_BLK * G_OCT, HALF_W), F32) for _ in range(6)],
        compiler_params=_cparams(("arbitrary", "arbitrary")),
        name="s5_scan",
    )(u, w1, w2, a16, h0)


MIX_TM = 512


def _mix_kernel(ys_ref, u_ref, gu_ref, vn_ref, d_ref, wglu_ref, bglu_ref, ws_ref, bs_ref,
                ya_ref, yb_ref):
    y = ys_ref[...] + d_ref[...] * u_ref[...].astype(F32)
    y = jax.nn.gelu(y)
    gate = jnp.dot(y.astype(BF16), wglu_ref[...], preferred_element_type=F32) + bglu_ref[...]
    ya_ref[...] = (y * jax.nn.sigmoid(gate)).astype(BF16)
    for c in range(MIX_TM // GM_CHUNK):
        rows = pl.ds(c * GM_CHUNK, GM_CHUNK)
        for g in range(GM_GROUPS):
            cols = pl.ds(g * GM_GROUP_DIM, GM_GROUP_DIM)
            mixed = jnp.dot(ws_ref[g], vn_ref[rows, cols], preferred_element_type=F32) + bs_ref[:, cols]
            yb_ref[rows, cols] = (gu_ref[rows, cols].astype(F32) * mixed).astype(BF16)


def _mix(ys, z, d_skip, w_glu_bf16, b_glu, w_s_bf16, b_s_full):
    zspec = lambda k: pl.BlockSpec((MIX_TM, SSM_WIDTH), lambda i: (i, k))
    const = lambda shape: pl.BlockSpec(shape, lambda i: tuple(0 for _ in shape))
    return pl.pallas_call(
        _mix_kernel,
        grid=(N_TOK // MIX_TM,),
        in_specs=[
            pl.BlockSpec((MIX_TM, SSM_WIDTH), lambda i: (i, 0)),
            zspec(0), zspec(1), zspec(2),
            const((1, SSM_WIDTH)),
            const((SSM_WIDTH, SSM_WIDTH)),
            const((1, SSM_WIDTH)),
            const((GM_GROUPS, GM_CHUNK, GM_CHUNK)),
            const((GM_CHUNK, GM_WIDTH)),
        ],
        out_specs=[
            pl.BlockSpec((MIX_TM, SSM_WIDTH), lambda i: (i, 0)),
            pl.BlockSpec((MIX_TM, GM_WIDTH), lambda i: (i, 0)),
        ],
        out_shape=[
            jax.ShapeDtypeStruct((N_TOK, SSM_WIDTH), BF16),
            jax.ShapeDtypeStruct((N_TOK, GM_WIDTH), BF16),
        ],
        compiler_params=_cparams(("arbitrary",)),
        name="mixers",
    )(ys, z, z, z, d_skip.reshape(1, SSM_WIDTH), w_glu_bf16, b_glu.reshape(1, SSM_WIDTH),
      w_s_bf16, b_s_full)


MRG_TM = 256
HALF_D = D_MODEL // 2


def _merge_kernel(ya_ref, yb_ref, ga0_ref, ga1_ref, gb0_ref, gb1_ref, x_ref, g1_ref, sc_ref, sh_ref,
                  n2_ref, wpa_ref, wpb_ref, wo_ref, wr_ref, br_ref,
                  xmid_ref, h2_ref, gid_ref, cw_ref):
    i = pl.program_id(0)
    seg = i // (SEG_TOK // MRG_TM)
    pa = jnp.dot(ya_ref[...], wpa_ref[...], preferred_element_type=F32)
    pb = jnp.dot(yb_ref[...], wpb_ref[...], preferred_element_type=F32)
    m_lo = ga0_ref[...].astype(F32) * pa[:, :HALF_D] + gb0_ref[...].astype(F32) * pb[:, :HALF_D]
    m_hi = ga1_ref[...].astype(F32) * pa[:, HALF_D:] + gb1_ref[...].astype(F32) * pb[:, HALF_D:]
    mix = jnp.dot(m_lo.astype(BF16), wo_ref[0:HALF_D, :], preferred_element_type=F32)
    mix = mix + jnp.dot(m_hi.astype(BF16), wo_ref[HALF_D:D_MODEL, :], preferred_element_type=F32)
    x = x_ref[...] + g1_ref[pl.ds(seg, 1), :] * mix
    xmid_ref[...] = x

    ms = jnp.mean(x * x, axis=-1, keepdims=True)
    y = x * lax.rsqrt(ms + EPS) * n2_ref[...]
    h2 = y * (1.0 + sc_ref[pl.ds(seg, 1), :]) + sh_ref[pl.ds(seg, 1), :]
    hi = h2.astype(BF16)
    lo = (h2 - hi.astype(F32)).astype(BF16)
    h2_ref[...] = hi

    nt = (((1,), (1,)), ((), ()))
    lt = (lax.dot_general(wr_ref[...], hi, nt, preferred_element_type=F32)
          + lax.dot_general(wr_ref[...], lo, nt, preferred_element_type=F32))
    logits = lt[0:N_EXPERTS] + lt[N_EXPERTS:2 * N_EXPERTS]
    scores = jax.nn.sigmoid(logits)
    sel = scores + br_ref[...]
    ng = N_EXPERT_GROUPS
    s = [sel[j * ng:(j + 1) * ng] for j in range(EXPERTS_PER_GROUP)]
    p = [scores[j * ng:(j + 1) * ng] for j in range(EXPERTS_PER_GROUP)]
    a, b = jnp.maximum(s[0], s[1]), jnp.minimum(s[0], s[1])
    c, d = jnp.maximum(s[2], s[3]), jnp.minimum(s[2], s[3])
    grp_score = jnp.maximum(a, c) + jnp.maximum(jnp.minimum(a, c), jnp.maximum(b, d))
    best = jnp.max(grp_score, axis=0, keepdims=True)
    g_iota = lax.broadcasted_iota(jnp.int32, grp_score.shape, 0)
    g_idx = jnp.min(jnp.where(grp_score == best, g_iota, ng), axis=0, keepdims=True)
    onehot = g_iota == g_idx
    v = [jnp.sum(jnp.where(onehot, sj, 0.0), axis=0, keepdims=True) for sj in s]
    q = [jnp.sum(jnp.where(onehot, pj, 0.0), axis=0, keepdims=True) for pj in p]
    picked = []
    for j in range(EXPERTS_PER_GROUP):
        rank = jnp.zeros(v[j].shape, jnp.int32)
        for o in range(EXPERTS_PER_GROUP):
            if o == j:
                continue
            ahead = (v[o] > v[j]) | ((v[o] == v[j]) & (o < j))
            rank = rank + ahead.astype(jnp.int32)
        picked.append(jnp.where(rank < 2, q[j], 0.0))
    total = picked[0] + picked[1] + picked[2] + picked[3]
    gid_ref[...] = g_idx
    for j in range(EXPERTS_PER_GROUP):
        cw_ref[pl.ds(j, 1), :] = picked[j] / total


def _merge(ya, yb, z, xres, mod_l, norm2_g, w_pa, w_pb, w_o, wr_t, br_col):
    n_t = N_TOK // MRG_TM
    zspec = lambda k: pl.BlockSpec((MRG_TM, HALF_D), lambda i: (i, k))
    mod_spec = lambda k: pl.BlockSpec((MOD_ROWS, D_MODEL), lambda i: (0, k))
    const = lambda shape: pl.BlockSpec(shape, lambda i: tuple(0 for _ in shape),
                                       pipeline_mode=pl.Buffered(1))
    return pl.pallas_call(
        _merge_kernel,
        grid=(n_t,),
        in_specs=[
            pl.BlockSpec((MRG_TM, SSM_WIDTH), lambda i: (i, 0)),
            pl.BlockSpec((MRG_TM, GM_WIDTH), lambda i: (i, 0)),
            zspec(3), zspec(4), zspec(5), zspec(6),
            pl.BlockSpec((MRG_TM, D_MODEL), lambda i: (i, 0)),
            mod_spec(2), mod_spec(4), mod_spec(3),
            const((1, D_MODEL)),
            const((SSM_WIDTH, D_MODEL)),
            const((GM_WIDTH, D_MODEL)),
            const((D_MODEL, D_MODEL)),
            const((2 * N_EXPERTS, D_MODEL)),
            const((N_EXPERTS, 1)),
        ],
        out_specs=[
            pl.BlockSpec((MRG_TM, D_MODEL), lambda i: (i, 0)),
            pl.BlockSpec((MRG_TM, D_MODEL), lambda i: (i, 0)),
            pl.BlockSpec((None, 1, MRG_TM), lambda i: (i, 0, 0)),
            pl.BlockSpec((None, EXPERTS_PER_GROUP, MRG_TM), lambda i: (i, 0, 0)),
        ],
        out_shape=[
            jax.ShapeDtypeStruct((N_TOK, D_MODEL), F32),
            jax.ShapeDtypeStruct((N_TOK, D_MODEL), BF16),
            jax.ShapeDtypeStruct((n_t, 1, MRG_TM), jnp.int32),
            jax.ShapeDtypeStruct((n_t, EXPERTS_PER_GROUP, MRG_TM), F32),
        ],
        compiler_params=_cparams(("arbitrary",)),
        name="merge_router",
    )(ya, yb, z, z, z, z, xres, mod_l, mod_l, mod_l, norm2_g.reshape(1, D_MODEL),
      w_pa, w_pb, w_o, wr_t, br_col)


def _expert_up_kernel(gid_ref, nblk_ref, x_ref, cw_ref, wg_ref, wu_ref, h_ref, wg_scr, wu_scr):
    b = pl.program_id(1)
    prev = gid_ref[jnp.maximum(b - 1, 0)]
    fresh = (b == 0) | (gid_ref[b] != prev)

    @pl.when(fresh)
    def _():
        wg_scr[...] = wg_ref[...].astype(BF16)
        wu_scr[...] = wu_ref[...].astype(BF16)

    @pl.when(b < nblk_ref[0])
    def _():
        x = x_ref[...]
        gate = jnp.dot(x, wg_scr[...], preferred_element_type=F32)
        up = jnp.dot(x, wu_scr[...], preferred_element_type=F32)
        h_ref[...] = (gate * jax.nn.sigmoid(gate) * up * cw_ref[...]).astype(BF16)

    @pl.when(b >= nblk_ref[0])
    def _():
        h_ref[...] = jnp.zeros(h_ref.shape, BF16)


def _expert_up(blk_gid, nblk, x_sorted, cw_sorted, e_gate, e_up):
    grid_spec = pltpu.PrefetchScalarGridSpec(
        num_scalar_prefetch=2,
        grid=(EXPERTS_PER_GROUP, MOE_NBLK),
        in_specs=[
            pl.BlockSpec((MOE_BLK, D_MODEL), lambda j, b, gid, nb: (b, 0)),
            pl.BlockSpec((None, MOE_BLK, 1), lambda j, b, gid, nb: (j, b, 0)),
            pl.BlockSpec((None, D_MODEL, EXPERT_FF),
                         lambda j, b, gid, nb: (gid[b] * EXPERTS_PER_GROUP + j, 0, 0)),
            pl.BlockSpec((None, D_MODEL, EXPERT_FF),
                         lambda j, b, gid, nb: (gid[b] * EXPERTS_PER_GROUP + j, 0, 0)),
        ],
        out_specs=pl.BlockSpec((MOE_BLK, EXPERT_FF), lambda j, b, gid, nb: (b, j)),
        scratch_shapes=[pltpu.VMEM((D_MODEL, EXPERT_FF), BF16), pltpu.VMEM((D_MODEL, EXPERT_FF), BF16)],
    )
    return pl.pallas_call(
        _expert_up_kernel,
        grid_spec=grid_spec,
        out_shape=jax.ShapeDtypeStruct((MOE_SLOTS, GROUP_FF), BF16),
        compiler_params=_cparams(("arbitrary", "arbitrary")),
        name="expert_up",
    )(blk_gid, nblk, x_sorted, cw_sorted, e_gate, e_up)


DOWN_TN = 512


def _expert_down_kernel(gid_ref, nblk_ref, h_ref, wd_ref, y_ref, wd_scr):
    b = pl.program_id(1)
    prev = gid_ref[jnp.maximum(b - 1, 0)]
    fresh = (b == 0) | (gid_ref[b] != prev)

    @pl.when(fresh)
    def _():
        wd_scr[...] = wd_ref[...].astype(BF16)

    y_ref[...] = jnp.dot(h_ref[...], wd_scr[...], preferred_element_type=F32)


def _expert_down(blk_gid, nblk, h_sorted, e_down_grouped):
    grid_spec = pltpu.PrefetchScalarGridSpec(
        num_scalar_prefetch=2,
        grid=(D_MODEL // DOWN_TN, MOE_NBLK),
        in_specs=[
            pl.BlockSpec((MOE_BLK, GROUP_FF), lambda n, b, gid, nb: (b, 0)),
            pl.BlockSpec((None, GROUP_FF, DOWN_TN), lambda n, b, gid, nb: (gid[b], 0, n)),
        ],
        out_specs=pl.BlockSpec((MOE_BLK, DOWN_TN), lambda n, b, gid, nb: (b, n)),
        scratch_shapes=[pltpu.VMEM((GROUP_FF, DOWN_TN), BF16)],
    )
    return pl.pallas_call(
        _expert_down_kernel,
        grid_spec=grid_spec,
        out_shape=jax.ShapeDtypeStruct((MOE_SLOTS, D_MODEL), F32),
        compiler_params=_cparams(("arbitrary", "arbitrary")),
        name="expert_down",
    )(blk_gid, nblk, h_sorted, e_down_grouped)


def _moe(h2, gid, cw, e_gate, e_up, e_down):
    onehot = (gid[:, None] == jnp.arange(N_EXPERT_GROUPS, dtype=jnp.int32)[None, :]).astype(jnp.int32)
    csum = jnp.cumsum(onehot, axis=0)
    counts = csum[-1]
    rank = jnp.sum(csum * onehot, axis=1) - 1
    padded = (counts + MOE_BLK - 1) // MOE_BLK * MOE_BLK
    pend = jnp.cumsum(padded)
    pstart = pend - padded
    pos = (pstart[gid] + rank).astype(jnp.int32)
    slot_tok = jnp.full((MOE_SLOTS,), N_TOK, jnp.int32).at[pos].set(jnp.arange(N_TOK, dtype=jnp.int32))
    blk_gid = jnp.minimum(
        jnp.searchsorted(pend, jnp.arange(MOE_NBLK, dtype=jnp.int32) * MOE_BLK, side='right'),
        N_EXPERT_GROUPS - 1).astype(jnp.int32)
    nblk = (pend[-1:] // MOE_BLK).astype(jnp.int32)
    x_pad = jnp.concatenate([h2, jnp.zeros((1, D_MODEL), h2.dtype)], axis=0)
    cw_pad = jnp.concatenate([cw, jnp.zeros((EXPERTS_PER_GROUP, 1), cw.dtype)], axis=1)
    x_sorted = x_pad[slot_tok]
    cw_sorted = cw_pad[:, slot_tok][:, :, None]
    hid = _expert_up(blk_gid, nblk, x_sorted, cw_sorted, e_gate, e_up)
    y_sorted = _expert_down(blk_gid, nblk, hid,
                            e_down.reshape(N_EXPERT_GROUPS, GROUP_FF, D_MODEL))
    return y_sorted[pos]


FIN_TM = 512


def _final_kernel(x_ref, y_ref, g2_ref, fg_ref, o_ref):
    i = pl.program_id(0)
    seg = i // (SEG_TOK // FIN_TM)
    x = x_ref[...] + g2_ref[pl.ds(seg, 1), :] * y_ref[...]
    ms = jnp.mean(x * x, axis=-1, keepdims=True)
    o_ref[...] = x * lax.rsqrt(ms + EPS) * fg_ref[...]


def _final_norm(xmid, moe_y, mod_l, final_g):
    return pl.pallas_call(
        _final_kernel,
        grid=(N_TOK // FIN_TM,),
        in_specs=[
            pl.BlockSpec((FIN_TM, D_MODEL), lambda i: (i, 0)),
            pl.BlockSpec((FIN_TM, D_MODEL), lambda i: (i, 0)),
            pl.BlockSpec((MOD_ROWS, D_MODEL), lambda i: (0, 5)),
            pl.BlockSpec((1, D_MODEL), lambda i: (0, 0)),
        ],
        out_specs=pl.BlockSpec((FIN_TM, D_MODEL), lambda i: (i, 0)),
        out_shape=jax.ShapeDtypeStruct((N_TOK, D_MODEL), F32),
        compiler_params=_cparams(("arbitrary",)),
        name="final_norm",
    )(xmid, moe_y, mod_l, final_g.reshape(1, D_MODEL))


def _grid_pos_embed(rows):
    quarter = D_MODEL // 4
    freqs = 1.0 / (POS_BASE ** (jnp.arange(quarter, dtype=F32) / quarter))
    er = jnp.arange(rows, dtype=F32)[:, None] * freqs
    ec = jnp.arange(GRID_W, dtype=F32)[:, None] * freqs
    row_emb = jnp.concatenate([jnp.sin(er), jnp.cos(er)], axis=-1)
    col_emb = jnp.concatenate([jnp.sin(ec), jnp.cos(ec)], axis=-1)
    pe = jnp.concatenate([
        jnp.broadcast_to(row_emb[:, None, :], (rows, GRID_W, D_MODEL // 2)),
        jnp.broadcast_to(col_emb[None, :, :], (rows, GRID_W, D_MODEL // 2))], axis=-1)
    return pe.reshape(rows * GRID_W, D_MODEL)


def kernel(x_prompt, x_sample, state_ssm_re, state_ssm_im, c, c_ctx, norm1_g, norm2_g, w_mod, b_mod,
           w_in, ssm_lam_re, ssm_lam_im, ssm_log_step, ssm_b_re, ssm_b_im, ssm_c_re, ssm_c_im, ssm_d,
           w_glu, b_glu, gm_ln_g, gm_w_s, gm_b_s, w_pa, w_pb, w_o, w_router, b_router,
           e_gate, e_up, e_down, final_g):
    x = jnp.concatenate([x_prompt.reshape(N_PROMPT, D_MODEL), x_sample.reshape(N_SAMPLE, D_MODEL)], axis=0)
    cvec = jnp.concatenate([c_ctx[None], c, jnp.zeros((MOD_ROWS - 1 - DEC_BATCH, D_MODEL), F32)], axis=0)
    mod = _modulation(cvec, w_mod, b_mod)

    perm = (jnp.arange(N_EXPERT_GROUPS)[None, :] * EXPERTS_PER_GROUP
            + jnp.arange(EXPERTS_PER_GROUP)[:, None]).reshape(N_EXPERTS)
    wr = w_router.astype(F32).T[perm]
    wr_hi = wr.astype(BF16)
    wr_lo = (wr - wr_hi.astype(F32)).astype(BF16)
    wr_t = jnp.concatenate([wr_hi, wr_lo], axis=0)
    br_col = b_router.astype(F32)[perm][:, None]

    pe = _grid_pos_embed(DEC_SEQ // GRID_W)
    seg_is_latent = jnp.concatenate([jnp.zeros((1, 1), F32), jnp.ones((DEC_BATCH, 1), F32),
                                     jnp.zeros((MOD_ROWS - 1 - DEC_BATCH, 1), F32)], axis=0)
    add = pe
    add_map = lambda i, j: (i % (DEC_SEQ // INP_TM), 0)
    gain = jnp.broadcast_to(seg_is_latent, (MOD_ROWS, D_MODEL))

    new_re, new_im = [], []
    xmid = moe_y = None
    for l in range(DEPTH):
        mod_l = mod[l]
        xres, z = _inproj(x, add, add_map, gain, mod_l, norm1_g[l], gm_ln_g[l], w_in[l].astype(BF16))

        w1, w2, a16 = _s5_prep(ssm_lam_re[l], ssm_lam_im[l], ssm_log_step[l], ssm_b_re[l], ssm_b_im[l],
                               ssm_c_re[l], ssm_c_im[l])
        u = z[:, :SSM_WIDTH].reshape(SCAN_ROWS, SCAN_T, SSM_GROUPS, SSM_GROUP)
        u = jnp.transpose(u, (2, 0, 1, 3)).reshape(SSM_GROUPS, SCAN_ROWS, SCAN_W)
        h0_lat = jnp.concatenate([state_ssm_re[:, l, 0], state_ssm_re[:, l, 1],
                                  state_ssm_im[:, l, 0], state_ssm_im[:, l, 1]], axis=-1).astype(F32)
        h0 = jnp.concatenate([jnp.zeros((1, SSM_GROUPS, STATE_W), F32), h0_lat], axis=0)
        ys, fs = _s5_scan(u, w1, w2, a16, h0)
        ys = jnp.transpose(ys.reshape(SSM_GROUPS, SCAN_ROWS, SCAN_T, SSM_GROUP), (1, 2, 0, 3))
        ys = ys.reshape(N_TOK, SSM_WIDTH)
        fin = fs[0]
        p = SSM_STATE
        new_re.append(jnp.stack([fin[:, :, 0:p], fin[::-1, :, p:2 * p]], axis=1))
        new_im.append(jnp.stack([fin[:, :, 2 * p:3 * p], fin[::-1, :, 3 * p:4 * p]], axis=1))

        b_s_full = jnp.repeat(jnp.transpose(gm_b_s[l].astype(F32)), GM_GROUP_DIM, axis=1)
        ya, yb = _mix(ys, z, ssm_d[l].astype(F32), w_glu[l].astype(BF16), b_glu[l].astype(F32),
                      gm_w_s[l].astype(BF16), b_s_full)
        xmid, h2, gid, cw = _merge(ya, yb, z, xres, mod_l, norm2_g[l], w_pa[l].astype(BF16),
                                   w_pb[l].astype(BF16), w_o[l].astype(BF16), wr_t, br_col)
        gid = gid.reshape(N_TOK)
        cw = jnp.transpose(cw, (1, 0, 2)).reshape(EXPERTS_PER_GROUP, N_TOK)
        moe_y = _moe(h2, gid, cw, e_gate[l], e_up[l], e_down[l])

        x, add, gain = xmid, moe_y, mod_l[:, 5 * D_MODEL:6 * D_MODEL]
        add_map = lambda i, j: (i, 0)

    y = _final_norm(xmid, moe_y, mod[DEPTH - 1], final_g)
    y_prompt = y[:N_PROMPT].reshape(BATCH, SEQ, D_MODEL)
    y_sample = y[N_PROMPT:].reshape(DEC_BATCH, DEC_SEQ, D_MODEL)
    new_state_re = jnp.stack(new_re, axis=1).astype(x_prompt.dtype)
    new_state_im = jnp.stack(new_im, axis=1).astype(x_prompt.dtype)
    return (y_prompt, y_sample, new_state_re, new_state_im)
```

```python
import functools

import jax
import jax.numpy as jnp
from jax import lax
from jax.experimental import pallas as pl
from jax.experimental.pallas import tpu as pltpu

F32 = jnp.float32
BF16 = jnp.bfloat16
HIGHEST = lax.Precision.HIGHEST

D_MODEL = 2048
BATCH = 16
SEQ = 256
DEPTH = 2
DEC_BATCH = 2
DEC_SEQ = 4096
GRID_W = 64
POS_BASE = 10000.0
EPS = 1e-6
SSM_WIDTH = D_MODEL // 2
SSM_GROUP = 16
SSM_GROUPS = SSM_WIDTH // SSM_GROUP
SSM_STATE = 64
GM_WIDTH = D_MODEL // 2
GM_CHUNK = 128
GM_GROUPS = 8
GM_GROUP_DIM = GM_WIDTH // GM_GROUPS
IN_WIDTH = SSM_WIDTH + 2 * GM_WIDTH + 2 * D_MODEL
N_EXPERTS = 32
N_EXPERT_GROUPS = 8
EXPERTS_PER_GROUP = N_EXPERTS // N_EXPERT_GROUPS
EXPERT_FF = D_MODEL // 4
N_MOD = 6

N_PROMPT = BATCH * SEQ
N_SAMPLE = DEC_BATCH * DEC_SEQ
N_TOK = N_PROMPT + N_SAMPLE
SEG_TOK = 4096
MOD_ROWS = 8
LANES = 128
SUBLANES = 8

SCAN_T = 16
SCAN_W = SCAN_T * SSM_GROUP
SCAN_ROWS = N_TOK // SCAN_T
SCAN_BLK = 256
SCAN_TOK = SCAN_BLK * SCAN_T
N_SCAN_BLK = SCAN_ROWS // SCAN_BLK
G_OCT = SUBLANES
STATE_W = 4 * SSM_STATE
HALF_W = 2 * SSM_STATE

MOE_BLK = 256
MOE_SLOTS = N_TOK + N_EXPERT_GROUPS * MOE_BLK
MOE_NBLK = MOE_SLOTS // MOE_BLK
GROUP_FF = EXPERTS_PER_GROUP * EXPERT_FF

VMEM_LIMIT = 56 * 1024 * 1024


def _cparams(sem):
    return pltpu.CompilerParams(dimension_semantics=sem, vmem_limit_bytes=VMEM_LIMIT)


MOD_TN = 1024


def _mod_kernel(c_ref, w_ref, b_ref, o_ref):
    c = c_ref[...]
    s = c * jax.nn.sigmoid(c)
    o_ref[...] = jnp.dot(s, w_ref[...], precision=HIGHEST, preferred_element_type=F32) + b_ref[...]


def _modulation(cvec, w_mod, b_mod):
    width = N_MOD * D_MODEL
    return pl.pallas_call(
        _mod_kernel,
        grid=(DEPTH, width // MOD_TN),
        in_specs=[
            pl.BlockSpec((MOD_ROWS, D_MODEL), lambda l, n: (0, 0)),
            pl.BlockSpec((None, D_MODEL, MOD_TN), lambda l, n: (l, 0, n)),
            pl.BlockSpec((None, 1, MOD_TN), lambda l, n: (l, 0, n)),
        ],
        out_specs=pl.BlockSpec((None, MOD_ROWS, MOD_TN), lambda l, n: (l, 0, n)),
        out_shape=jax.ShapeDtypeStruct((DEPTH, MOD_ROWS, width), F32),
        compiler_params=_cparams(("arbitrary", "arbitrary")),
        name="adaln_mod",
    )(cvec, w_mod, b_mod.reshape(DEPTH, 1, width))


def _mod_spec(l, k, nargs):
    if nargs == 1:
        return pl.BlockSpec((None, MOD_ROWS, D_MODEL), lambda i: (l, 0, k))
    return pl.BlockSpec((None, MOD_ROWS, D_MODEL), lambda i, j: (l, 0, k))


INP_TM = 512
INP_TN = 1024
INP_NJ = IN_WIDTH // INP_TN
INP_PROMPT_TILES = N_PROMPT // INP_TM
Z_WIDTH = IN_WIDTH - SSM_WIDTH


def _inproj_kernel(xa_ref, xb_ref, add_ref, gain_ref, sc_ref, sh_ref, g_ref, ln_ref, w_ref,
                   xres_ref, u_ref, z_ref, h_scr, *, first):
    i = pl.program_id(0)
    j = pl.program_id(1)
    seg = i // (SEG_TOK // INP_TM)

    @pl.when(j == 0)
    def _():
        if first:
            latent = i >= INP_PROMPT_TILES
            x = jnp.where(latent, xb_ref[...] + add_ref[...], xa_ref[...])
        else:
            x = xa_ref[...] + gain_ref[pl.ds(seg, 1), :] * add_ref[...]
        xres_ref[...] = x
        ms = jnp.mean(x * x, axis=-1, keepdims=True)
        y = x * lax.rsqrt(ms + EPS) * g_ref[...]
        h = y * (1.0 + sc_ref[pl.ds(seg, 1), :]) + sh_ref[pl.ds(seg, 1), :]
        h_scr[...] = h.astype(BF16)

    acc = jnp.dot(h_scr[...], w_ref[...], preferred_element_type=F32)

    @pl.when(j == 0)
    def _():
        u_ref[...] = acc

    @pl.when(j == 1)
    def _():
        z_ref[...] = jax.nn.gelu(acc).astype(BF16)

    @pl.when(j == 2)
    def _():
        v = jax.nn.gelu(acc)
        mu = jnp.mean(v, axis=-1, keepdims=True)
        var = jnp.mean(jnp.square(v - mu), axis=-1, keepdims=True)
        z_ref[...] = ((v - mu) * lax.rsqrt(var + EPS) * ln_ref[...]).astype(BF16)

    @pl.when(j >= 3)
    def _():
        z_ref[...] = jax.nn.sigmoid(acc).astype(BF16)


def _inproj(l, xa, xb, add, gain_mod, mod, norm1_g, gm_ln_g, w_in_bf16):
    first = l == 0
    row_tile = lambda m: pl.BlockSpec((INP_TM, D_MODEL), m)
    if first:
        xa_map = lambda i, j: (jnp.minimum(i, INP_PROMPT_TILES - 1), 0)
        xb_map = lambda i, j: (jnp.maximum(i - INP_PROMPT_TILES, 0), 0)
        add_map = lambda i, j: (i % (DEC_SEQ // INP_TM), 0)
        gain_spec = _mod_spec(0, 5, 2)
    else:
        xa_map = add_map = lambda i, j: (i, 0)
        xb_map = lambda i, j: (0, 0)
        gain_spec = _mod_spec(l - 1, 5, 2)
    vec = lambda w: pl.BlockSpec((None, 1, w), lambda i, j: (l, 0, 0))
    return pl.pallas_call(
        functools.partial(_inproj_kernel, first=first),
        grid=(N_TOK // INP_TM, INP_NJ),
        in_specs=[
            row_tile(xa_map), row_tile(xb_map), row_tile(add_map), gain_spec,
            _mod_spec(l, 1, 2), _mod_spec(l, 0, 2),
            vec(D_MODEL), vec(GM_WIDTH),
            pl.BlockSpec((None, D_MODEL, INP_TN), lambda i, j: (l, 0, j)),
        ],
        out_specs=[
            pl.BlockSpec((INP_TM, D_MODEL), lambda i, j: (i, 0)),
            pl.BlockSpec((INP_TM, SSM_WIDTH), lambda i, j: (i, 0)),
            pl.BlockSpec((INP_TM, INP_TN), lambda i, j: (i, jnp.maximum(j - 1, 0))),
        ],
        out_shape=[
            jax.ShapeDtypeStruct((N_TOK, D_MODEL), F32),
            jax.ShapeDtypeStruct((N_TOK, SSM_WIDTH), F32),
            jax.ShapeDtypeStruct((N_TOK, Z_WIDTH), BF16),
        ],
        scratch_shapes=[pltpu.VMEM((INP_TM, D_MODEL), BF16)],
        compiler_params=_cparams(("arbitrary", "arbitrary")),
        name="in_proj",
    )(xa, xb, add, gain_mod, mod, mod, norm1_g.reshape(DEPTH, 1, D_MODEL),
      gm_ln_g.reshape(DEPTH, 1, GM_WIDTH), w_in_bf16)


def _s5_prep(lam_re, lam_im, log_step, b_re, b_im, c_re, c_im):
    lr = lam_re.astype(F32)
    li = lam_im.astype(F32)
    dt = jnp.exp(log_step.astype(F32))[..., None]
    mag = jnp.exp(lr * dt)
    ar = mag * jnp.cos(li * dt)
    ai = mag * jnp.sin(li * dt)
    den = lr * lr + li * li
    qr = ((ar - 1.0) * lr + ai * li) / den
    qi = (ai * lr - (ar - 1.0) * li) / den
    br = b_re.astype(F32)
    bi = b_im.astype(F32)
    bbr = qr[..., None] * br - qi[..., None] * bi
    bbi = qr[..., None] * bi + qi[..., None] * br
    k = jnp.arange(SCAN_T + 1, dtype=F32)[:, None, None, None]
    pmag = jnp.exp(k * (lr * dt))
    pr = pmag * jnp.cos(k * (li * dt))
    pi = pmag * jnp.sin(k * (li * dt))

    pf_r, pf_i = pr[:SCAN_T, 0][::-1], pi[:SCAN_T, 0][::-1]
    pb_r, pb_i = pr[:SCAN_T, 1], pi[:SCAN_T, 1]

    def lam_b(p_r, p_i, d):
        re = p_r[:, :, :, None] * bbr[d][None] - p_i[:, :, :, None] * bbi[d][None]
        im = p_r[:, :, :, None] * bbi[d][None] + p_i[:, :, :, None] * bbr[d][None]
        to = lambda a: jnp.transpose(a, (1, 0, 3, 2)).reshape(SSM_GROUPS, SCAN_W, SSM_STATE)
        return to(re), to(im)

    f_re, f_im = lam_b(pf_r, pf_i, 0)
    b_re_, b_im_ = lam_b(pb_r, pb_i, 1)
    w1 = jnp.concatenate([f_re, b_re_, f_im, b_im_], axis=-1)

    cr = c_re.astype(F32)
    ci = c_im.astype(F32)
    clr = cr[None] * pr[:, :, :, None, :] - ci[None] * pi[:, :, :, None, :]
    cli = cr[None] * pi[:, :, :, None, :] + ci[None] * pr[:, :, :, None, :]
    bbt = jnp.transpose(jnp.concatenate([bbr, -bbi], axis=2), (0, 1, 3, 2))
    clt = jnp.transpose(jnp.concatenate([clr, cli], axis=-1), (1, 2, 4, 0, 3))
    clt = clt.reshape(2, SSM_GROUPS, 2 * SSM_STATE, (SCAN_T + 1) * SSM_GROUP)
    ktab = jnp.einsum('dgjq,dgqc->dgjc', bbt, clt, precision=HIGHEST)
    kern = jnp.transpose(ktab.reshape(2, SSM_GROUPS, SSM_GROUP, SCAN_T + 1, SSM_GROUP), (0, 1, 3, 4, 2))
    s_idx = jnp.arange(SCAN_T)[:, None]
    t_idx = jnp.arange(SCAN_T)[None, :]
    lag_f = jnp.clip(t_idx - s_idx, 0, SCAN_T)
    lag_b = jnp.clip(s_idx - t_idx, 0, SCAN_T)
    mf = (t_idx >= s_idx).astype(F32)[None, :, :, None, None]
    mb = (s_idx >= t_idx).astype(F32)[None, :, :, None, None]
    toep = kern[0][:, lag_f] * mf + kern[1][:, lag_b] * mb
    m = jnp.transpose(toep, (0, 1, 4, 2, 3)).reshape(SSM_GROUPS, SCAN_W, SCAN_W)

    def carry_rows(a):
        return jnp.transpose(a, (1, 3, 0, 2)).reshape(SSM_GROUPS, SSM_STATE, SCAN_W)

    x_re = carry_rows(clr[1:SCAN_T + 1, 0])
    x_im = carry_rows(-cli[1:SCAN_T + 1, 0])
    y_re = carry_rows(clr[1:SCAN_T + 1, 1][::-1])
    y_im = carry_rows(-cli[1:SCAN_T + 1, 1][::-1])
    w2 = jnp.concatenate([m, x_re, y_re, x_im, y_im], axis=1)

    a16 = jnp.concatenate([pr[SCAN_T, 0], pr[SCAN_T, 1], pi[SCAN_T, 0], pi[SCAN_T, 1]], axis=-1)
    return w1.astype(BF16), w2.astype(BF16), a16


def _s5_kernel(u_ref, w1_ref, w2_ref, a_ref, h0_ref, y_ref, fs_ref,
               t_scr, ug_scr, vr_scr, vi_scr, cr_scr, ci_scr, fr_scr, fi_scr):
    blk = pl.program_id(1)
    seq_rows = jnp.where(blk == 0, SEQ // SCAN_T, DEC_SEQ // SCAN_T)

    for s in range(SCAN_T):
        t_scr[s] = u_ref[pl.ds(s, SCAN_BLK, stride=SCAN_T), :].T
    for g in range(G_OCT):
        stacked = t_scr[:, pl.ds(g * SSM_GROUP, SSM_GROUP), :].reshape(SCAN_W, SCAN_BLK)
        ug_scr[g] = stacked.T.astype(BF16)

    for g in range(G_OCT):
        v = jnp.dot(ug_scr[g], w1_ref[g], preferred_element_type=F32)
        vr_scr[pl.ds(g, SCAN_BLK, stride=G_OCT), :] = v[:, 0:HALF_W]
        vi_scr[pl.ds(g, SCAN_BLK, stride=G_OCT), :] = v[:, HALF_W:STATE_W]

    a_r = a_ref[:, 0:HALF_W]
    a_i = a_ref[:, HALF_W:STATE_W]
    h0_r = h0_ref[:, 0:HALF_W]
    h0_i = h0_ref[:, HALF_W:STATE_W]
    fwd_lanes = lax.broadcasted_iota(jnp.int32, (G_OCT, HALF_W), 1) < SSM_STATE
    bwd_lanes = jnp.logical_not(fwd_lanes)

    def step(k, carry):
        s_r, s_i = carry
        rf = pl.ds(pl.multiple_of(k * G_OCT, G_OCT), G_OCT)
        rb = pl.ds(pl.multiple_of((SCAN_BLK - 1 - k) * G_OCT, G_OCT), G_OCT)
        restart = (k & (seq_rows - 1)) == 0
        s_r = jnp.where(restart, h0_r, s_r)
        s_i = jnp.where(restart, h0_i, s_i)
        pltpu.store(cr_scr.at[rf, :], s_r, mask=fwd_lanes)
        pltpu.store(cr_scr.at[rb, :], s_r, mask=bwd_lanes)
        pltpu.store(ci_scr.at[rf, :], s_i, mask=fwd_lanes)
        pltpu.store(ci_scr.at[rb, :], s_i, mask=bwd_lanes)
        v_r = jnp.where(fwd_lanes, vr_scr[rf, :], vr_scr[rb, :])
        v_i = jnp.where(fwd_lanes, vi_scr[rf, :], vi_scr[rb, :])
        n_r = a_r * s_r - a_i * s_i + v_r
        n_i = a_r * s_i + a_i * s_r + v_i
        fr_scr[rf, :] = n_r
        fi_scr[rf, :] = n_i
        return n_r, n_i

    zero = jnp.zeros((G_OCT, HALF_W), F32)
    lax.fori_loop(0, SCAN_BLK, step, (zero, zero))

    for g in range(G_OCT):
        c_r = cr_scr[pl.ds(g, SCAN_BLK, stride=G_OCT), :].astype(BF16)
        c_i = ci_scr[pl.ds(g, SCAN_BLK, stride=G_OCT), :].astype(BF16)
        y = jnp.dot(ug_scr[g], w2_ref[g, 0:SCAN_W, :], preferred_element_type=F32)
        y = y + jnp.dot(c_r, w2_ref[g, SCAN_W:SCAN_W + HALF_W, :], preferred_element_type=F32)
        y = y + jnp.dot(c_i, w2_ref[g, SCAN_W + HALF_W:SCAN_W + STATE_W, :], preferred_element_type=F32)
        t_scr[:, pl.ds(g * SSM_GROUP, SSM_GROUP), :] = y.T.reshape(SCAN_T, SSM_GROUP, SCAN_BLK)
    for s in range(SCAN_T):
        y_ref[pl.ds(s, SCAN_BLK, stride=SCAN_T), :] = t_scr[s].T

    rows_per_seq = SEQ // SCAN_T
    for q in range(SCAN_BLK // rows_per_seq):
        last = pl.ds((q * rows_per_seq + rows_per_seq - 1) * G_OCT, G_OCT)
        fs_ref[q, :, 0:HALF_W] = fr_scr[last, :]
        fs_ref[q, :, HALF_W:STATE_W] = fi_scr[last, :]


def _s5_scan(l, u, w1, w2, a16, h0):
    n_oct = SSM_GROUPS // G_OCT
    n_fin = SCAN_BLK // (SEQ // SCAN_T)
    return pl.pallas_call(
        _s5_kernel,
        grid=(n_oct, N_SCAN_BLK),
        in_specs=[
            pl.BlockSpec((SCAN_TOK, LANES), lambda o, b: (b, o)),
            pl.BlockSpec((None, G_OCT, SCAN_W, STATE_W), lambda o, b: (l, o, 0, 0)),
            pl.BlockSpec((None, G_OCT, SCAN_W + STATE_W, SCAN_W), lambda o, b: (l, o, 0, 0)),
            pl.BlockSpec((None, G_OCT, STATE_W), lambda o, b: (l, o, 0)),
            pl.BlockSpec((None, None, G_OCT, STATE_W), lambda o, b: (l, b, o, 0)),
        ],
        out_specs=[
            pl.BlockSpec((SCAN_TOK, LANES), lambda o, b: (b, o)),
            pl.BlockSpec((None, n_fin, G_OCT, STATE_W), lambda o, b: (b, 0, o, 0)),
        ],
        out_shape=[
            jax.ShapeDtypeStruct((N_TOK, SSM_WIDTH), F32),
            jax.ShapeDtypeStruct((N_SCAN_BLK, n_fin, SSM_GROUPS, STATE_W), F32),
        ],
        scratch_shapes=[
            pltpu.VMEM((SCAN_T, LANES, SCAN_BLK), F32),
            pltpu.VMEM((G_OCT, SCAN_BLK, SCAN_W), BF16),
        ] + [pltpu.VMEM((SCAN_BLK * G_OCT, HALF_W), F32) for _ in range(6)],
        compiler_params=_cparams(("arbitrary", "arbitrary")),
        name="s5_scan",
    )(u, w1, w2, a16, h0)


MIX_TM = 512


def _mix_kernel(ys_ref, u_ref, gu_ref, vn_ref, d_ref, wglu_ref, bglu_ref, ws_ref, bs_ref,
                ya_ref, yb_ref):
    y = ys_ref[...] + d_ref[...] * u_ref[...]
    y = jax.nn.gelu(y)
    gate = jnp.dot(y.astype(BF16), wglu_ref[...], preferred_element_type=F32) + bglu_ref[...]
    ya_ref[...] = (y * jax.nn.sigmoid(gate)).astype(BF16)
    for c in range(MIX_TM // GM_CHUNK):
        rows = pl.ds(c * GM_CHUNK, GM_CHUNK)
        for g in range(GM_GROUPS):
            cols = pl.ds(g * GM_GROUP_DIM, GM_GROUP_DIM)
            mixed = jnp.dot(ws_ref[g], vn_ref[rows, cols], preferred_element_type=F32) + bs_ref[:, cols]
            yb_ref[rows, cols] = (gu_ref[rows, cols].astype(F32) * mixed).astype(BF16)


def _mix(l, ys, u, z, d_skip, w_glu_bf16, b_glu, w_s_bf16, b_s_full):
    tile = lambda k: pl.BlockSpec((MIX_TM, SSM_WIDTH), lambda i: (i, k))
    lay = lambda *shape: pl.BlockSpec((None,) + shape, lambda i: (l,) + tuple(0 for _ in shape))
    return pl.pallas_call(
        _mix_kernel,
        grid=(N_TOK // MIX_TM,),
        in_specs=[
            tile(0), tile(0), tile(0), tile(1),
            lay(1, SSM_WIDTH),
            lay(SSM_WIDTH, SSM_WIDTH),
            lay(1, SSM_WIDTH),
            lay(GM_GROUPS, GM_CHUNK, GM_CHUNK),
            lay(GM_CHUNK, GM_WIDTH),
        ],
        out_specs=[tile(0), tile(0)],
        out_shape=[
            jax.ShapeDtypeStruct((N_TOK, SSM_WIDTH), BF16),
            jax.ShapeDtypeStruct((N_TOK, GM_WIDTH), BF16),
        ],
        compiler_params=_cparams(("arbitrary",)),
        name="mixers",
    )(ys, u, z, z, d_skip.reshape(DEPTH, 1, SSM_WIDTH), w_glu_bf16, b_glu.reshape(DEPTH, 1, SSM_WIDTH),
      w_s_bf16, b_s_full)


MRG_TM = 256
HALF_D = D_MODEL // 2


def _merge_kernel(ya_ref, yb_ref, ga0_ref, ga1_ref, gb0_ref, gb1_ref, x_ref, g1_ref, sc_ref, sh_ref,
                  n2_ref, wpa_ref, wpb_ref, wo_ref, wr_ref, br_ref,
                  xmid_ref, h2_ref, gid_ref, cw_ref):
    i = pl.program_id(0)
    seg = i // (SEG_TOK // MRG_TM)
    pa = jnp.dot(ya_ref[...], wpa_ref[...], preferred_element_type=F32)
    pb = jnp.dot(yb_ref[...], wpb_ref[...], preferred_element_type=F32)
    m_lo = ga0_ref[...].astype(F32) * pa[:, :HALF_D] + gb0_ref[...].astype(F32) * pb[:, :HALF_D]
    m_hi = ga1_ref[...].astype(F32) * pa[:, HALF_D:] + gb1_ref[...].astype(F32) * pb[:, HALF_D:]
    mix = jnp.dot(m_lo.astype(BF16), wo_ref[0:HALF_D, :], preferred_element_type=F32)
    mix = mix + jnp.dot(m_hi.astype(BF16), wo_ref[HALF_D:D_MODEL, :], preferred_element_type=F32)
    x = x_ref[...] + g1_ref[pl.ds(seg, 1), :] * mix
    xmid_ref[...] = x

    ms = jnp.mean(x * x, axis=-1, keepdims=True)
    y = x * lax.rsqrt(ms + EPS) * n2_ref[...]
    h2 = y * (1.0 + sc_ref[pl.ds(seg, 1), :]) + sh_ref[pl.ds(seg, 1), :]
    hi = h2.astype(BF16)
    lo = (h2 - hi.astype(F32)).astype(BF16)
    h2_ref[...] = h2

    nt = (((1,), (1,)), ((), ()))
    lt = (lax.dot_general(wr_ref[...], hi, nt, preferred_element_type=F32)
          + lax.dot_general(wr_ref[...], lo, nt, preferred_element_type=F32))
    logits = lt[0:N_EXPERTS] + lt[N_EXPERTS:2 * N_EXPERTS]
    scores = jax.nn.sigmoid(logits)
    sel = scores + br_ref[...]
    ng = N_EXPERT_GROUPS
    s = [sel[j * ng:(j + 1) * ng] for j in range(EXPERTS_PER_GROUP)]
    p = [scores[j * ng:(j + 1) * ng] for j in range(EXPERTS_PER_GROUP)]
    a, b = jnp.maximum(s[0], s[1]), jnp.minimum(s[0], s[1])
    c, d = jnp.maximum(s[2], s[3]), jnp.minimum(s[2], s[3])
    grp_score = jnp.maximum(a, c) + jnp.maximum(jnp.minimum(a, c), jnp.maximum(b, d))
    best = jnp.max(grp_score, axis=0, keepdims=True)
    g_iota = lax.broadcasted_iota(jnp.int32, grp_score.shape, 0)
    g_idx = jnp.min(jnp.where(grp_score == best, g_iota, ng), axis=0, keepdims=True)
    onehot = g_iota == g_idx
    v = [jnp.sum(jnp.where(onehot, sj, 0.0), axis=0, keepdims=True) for sj in s]
    q = [jnp.sum(jnp.where(onehot, pj, 0.0), axis=0, keepdims=True) for pj in p]
    picked = []
    for j in range(EXPERTS_PER_GROUP):
        rank = jnp.zeros(v[j].shape, jnp.int32)
        for o in range(EXPERTS_PER_GROUP):
            if o == j:
                continue
            ahead = (v[o] > v[j]) | ((v[o] == v[j]) & (o < j))
            rank = rank + ahead.astype(jnp.int32)
        picked.append(jnp.where(rank < 2, q[j], 0.0))
    total = picked[0] + picked[1] + picked[2] + picked[3]
    gid_ref[...] = g_idx
    cw_rows = jnp.concatenate([pj / total for pj in picked]
                              + [jnp.zeros((LANES - EXPERTS_PER_GROUP, MRG_TM), F32)], axis=0)
    cw_ref[...] = cw_rows.T


def _merge(l, ya, yb, z, xres, mod, norm2_g, w_pa, w_pb, w_o, wr_t, br_col):
    n_t = N_TOK // MRG_TM
    zspec = lambda k: pl.BlockSpec((MRG_TM, HALF_D), lambda i: (i, k))
    once = pl.Buffered(1)
    lay = lambda *shape: pl.BlockSpec((None,) + shape, lambda i: (l,) + tuple(0 for _ in shape),
                                      pipeline_mode=once)
    const = lambda *shape: pl.BlockSpec(shape, lambda i: tuple(0 for _ in shape), pipeline_mode=once)
    return pl.pallas_call(
        _merge_kernel,
        grid=(n_t,),
        in_specs=[
            pl.BlockSpec((MRG_TM, SSM_WIDTH), lambda i: (i, 0)),
            pl.BlockSpec((MRG_TM, GM_WIDTH), lambda i: (i, 0)),
            zspec(2), zspec(3), zspec(4), zspec(5),
            pl.BlockSpec((MRG_TM, D_MODEL), lambda i: (i, 0)),
            _mod_spec(l, 2, 1), _mod_spec(l, 4, 1), _mod_spec(l, 3, 1),
            lay(1, D_MODEL),
            lay(SSM_WIDTH, D_MODEL),
            lay(GM_WIDTH, D_MODEL),
            lay(D_MODEL, D_MODEL),
            const(2 * N_EXPERTS, D_MODEL),
            const(N_EXPERTS, 1),
        ],
        out_specs=[
            pl.BlockSpec((MRG_TM, D_MODEL), lambda i: (i, 0)),
            pl.BlockSpec((MRG_TM, D_MODEL), lambda i: (i, 0)),
            pl.BlockSpec((None, 1, MRG_TM), lambda i: (i, 0, 0)),
            pl.BlockSpec((MRG_TM, LANES), lambda i: (i, 0)),
        ],
        out_shape=[
            jax.ShapeDtypeStruct((N_TOK, D_MODEL), F32),
            jax.ShapeDtypeStruct((N_TOK, D_MODEL), F32),
            jax.ShapeDtypeStruct((n_t, 1, MRG_TM), jnp.int32),
            jax.ShapeDtypeStruct((N_TOK, LANES), F32),
        ],
        compiler_params=_cparams(("arbitrary",)),
        name="merge_router",
    )(ya, yb, z, z, z, z, xres, mod, mod, mod, norm2_g.reshape(DEPTH, 1, D_MODEL),
      w_pa, w_pb, w_o, wr_t, br_col)


def _expert_up_kernel(gid_ref, nblk_ref, x_ref, cw_ref, wg_ref, wu_ref, h_ref, wg_scr, wu_scr):
    j = pl.program_id(0)
    b = pl.program_id(1)
    prev = gid_ref[jnp.maximum(b - 1, 0)]
    fresh = (b == 0) | (gid_ref[b] != prev)

    @pl.when(fresh)
    def _():
        wg_scr[...] = wg_ref[...].astype(BF16)
        wu_scr[...] = wu_ref[...].astype(BF16)

    @pl.when(b < nblk_ref[0])
    def _():
        x = x_ref[...].astype(BF16)
        lane = lax.broadcasted_iota(jnp.int32, cw_ref.shape, 1)
        w_row = jnp.sum(jnp.where(lane == j, cw_ref[...], 0.0), axis=1, keepdims=True)
        gate = jnp.dot(x, wg_scr[...], preferred_element_type=F32)
        up = jnp.dot(x, wu_scr[...], preferred_element_type=F32)
        h_ref[...] = (gate * jax.nn.sigmoid(gate) * up * w_row).astype(BF16)

    @pl.when(b >= nblk_ref[0])
    def _():
        h_ref[...] = jnp.zeros(h_ref.shape, BF16)


def _expert_up(l, blk_gid, nblk, x_sorted, cw_sorted, e_gate, e_up):
    wspec = pl.BlockSpec((None, None, D_MODEL, EXPERT_FF),
                         lambda j, b, gid, nb: (l, gid[b] * EXPERTS_PER_GROUP + j, 0, 0))
    grid_spec = pltpu.PrefetchScalarGridSpec(
        num_scalar_prefetch=2,
        grid=(EXPERTS_PER_GROUP, MOE_NBLK),
        in_specs=[
            pl.BlockSpec((MOE_BLK, D_MODEL), lambda j, b, gid, nb: (b, 0)),
            pl.BlockSpec((MOE_BLK, LANES), lambda j, b, gid, nb: (b, 0)),
            wspec, wspec,
        ],
        out_specs=pl.BlockSpec((MOE_BLK, EXPERT_FF), lambda j, b, gid, nb: (b, j)),
        scratch_shapes=[pltpu.VMEM((D_MODEL, EXPERT_FF), BF16), pltpu.VMEM((D_MODEL, EXPERT_FF), BF16)],
    )
    return pl.pallas_call(
        _expert_up_kernel,
        grid_spec=grid_spec,
        out_shape=jax.ShapeDtypeStruct((MOE_SLOTS, GROUP_FF), BF16),
        compiler_params=_cparams(("arbitrary", "arbitrary")),
        name="expert_up",
    )(blk_gid, nblk, x_sorted, cw_sorted, e_gate, e_up)


DOWN_TN = 512


def _expert_down_kernel(gid_ref, nblk_ref, h_ref, wd_ref, y_ref, wd_scr):
    b = pl.program_id(1)
    prev = gid_ref[jnp.maximum(b - 1, 0)]
    fresh = (b == 0) | (gid_ref[b] != prev)

    @pl.when(fresh)
    def _():
        wd_scr[...] = wd_ref[...].astype(BF16)

    y_ref[...] = jnp.dot(h_ref[...], wd_scr[...], preferred_element_type=F32)


def _expert_down(l, blk_gid, nblk, h_sorted, e_down_grouped):
    grid_spec = pltpu.PrefetchScalarGridSpec(
        num_scalar_prefetch=2,
        grid=(D_MODEL // DOWN_TN, MOE_NBLK),
        in_specs=[
            pl.BlockSpec((MOE_BLK, GROUP_FF), lambda n, b, gid, nb: (b, 0)),
            pl.BlockSpec((None, None, GROUP_FF, DOWN_TN), lambda n, b, gid, nb: (l, gid[b], 0, n)),
        ],
        out_specs=pl.BlockSpec((MOE_BLK, DOWN_TN), lambda n, b, gid, nb: (b, n)),
        scratch_shapes=[pltpu.VMEM((GROUP_FF, DOWN_TN), BF16)],
    )
    return pl.pallas_call(
        _expert_down_kernel,
        grid_spec=grid_spec,
        out_shape=jax.ShapeDtypeStruct((MOE_SLOTS, D_MODEL), F32),
        compiler_params=_cparams(("arbitrary", "arbitrary")),
        name="expert_down",
    )(blk_gid, nblk, h_sorted, e_down_grouped)


def _moe(l, h2, gid, cw, e_gate, e_up, e_down_grouped):
    onehot = (gid[:, None] == jnp.arange(N_EXPERT_GROUPS, dtype=jnp.int32)[None, :]).astype(jnp.int32)
    csum = jnp.cumsum(onehot, axis=0)
    counts = csum[-1]
    rank = jnp.sum(csum * onehot, axis=1) - 1
    padded = (counts + MOE_BLK - 1) // MOE_BLK * MOE_BLK
    pend = jnp.cumsum(padded)
    pstart = pend - padded
    pos = (pstart[gid] + rank).astype(jnp.int32)
    slot_tok = jnp.full((MOE_SLOTS,), N_TOK, jnp.int32).at[pos].set(jnp.arange(N_TOK, dtype=jnp.int32))
    blk_gid = jnp.minimum(
        jnp.searchsorted(pend, jnp.arange(MOE_NBLK, dtype=jnp.int32) * MOE_BLK, side='right'),
        N_EXPERT_GROUPS - 1).astype(jnp.int32)
    nblk = (pend[-1:] // MOE_BLK).astype(jnp.int32)
    filled = (slot_tok < N_TOK)[:, None]
    src = jnp.minimum(slot_tok, N_TOK - 1)
    x_sorted = h2[src]
    cw_sorted = jnp.where(filled, cw[src], 0.0)
    hid = _expert_up(l, blk_gid, nblk, x_sorted, cw_sorted, e_gate, e_up)
    y_sorted = _expert_down(l, blk_gid, nblk, hid, e_down_grouped)
    return y_sorted[pos]


FIN_TM = 512
FIN_PROMPT_TILES = N_PROMPT // FIN_TM


def _final_kernel(x_ref, y_ref, g2_ref, fg_ref, op_ref, os_ref):
    i = pl.program_id(0)
    seg = i // (SEG_TOK // FIN_TM)
    x = x_ref[...] + g2_ref[pl.ds(seg, 1), :] * y_ref[...]
    ms = jnp.mean(x * x, axis=-1, keepdims=True)
    out = x * lax.rsqrt(ms + EPS) * fg_ref[...]

    @pl.when(i < FIN_PROMPT_TILES)
    def _():
        op_ref[...] = out

    @pl.when(i >= FIN_PROMPT_TILES)
    def _():
        os_ref[...] = out


def _final_norm(xmid, moe_y, mod, final_g):
    return pl.pallas_call(
        _final_kernel,
        grid=(N_TOK // FIN_TM,),
        in_specs=[
            pl.BlockSpec((FIN_TM, D_MODEL), lambda i: (i, 0)),
            pl.BlockSpec((FIN_TM, D_MODEL), lambda i: (i, 0)),
            _mod_spec(DEPTH - 1, 5, 1),
            pl.BlockSpec((1, D_MODEL), lambda i: (0, 0)),
        ],
        out_specs=[
            pl.BlockSpec((FIN_TM, D_MODEL), lambda i: (jnp.minimum(i, FIN_PROMPT_TILES - 1), 0)),
            pl.BlockSpec((FIN_TM, D_MODEL), lambda i: (jnp.maximum(i - FIN_PROMPT_TILES, 0), 0)),
        ],
        out_shape=[
            jax.ShapeDtypeStruct((N_PROMPT, D_MODEL), F32),
            jax.ShapeDtypeStruct((N_SAMPLE, D_MODEL), F32),
        ],
        compiler_params=_cparams(("arbitrary",)),
        name="final_norm",
    )(xmid, moe_y, mod, final_g.reshape(1, D_MODEL))


def _grid_pos_embed(rows):
    quarter = D_MODEL // 4
    freqs = 1.0 / (POS_BASE ** (jnp.arange(quarter, dtype=F32) / quarter))
    er = jnp.arange(rows, dtype=F32)[:, None] * freqs
    ec = jnp.arange(GRID_W, dtype=F32)[:, None] * freqs
    row_emb = jnp.concatenate([jnp.sin(er), jnp.cos(er)], axis=-1)
    col_emb = jnp.concatenate([jnp.sin(ec), jnp.cos(ec)], axis=-1)
    pe = jnp.concatenate([
        jnp.broadcast_to(row_emb[:, None, :], (rows, GRID_W, D_MODEL // 2)),
        jnp.broadcast_to(col_emb[None, :, :], (rows, GRID_W, D_MODEL // 2))], axis=-1)
    return pe.reshape(rows * GRID_W, D_MODEL)


def kernel(x_prompt, x_sample, state_ssm_re, state_ssm_im, c, c_ctx, norm1_g, norm2_g, w_mod, b_mod,
           w_in, ssm_lam_re, ssm_lam_im, ssm_log_step, ssm_b_re, ssm_b_im, ssm_c_re, ssm_c_im, ssm_d,
           w_glu, b_glu, gm_ln_g, gm_w_s, gm_b_s, w_pa, w_pb, w_o, w_router, b_router,
           e_gate, e_up, e_down, final_g):
    cvec = jnp.concatenate([c_ctx[None], c, jnp.zeros((MOD_ROWS - 1 - DEC_BATCH, D_MODEL), F32)], axis=0)
    mod = _modulation(cvec, w_mod, b_mod)

    perm = (jnp.arange(N_EXPERT_GROUPS)[None, :] * EXPERTS_PER_GROUP
            + jnp.arange(EXPERTS_PER_GROUP)[:, None]).reshape(N_EXPERTS)
    wr = w_router.astype(F32).T[perm]
    wr_hi = wr.astype(BF16)
    wr_lo = (wr - wr_hi.astype(F32)).astype(BF16)
    wr_t = jnp.concatenate([wr_hi, wr_lo], axis=0)
    br_col = b_router.astype(F32)[perm][:, None]

    w_in_b, w_glu_b, w_s_b = w_in.astype(BF16), w_glu.astype(BF16), gm_w_s.astype(BF16)
    w_pa_b, w_pb_b, w_o_b = w_pa.astype(BF16), w_pb.astype(BF16), w_o.astype(BF16)
    b_s_full = jnp.repeat(jnp.transpose(gm_b_s.astype(F32), (0, 2, 1)), GM_GROUP_DIM, axis=2)
    e_down_grouped = e_down.reshape(DEPTH, N_EXPERT_GROUPS, GROUP_FF, D_MODEL)
    w1, w2, a16 = jax.vmap(_s5_prep)(ssm_lam_re, ssm_lam_im, ssm_log_step, ssm_b_re, ssm_b_im,
                                     ssm_c_re, ssm_c_im)
    h0_lat = jnp.concatenate([state_ssm_re[:, :, 0], state_ssm_re[:, :, 1],
                              state_ssm_im[:, :, 0], state_ssm_im[:, :, 1]], axis=-1).astype(F32)
    h0 = jnp.concatenate([jnp.zeros((DEPTH, 1, SSM_GROUPS, STATE_W), F32),
                          jnp.transpose(h0_lat, (1, 0, 2, 3))], axis=1)

    xa = x_prompt.reshape(N_PROMPT, D_MODEL)
    xb = x_sample.reshape(N_SAMPLE, D_MODEL)
    add = _grid_pos_embed(DEC_SEQ // GRID_W)

    new_re, new_im = [], []
    xmid = moe_y = None
    for l in range(DEPTH):
        xres, u, z = _inproj(l, xa, xb, add, mod, mod, norm1_g, gm_ln_g, w_in_b)
        ys, fs = _s5_scan(l, u, w1, w2, a16, h0)
        fin = fs[0]
        p = SSM_STATE
        new_re.append(jnp.stack([fin[:, :, 0:p], fin[::-1, :, p:2 * p]], axis=1))
        new_im.append(jnp.stack([fin[:, :, 2 * p:3 * p], fin[::-1, :, 3 * p:4 * p]], axis=1))

        ya, yb = _mix(l, ys, u, z, ssm_d.astype(F32), w_glu_b, b_glu.astype(F32), w_s_b, b_s_full)
        xmid, h2, gid, cw = _merge(l, ya, yb, z, xres, mod, norm2_g, w_pa_b, w_pb_b, w_o_b, wr_t, br_col)
        moe_y = _moe(l, h2, gid.reshape(N_TOK), cw, e_gate, e_up, e_down_grouped)
        xa, xb, add = xmid, xmid, moe_y

    y_prompt, y_sample = _final_norm(xmid, moe_y, mod, final_g)
    new_state_re = jnp.stack(new_re, axis=1).astype(x_prompt.dtype)
    new_state_im = jnp.stack(new_im, axis=1).astype(x_prompt.dtype)
    return (y_prompt.reshape(BATCH, SEQ, D_MODEL), y_sample.reshape(DEC_BATCH, DEC_SEQ, D_MODEL),
            new_state_re, new_state_im)
```

```python
import functools

import jax
import jax.numpy as jnp
from jax import lax
from jax.experimental import pallas as pl
from jax.experimental.pallas import tpu as pltpu

F32 = jnp.float32
BF16 = jnp.bfloat16
HIGHEST = lax.Precision.HIGHEST

D_MODEL = 2048
BATCH = 16
SEQ = 256
DEPTH = 2
DEC_BATCH = 2
DEC_SEQ = 4096
GRID_W = 64
POS_BASE = 10000.0
EPS = 1e-6
SSM_WIDTH = D_MODEL // 2
SSM_GROUP = 16
SSM_GROUPS = SSM_WIDTH // SSM_GROUP
SSM_STATE = 64
GM_WIDTH = D_MODEL // 2
GM_CHUNK = 128
GM_GROUPS = 8
GM_GROUP_DIM = GM_WIDTH // GM_GROUPS
IN_WIDTH = SSM_WIDTH + 2 * GM_WIDTH + 2 * D_MODEL
N_EXPERTS = 32
N_EXPERT_GROUPS = 8
EXPERTS_PER_GROUP = N_EXPERTS // N_EXPERT_GROUPS
EXPERT_FF = D_MODEL // 4
N_MOD = 6

N_PROMPT = BATCH * SEQ
N_SAMPLE = DEC_BATCH * DEC_SEQ
N_TOK = N_PROMPT + N_SAMPLE
SEG_TOK = 4096
MOD_ROWS = 8
LANES = 128
SUBLANES = 8

SCAN_T = 16
SCAN_W = SCAN_T * SSM_GROUP
SCAN_ROWS = N_TOK // SCAN_T
SCAN_BLK = 256
SCAN_TOK = SCAN_BLK * SCAN_T
N_SCAN_BLK = SCAN_ROWS // SCAN_BLK
G_OCT = SUBLANES
STATE_W = 4 * SSM_STATE
HALF_W = 2 * SSM_STATE

MOE_BLK = 256
MOE_SLOTS = N_TOK + N_EXPERT_GROUPS * MOE_BLK
MOE_NBLK = MOE_SLOTS // MOE_BLK
GROUP_FF = EXPERTS_PER_GROUP * EXPERT_FF

VMEM_LIMIT = 56 * 1024 * 1024


def _cparams(sem):
    return pltpu.CompilerParams(dimension_semantics=sem, vmem_limit_bytes=VMEM_LIMIT)


MOD_TN = 1024


def _mod_kernel(c_ref, w_ref, b_ref, o_ref):
    c = c_ref[...]
    s = c * jax.nn.sigmoid(c)
    o_ref[...] = jnp.dot(s, w_ref[...], precision=HIGHEST, preferred_element_type=F32) + b_ref[...]


def _modulation(cvec, w_mod, b_mod):
    width = N_MOD * D_MODEL
    return pl.pallas_call(
        _mod_kernel,
        grid=(DEPTH, width // MOD_TN),
        in_specs=[
            pl.BlockSpec((MOD_ROWS, D_MODEL), lambda l, n: (0, 0)),
            pl.BlockSpec((None, D_MODEL, MOD_TN), lambda l, n: (l, 0, n)),
            pl.BlockSpec((None, 1, MOD_TN), lambda l, n: (l, 0, n)),
        ],
        out_specs=pl.BlockSpec((None, MOD_ROWS, MOD_TN), lambda l, n: (l, 0, n)),
        out_shape=jax.ShapeDtypeStruct((DEPTH, MOD_ROWS, width), F32),
        compiler_params=_cparams(("arbitrary", "arbitrary")),
        name="adaln_mod",
    )(cvec, w_mod, b_mod.reshape(DEPTH, 1, width))


def _mod_spec(l, k, nargs):
    if nargs == 1:
        return pl.BlockSpec((None, MOD_ROWS, D_MODEL), lambda i: (l, 0, k))
    return pl.BlockSpec((None, MOD_ROWS, D_MODEL), lambda i, j: (l, 0, k))


INP_TM = 256
INP_CH = 256
INP_PROMPT_TILES = N_PROMPT // INP_TM
Z_WIDTH = IN_WIDTH - SSM_WIDTH
INP_VMEM_LIMIT = 60 * 1024 * 1024


def _inproj_kernel(*refs, first):
    if first:
        xa_ref, xb_ref, add_ref = refs[:3]
    else:
        xa_ref, add_ref, gain_ref = refs[:3]
    sc_ref, sh_ref, g_ref, ln_ref, w_ref, xres_ref, u_ref, z_ref, h_scr, v_scr = refs[3:]
    i = pl.program_id(0)
    seg = i // (SEG_TOK // INP_TM)

    if first:
        latent = i >= INP_PROMPT_TILES
        x = jnp.where(latent, xb_ref[...] + add_ref[...], xa_ref[...])
    else:
        x = xa_ref[...] + gain_ref[pl.ds(seg, 1), :] * add_ref[...]
    xres_ref[...] = x
    ms = jnp.mean(x * x, axis=-1, keepdims=True)
    y = x * lax.rsqrt(ms + EPS) * g_ref[...]
    h = y * (1.0 + sc_ref[pl.ds(seg, 1), :]) + sh_ref[pl.ds(seg, 1), :]
    h_scr[...] = h.astype(BF16)

    def proj(col):
        return jnp.dot(h_scr[...], w_ref[:, pl.ds(col, INP_CH)], preferred_element_type=F32)

    n_ch = SSM_WIDTH // INP_CH
    for c in range(n_ch):
        u_ref[:, pl.ds(c * INP_CH, INP_CH)] = proj(c * INP_CH)
    for c in range(n_ch):
        z_ref[:, pl.ds(c * INP_CH, INP_CH)] = jax.nn.gelu(proj(SSM_WIDTH + c * INP_CH)).astype(BF16)
    row_sum = jnp.zeros((INP_TM, 1), F32)
    for c in range(n_ch):
        v = jax.nn.gelu(proj(SSM_WIDTH + GM_WIDTH + c * INP_CH))
        v_scr[:, pl.ds(c * INP_CH, INP_CH)] = v
        row_sum = row_sum + jnp.sum(v, axis=-1, keepdims=True)
    mu = row_sum * (1.0 / GM_WIDTH)
    dev = v_scr[...] - mu
    var = jnp.mean(jnp.square(dev), axis=-1, keepdims=True)
    z_ref[:, pl.ds(GM_WIDTH, GM_WIDTH)] = (dev * lax.rsqrt(var + EPS) * ln_ref[...]).astype(BF16)
    gates = SSM_WIDTH + 2 * GM_WIDTH
    for c in range(2 * D_MODEL // INP_CH):
        z_ref[:, pl.ds(2 * GM_WIDTH + c * INP_CH, INP_CH)] = jax.nn.sigmoid(
            proj(gates + c * INP_CH)).astype(BF16)


def _inproj(l, xa, xb_or_add, add_or_gain, mod, norm1_g, gm_ln_g, w_in_bf16):
    first = l == 0
    row_tile = lambda m: pl.BlockSpec((INP_TM, D_MODEL), m)
    if first:
        lead = [row_tile(lambda i: (jnp.minimum(i, INP_PROMPT_TILES - 1), 0)),
                row_tile(lambda i: (jnp.maximum(i - INP_PROMPT_TILES, 0), 0)),
                row_tile(lambda i: (i % (DEC_SEQ // INP_TM), 0))]
    else:
        lead = [row_tile(lambda i: (i, 0)), row_tile(lambda i: (i, 0)), _mod_spec(l - 1, 5, 1)]
    vec = lambda w: pl.BlockSpec((None, 1, w), lambda i: (l, 0, 0))
    return pl.pallas_call(
        functools.partial(_inproj_kernel, first=first),
        grid=(N_TOK // INP_TM,),
        in_specs=lead + [
            _mod_spec(l, 1, 1), _mod_spec(l, 0, 1),
            vec(D_MODEL), vec(GM_WIDTH),
            pl.BlockSpec((None, D_MODEL, IN_WIDTH), lambda i: (l, 0, 0), pipeline_mode=pl.Buffered(1)),
        ],
        out_specs=[
            pl.BlockSpec((INP_TM, D_MODEL), lambda i: (i, 0)),
            pl.BlockSpec((INP_TM, SSM_WIDTH), lambda i: (i, 0)),
            pl.BlockSpec((INP_TM, Z_WIDTH), lambda i: (i, 0)),
        ],
        out_shape=[
            jax.ShapeDtypeStruct((N_TOK, D_MODEL), F32),
            jax.ShapeDtypeStruct((N_TOK, SSM_WIDTH), F32),
            jax.ShapeDtypeStruct((N_TOK, Z_WIDTH), BF16),
        ],
        scratch_shapes=[pltpu.VMEM((INP_TM, D_MODEL), BF16), pltpu.VMEM((INP_TM, GM_WIDTH), F32)],
        compiler_params=pltpu.CompilerParams(dimension_semantics=("arbitrary",),
                                             vmem_limit_bytes=INP_VMEM_LIMIT),
        name="in_proj",
    )(xa, xb_or_add, add_or_gain, mod, mod, norm1_g.reshape(DEPTH, 1, D_MODEL),
      gm_ln_g.reshape(DEPTH, 1, GM_WIDTH), w_in_bf16)


def _s5_prep(lam_re, lam_im, log_step, b_re, b_im, c_re, c_im):
    lr = lam_re.astype(F32)
    li = lam_im.astype(F32)
    dt = jnp.exp(log_step.astype(F32))[..., None]
    mag = jnp.exp(lr * dt)
    ar = mag * jnp.cos(li * dt)
    ai = mag * jnp.sin(li * dt)
    den = lr * lr + li * li
    qr = ((ar - 1.0) * lr + ai * li) / den
    qi = (ai * lr - (ar - 1.0) * li) / den
    br = b_re.astype(F32)
    bi = b_im.astype(F32)
    bbr = qr[..., None] * br - qi[..., None] * bi
    bbi = qr[..., None] * bi + qi[..., None] * br
    k = jnp.arange(SCAN_T + 1, dtype=F32)[:, None, None, None]
    pmag = jnp.exp(k * (lr * dt))
    pr = pmag * jnp.cos(k * (li * dt))
    pi = pmag * jnp.sin(k * (li * dt))

    pf_r, pf_i = pr[:SCAN_T, 0][::-1], pi[:SCAN_T, 0][::-1]
    pb_r, pb_i = pr[:SCAN_T, 1], pi[:SCAN_T, 1]

    def lam_b(p_r, p_i, d):
        re = p_r[:, :, :, None] * bbr[d][None] - p_i[:, :, :, None] * bbi[d][None]
        im = p_r[:, :, :, None] * bbi[d][None] + p_i[:, :, :, None] * bbr[d][None]
        to = lambda a: jnp.transpose(a, (1, 0, 3, 2)).reshape(SSM_GROUPS, SCAN_W, SSM_STATE)
        return to(re), to(im)

    f_re, f_im = lam_b(pf_r, pf_i, 0)
    b_re_, b_im_ = lam_b(pb_r, pb_i, 1)
    w1 = jnp.concatenate([f_re, b_re_, f_im, b_im_], axis=-1)

    cr = c_re.astype(F32)
    ci = c_im.astype(F32)
    clr = cr[None] * pr[:, :, :, None, :] - ci[None] * pi[:, :, :, None, :]
    cli = cr[None] * pi[:, :, :, None, :] + ci[None] * pr[:, :, :, None, :]
    bbt = jnp.transpose(jnp.concatenate([bbr, -bbi], axis=2), (0, 1, 3, 2))
    clt = jnp.transpose(jnp.concatenate([clr, cli], axis=-1), (1, 2, 4, 0, 3))
    clt = clt.reshape(2, SSM_GROUPS, 2 * SSM_STATE, (SCAN_T + 1) * SSM_GROUP)
    ktab = jnp.einsum('dgjq,dgqc->dgjc', bbt, clt, precision=HIGHEST)
    kern = jnp.transpose(ktab.reshape(2, SSM_GROUPS, SSM_GROUP, SCAN_T + 1, SSM_GROUP), (0, 1, 3, 4, 2))
    s_idx = jnp.arange(SCAN_T)[:, None]
    t_idx = jnp.arange(SCAN_T)[None, :]
    lag_f = jnp.clip(t_idx - s_idx, 0, SCAN_T)
    lag_b = jnp.clip(s_idx - t_idx, 0, SCAN_T)
    mf = (t_idx >= s_idx).astype(F32)[None, :, :, None, None]
    mb = (s_idx >= t_idx).astype(F32)[None, :, :, None, None]
    toep = kern[0][:, lag_f] * mf + kern[1][:, lag_b] * mb
    m = jnp.transpose(toep, (0, 1, 4, 2, 3)).reshape(SSM_GROUPS, SCAN_W, SCAN_W)

    def carry_rows(a):
        return jnp.transpose(a, (1, 3, 0, 2)).reshape(SSM_GROUPS, SSM_STATE, SCAN_W)

    x_re = carry_rows(clr[1:SCAN_T + 1, 0])
    x_im = carry_rows(-cli[1:SCAN_T + 1, 0])
    y_re = carry_rows(clr[1:SCAN_T + 1, 1][::-1])
    y_im = carry_rows(-cli[1:SCAN_T + 1, 1][::-1])
    w2 = jnp.concatenate([m, x_re, y_re, x_im, y_im], axis=1)

    a16 = jnp.concatenate([pr[SCAN_T, 0], pr[SCAN_T, 1], pi[SCAN_T, 0], pi[SCAN_T, 1]], axis=-1)
    return w1.astype(BF16), w2.astype(BF16), a16


def _s5_kernel(u_ref, w1_ref, w2_ref, a_ref, h0_ref, y_ref, fs_ref,
               t_scr, ug_scr, vr_scr, vi_scr, cr_scr, ci_scr, fr_scr, fi_scr):
    blk = pl.program_id(1)
    seq_rows = jnp.where(blk == 0, SEQ // SCAN_T, DEC_SEQ // SCAN_T)

    for s in range(SCAN_T):
        t_scr[s] = u_ref[pl.ds(s, SCAN_BLK, stride=SCAN_T), :].T
    for g in range(G_OCT):
        stacked = t_scr[:, pl.ds(g * SSM_GROUP, SSM_GROUP), :].reshape(SCAN_W, SCAN_BLK)
        ug_scr[g] = stacked.T.astype(BF16)

    for g in range(G_OCT):
        v = jnp.dot(ug_scr[g], w1_ref[g], preferred_element_type=F32)
        vr_scr[pl.ds(g, SCAN_BLK, stride=G_OCT), :] = v[:, 0:HALF_W]
        vi_scr[pl.ds(g, SCAN_BLK, stride=G_OCT), :] = v[:, HALF_W:STATE_W]

    a_r = a_ref[:, 0:HALF_W]
    a_i = a_ref[:, HALF_W:STATE_W]
    h0_r = h0_ref[:, 0:HALF_W]
    h0_i = h0_ref[:, HALF_W:STATE_W]
    fwd_lanes = lax.broadcasted_iota(jnp.int32, (G_OCT, HALF_W), 1) < SSM_STATE
    bwd_lanes = jnp.logical_not(fwd_lanes)

    def step(k, carry):
        s_r, s_i = carry
        rf = pl.ds(pl.multiple_of(k * G_OCT, G_OCT), G_OCT)
        rb = pl.ds(pl.multiple_of((SCAN_BLK - 1 - k) * G_OCT, G_OCT), G_OCT)
        restart = (k & (seq_rows - 1)) == 0
        s_r = jnp.where(restart, h0_r, s_r)
        s_i = jnp.where(restart, h0_i, s_i)
        pltpu.store(cr_scr.at[rf, :], s_r, mask=fwd_lanes)
        pltpu.store(cr_scr.at[rb, :], s_r, mask=bwd_lanes)
        pltpu.store(ci_scr.at[rf, :], s_i, mask=fwd_lanes)
        pltpu.store(ci_scr.at[rb, :], s_i, mask=bwd_lanes)
        v_r = jnp.where(fwd_lanes, vr_scr[rf, :], vr_scr[rb, :])
        v_i = jnp.where(fwd_lanes, vi_scr[rf, :], vi_scr[rb, :])
        n_r = a_r * s_r - a_i * s_i + v_r
        n_i = a_r * s_i + a_i * s_r + v_i
        fr_scr[rf, :] = n_r
        fi_scr[rf, :] = n_i
        return n_r, n_i

    zero = jnp.zeros((G_OCT, HALF_W), F32)
    lax.fori_loop(0, SCAN_BLK, step, (zero, zero), unroll=4)

    for g in range(G_OCT):
        c_r = cr_scr[pl.ds(g, SCAN_BLK, stride=G_OCT), :].astype(BF16)
        c_i = ci_scr[pl.ds(g, SCAN_BLK, stride=G_OCT), :].astype(BF16)
        y = jnp.dot(ug_scr[g], w2_ref[g, 0:SCAN_W, :], preferred_element_type=F32)
        y = y + jnp.dot(c_r, w2_ref[g, SCAN_W:SCAN_W + HALF_W, :], preferred_element_type=F32)
        y = y + jnp.dot(c_i, w2_ref[g, SCAN_W + HALF_W:SCAN_W + STATE_W, :], preferred_element_type=F32)
        t_scr[:, pl.ds(g * SSM_GROUP, SSM_GROUP), :] = y.T.reshape(SCAN_T, SSM_GROUP, SCAN_BLK)
    for s in range(SCAN_T):
        y_ref[pl.ds(s, SCAN_BLK, stride=SCAN_T), :] = t_scr[s].T

    rows_per_seq = SEQ // SCAN_T
    for q in range(SCAN_BLK // rows_per_seq):
        last = pl.ds((q * rows_per_seq + rows_per_seq - 1) * G_OCT, G_OCT)
        fs_ref[q, :, 0:HALF_W] = fr_scr[last, :]
        fs_ref[q, :, HALF_W:STATE_W] = fi_scr[last, :]


def _s5_scan(l, u, w1, w2, a16, h0):
    n_oct = SSM_GROUPS // G_OCT
    n_fin = SCAN_BLK // (SEQ // SCAN_T)
    return pl.pallas_call(
        _s5_kernel,
        grid=(n_oct, N_SCAN_BLK),
        in_specs=[
            pl.BlockSpec((SCAN_TOK, LANES), lambda o, b: (b, o)),
            pl.BlockSpec((None, G_OCT, SCAN_W, STATE_W), lambda o, b: (l, o, 0, 0)),
            pl.BlockSpec((None, G_OCT, SCAN_W + STATE_W, SCAN_W), lambda o, b: (l, o, 0, 0)),
            pl.BlockSpec((None, G_OCT, STATE_W), lambda o, b: (l, o, 0)),
            pl.BlockSpec((None, None, G_OCT, STATE_W), lambda o, b: (l, b, o, 0)),
        ],
        out_specs=[
            pl.BlockSpec((SCAN_TOK, LANES), lambda o, b: (b, o)),
            pl.BlockSpec((None, n_fin, G_OCT, STATE_W), lambda o, b: (b, 0, o, 0)),
        ],
        out_shape=[
            jax.ShapeDtypeStruct((N_TOK, SSM_WIDTH), F32),
            jax.ShapeDtypeStruct((N_SCAN_BLK, n_fin, SSM_GROUPS, STATE_W), F32),
        ],
        scratch_shapes=[
            pltpu.VMEM((SCAN_T, LANES, SCAN_BLK), F32),
            pltpu.VMEM((G_OCT, SCAN_BLK, SCAN_W), BF16),
        ] + [pltpu.VMEM((SCAN_BLK * G_OCT, HALF_W), F32) for _ in range(6)],
        compiler_params=_cparams(("arbitrary", "arbitrary")),
        name="s5_scan",
    )(u, w1, w2, a16, h0)


MIX_TM = 512


def _mix_kernel(ys_ref, u_ref, gu_ref, vn_ref, d_ref, wglu_ref, bglu_ref, ws_ref, bs_ref,
                ya_ref, yb_ref):
    y = ys_ref[...] + d_ref[...] * u_ref[...]
    y = jax.nn.gelu(y)
    gate = jnp.dot(y.astype(BF16), wglu_ref[...], preferred_element_type=F32) + bglu_ref[...]
    ya_ref[...] = (y * jax.nn.sigmoid(gate)).astype(BF16)
    for c in range(MIX_TM // GM_CHUNK):
        rows = pl.ds(c * GM_CHUNK, GM_CHUNK)
        for g in range(GM_GROUPS):
            cols = pl.ds(g * GM_GROUP_DIM, GM_GROUP_DIM)
            mixed = jnp.dot(ws_ref[g], vn_ref[rows, cols], preferred_element_type=F32) + bs_ref[:, cols]
            yb_ref[rows, cols] = (gu_ref[rows, cols].astype(F32) * mixed).astype(BF16)


def _mix(l, ys, u, z, d_skip, w_glu_bf16, b_glu, w_s_bf16, b_s_full):
    tile = lambda k: pl.BlockSpec((MIX_TM, SSM_WIDTH), lambda i: (i, k))
    lay = lambda *shape: pl.BlockSpec((None,) + shape, lambda i: (l,) + tuple(0 for _ in shape))
    return pl.pallas_call(
        _mix_kernel,
        grid=(N_TOK // MIX_TM,),
        in_specs=[
            tile(0), tile(0), tile(0), tile(1),
            lay(1, SSM_WIDTH),
            lay(SSM_WIDTH, SSM_WIDTH),
            lay(1, SSM_WIDTH),
            lay(GM_GROUPS, GM_CHUNK, GM_CHUNK),
            lay(GM_CHUNK, GM_WIDTH),
        ],
        out_specs=[tile(0), tile(0)],
        out_shape=[
            jax.ShapeDtypeStruct((N_TOK, SSM_WIDTH), BF16),
            jax.ShapeDtypeStruct((N_TOK, GM_WIDTH), BF16),
        ],
        compiler_params=_cparams(("arbitrary",)),
        name="mixers",
    )(ys, u, z, z, d_skip.reshape(DEPTH, 1, SSM_WIDTH), w_glu_bf16, b_glu.reshape(DEPTH, 1, SSM_WIDTH),
      w_s_bf16, b_s_full)


MRG_TM = 256
HALF_D = D_MODEL // 2


def _merge_kernel(ya_ref, yb_ref, ga0_ref, ga1_ref, gb0_ref, gb1_ref, x_ref, g1_ref, sc_ref, sh_ref,
                  n2_ref, wpa_ref, wpb_ref, wo_ref, wr_ref, br_ref,
                  xmid_ref, h2_ref, gid_ref, cw_ref):
    i = pl.program_id(0)
    seg = i // (SEG_TOK // MRG_TM)
    pa = jnp.dot(ya_ref[...], wpa_ref[...], preferred_element_type=F32)
    pb = jnp.dot(yb_ref[...], wpb_ref[...], preferred_element_type=F32)
    m_lo = ga0_ref[...].astype(F32) * pa[:, :HALF_D] + gb0_ref[...].astype(F32) * pb[:, :HALF_D]
    m_hi = ga1_ref[...].astype(F32) * pa[:, HALF_D:] + gb1_ref[...].astype(F32) * pb[:, HALF_D:]
    mix = jnp.dot(m_lo.astype(BF16), wo_ref[0:HALF_D, :], preferred_element_type=F32)
    mix = mix + jnp.dot(m_hi.astype(BF16), wo_ref[HALF_D:D_MODEL, :], preferred_element_type=F32)
    x = x_ref[...] + g1_ref[pl.ds(seg, 1), :] * mix
    xmid_ref[...] = x

    ms = jnp.mean(x * x, axis=-1, keepdims=True)
    y = x * lax.rsqrt(ms + EPS) * n2_ref[...]
    h2 = y * (1.0 + sc_ref[pl.ds(seg, 1), :]) + sh_ref[pl.ds(seg, 1), :]
    hi = h2.astype(BF16)
    hi_f = hi.astype(F32)
    lo = (h2 - hi_f).astype(BF16)
    bits = lax.bitcast_convert_type(hi_f, jnp.uint32)
    h2_ref[...] = bits[:, :HALF_D] | (bits[:, HALF_D:] >> 16)

    nt = (((1,), (1,)), ((), ()))
    lt = (lax.dot_general(wr_ref[...], hi, nt, preferred_element_type=F32)
          + lax.dot_general(wr_ref[...], lo, nt, preferred_element_type=F32))
    logits = lt[0:N_EXPERTS] + lt[N_EXPERTS:2 * N_EXPERTS]
    scores = jax.nn.sigmoid(logits)
    sel = scores + br_ref[...]
    ng = N_EXPERT_GROUPS
    s = [sel[j * ng:(j + 1) * ng] for j in range(EXPERTS_PER_GROUP)]
    p = [scores[j * ng:(j + 1) * ng] for j in range(EXPERTS_PER_GROUP)]
    a, b = jnp.maximum(s[0], s[1]), jnp.minimum(s[0], s[1])
    c, d = jnp.maximum(s[2], s[3]), jnp.minimum(s[2], s[3])
    grp_score = jnp.maximum(a, c) + jnp.maximum(jnp.minimum(a, c), jnp.maximum(b, d))
    best = jnp.max(grp_score, axis=0, keepdims=True)
    g_iota = lax.broadcasted_iota(jnp.int32, grp_score.shape, 0)
    g_idx = jnp.min(jnp.where(grp_score == best, g_iota, ng), axis=0, keepdims=True)
    onehot = g_iota == g_idx
    v = [jnp.sum(jnp.where(onehot, sj, 0.0), axis=0, keepdims=True) for sj in s]
    q = [jnp.sum(jnp.where(onehot, pj, 0.0), axis=0, keepdims=True) for pj in p]
    picked = []
    for j in range(EXPERTS_PER_GROUP):
        rank = jnp.zeros(v[j].shape, jnp.int32)
        for o in range(EXPERTS_PER_GROUP):
            if o == j:
                continue
            ahead = (v[o] > v[j]) | ((v[o] == v[j]) & (o < j))
            rank = rank + ahead.astype(jnp.int32)
        picked.append(jnp.where(rank < 2, q[j], 0.0))
    total = picked[0] + picked[1] + picked[2] + picked[3]
    gid_ref[...] = g_idx
    cw_rows = jnp.concatenate([pj / total for pj in picked]
                              + [jnp.zeros((LANES - EXPERTS_PER_GROUP, MRG_TM), F32)], axis=0)
    cw_ref[...] = cw_rows.T


def _merge(l, ya, yb, z, xres, mod, norm2_g, w_pa, w_pb, w_o, wr_t, br_col):
    n_t = N_TOK // MRG_TM
    zspec = lambda k: pl.BlockSpec((MRG_TM, HALF_D), lambda i: (i, k))
    once = pl.Buffered(1)
    lay = lambda *shape: pl.BlockSpec((None,) + shape, lambda i: (l,) + tuple(0 for _ in shape),
                                      pipeline_mode=once)
    const = lambda *shape: pl.BlockSpec(shape, lambda i: tuple(0 for _ in shape), pipeline_mode=once)
    return pl.pallas_call(
        _merge_kernel,
        grid=(n_t,),
        in_specs=[
            pl.BlockSpec((MRG_TM, SSM_WIDTH), lambda i: (i, 0)),
            pl.BlockSpec((MRG_TM, GM_WIDTH), lambda i: (i, 0)),
            zspec(2), zspec(3), zspec(4), zspec(5),
            pl.BlockSpec((MRG_TM, D_MODEL), lambda i: (i, 0)),
            _mod_spec(l, 2, 1), _mod_spec(l, 4, 1), _mod_spec(l, 3, 1),
            lay(1, D_MODEL),
            lay(SSM_WIDTH, D_MODEL),
            lay(GM_WIDTH, D_MODEL),
            lay(D_MODEL, D_MODEL),
            const(2 * N_EXPERTS, D_MODEL),
            const(N_EXPERTS, 1),
        ],
        out_specs=[
            pl.BlockSpec((MRG_TM, D_MODEL), lambda i: (i, 0)),
            pl.BlockSpec((MRG_TM, HALF_D), lambda i: (i, 0)),
            pl.BlockSpec((None, 1, MRG_TM), lambda i: (i, 0, 0)),
            pl.BlockSpec((MRG_TM, LANES), lambda i: (i, 0)),
        ],
        out_shape=[
            jax.ShapeDtypeStruct((N_TOK, D_MODEL), F32),
            jax.ShapeDtypeStruct((N_TOK, HALF_D), jnp.uint32),
            jax.ShapeDtypeStruct((n_t, 1, MRG_TM), jnp.int32),
            jax.ShapeDtypeStruct((N_TOK, LANES), F32),
        ],
        compiler_params=_cparams(("arbitrary",)),
        name="merge_router",
    )(ya, yb, z, z, z, z, xres, mod, mod, mod, norm2_g.reshape(DEPTH, 1, D_MODEL),
      w_pa, w_pb, w_o, wr_t, br_col)


def _expert_up_kernel(gid_ref, nblk_ref, x_ref, cw_ref, wg_ref, wu_ref, h_ref, wg_scr, wu_scr):
    j = pl.program_id(0)
    b = pl.program_id(1)
    prev = gid_ref[jnp.maximum(b - 1, 0)]
    fresh = (b == 0) | (gid_ref[b] != prev)

    @pl.when(fresh)
    def _():
        wg_scr[...] = wg_ref[...].astype(BF16)
        wu_scr[...] = wu_ref[...].astype(BF16)

    @pl.when(b < nblk_ref[0])
    def _():
        packed = x_ref[...]
        x_lo = lax.bitcast_convert_type(packed & jnp.uint32(0xFFFF0000), F32).astype(BF16)
        x_hi = lax.bitcast_convert_type(packed << 16, F32).astype(BF16)
        lane = lax.broadcasted_iota(jnp.int32, cw_ref.shape, 1)
        w_row = jnp.sum(jnp.where(lane == j, cw_ref[...], 0.0), axis=1, keepdims=True)
        gate = (jnp.dot(x_lo, wg_scr[0:HALF_D, :], preferred_element_type=F32)
                + jnp.dot(x_hi, wg_scr[HALF_D:D_MODEL, :], preferred_element_type=F32))
        up = (jnp.dot(x_lo, wu_scr[0:HALF_D, :], preferred_element_type=F32)
              + jnp.dot(x_hi, wu_scr[HALF_D:D_MODEL, :], preferred_element_type=F32))
        h_ref[...] = (gate * jax.nn.sigmoid(gate) * up * w_row).astype(BF16)

    @pl.when(b >= nblk_ref[0])
    def _():
        h_ref[...] = jnp.zeros(h_ref.shape, BF16)


def _expert_up(l, blk_gid, nblk, x_sorted, cw_sorted, e_gate, e_up):
    wspec = pl.BlockSpec((None, None, D_MODEL, EXPERT_FF),
                         lambda j, b, gid, nb: (l, gid[b] * EXPERTS_PER_GROUP + j, 0, 0))
    grid_spec = pltpu.PrefetchScalarGridSpec(
        num_scalar_prefetch=2,
        grid=(EXPERTS_PER_GROUP, MOE_NBLK),
        in_specs=[
            pl.BlockSpec((MOE_BLK, HALF_D), lambda j, b, gid, nb: (b, 0)),
            pl.BlockSpec((MOE_BLK, LANES), lambda j, b, gid, nb: (b, 0)),
            wspec, wspec,
        ],
        out_specs=pl.BlockSpec((MOE_BLK, EXPERT_FF), lambda j, b, gid, nb: (b, j)),
        scratch_shapes=[pltpu.VMEM((D_MODEL, EXPERT_FF), BF16), pltpu.VMEM((D_MODEL, EXPERT_FF), BF16)],
    )
    return pl.pallas_call(
        _expert_up_kernel,
        grid_spec=grid_spec,
        out_shape=jax.ShapeDtypeStruct((MOE_SLOTS, GROUP_FF), BF16),
        compiler_params=_cparams(("arbitrary", "arbitrary")),
        name="expert_up",
    )(blk_gid, nblk, x_sorted, cw_sorted, e_gate, e_up)


DOWN_CH = 512


def _expert_down_kernel(gid_ref, nblk_ref, h_ref, wd_ref, y_ref, wd_scr):
    b = pl.program_id(0)
    prev = gid_ref[jnp.maximum(b - 1, 0)]
    fresh = (b == 0) | (gid_ref[b] != prev)

    @pl.when(fresh)
    def _():
        for c in range(GROUP_FF // DOWN_CH):
            rows = pl.ds(c * DOWN_CH, DOWN_CH)
            wd_scr[rows, :] = wd_ref[rows, :].astype(BF16)

    for c in range(D_MODEL // DOWN_CH):
        cols = pl.ds(c * DOWN_CH, DOWN_CH)
        y_ref[:, cols] = jnp.dot(h_ref[...], wd_scr[:, cols], preferred_element_type=F32)


def _expert_down(l, blk_gid, nblk, h_sorted, e_down_grouped):
    grid_spec = pltpu.PrefetchScalarGridSpec(
        num_scalar_prefetch=2,
        grid=(MOE_NBLK,),
        in_specs=[
            pl.BlockSpec((MOE_BLK, GROUP_FF), lambda b, gid, nb: (b, 0)),
            pl.BlockSpec((None, None, GROUP_FF, D_MODEL), lambda b, gid, nb: (l, gid[b], 0, 0)),
        ],
        out_specs=pl.BlockSpec((MOE_BLK, D_MODEL), lambda b, gid, nb: (b, 0)),
        scratch_shapes=[pltpu.VMEM((GROUP_FF, D_MODEL), BF16)],
    )
    return pl.pallas_call(
        _expert_down_kernel,
        grid_spec=grid_spec,
        out_shape=jax.ShapeDtypeStruct((MOE_SLOTS, D_MODEL), F32),
        compiler_params=_cparams(("arbitrary",)),
        name="expert_down",
    )(blk_gid, nblk, h_sorted, e_down_grouped)


def _moe(l, h2, gid, cw, e_gate, e_up, e_down_grouped):
    onehot = (gid[:, None] == jnp.arange(N_EXPERT_GROUPS, dtype=jnp.int32)[None, :]).astype(jnp.int32)
    csum = jnp.cumsum(onehot, axis=0)
    counts = csum[-1]
    rank = jnp.sum(csum * onehot, axis=1) - 1
    padded = (counts + MOE_BLK - 1) // MOE_BLK * MOE_BLK
    pend = jnp.cumsum(padded)
    pstart = pend - padded
    pos = (pstart[gid] + rank).astype(jnp.int32)
    slot_tok = jnp.full((MOE_SLOTS,), N_TOK, jnp.int32).at[pos].set(jnp.arange(N_TOK, dtype=jnp.int32))
    blk_gid = jnp.minimum(
        jnp.searchsorted(pend, jnp.arange(MOE_NBLK, dtype=jnp.int32) * MOE_BLK, side='right'),
        N_EXPERT_GROUPS - 1).astype(jnp.int32)
    nblk = (pend[-1:] // MOE_BLK).astype(jnp.int32)
    filled = (slot_tok < N_TOK)[:, None]
    src = jnp.minimum(slot_tok, N_TOK - 1)
    x_sorted = h2[src]
    cw_sorted = jnp.where(filled, cw[src], 0.0)
    hid = _expert_up(l, blk_gid, nblk, x_sorted, cw_sorted, e_gate, e_up)
    y_sorted = _expert_down(l, blk_gid, nblk, hid, e_down_grouped)
    return y_sorted[pos]


FIN_TM = 512
FIN_PROMPT_TILES = N_PROMPT // FIN_TM


def _final_kernel(x_ref, y_ref, g2_ref, fg_ref, op_ref, os_ref):
    i = pl.program_id(0)
    seg = i // (SEG_TOK // FIN_TM)
    x = x_ref[...] + g2_ref[pl.ds(seg, 1), :] * y_ref[...]
    ms = jnp.mean(x * x, axis=-1, keepdims=True)
    out = x * lax.rsqrt(ms + EPS) * fg_ref[...]

    @pl.when(i < FIN_PROMPT_TILES)
    def _():
        op_ref[...] = out

    @pl.when(i >= FIN_PROMPT_TILES)
    def _():
        os_ref[...] = out


def _final_norm(xmid, moe_y, mod, final_g):
    return pl.pallas_call(
        _final_kernel,
        grid=(N_TOK // FIN_TM,),
        in_specs=[
            pl.BlockSpec((FIN_TM, D_MODEL), lambda i: (i, 0)),
            pl.BlockSpec((FIN_TM, D_MODEL), lambda i: (i, 0)),
            _mod_spec(DEPTH - 1, 5, 1),
            pl.BlockSpec((1, D_MODEL), lambda i: (0, 0)),
        ],
        out_specs=[
            pl.BlockSpec((FIN_TM, D_MODEL), lambda i: (jnp.minimum(i, FIN_PROMPT_TILES - 1), 0)),
            pl.BlockSpec((FIN_TM, D_MODEL), lambda i: (jnp.maximum(i - FIN_PROMPT_TILES, 0), 0)),
        ],
        out_shape=[
            jax.ShapeDtypeStruct((N_PROMPT, D_MODEL), F32),
            jax.ShapeDtypeStruct((N_SAMPLE, D_MODEL), F32),
        ],
        compiler_params=_cparams(("arbitrary",)),
        name="final_norm",
    )(xmid, moe_y, mod, final_g.reshape(1, D_MODEL))


def _grid_pos_embed(rows):
    quarter = D_MODEL // 4
    freqs = 1.0 / (POS_BASE ** (jnp.arange(quarter, dtype=F32) / quarter))
    er = jnp.arange(rows, dtype=F32)[:, None] * freqs
    ec = jnp.arange(GRID_W, dtype=F32)[:, None] * freqs
    row_emb = jnp.concatenate([jnp.sin(er), jnp.cos(er)], axis=-1)
    col_emb = jnp.concatenate([jnp.sin(ec), jnp.cos(ec)], axis=-1)
    pe = jnp.concatenate([
        jnp.broadcast_to(row_emb[:, None, :], (rows, GRID_W, D_MODEL // 2)),
        jnp.broadcast_to(col_emb[None, :, :], (rows, GRID_W, D_MODEL // 2))], axis=-1)
    return pe.reshape(rows * GRID_W, D_MODEL)


def kernel(x_prompt, x_sample, state_ssm_re, state_ssm_im, c, c_ctx, norm1_g, norm2_g, w_mod, b_mod,
           w_in, ssm_lam_re, ssm_lam_im, ssm_log_step, ssm_b_re, ssm_b_im, ssm_c_re, ssm_c_im, ssm_d,
           w_glu, b_glu, gm_ln_g, gm_w_s, gm_b_s, w_pa, w_pb, w_o, w_router, b_router,
           e_gate, e_up, e_down, final_g):
    cvec = jnp.concatenate([c_ctx[None], c, jnp.zeros((MOD_ROWS - 1 - DEC_BATCH, D_MODEL), F32)], axis=0)
    mod = _modulation(cvec, w_mod, b_mod)

    perm = (jnp.arange(N_EXPERT_GROUPS)[None, :] * EXPERTS_PER_GROUP
            + jnp.arange(EXPERTS_PER_GROUP)[:, None]).reshape(N_EXPERTS)
    wr = w_router.astype(F32).T[perm]
    wr_hi = wr.astype(BF16)
    wr_lo = (wr - wr_hi.astype(F32)).astype(BF16)
    wr_t = jnp.concatenate([wr_hi, wr_lo], axis=0)
    br_col = b_router.astype(F32)[perm][:, None]

    w_in_b, w_glu_b, w_s_b = w_in.astype(BF16), w_glu.astype(BF16), gm_w_s.astype(BF16)
    w_pa_b, w_pb_b, w_o_b = w_pa.astype(BF16), w_pb.astype(BF16), w_o.astype(BF16)
    b_s_full = jnp.repeat(jnp.transpose(gm_b_s.astype(F32), (0, 2, 1)), GM_GROUP_DIM, axis=2)
    e_down_grouped = e_down.reshape(DEPTH, N_EXPERT_GROUPS, GROUP_FF, D_MODEL)
    w1, w2, a16 = jax.vmap(_s5_prep)(ssm_lam_re, ssm_lam_im, ssm_log_step, ssm_b_re, ssm_b_im,
                                     ssm_c_re, ssm_c_im)
    h0_lat = jnp.concatenate([state_ssm_re[:, :, 0], state_ssm_re[:, :, 1],
                              state_ssm_im[:, :, 0], state_ssm_im[:, :, 1]], axis=-1).astype(F32)
    h0 = jnp.concatenate([jnp.zeros((DEPTH, 1, SSM_GROUPS, STATE_W), F32),
                          jnp.transpose(h0_lat, (1, 0, 2, 3))], axis=1)

    inproj_in = (x_prompt.reshape(N_PROMPT, D_MODEL), x_sample.reshape(N_SAMPLE, D_MODEL),
                 _grid_pos_embed(DEC_SEQ // GRID_W))

    new_re, new_im = [], []
    xmid = moe_y = None
    for l in range(DEPTH):
        xres, u, z = _inproj(l, *inproj_in, mod, norm1_g, gm_ln_g, w_in_b)
        ys, fs = _s5_scan(l, u, w1, w2, a16, h0)
        fin = fs[0]
        p = SSM_STATE
        new_re.append(jnp.stack([fin[:, :, 0:p], fin[::-1, :, p:2 * p]], axis=1))
        new_im.append(jnp.stack([fin[:, :, 2 * p:3 * p], fin[::-1, :, 3 * p:4 * p]], axis=1))

        ya, yb = _mix(l, ys, u, z, ssm_d.astype(F32), w_glu_b, b_glu.astype(F32), w_s_b, b_s_full)
        xmid, h2, gid, cw = _merge(l, ya, yb, z, xres, mod, norm2_g, w_pa_b, w_pb_b, w_o_b, wr_t, br_col)
        moe_y = _moe(l, h2, gid.reshape(N_TOK), cw, e_gate, e_up, e_down_grouped)
        inproj_in = (xmid, moe_y, mod)

    y_prompt, y_sample = _final_norm(xmid, moe_y, mod, final_g)
    new_state_re = jnp.stack(new_re, axis=1).astype(x_prompt.dtype)
    new_state_im = jnp.stack(new_im, axis=1).astype(x_prompt.dtype)
    return (y_prompt.reshape(BATCH, SEQ, D_MODEL), y_sample.reshape(DEC_BATCH, DEC_SEQ, D_MODEL),
            new_state_re, new_state_im)
```

```python
import functools

import jax
import jax.numpy as jnp
from jax import lax
from jax.experimental import pallas as pl
from jax.experimental.pallas import tpu as pltpu

F32 = jnp.float32
BF16 = jnp.bfloat16
HIGHEST = lax.Precision.HIGHEST

D_MODEL = 2048
BATCH = 16
SEQ = 256
DEPTH = 2
DEC_BATCH = 2
DEC_SEQ = 4096
GRID_W = 64
POS_BASE = 10000.0
EPS = 1e-6
SSM_WIDTH = D_MODEL // 2
SSM_GROUP = 16
SSM_GROUPS = SSM_WIDTH // SSM_GROUP
SSM_STATE = 64
GM_WIDTH = D_MODEL // 2
GM_CHUNK = 128
GM_GROUPS = 8
GM_GROUP_DIM = GM_WIDTH // GM_GROUPS
IN_WIDTH = SSM_WIDTH + 2 * GM_WIDTH + 2 * D_MODEL
N_EXPERTS = 32
N_EXPERT_GROUPS = 8
EXPERTS_PER_GROUP = N_EXPERTS // N_EXPERT_GROUPS
EXPERT_FF = D_MODEL // 4
N_MOD = 6

N_PROMPT = BATCH * SEQ
N_SAMPLE = DEC_BATCH * DEC_SEQ
N_TOK = N_PROMPT + N_SAMPLE
SEG_TOK = 4096
MOD_ROWS = 8
LANES = 128
SUBLANES = 8

SCAN_T = 16
SCAN_W = SCAN_T * SSM_GROUP
SCAN_ROWS = N_TOK // SCAN_T
SCAN_BLK = 256
SCAN_TOK = SCAN_BLK * SCAN_T
N_SCAN_BLK = SCAN_ROWS // SCAN_BLK
G_OCT = SUBLANES
STATE_W = 4 * SSM_STATE
HALF_W = 2 * SSM_STATE

MOE_BLK = 512
MOE_SLOTS = N_TOK + N_EXPERT_GROUPS * MOE_BLK
MOE_NBLK = MOE_SLOTS // MOE_BLK
GROUP_FF = EXPERTS_PER_GROUP * EXPERT_FF

VMEM_LIMIT = 56 * 1024 * 1024


def _cparams(sem):
    return pltpu.CompilerParams(dimension_semantics=sem, vmem_limit_bytes=VMEM_LIMIT)


MOD_TN = 1024


def _mod_kernel(c_ref, w_ref, b_ref, o_ref):
    c = c_ref[...]
    s = c * jax.nn.sigmoid(c)
    s_hi = s.astype(BF16)
    s_lo = (s - s_hi.astype(F32)).astype(BF16)
    w = w_ref[...]
    w_hi = w.astype(BF16)
    w_lo = (w - w_hi.astype(F32)).astype(BF16)
    both = jnp.dot(jnp.concatenate([s_hi, s_lo], axis=0), w_hi, preferred_element_type=F32)
    cross = jnp.dot(s_hi, w_lo, preferred_element_type=F32)
    o_ref[...] = both[0:MOD_ROWS] + both[MOD_ROWS:2 * MOD_ROWS] + cross + b_ref[...]


def _modulation(cvec, w_mod, b_mod):
    width = N_MOD * D_MODEL
    return pl.pallas_call(
        _mod_kernel,
        grid=(DEPTH, width // MOD_TN),
        in_specs=[
            pl.BlockSpec((MOD_ROWS, D_MODEL), lambda l, n: (0, 0)),
            pl.BlockSpec((None, D_MODEL, MOD_TN), lambda l, n: (l, 0, n)),
            pl.BlockSpec((None, 1, MOD_TN), lambda l, n: (l, 0, n)),
        ],
        out_specs=pl.BlockSpec((None, MOD_ROWS, MOD_TN), lambda l, n: (l, 0, n)),
        out_shape=jax.ShapeDtypeStruct((DEPTH, MOD_ROWS, width), F32),
        compiler_params=_cparams(("arbitrary", "arbitrary")),
        name="adaln_mod",
    )(cvec, w_mod, b_mod.reshape(DEPTH, 1, width))


def _mod_spec(l, k, nargs):
    if nargs == 1:
        return pl.BlockSpec((None, MOD_ROWS, D_MODEL), lambda i: (l, 0, k))
    return pl.BlockSpec((None, MOD_ROWS, D_MODEL), lambda i, j: (l, 0, k))


INP_TM = 256
INP_CH = 256
INP_PROMPT_TILES = N_PROMPT // INP_TM
Z_WIDTH = IN_WIDTH - SSM_WIDTH
INP_VMEM_LIMIT = 60 * 1024 * 1024


def _inproj_kernel(*refs, first):
    if first:
        xa_ref, xb_ref, add_ref = refs[:3]
    else:
        xa_ref, add_ref, gain_ref = refs[:3]
    sc_ref, sh_ref, g_ref, ln_ref, w_ref, xres_ref, u_ref, z_ref, h_scr, v_scr = refs[3:]
    i = pl.program_id(0)
    seg = i // (SEG_TOK // INP_TM)

    if first:
        latent = i >= INP_PROMPT_TILES
        x = jnp.where(latent, xb_ref[...] + add_ref[...], xa_ref[...])
    else:
        x = xa_ref[...] + gain_ref[pl.ds(seg, 1), :] * add_ref[...]
    xres_ref[...] = x
    ms = jnp.mean(x * x, axis=-1, keepdims=True)
    y = x * lax.rsqrt(ms + EPS) * g_ref[...]
    h = y * (1.0 + sc_ref[pl.ds(seg, 1), :]) + sh_ref[pl.ds(seg, 1), :]
    h_scr[...] = h.astype(BF16)

    def proj(col):
        return jnp.dot(h_scr[...], w_ref[:, pl.ds(col, INP_CH)], preferred_element_type=F32)

    n_ch = SSM_WIDTH // INP_CH
    for c in range(n_ch):
        u_ref[:, pl.ds(c * INP_CH, INP_CH)] = proj(c * INP_CH)
    for c in range(n_ch):
        z_ref[:, pl.ds(c * INP_CH, INP_CH)] = jax.nn.gelu(proj(SSM_WIDTH + c * INP_CH)).astype(BF16)
    row_sum = jnp.zeros((INP_TM, 1), F32)
    for c in range(n_ch):
        v = jax.nn.gelu(proj(SSM_WIDTH + GM_WIDTH + c * INP_CH))
        v_scr[:, pl.ds(c * INP_CH, INP_CH)] = v
        row_sum = row_sum + jnp.sum(v, axis=-1, keepdims=True)
    mu = row_sum * (1.0 / GM_WIDTH)
    dev = v_scr[...] - mu
    var = jnp.mean(jnp.square(dev), axis=-1, keepdims=True)
    z_ref[:, pl.ds(GM_WIDTH, GM_WIDTH)] = (dev * lax.rsqrt(var + EPS) * ln_ref[...]).astype(BF16)
    gates = SSM_WIDTH + 2 * GM_WIDTH
    for c in range(2 * D_MODEL // INP_CH):
        z_ref[:, pl.ds(2 * GM_WIDTH + c * INP_CH, INP_CH)] = jax.nn.sigmoid(
            proj(gates + c * INP_CH)).astype(BF16)


def _inproj(l, xa, xb_or_add, add_or_gain, mod, norm1_g, gm_ln_g, w_in_bf16):
    first = l == 0
    row_tile = lambda m: pl.BlockSpec((INP_TM, D_MODEL), m)
    if first:
        lead = [row_tile(lambda i: (jnp.minimum(i, INP_PROMPT_TILES - 1), 0)),
                row_tile(lambda i: (jnp.maximum(i - INP_PROMPT_TILES, 0), 0)),
                row_tile(lambda i: (i % (DEC_SEQ // INP_TM), 0))]
    else:
        lead = [row_tile(lambda i: (i, 0)), row_tile(lambda i: (i, 0)), _mod_spec(l - 1, 5, 1)]
    vec = lambda w: pl.BlockSpec((None, 1, w), lambda i: (l, 0, 0))
    return pl.pallas_call(
        functools.partial(_inproj_kernel, first=first),
        grid=(N_TOK // INP_TM,),
        in_specs=lead + [
            _mod_spec(l, 1, 1), _mod_spec(l, 0, 1),
            vec(D_MODEL), vec(GM_WIDTH),
            pl.BlockSpec((None, D_MODEL, IN_WIDTH), lambda i: (l, 0, 0), pipeline_mode=pl.Buffered(1)),
        ],
        out_specs=[
            pl.BlockSpec((INP_TM, D_MODEL), lambda i: (i, 0)),
            pl.BlockSpec((INP_TM, SSM_WIDTH), lambda i: (i, 0)),
            pl.BlockSpec((INP_TM, Z_WIDTH), lambda i: (i, 0)),
        ],
        out_shape=[
            jax.ShapeDtypeStruct((N_TOK, D_MODEL), F32),
            jax.ShapeDtypeStruct((N_TOK, SSM_WIDTH), F32),
            jax.ShapeDtypeStruct((N_TOK, Z_WIDTH), BF16),
        ],
        scratch_shapes=[pltpu.VMEM((INP_TM, D_MODEL), BF16), pltpu.VMEM((INP_TM, GM_WIDTH), F32)],
        compiler_params=pltpu.CompilerParams(dimension_semantics=("arbitrary",),
                                             vmem_limit_bytes=INP_VMEM_LIMIT),
        name="in_proj",
    )(xa, xb_or_add, add_or_gain, mod, mod, norm1_g.reshape(DEPTH, 1, D_MODEL),
      gm_ln_g.reshape(DEPTH, 1, GM_WIDTH), w_in_bf16)


def _s5_prep(lam_re, lam_im, log_step, b_re, b_im, c_re, c_im):
    lr = lam_re.astype(F32)
    li = lam_im.astype(F32)
    dt = jnp.exp(log_step.astype(F32))[..., None]
    mag = jnp.exp(lr * dt)
    ar = mag * jnp.cos(li * dt)
    ai = mag * jnp.sin(li * dt)
    den = lr * lr + li * li
    qr = ((ar - 1.0) * lr + ai * li) / den
    qi = (ai * lr - (ar - 1.0) * li) / den
    br = b_re.astype(F32)
    bi = b_im.astype(F32)
    bbr = qr[..., None] * br - qi[..., None] * bi
    bbi = qr[..., None] * bi + qi[..., None] * br
    k = jnp.arange(SCAN_T + 1, dtype=F32)[:, None, None, None]
    pmag = jnp.exp(k * (lr * dt))
    pr = pmag * jnp.cos(k * (li * dt))
    pi = pmag * jnp.sin(k * (li * dt))

    pf_r, pf_i = pr[:SCAN_T, 0][::-1], pi[:SCAN_T, 0][::-1]
    pb_r, pb_i = pr[:SCAN_T, 1], pi[:SCAN_T, 1]

    def lam_b(p_r, p_i, d):
        re = p_r[:, :, :, None] * bbr[d][None] - p_i[:, :, :, None] * bbi[d][None]
        im = p_r[:, :, :, None] * bbi[d][None] + p_i[:, :, :, None] * bbr[d][None]
        to = lambda a: jnp.transpose(a, (1, 0, 3, 2)).reshape(SSM_GROUPS, SCAN_W, SSM_STATE)
        return to(re), to(im)

    f_re, f_im = lam_b(pf_r, pf_i, 0)
    b_re_, b_im_ = lam_b(pb_r, pb_i, 1)
    w1 = jnp.concatenate([f_re, b_re_, f_im, b_im_], axis=-1)

    cr = c_re.astype(F32)
    ci = c_im.astype(F32)
    clr = cr[None] * pr[:, :, :, None, :] - ci[None] * pi[:, :, :, None, :]
    cli = cr[None] * pi[:, :, :, None, :] + ci[None] * pr[:, :, :, None, :]
    bbt = jnp.transpose(jnp.concatenate([bbr, -bbi], axis=2), (0, 1, 3, 2))
    clt = jnp.transpose(jnp.concatenate([clr, cli], axis=-1), (1, 2, 4, 0, 3))
    clt = clt.reshape(2, SSM_GROUPS, 2 * SSM_STATE, (SCAN_T + 1) * SSM_GROUP)
    ktab = jnp.einsum('dgjq,dgqc->dgjc', bbt, clt, precision=HIGHEST)
    kern = jnp.transpose(ktab.reshape(2, SSM_GROUPS, SSM_GROUP, SCAN_T + 1, SSM_GROUP), (0, 1, 3, 4, 2))
    s_idx = jnp.arange(SCAN_T)[:, None]
    t_idx = jnp.arange(SCAN_T)[None, :]
    lag_f = jnp.clip(t_idx - s_idx, 0, SCAN_T)
    lag_b = jnp.clip(s_idx - t_idx, 0, SCAN_T)
    mf = (t_idx >= s_idx).astype(F32)[None, :, :, None, None]
    mb = (s_idx >= t_idx).astype(F32)[None, :, :, None, None]
    toep = kern[0][:, lag_f] * mf + kern[1][:, lag_b] * mb
    m = jnp.transpose(toep, (0, 1, 4, 2, 3)).reshape(SSM_GROUPS, SCAN_W, SCAN_W)

    def carry_rows(a):
        return jnp.transpose(a, (1, 3, 0, 2)).reshape(SSM_GROUPS, SSM_STATE, SCAN_W)

    x_re = carry_rows(clr[1:SCAN_T + 1, 0])
    x_im = carry_rows(-cli[1:SCAN_T + 1, 0])
    y_re = carry_rows(clr[1:SCAN_T + 1, 1][::-1])
    y_im = carry_rows(-cli[1:SCAN_T + 1, 1][::-1])
    w2 = jnp.concatenate([m, x_re, y_re, x_im, y_im], axis=1)

    a16 = jnp.concatenate([pr[SCAN_T, 0], pr[SCAN_T, 1], pi[SCAN_T, 0], pi[SCAN_T, 1]], axis=-1)
    return w1.astype(BF16), w2.astype(BF16), a16


def _s5_kernel(u_ref, w1_ref, w2_ref, a_ref, h0_ref, y_ref, fs_ref,
               t_scr, ug_scr, vr_scr, vi_scr, cr_scr, ci_scr, fr_scr, fi_scr):
    blk = pl.program_id(1)
    seq_rows = jnp.where(blk == 0, SEQ // SCAN_T, DEC_SEQ // SCAN_T)

    for s in range(SCAN_T):
        t_scr[s] = u_ref[pl.ds(s, SCAN_BLK, stride=SCAN_T), :].T
    for g in range(G_OCT):
        stacked = t_scr[:, pl.ds(g * SSM_GROUP, SSM_GROUP), :].reshape(SCAN_W, SCAN_BLK)
        ug_scr[g] = stacked.T.astype(BF16)

    for g in range(G_OCT):
        v = jnp.dot(ug_scr[g], w1_ref[g], preferred_element_type=F32)
        vr_scr[pl.ds(g, SCAN_BLK, stride=G_OCT), :] = v[:, 0:HALF_W]
        vi_scr[pl.ds(g, SCAN_BLK, stride=G_OCT), :] = v[:, HALF_W:STATE_W]

    a_r = a_ref[:, 0:HALF_W]
    a_i = a_ref[:, HALF_W:STATE_W]
    h0_r = h0_ref[:, 0:HALF_W]
    h0_i = h0_ref[:, HALF_W:STATE_W]
    fwd_lanes = lax.broadcasted_iota(jnp.int32, (G_OCT, HALF_W), 1) < SSM_STATE
    bwd_lanes = jnp.logical_not(fwd_lanes)

    def step(k, carry):
        s_r, s_i = carry
        rf = pl.ds(pl.multiple_of(k * G_OCT, G_OCT), G_OCT)
        rb = pl.ds(pl.multiple_of((SCAN_BLK - 1 - k) * G_OCT, G_OCT), G_OCT)
        restart = (k & (seq_rows - 1)) == 0
        s_r = jnp.where(restart, h0_r, s_r)
        s_i = jnp.where(restart, h0_i, s_i)
        pltpu.store(cr_scr.at[rf, :], s_r, mask=fwd_lanes)
        pltpu.store(cr_scr.at[rb, :], s_r, mask=bwd_lanes)
        pltpu.store(ci_scr.at[rf, :], s_i, mask=fwd_lanes)
        pltpu.store(ci_scr.at[rb, :], s_i, mask=bwd_lanes)
        v_r = jnp.where(fwd_lanes, vr_scr[rf, :], vr_scr[rb, :])
        v_i = jnp.where(fwd_lanes, vi_scr[rf, :], vi_scr[rb, :])
        n_r = a_r * s_r - a_i * s_i + v_r
        n_i = a_r * s_i + a_i * s_r + v_i
        fr_scr[rf, :] = n_r
        fi_scr[rf, :] = n_i
        return n_r, n_i

    zero = jnp.zeros((G_OCT, HALF_W), F32)
    lax.fori_loop(0, SCAN_BLK, step, (zero, zero), unroll=4)

    for g in range(G_OCT):
        c_r = cr_scr[pl.ds(g, SCAN_BLK, stride=G_OCT), :].astype(BF16)
        c_i = ci_scr[pl.ds(g, SCAN_BLK, stride=G_OCT), :].astype(BF16)
        y = jnp.dot(ug_scr[g], w2_ref[g, 0:SCAN_W, :], preferred_element_type=F32)
        y = y + jnp.dot(c_r, w2_ref[g, SCAN_W:SCAN_W + HALF_W, :], preferred_element_type=F32)
        y = y + jnp.dot(c_i, w2_ref[g, SCAN_W + HALF_W:SCAN_W + STATE_W, :], preferred_element_type=F32)
        t_scr[:, pl.ds(g * SSM_GROUP, SSM_GROUP), :] = y.T.reshape(SCAN_T, SSM_GROUP, SCAN_BLK)
    for s in range(SCAN_T):
        y_ref[pl.ds(s, SCAN_BLK, stride=SCAN_T), :] = t_scr[s].T

    rows_per_seq = SEQ // SCAN_T
    for q in range(SCAN_BLK // rows_per_seq):
        last = pl.ds((q * rows_per_seq + rows_per_seq - 1) * G_OCT, G_OCT)
        fs_ref[q, :, 0:HALF_W] = fr_scr[last, :]
        fs_ref[q, :, HALF_W:STATE_W] = fi_scr[last, :]


def _s5_scan(l, u, w1, w2, a16, h0):
    n_oct = SSM_GROUPS // G_OCT
    n_fin = SCAN_BLK // (SEQ // SCAN_T)
    return pl.pallas_call(
        _s5_kernel,
        grid=(n_oct, N_SCAN_BLK),
        in_specs=[
            pl.BlockSpec((SCAN_TOK, LANES), lambda o, b: (b, o)),
            pl.BlockSpec((None, G_OCT, SCAN_W, STATE_W), lambda o, b: (l, o, 0, 0)),
            pl.BlockSpec((None, G_OCT, SCAN_W + STATE_W, SCAN_W), lambda o, b: (l, o, 0, 0)),
            pl.BlockSpec((None, G_OCT, STATE_W), lambda o, b: (l, o, 0)),
            pl.BlockSpec((None, None, G_OCT, STATE_W), lambda o, b: (l, b, o, 0)),
        ],
        out_specs=[
            pl.BlockSpec((SCAN_TOK, LANES), lambda o, b: (b, o)),
            pl.BlockSpec((None, n_fin, G_OCT, STATE_W), lambda o, b: (b, 0, o, 0)),
        ],
        out_shape=[
            jax.ShapeDtypeStruct((N_TOK, SSM_WIDTH), F32),
            jax.ShapeDtypeStruct((N_SCAN_BLK, n_fin, SSM_GROUPS, STATE_W), F32),
        ],
        scratch_shapes=[
            pltpu.VMEM((SCAN_T, LANES, SCAN_BLK), F32),
            pltpu.VMEM((G_OCT, SCAN_BLK, SCAN_W), BF16),
        ] + [pltpu.VMEM((SCAN_BLK * G_OCT, HALF_W), F32) for _ in range(6)],
        compiler_params=_cparams(("arbitrary", "arbitrary")),
        name="s5_scan",
    )(u, w1, w2, a16, h0)


MIX_TM = 512


def _mix_kernel(ys_ref, u_ref, gu_ref, vn_ref, d_ref, wglu_ref, bglu_ref, ws_ref, bs_ref,
                ya_ref, yb_ref):
    y = ys_ref[...] + d_ref[...] * u_ref[...]
    y = jax.nn.gelu(y)
    gate = jnp.dot(y.astype(BF16), wglu_ref[...], preferred_element_type=F32) + bglu_ref[...]
    ya_ref[...] = (y * jax.nn.sigmoid(gate)).astype(BF16)
    for c in range(MIX_TM // GM_CHUNK):
        rows = pl.ds(c * GM_CHUNK, GM_CHUNK)
        for g in range(GM_GROUPS):
            cols = pl.ds(g * GM_GROUP_DIM, GM_GROUP_DIM)
            mixed = jnp.dot(ws_ref[g], vn_ref[rows, cols], preferred_element_type=F32) + bs_ref[:, cols]
            yb_ref[rows, cols] = (gu_ref[rows, cols].astype(F32) * mixed).astype(BF16)


def _mix(l, ys, u, z, d_skip, w_glu_bf16, b_glu, w_s_bf16, b_s_full):
    tile = lambda k: pl.BlockSpec((MIX_TM, SSM_WIDTH), lambda i: (i, k))
    lay = lambda *shape: pl.BlockSpec((None,) + shape, lambda i: (l,) + tuple(0 for _ in shape))
    return pl.pallas_call(
        _mix_kernel,
        grid=(N_TOK // MIX_TM,),
        in_specs=[
            tile(0), tile(0), tile(0), tile(1),
            lay(1, SSM_WIDTH),
            lay(SSM_WIDTH, SSM_WIDTH),
            lay(1, SSM_WIDTH),
            lay(GM_GROUPS, GM_CHUNK, GM_CHUNK),
            lay(GM_CHUNK, GM_WIDTH),
        ],
        out_specs=[tile(0), tile(0)],
        out_shape=[
            jax.ShapeDtypeStruct((N_TOK, SSM_WIDTH), BF16),
            jax.ShapeDtypeStruct((N_TOK, GM_WIDTH), BF16),
        ],
        compiler_params=_cparams(("arbitrary",)),
        name="mixers",
    )(ys, u, z, z, d_skip.reshape(DEPTH, 1, SSM_WIDTH), w_glu_bf16, b_glu.reshape(DEPTH, 1, SSM_WIDTH),
      w_s_bf16, b_s_full)


MRG_TM = 512
MRG_SUB = 256
HALF_D = D_MODEL // 2


def _merge_kernel(ya_ref, yb_ref, ga0_ref, ga1_ref, gb0_ref, gb1_ref, x_ref, g1_ref, sc_ref, sh_ref,
                  n2_ref, wpa_ref, wpb_ref, wo_ref, wr_ref, br_ref, tri_ref,
                  xmid_ref, h2_ref, gid_ref, rank_ref, cw_ref, cnt_ref, cnt_scr):
    i = pl.program_id(0)
    seg = i // (SEG_TOK // MRG_TM)

    @pl.when(i == 0)
    def _():
        cnt_scr[...] = jnp.zeros(cnt_scr.shape, F32)

    for sub in range(MRG_TM // MRG_SUB):
        _merge_rows(pl.ds(sub * MRG_SUB, MRG_SUB), seg,
                    ya_ref, yb_ref, ga0_ref, ga1_ref, gb0_ref, gb1_ref, x_ref, g1_ref, sc_ref, sh_ref,
                    n2_ref, wpa_ref, wpb_ref, wo_ref, wr_ref, br_ref, tri_ref,
                    xmid_ref, h2_ref, gid_ref, rank_ref, cw_ref, cnt_scr)
    cnt_ref[...] = jnp.broadcast_to(cnt_scr[...], cnt_ref.shape)


def _merge_rows(rows, seg, ya_ref, yb_ref, ga0_ref, ga1_ref, gb0_ref, gb1_ref, x_ref, g1_ref, sc_ref, sh_ref,
                n2_ref, wpa_ref, wpb_ref, wo_ref, wr_ref, br_ref, tri_ref,
                xmid_ref, h2_ref, gid_ref, rank_ref, cw_ref, cnt_scr):
    pa = jnp.dot(ya_ref[rows, :], wpa_ref[...], preferred_element_type=F32)
    pb = jnp.dot(yb_ref[rows, :], wpb_ref[...], preferred_element_type=F32)
    m_lo = ga0_ref[rows, :].astype(F32) * pa[:, :HALF_D] + gb0_ref[rows, :].astype(F32) * pb[:, :HALF_D]
    m_hi = ga1_ref[rows, :].astype(F32) * pa[:, HALF_D:] + gb1_ref[rows, :].astype(F32) * pb[:, HALF_D:]
    mix = jnp.dot(m_lo.astype(BF16), wo_ref[0:HALF_D, :], preferred_element_type=F32)
    mix = mix + jnp.dot(m_hi.astype(BF16), wo_ref[HALF_D:D_MODEL, :], preferred_element_type=F32)
    x = x_ref[rows, :] + g1_ref[pl.ds(seg, 1), :] * mix
    xmid_ref[rows, :] = x

    ms = jnp.mean(x * x, axis=-1, keepdims=True)
    y = x * lax.rsqrt(ms + EPS) * n2_ref[...]
    h2 = y * (1.0 + sc_ref[pl.ds(seg, 1), :]) + sh_ref[pl.ds(seg, 1), :]
    hi = h2.astype(BF16)
    hi_f = hi.astype(F32)
    lo = (h2 - hi_f).astype(BF16)
    bits = lax.bitcast_convert_type(hi_f, jnp.uint32)
    h2_ref[rows, :] = bits[:, :HALF_D] | (bits[:, HALF_D:] >> 16)

    nt = (((1,), (1,)), ((), ()))
    lt = (lax.dot_general(wr_ref[...], hi, nt, preferred_element_type=F32)
          + lax.dot_general(wr_ref[...], lo, nt, preferred_element_type=F32))
    logits = lt[0:N_EXPERTS] + lt[N_EXPERTS:2 * N_EXPERTS]
    scores = jax.nn.sigmoid(logits)
    sel = scores + br_ref[...]
    ng = N_EXPERT_GROUPS
    s = [sel[j * ng:(j + 1) * ng] for j in range(EXPERTS_PER_GROUP)]
    p = [scores[j * ng:(j + 1) * ng] for j in range(EXPERTS_PER_GROUP)]
    a, b = jnp.maximum(s[0], s[1]), jnp.minimum(s[0], s[1])
    c, d = jnp.maximum(s[2], s[3]), jnp.minimum(s[2], s[3])
    grp_score = jnp.maximum(a, c) + jnp.maximum(jnp.minimum(a, c), jnp.maximum(b, d))
    best = jnp.max(grp_score, axis=0, keepdims=True)
    g_iota = lax.broadcasted_iota(jnp.int32, grp_score.shape, 0)
    g_idx = jnp.min(jnp.where(grp_score == best, g_iota, ng), axis=0, keepdims=True)
    onehot = g_iota == g_idx
    v = [jnp.sum(jnp.where(onehot, sj, 0.0), axis=0, keepdims=True) for sj in s]
    q = [jnp.sum(jnp.where(onehot, pj, 0.0), axis=0, keepdims=True) for pj in p]
    picked = []
    for j in range(EXPERTS_PER_GROUP):
        rank = jnp.zeros(v[j].shape, jnp.int32)
        for o in range(EXPERTS_PER_GROUP):
            if o == j:
                continue
            ahead = (v[o] > v[j]) | ((v[o] == v[j]) & (o < j))
            rank = rank + ahead.astype(jnp.int32)
        picked.append(jnp.where(rank < 2, q[j], 0.0))
    total = picked[0] + picked[1] + picked[2] + picked[3]
    gid_ref[:, rows] = g_idx
    cw_rows = jnp.concatenate([pj / total for pj in picked]
                              + [jnp.zeros((LANES - EXPERTS_PER_GROUP, MRG_SUB), F32)], axis=0)
    cw_ref[rows, :] = cw_rows.T

    hot = onehot.astype(BF16)
    within = jnp.dot(hot, tri_ref[...], preferred_element_type=F32)
    before = jnp.sum(jnp.where(onehot, within + cnt_scr[...], 0.0), axis=0, keepdims=True) - 1.0
    rank_ref[:, rows] = before.astype(jnp.int32)
    cnt_scr[...] = cnt_scr[...] + within[:, MRG_SUB - 1:MRG_SUB]


def _merge(l, ya, yb, z, xres, mod, norm2_g, w_pa, w_pb, w_o, wr_t, br_col):
    n_t = N_TOK // MRG_TM
    tri = (jnp.arange(MRG_SUB)[:, None] <= jnp.arange(MRG_SUB)[None, :]).astype(BF16)
    zspec = lambda k: pl.BlockSpec((MRG_TM, HALF_D), lambda i: (i, k))
    once = pl.Buffered(1)
    lay = lambda *shape: pl.BlockSpec((None,) + shape, lambda i: (l,) + tuple(0 for _ in shape),
                                      pipeline_mode=once)
    const = lambda *shape: pl.BlockSpec(shape, lambda i: tuple(0 for _ in shape), pipeline_mode=once)
    return pl.pallas_call(
        _merge_kernel,
        grid=(n_t,),
        in_specs=[
            pl.BlockSpec((MRG_TM, SSM_WIDTH), lambda i: (i, 0)),
            pl.BlockSpec((MRG_TM, GM_WIDTH), lambda i: (i, 0)),
            zspec(2), zspec(3), zspec(4), zspec(5),
            pl.BlockSpec((MRG_TM, D_MODEL), lambda i: (i, 0)),
            _mod_spec(l, 2, 1), _mod_spec(l, 4, 1), _mod_spec(l, 3, 1),
            lay(1, D_MODEL),
            lay(SSM_WIDTH, D_MODEL),
            lay(GM_WIDTH, D_MODEL),
            lay(D_MODEL, D_MODEL),
            const(2 * N_EXPERTS, D_MODEL),
            const(N_EXPERTS, 1),
            const(MRG_SUB, MRG_SUB),
        ],
        out_specs=[
            pl.BlockSpec((MRG_TM, D_MODEL), lambda i: (i, 0)),
            pl.BlockSpec((MRG_TM, HALF_D), lambda i: (i, 0)),
            pl.BlockSpec((None, 1, MRG_TM), lambda i: (i, 0, 0)),
            pl.BlockSpec((None, 1, MRG_TM), lambda i: (i, 0, 0)),
            pl.BlockSpec((MRG_TM, LANES), lambda i: (i, 0)),
            pl.BlockSpec((N_EXPERT_GROUPS, LANES), lambda i: (0, 0)),
        ],
        out_shape=[
            jax.ShapeDtypeStruct((N_TOK, D_MODEL), F32),
            jax.ShapeDtypeStruct((N_TOK, HALF_D), jnp.uint32),
            jax.ShapeDtypeStruct((n_t, 1, MRG_TM), jnp.int32),
            jax.ShapeDtypeStruct((n_t, 1, MRG_TM), jnp.int32),
            jax.ShapeDtypeStruct((N_TOK, LANES), F32),
            jax.ShapeDtypeStruct((N_EXPERT_GROUPS, LANES), F32),
        ],
        scratch_shapes=[pltpu.VMEM((N_EXPERT_GROUPS, 1), F32)],
        compiler_params=_cparams(("arbitrary",)),
        name="merge_router",
    )(ya, yb, z, z, z, z, xres, mod, mod, mod, norm2_g.reshape(DEPTH, 1, D_MODEL),
      w_pa, w_pb, w_o, wr_t, br_col, tri)


def _expert_up_kernel(gid_ref, nblk_ref, x_ref, cw_ref, wg_ref, wu_ref, h_ref, wg_scr, wu_scr):
    j = pl.program_id(0)
    b = pl.program_id(1)
    prev = gid_ref[jnp.maximum(b - 1, 0)]
    fresh = (b == 0) | (gid_ref[b] != prev)

    @pl.when(fresh)
    def _():
        wg_scr[...] = wg_ref[...].astype(BF16)
        wu_scr[...] = wu_ref[...].astype(BF16)

    @pl.when(b < nblk_ref[0])
    def _():
        packed = x_ref[...]
        x_lo = lax.bitcast_convert_type(packed & jnp.uint32(0xFFFF0000), F32).astype(BF16)
        x_hi = lax.bitcast_convert_type(packed << 16, F32).astype(BF16)
        lane = lax.broadcasted_iota(jnp.int32, cw_ref.shape, 1)
        w_row = jnp.sum(jnp.where(lane == j, cw_ref[...], 0.0), axis=1, keepdims=True)
        gate = (jnp.dot(x_lo, wg_scr[0:HALF_D, :], preferred_element_type=F32)
                + jnp.dot(x_hi, wg_scr[HALF_D:D_MODEL, :], preferred_element_type=F32))
        up = (jnp.dot(x_lo, wu_scr[0:HALF_D, :], preferred_element_type=F32)
              + jnp.dot(x_hi, wu_scr[HALF_D:D_MODEL, :], preferred_element_type=F32))
        h_ref[...] = (gate * jax.nn.sigmoid(gate) * up * w_row).astype(BF16)

    @pl.when(b >= nblk_ref[0])
    def _():
        h_ref[...] = jnp.zeros(h_ref.shape, BF16)


def _expert_up(l, blk_gid, nblk, x_sorted, cw_sorted, e_gate, e_up):
    wspec = pl.BlockSpec((None, None, D_MODEL, EXPERT_FF),
                         lambda j, b, gid, nb: (l, gid[b] * EXPERTS_PER_GROUP + j, 0, 0))
    grid_spec = pltpu.PrefetchScalarGridSpec(
        num_scalar_prefetch=2,
        grid=(EXPERTS_PER_GROUP, MOE_NBLK),
        in_specs=[
            pl.BlockSpec((MOE_BLK, HALF_D), lambda j, b, gid, nb: (b, 0)),
            pl.BlockSpec((MOE_BLK, LANES), lambda j, b, gid, nb: (b, 0)),
            wspec, wspec,
        ],
        out_specs=pl.BlockSpec((MOE_BLK, EXPERT_FF), lambda j, b, gid, nb: (b, j)),
        scratch_shapes=[pltpu.VMEM((D_MODEL, EXPERT_FF), BF16), pltpu.VMEM((D_MODEL, EXPERT_FF), BF16)],
    )
    return pl.pallas_call(
        _expert_up_kernel,
        grid_spec=grid_spec,
        out_shape=jax.ShapeDtypeStruct((MOE_SLOTS, GROUP_FF), BF16),
        compiler_params=_cparams(("arbitrary", "arbitrary")),
        name="expert_up",
    )(blk_gid, nblk, x_sorted, cw_sorted, e_gate, e_up)


DOWN_CH = 512


def _expert_down_kernel(gid_ref, nblk_ref, h_ref, wd_ref, y_ref, wd_scr):
    b = pl.program_id(0)
    prev = gid_ref[jnp.maximum(b - 1, 0)]
    fresh = (b == 0) | (gid_ref[b] != prev)

    @pl.when(fresh)
    def _():
        for c in range(GROUP_FF // DOWN_CH):
            rows = pl.ds(c * DOWN_CH, DOWN_CH)
            wd_scr[rows, :] = wd_ref[rows, :].astype(BF16)

    for c in range(D_MODEL // DOWN_CH):
        cols = pl.ds(c * DOWN_CH, DOWN_CH)
        y_ref[:, cols] = jnp.dot(h_ref[...], wd_scr[:, cols], preferred_element_type=F32)


def _expert_down(l, blk_gid, nblk, h_sorted, e_down_grouped):
    grid_spec = pltpu.PrefetchScalarGridSpec(
        num_scalar_prefetch=2,
        grid=(MOE_NBLK,),
        in_specs=[
            pl.BlockSpec((MOE_BLK, GROUP_FF), lambda b, gid, nb: (b, 0)),
            pl.BlockSpec((None, None, GROUP_FF, D_MODEL), lambda b, gid, nb: (l, gid[b], 0, 0)),
        ],
        out_specs=pl.BlockSpec((MOE_BLK, D_MODEL), lambda b, gid, nb: (b, 0)),
        scratch_shapes=[pltpu.VMEM((GROUP_FF, D_MODEL), BF16)],
    )
    return pl.pallas_call(
        _expert_down_kernel,
        grid_spec=grid_spec,
        out_shape=jax.ShapeDtypeStruct((MOE_SLOTS, D_MODEL), F32),
        compiler_params=_cparams(("arbitrary",)),
        name="expert_down",
    )(blk_gid, nblk, h_sorted, e_down_grouped)


def _moe(l, h2, gid, rank, counts, cw, e_gate, e_up, e_down_grouped):
    padded = (counts + MOE_BLK - 1) // MOE_BLK * MOE_BLK
    pend = jnp.cumsum(padded)
    pstart = pend - padded
    pos = (pstart[gid] + rank).astype(jnp.int32)
    slot_tok = jnp.full((MOE_SLOTS,), N_TOK, jnp.int32).at[pos].set(jnp.arange(N_TOK, dtype=jnp.int32))
    blk_start = jnp.arange(MOE_NBLK, dtype=jnp.int32) * MOE_BLK
    blk_gid = jnp.minimum(jnp.sum((blk_start[:, None] >= pend[None, :]).astype(jnp.int32), axis=1),
                          N_EXPERT_GROUPS - 1)
    nblk = (pend[-1:] // MOE_BLK).astype(jnp.int32)
    filled = (slot_tok < N_TOK)[:, None]
    src = jnp.minimum(slot_tok, N_TOK - 1)
    x_sorted = h2[src]
    cw_sorted = jnp.where(filled, cw[src], 0.0)
    hid = _expert_up(l, blk_gid, nblk, x_sorted, cw_sorted, e_gate, e_up)
    y_sorted = _expert_down(l, blk_gid, nblk, hid, e_down_grouped)
    return y_sorted[pos]


FIN_TM = 512
FIN_PROMPT_TILES = N_PROMPT // FIN_TM


def _final_kernel(x_ref, y_ref, g2_ref, fg_ref, op_ref, os_ref):
    i = pl.program_id(0)
    seg = i // (SEG_TOK // FIN_TM)
    x = x_ref[...] + g2_ref[pl.ds(seg, 1), :] * y_ref[...]
    ms = jnp.mean(x * x, axis=-1, keepdims=True)
    out = x * lax.rsqrt(ms + EPS) * fg_ref[...]

    @pl.when(i < FIN_PROMPT_TILES)
    def _():
        op_ref[...] = out

    @pl.when(i >= FIN_PROMPT_TILES)
    def _():
        os_ref[...] = out


def _final_norm(xmid, moe_y, mod, final_g):
    return pl.pallas_call(
        _final_kernel,
        grid=(N_TOK // FIN_TM,),
        in_specs=[
            pl.BlockSpec((FIN_TM, D_MODEL), lambda i: (i, 0)),
            pl.BlockSpec((FIN_TM, D_MODEL), lambda i: (i, 0)),
            _mod_spec(DEPTH - 1, 5, 1),
            pl.BlockSpec((1, D_MODEL), lambda i: (0, 0)),
        ],
        out_specs=[
            pl.BlockSpec((FIN_TM, D_MODEL), lambda i: (jnp.minimum(i, FIN_PROMPT_TILES - 1), 0)),
            pl.BlockSpec((FIN_TM, D_MODEL), lambda i: (jnp.maximum(i - FIN_PROMPT_TILES, 0), 0)),
        ],
        out_shape=[
            jax.ShapeDtypeStruct((N_PROMPT, D_MODEL), F32),
            jax.ShapeDtypeStruct((N_SAMPLE, D_MODEL), F32),
        ],
        compiler_params=_cparams(("arbitrary",)),
        name="final_norm",
    )(xmid, moe_y, mod, final_g.reshape(1, D_MODEL))


def _grid_pos_embed(rows):
    quarter = D_MODEL // 4
    freqs = 1.0 / (POS_BASE ** (jnp.arange(quarter, dtype=F32) / quarter))
    er = jnp.arange(rows, dtype=F32)[:, None] * freqs
    ec = jnp.arange(GRID_W, dtype=F32)[:, None] * freqs
    row_emb = jnp.concatenate([jnp.sin(er), jnp.cos(er)], axis=-1)
    col_emb = jnp.concatenate([jnp.sin(ec), jnp.cos(ec)], axis=-1)
    pe = jnp.concatenate([
        jnp.broadcast_to(row_emb[:, None, :], (rows, GRID_W, D_MODEL // 2)),
        jnp.broadcast_to(col_emb[None, :, :], (rows, GRID_W, D_MODEL // 2))], axis=-1)
    return pe.reshape(rows * GRID_W, D_MODEL)


def kernel(x_prompt, x_sample, state_ssm_re, state_ssm_im, c, c_ctx, norm1_g, norm2_g, w_mod, b_mod,
           w_in, ssm_lam_re, ssm_lam_im, ssm_log_step, ssm_b_re, ssm_b_im, ssm_c_re, ssm_c_im, ssm_d,
           w_glu, b_glu, gm_ln_g, gm_w_s, gm_b_s, w_pa, w_pb, w_o, w_router, b_router,
           e_gate, e_up, e_down, final_g):
    cvec = jnp.concatenate([c_ctx[None], c, jnp.zeros((MOD_ROWS - 1 - DEC_BATCH, D_MODEL), F32)], axis=0)
    mod = _modulation(cvec, w_mod, b_mod)

    perm = (jnp.arange(N_EXPERT_GROUPS)[None, :] * EXPERTS_PER_GROUP
            + jnp.arange(EXPERTS_PER_GROUP)[:, None]).reshape(N_EXPERTS)
    wr = w_router.astype(F32).T[perm]
    wr_hi = wr.astype(BF16)
    wr_lo = (wr - wr_hi.astype(F32)).astype(BF16)
    wr_t = jnp.concatenate([wr_hi, wr_lo], axis=0)
    br_col = b_router.astype(F32)[perm][:, None]

    w_in_b, w_glu_b, w_s_b = w_in.astype(BF16), w_glu.astype(BF16), gm_w_s.astype(BF16)
    w_pa_b, w_pb_b, w_o_b = w_pa.astype(BF16), w_pb.astype(BF16), w_o.astype(BF16)
    b_s_full = jnp.repeat(jnp.transpose(gm_b_s.astype(F32), (0, 2, 1)), GM_GROUP_DIM, axis=2)
    e_down_grouped = e_down.reshape(DEPTH, N_EXPERT_GROUPS, GROUP_FF, D_MODEL)
    w1, w2, a16 = jax.vmap(_s5_prep)(ssm_lam_re, ssm_lam_im, ssm_log_step, ssm_b_re, ssm_b_im,
                                     ssm_c_re, ssm_c_im)
    h0_lat = jnp.concatenate([state_ssm_re[:, :, 0], state_ssm_re[:, :, 1],
                              state_ssm_im[:, :, 0], state_ssm_im[:, :, 1]], axis=-1).astype(F32)
    h0 = jnp.concatenate([jnp.zeros((DEPTH, 1, SSM_GROUPS, STATE_W), F32),
                          jnp.transpose(h0_lat, (1, 0, 2, 3))], axis=1)

    inproj_in = (x_prompt.reshape(N_PROMPT, D_MODEL), x_sample.reshape(N_SAMPLE, D_MODEL),
                 _grid_pos_embed(DEC_SEQ // GRID_W))

    new_re, new_im = [], []
    xmid = moe_y = None
    for l in range(DEPTH):
        xres, u, z = _inproj(l, *inproj_in, mod, norm1_g, gm_ln_g, w_in_b)
        ys, fs = _s5_scan(l, u, w1, w2, a16, h0)
        fin = fs[0]
        p = SSM_STATE
        new_re.append(jnp.stack([fin[:, :, 0:p], fin[::-1, :, p:2 * p]], axis=1))
        new_im.append(jnp.stack([fin[:, :, 2 * p:3 * p], fin[::-1, :, 3 * p:4 * p]], axis=1))

        ya, yb = _mix(l, ys, u, z, ssm_d.astype(F32), w_glu_b, b_glu.astype(F32), w_s_b, b_s_full)
        xmid, h2, gid, rank, cw, cnt = _merge(l, ya, yb, z, xres, mod, norm2_g, w_pa_b, w_pb_b, w_o_b,
                                              wr_t, br_col)
        moe_y = _moe(l, h2, gid.reshape(N_TOK), rank.reshape(N_TOK), cnt[:, 0].astype(jnp.int32), cw,
                     e_gate, e_up, e_down_grouped)
        inproj_in = (xmid, moe_y, mod)

    y_prompt, y_sample = _final_norm(xmid, moe_y, mod, final_g)
    new_state_re = jnp.stack(new_re, axis=1).astype(x_prompt.dtype)
    new_state_im = jnp.stack(new_im, axis=1).astype(x_prompt.dtype)
    return (y_prompt.reshape(BATCH, SEQ, D_MODEL), y_sample.reshape(DEC_BATCH, DEC_SEQ, D_MODEL),
            new_state_re, new_state_im)
```

```python
import functools

import jax
import jax.numpy as jnp
from jax import lax
from jax.experimental import pallas as pl
from jax.experimental.pallas import tpu as pltpu

F32 = jnp.float32
BF16 = jnp.bfloat16
HIGHEST = lax.Precision.HIGHEST

D_MODEL = 2048
BATCH = 16
SEQ = 256
DEPTH = 2
DEC_BATCH = 2
DEC_SEQ = 4096
GRID_W = 64
POS_BASE = 10000.0
EPS = 1e-6
SSM_WIDTH = D_MODEL // 2
SSM_GROUP = 16
SSM_GROUPS = SSM_WIDTH // SSM_GROUP
SSM_STATE = 64
GM_WIDTH = D_MODEL // 2
GM_CHUNK = 128
GM_GROUPS = 8
GM_GROUP_DIM = GM_WIDTH // GM_GROUPS
IN_WIDTH = SSM_WIDTH + 2 * GM_WIDTH + 2 * D_MODEL
N_EXPERTS = 32
N_EXPERT_GROUPS = 8
EXPERTS_PER_GROUP = N_EXPERTS // N_EXPERT_GROUPS
EXPERT_FF = D_MODEL // 4
N_MOD = 6

N_PROMPT = BATCH * SEQ
N_SAMPLE = DEC_BATCH * DEC_SEQ
N_TOK = N_PROMPT + N_SAMPLE
SEG_TOK = 4096
MOD_ROWS = 8
LANES = 128
SUBLANES = 8

SCAN_T = 16
SCAN_W = SCAN_T * SSM_GROUP
SCAN_ROWS = N_TOK // SCAN_T
SCAN_BLK = 256
SCAN_TOK = SCAN_BLK * SCAN_T
N_SCAN_BLK = SCAN_ROWS // SCAN_BLK
G_OCT = SUBLANES
STATE_W = 4 * SSM_STATE
HALF_W = 2 * SSM_STATE

MOE_BLK = 512
MOE_SLOTS = N_TOK + N_EXPERT_GROUPS * MOE_BLK
MOE_NBLK = MOE_SLOTS // MOE_BLK
GROUP_FF = EXPERTS_PER_GROUP * EXPERT_FF

VMEM_LIMIT = 56 * 1024 * 1024


def _cparams(sem):
    return pltpu.CompilerParams(dimension_semantics=sem, vmem_limit_bytes=VMEM_LIMIT)


MOD_TN = 1024


def _mod_kernel(c_ref, w_ref, b_ref, o_ref):
    c = c_ref[...]
    s = c * jax.nn.sigmoid(c)
    s_hi = s.astype(BF16)
    s_lo = (s - s_hi.astype(F32)).astype(BF16)
    w = w_ref[...]
    w_hi = w.astype(BF16)
    w_lo = (w - w_hi.astype(F32)).astype(BF16)
    both = jnp.dot(jnp.concatenate([s_hi, s_lo], axis=0), w_hi, preferred_element_type=F32)
    cross = jnp.dot(s_hi, w_lo, preferred_element_type=F32)
    o_ref[...] = both[0:MOD_ROWS] + both[MOD_ROWS:2 * MOD_ROWS] + cross + b_ref[...]


def _modulation(cvec, w_mod, b_mod):
    width = N_MOD * D_MODEL
    return pl.pallas_call(
        _mod_kernel,
        grid=(DEPTH, width // MOD_TN),
        in_specs=[
            pl.BlockSpec((MOD_ROWS, D_MODEL), lambda l, n: (0, 0)),
            pl.BlockSpec((None, D_MODEL, MOD_TN), lambda l, n: (l, 0, n)),
            pl.BlockSpec((None, 1, MOD_TN), lambda l, n: (l, 0, n)),
        ],
        out_specs=pl.BlockSpec((None, MOD_ROWS, MOD_TN), lambda l, n: (l, 0, n)),
        out_shape=jax.ShapeDtypeStruct((DEPTH, MOD_ROWS, width), F32),
        compiler_params=_cparams(("arbitrary", "arbitrary")),
        name="adaln_mod",
    )(cvec, w_mod, b_mod.reshape(DEPTH, 1, width))


def _mod_spec(l, k, nargs):
    if nargs == 1:
        return pl.BlockSpec((None, MOD_ROWS, D_MODEL), lambda i: (l, 0, k))
    return pl.BlockSpec((None, MOD_ROWS, D_MODEL), lambda i, j: (l, 0, k))


INP_TM = 256
INP_CH = 256
INP_PROMPT_TILES = N_PROMPT // INP_TM
Z_WIDTH = IN_WIDTH - SSM_WIDTH
INP_VMEM_LIMIT = 60 * 1024 * 1024


def _inproj_kernel(*refs, first):
    if first:
        xa_ref, xb_ref, add_ref = refs[:3]
    else:
        xa_ref, add_ref, gain_ref = refs[:3]
    sc_ref, sh_ref, g_ref, ln_ref, w_ref, xres_ref, u_ref, z_ref, h_scr, v_scr = refs[3:]
    i = pl.program_id(0)
    seg = i // (SEG_TOK // INP_TM)

    if first:
        latent = i >= INP_PROMPT_TILES
        x = jnp.where(latent, xb_ref[...] + add_ref[...], xa_ref[...])
    else:
        x = xa_ref[...] + gain_ref[pl.ds(seg, 1), :] * add_ref[...]
    xres_ref[...] = x
    ms = jnp.mean(x * x, axis=-1, keepdims=True)
    y = x * lax.rsqrt(ms + EPS) * g_ref[...]
    h = y * (1.0 + sc_ref[pl.ds(seg, 1), :]) + sh_ref[pl.ds(seg, 1), :]
    h_scr[...] = h.astype(BF16)

    def proj(col):
        return jnp.dot(h_scr[...], w_ref[:, pl.ds(col, INP_CH)], preferred_element_type=F32)

    n_ch = SSM_WIDTH // INP_CH
    for c in range(n_ch):
        u_ref[:, pl.ds(c * INP_CH, INP_CH)] = proj(c * INP_CH)
    for c in range(n_ch):
        z_ref[:, pl.ds(c * INP_CH, INP_CH)] = jax.nn.gelu(proj(SSM_WIDTH + c * INP_CH)).astype(BF16)
    row_sum = jnp.zeros((INP_TM, 1), F32)
    for c in range(n_ch):
        v = jax.nn.gelu(proj(SSM_WIDTH + GM_WIDTH + c * INP_CH))
        v_scr[:, pl.ds(c * INP_CH, INP_CH)] = v
        row_sum = row_sum + jnp.sum(v, axis=-1, keepdims=True)
    mu = row_sum * (1.0 / GM_WIDTH)
    dev = v_scr[...] - mu
    var = jnp.mean(jnp.square(dev), axis=-1, keepdims=True)
    z_ref[:, pl.ds(GM_WIDTH, GM_WIDTH)] = (dev * lax.rsqrt(var + EPS) * ln_ref[...]).astype(BF16)
    gates = SSM_WIDTH + 2 * GM_WIDTH
    for c in range(2 * D_MODEL // INP_CH):
        z_ref[:, pl.ds(2 * GM_WIDTH + c * INP_CH, INP_CH)] = jax.nn.sigmoid(
            proj(gates + c * INP_CH)).astype(BF16)


def _inproj(l, xa, xb_or_add, add_or_gain, mod, norm1_g, gm_ln_g, w_in_bf16):
    first = l == 0
    row_tile = lambda m: pl.BlockSpec((INP_TM, D_MODEL), m)
    if first:
        lead = [row_tile(lambda i: (jnp.minimum(i, INP_PROMPT_TILES - 1), 0)),
                row_tile(lambda i: (jnp.maximum(i - INP_PROMPT_TILES, 0), 0)),
                row_tile(lambda i: (i % (DEC_SEQ // INP_TM), 0))]
    else:
        lead = [row_tile(lambda i: (i, 0)), row_tile(lambda i: (i, 0)), _mod_spec(l - 1, 5, 1)]
    vec = lambda w: pl.BlockSpec((None, 1, w), lambda i: (l, 0, 0))
    return pl.pallas_call(
        functools.partial(_inproj_kernel, first=first),
        grid=(N_TOK // INP_TM,),
        in_specs=lead + [
            _mod_spec(l, 1, 1), _mod_spec(l, 0, 1),
            vec(D_MODEL), vec(GM_WIDTH),
            pl.BlockSpec((None, D_MODEL, IN_WIDTH), lambda i: (l, 0, 0), pipeline_mode=pl.Buffered(1)),
        ],
        out_specs=[
            pl.BlockSpec((INP_TM, D_MODEL), lambda i: (i, 0)),
            pl.BlockSpec((INP_TM, SSM_WIDTH), lambda i: (i, 0)),
            pl.BlockSpec((INP_TM, Z_WIDTH), lambda i: (i, 0)),
        ],
        out_shape=[
            jax.ShapeDtypeStruct((N_TOK, D_MODEL), F32),
            jax.ShapeDtypeStruct((N_TOK, SSM_WIDTH), F32),
            jax.ShapeDtypeStruct((N_TOK, Z_WIDTH), BF16),
        ],
        scratch_shapes=[pltpu.VMEM((INP_TM, D_MODEL), BF16), pltpu.VMEM((INP_TM, GM_WIDTH), F32)],
        compiler_params=pltpu.CompilerParams(dimension_semantics=("arbitrary",),
                                             vmem_limit_bytes=INP_VMEM_LIMIT),
        name="in_proj",
    )(xa, xb_or_add, add_or_gain, mod, mod, norm1_g.reshape(DEPTH, 1, D_MODEL),
      gm_ln_g.reshape(DEPTH, 1, GM_WIDTH), w_in_bf16)


PK_BRE, PK_BIM, PK_CRE, PK_CIM = 0, 16, 32, 48
PK_LR, PK_LI = 64, 65


def _split_bf16(x):
    hi = x.astype(BF16)
    return hi, (x - hi.astype(F32)).astype(BF16)


POW_ROWS = 24


def _prep_kernel(pk_ref, row_ref, pow_ref, tile_ref, w1_ref, w2_ref, a_ref):
    p = SSM_STATE
    k_sub = jnp.minimum(lax.broadcasted_iota(jnp.int32, (POW_ROWS, LANES), 0), SCAN_T).astype(F32)
    lane = lax.broadcasted_iota(jnp.int32, (p, LANES), 1)
    col = lax.broadcasted_iota(jnp.int32, (SSM_GROUP, SCAN_W), 1)

    def spread_pow(x_r, x_i, which):
        parts = jnp.concatenate(_split_bf16(x_r) + _split_bf16(x_i), axis=0)
        out = jnp.dot(parts, pow_ref[which], preferred_element_type=F32)
        return out[0:p] + out[p:2 * p], out[2 * p:3 * p] + out[3 * p:4 * p]

    def spread_tiles(x):
        out = jnp.dot(jnp.concatenate(_split_bf16(x), axis=0), tile_ref[...], preferred_element_type=F32)
        out = out[0:p] + out[p:2 * p]
        return [out[:, n * SCAN_W:(n + 1) * SCAN_W] for n in range(4)]

    def group(g, _):
        rows = row_ref[g]
        grow_r = rows[0:1] * rows[2:3]
        grow_i = rows[1:2] * rows[2:3]
        mag = jnp.exp(grow_r * k_sub)
        ang = grow_i * k_sub
        unused = jnp.zeros((LANES - POW_ROWS, LANES), F32)
        pw_t_r = jnp.concatenate([mag * jnp.cos(ang), unused], axis=0).T
        pw_t_i = jnp.concatenate([mag * jnp.sin(ang), unused], axis=0).T
        per_dir = []
        for d in range(2):
            pk = pk_ref[d, g]
            lr = pk[:, PK_LR:PK_LR + 1]
            li = pk[:, PK_LI:PK_LI + 1]
            p_r = pw_t_r[d * p:(d + 1) * p]
            p_i = pw_t_i[d * p:(d + 1) * p]
            a_r = p_r[:, 1:2]
            a_i = p_i[:, 1:2]
            den = lr * lr + li * li
            q_r = ((a_r - 1.0) * lr + a_i * li) / den
            q_i = (a_i * lr - (a_r - 1.0) * li) / den
            per_dir.append((pk, p_r, p_i, q_r, q_i))

        w1_rows, lag, carry = [], [], []
        for d in range(2):
            pk, p_r, p_i, q_r, q_i = per_dir[d]
            b_r, b_i, c_r, c_i = spread_tiles(pk)
            bb_r = q_r * b_r - q_i * b_i
            bb_i = q_r * b_i + q_i * b_r
            pw_r, pw_i = spread_pow(p_r, p_i, 1 if d == 0 else 0)
            w1_rows.append((pw_r * bb_r - pw_i * bb_i, pw_r * bb_i + pw_i * bb_r))
            pl_r, pl_i = spread_pow(p_r, p_i, 0 if d == 0 else 1)
            cl_r = c_r * pl_r - c_i * pl_i
            cl_i = c_r * pl_i + c_i * pl_r
            pk_im = pltpu.roll(pk, LANES - (PK_BIM - PK_BRE), 1)
            bt_r = (q_r * pk - q_i * pk_im).T[0:SSM_GROUP, :]
            bt_i = (q_r * pk_im + q_i * pk).T[0:SSM_GROUP, :]
            lag.append(jnp.dot(bt_r, cl_r, precision=HIGHEST, preferred_element_type=F32)
                       - jnp.dot(bt_i, cl_i, precision=HIGHEST, preferred_element_type=F32))
            pc_r, pc_i = spread_pow(p_r, p_i, 2 if d == 0 else 3)
            carry.append((c_r * pc_r - c_i * pc_i, -(c_r * pc_i + c_i * pc_r)))

        (f_re, f_im), (b_re, b_im) = w1_rows
        w1_ref[g] = jnp.concatenate([f_re, b_re, f_im, b_im], axis=0).T.astype(BF16)

        for s in range(SCAN_T):
            fwd = lag[0] if s == 0 else pltpu.roll(lag[0], SSM_GROUP * s, 1)
            fwd = jnp.where(col >= SSM_GROUP * s, fwd, 0.0)
            shift_b = SSM_GROUP * (SCAN_T - 1 - s)
            bwd = lag[1] if shift_b == 0 else pltpu.roll(lag[1], SCAN_W - shift_b, 1)
            bwd = jnp.where(col < SSM_GROUP * (s + 1), bwd, 0.0)
            w2_ref[g, pl.ds(SSM_GROUP * s, SSM_GROUP), :] = (fwd + bwd).astype(BF16)
        (x_re, x_im), (y_re, y_im) = carry
        for n, rows in enumerate((x_re, y_re, x_im, y_im)):
            w2_ref[g, pl.ds(SCAN_W + SSM_STATE * n, SSM_STATE), :] = rows.astype(BF16)

        cols = [per_dir[0][1], per_dir[1][1], per_dir[0][2], per_dir[1][2]]
        a_cols = jnp.zeros((SSM_STATE, LANES), F32)
        for n, c in enumerate(cols):
            a_cols = jnp.where(lane == n, c[:, SCAN_T:SCAN_T + 1], a_cols)
        a_ref[g] = a_cols
        return 0

    lax.fori_loop(0, G_OCT, group, 0, unroll=2)


def _s5_prep(lam_re, lam_im, log_step, b_re, b_im, c_re, c_im):
    shape = (DEPTH, 2, SSM_GROUPS, SSM_STATE)
    lr = lam_re.astype(F32)
    li = lam_im.astype(F32)
    dt = jnp.broadcast_to(jnp.exp(log_step.astype(F32))[..., None], shape)
    pk = jnp.concatenate([
        b_re.astype(F32), b_im.astype(F32),
        jnp.swapaxes(c_re.astype(F32), -1, -2), jnp.swapaxes(c_im.astype(F32), -1, -2),
        lr[..., None], li[..., None],
        jnp.zeros(shape + (LANES - PK_LI - 1,), F32)], axis=-1)
    both_dirs = lambda a: jnp.concatenate([a[:, 0], a[:, 1]], axis=-1)
    rows = jnp.stack([both_dirs(lr), both_dirs(li), both_dirs(dt)], axis=2)
    rows = jnp.concatenate([rows, jnp.zeros((DEPTH, SSM_GROUPS, SUBLANES - 3, LANES), F32)], axis=2)

    blk = jnp.arange(SCAN_W) // SSM_GROUP
    k = jnp.arange(LANES)[:, None]
    pows = [k == blk[None, :], k == (SCAN_T - 1 - blk)[None, :], k == (blk + 1)[None, :],
            k == (SCAN_T - blk)[None, :]]
    h = (jnp.arange(SCAN_W) % SSM_GROUP)[None, :]
    sel_pow = jnp.stack(pows).astype(BF16)
    sel_tile = jnp.concatenate([k == h + off for off in (PK_BRE, PK_BIM, PK_CRE, PK_CIM)],
                               axis=1).astype(BF16)

    n_oct = SSM_GROUPS // G_OCT
    w1, w2, a_cols = pl.pallas_call(
        _prep_kernel,
        grid=(DEPTH, n_oct),
        in_specs=[
            pl.BlockSpec((None, 2, G_OCT, SSM_STATE, LANES), lambda l, o: (l, 0, o, 0, 0)),
            pl.BlockSpec((None, G_OCT, SUBLANES, LANES), lambda l, o: (l, o, 0, 0)),
            pl.BlockSpec((4, LANES, SCAN_W), lambda l, o: (0, 0, 0)),
            pl.BlockSpec((LANES, 4 * SCAN_W), lambda l, o: (0, 0)),
        ],
        out_specs=[
            pl.BlockSpec((None, G_OCT, SCAN_W, STATE_W), lambda l, o: (l, o, 0, 0)),
            pl.BlockSpec((None, G_OCT, SCAN_W + STATE_W, SCAN_W), lambda l, o: (l, o, 0, 0)),
            pl.BlockSpec((None, G_OCT, SSM_STATE, LANES), lambda l, o: (l, o, 0, 0)),
        ],
        out_shape=[
            jax.ShapeDtypeStruct((DEPTH, SSM_GROUPS, SCAN_W, STATE_W), BF16),
            jax.ShapeDtypeStruct((DEPTH, SSM_GROUPS, SCAN_W + STATE_W, SCAN_W), BF16),
            jax.ShapeDtypeStruct((DEPTH, SSM_GROUPS, SSM_STATE, LANES), F32),
        ],
        compiler_params=_cparams(("arbitrary", "arbitrary")),
        name="s5_prep",
    )(pk, rows, sel_pow, sel_tile)
    a16 = jnp.swapaxes(a_cols[..., 0:4], -1, -2).reshape(DEPTH, SSM_GROUPS, STATE_W)
    return w1, w2, a16


def _s5_kernel(u_ref, w1_ref, w2_ref, a_ref, h0_ref, y_ref, fs_ref,
               t_scr, ug_scr, vr_scr, vi_scr, cr_scr, ci_scr, fr_scr, fi_scr):
    blk = pl.program_id(1)
    seq_rows = jnp.where(blk == 0, SEQ // SCAN_T, DEC_SEQ // SCAN_T)

    for s in range(SCAN_T):
        t_scr[s] = u_ref[pl.ds(s, SCAN_BLK, stride=SCAN_T), :].T
    for g in range(G_OCT):
        stacked = t_scr[:, pl.ds(g * SSM_GROUP, SSM_GROUP), :].reshape(SCAN_W, SCAN_BLK)
        ug_scr[g] = stacked.T.astype(BF16)

    for g in range(G_OCT):
        v = jnp.dot(ug_scr[g], w1_ref[g], preferred_element_type=F32)
        vr_scr[pl.ds(g, SCAN_BLK, stride=G_OCT), :] = v[:, 0:HALF_W]
        vi_scr[pl.ds(g, SCAN_BLK, stride=G_OCT), :] = v[:, HALF_W:STATE_W]

    a_r = a_ref[:, 0:HALF_W]
    a_i = a_ref[:, HALF_W:STATE_W]
    h0_r = h0_ref[:, 0:HALF_W]
    h0_i = h0_ref[:, HALF_W:STATE_W]
    fwd_lanes = lax.broadcasted_iota(jnp.int32, (G_OCT, HALF_W), 1) < SSM_STATE
    bwd_lanes = jnp.logical_not(fwd_lanes)

    def step(k, carry):
        s_r, s_i = carry
        rf = pl.ds(pl.multiple_of(k * G_OCT, G_OCT), G_OCT)
        rb = pl.ds(pl.multiple_of((SCAN_BLK - 1 - k) * G_OCT, G_OCT), G_OCT)
        restart = (k & (seq_rows - 1)) == 0
        s_r = jnp.where(restart, h0_r, s_r)
        s_i = jnp.where(restart, h0_i, s_i)
        pltpu.store(cr_scr.at[rf, :], s_r, mask=fwd_lanes)
        pltpu.store(cr_scr.at[rb, :], s_r, mask=bwd_lanes)
        pltpu.store(ci_scr.at[rf, :], s_i, mask=fwd_lanes)
        pltpu.store(ci_scr.at[rb, :], s_i, mask=bwd_lanes)
        v_r = jnp.where(fwd_lanes, vr_scr[rf, :], vr_scr[rb, :])
        v_i = jnp.where(fwd_lanes, vi_scr[rf, :], vi_scr[rb, :])
        n_r = a_r * s_r - a_i * s_i + v_r
        n_i = a_r * s_i + a_i * s_r + v_i
        fr_scr[rf, :] = n_r
        fi_scr[rf, :] = n_i
        return n_r, n_i

    zero = jnp.zeros((G_OCT, HALF_W), F32)
    lax.fori_loop(0, SCAN_BLK, step, (zero, zero), unroll=4)

    for g in range(G_OCT):
        c_r = cr_scr[pl.ds(g, SCAN_BLK, stride=G_OCT), :].astype(BF16)
        c_i = ci_scr[pl.ds(g, SCAN_BLK, stride=G_OCT), :].astype(BF16)
        y = jnp.dot(ug_scr[g], w2_ref[g, 0:SCAN_W, :], preferred_element_type=F32)
        y = y + jnp.dot(c_r, w2_ref[g, SCAN_W:SCAN_W + HALF_W, :], preferred_element_type=F32)
        y = y + jnp.dot(c_i, w2_ref[g, SCAN_W + HALF_W:SCAN_W + STATE_W, :], preferred_element_type=F32)
        t_scr[:, pl.ds(g * SSM_GROUP, SSM_GROUP), :] = y.T.reshape(SCAN_T, SSM_GROUP, SCAN_BLK)
    for s in range(SCAN_T):
        y_ref[pl.ds(s, SCAN_BLK, stride=SCAN_T), :] = t_scr[s].T

    rows_per_seq = SEQ // SCAN_T
    for q in range(SCAN_BLK // rows_per_seq):
        last = pl.ds((q * rows_per_seq + rows_per_seq - 1) * G_OCT, G_OCT)
        fs_ref[q, :, 0:HALF_W] = fr_scr[last, :]
        fs_ref[q, :, HALF_W:STATE_W] = fi_scr[last, :]


def _s5_scan(l, u, w1, w2, a16, h0):
    n_oct = SSM_GROUPS // G_OCT
    n_fin = SCAN_BLK // (SEQ // SCAN_T)
    return pl.pallas_call(
        _s5_kernel,
        grid=(n_oct, N_SCAN_BLK),
        in_specs=[
            pl.BlockSpec((SCAN_TOK, LANES), lambda o, b: (b, o)),
            pl.BlockSpec((None, G_OCT, SCAN_W, STATE_W), lambda o, b: (l, o, 0, 0)),
            pl.BlockSpec((None, G_OCT, SCAN_W + STATE_W, SCAN_W), lambda o, b: (l, o, 0, 0)),
            pl.BlockSpec((None, G_OCT, STATE_W), lambda o, b: (l, o, 0)),
            pl.BlockSpec((None, None, G_OCT, STATE_W), lambda o, b: (l, b, o, 0)),
        ],
        out_specs=[
            pl.BlockSpec((SCAN_TOK, LANES), lambda o, b: (b, o)),
            pl.BlockSpec((None, n_fin, G_OCT, STATE_W), lambda o, b: (b, 0, o, 0)),
        ],
        out_shape=[
            jax.ShapeDtypeStruct((N_TOK, SSM_WIDTH), F32),
            jax.ShapeDtypeStruct((N_SCAN_BLK, n_fin, SSM_GROUPS, STATE_W), F32),
        ],
        scratch_shapes=[
            pltpu.VMEM((SCAN_T, LANES, SCAN_BLK), F32),
            pltpu.VMEM((G_OCT, SCAN_BLK, SCAN_W), BF16),
        ] + [pltpu.VMEM((SCAN_BLK * G_OCT, HALF_W), F32) for _ in range(6)],
        compiler_params=_cparams(("arbitrary", "arbitrary")),
        name="s5_scan",
    )(u, w1, w2, a16, h0)


MIX_TM = 512


def _mix_kernel(ys_ref, u_ref, gu_ref, vn_ref, d_ref, wglu_ref, bglu_ref, ws_ref, bs_ref,
                ya_ref, yb_ref):
    y = ys_ref[...] + d_ref[...] * u_ref[...]
    y = jax.nn.gelu(y)
    gate = jnp.dot(y.astype(BF16), wglu_ref[...], preferred_element_type=F32) + bglu_ref[...]
    ya_ref[...] = (y * jax.nn.sigmoid(gate)).astype(BF16)
    for c in range(MIX_TM // GM_CHUNK):
        rows = pl.ds(c * GM_CHUNK, GM_CHUNK)
        for g in range(GM_GROUPS):
            cols = pl.ds(g * GM_GROUP_DIM, GM_GROUP_DIM)
            mixed = jnp.dot(ws_ref[g], vn_ref[rows, cols], preferred_element_type=F32) + bs_ref[:, cols]
            yb_ref[rows, cols] = (gu_ref[rows, cols].astype(F32) * mixed).astype(BF16)


def _mix(l, ys, u, z, d_skip, w_glu_bf16, b_glu, w_s_bf16, b_s_full):
    tile = lambda k: pl.BlockSpec((MIX_TM, SSM_WIDTH), lambda i: (i, k))
    lay = lambda *shape: pl.BlockSpec((None,) + shape, lambda i: (l,) + tuple(0 for _ in shape))
    return pl.pallas_call(
        _mix_kernel,
        grid=(N_TOK // MIX_TM,),
        in_specs=[
            tile(0), tile(0), tile(0), tile(1),
            lay(1, SSM_WIDTH),
            lay(SSM_WIDTH, SSM_WIDTH),
            lay(1, SSM_WIDTH),
            lay(GM_GROUPS, GM_CHUNK, GM_CHUNK),
            lay(GM_CHUNK, GM_WIDTH),
        ],
        out_specs=[tile(0), tile(0)],
        out_shape=[
            jax.ShapeDtypeStruct((N_TOK, SSM_WIDTH), BF16),
            jax.ShapeDtypeStruct((N_TOK, GM_WIDTH), BF16),
        ],
        compiler_params=_cparams(("arbitrary",)),
        name="mixers",
    )(ys, u, z, z, d_skip.reshape(DEPTH, 1, SSM_WIDTH), w_glu_bf16, b_glu.reshape(DEPTH, 1, SSM_WIDTH),
      w_s_bf16, b_s_full)


MRG_TM = 512
MRG_SUB = 256
HALF_D = D_MODEL // 2
REC_W = HALF_D + LANES


def _merge_kernel(ya_ref, yb_ref, ga0_ref, ga1_ref, gb0_ref, gb1_ref, x_ref, g1_ref, sc_ref, sh_ref,
                  n2_ref, wpa_ref, wpb_ref, wo_ref, wr_ref, br_ref, tri_ref,
                  xmid_ref, rec_ref, gid_ref, rank_ref, cnt_ref, cnt_scr):
    i = pl.program_id(0)
    seg = i // (SEG_TOK // MRG_TM)

    @pl.when(i == 0)
    def _():
        cnt_scr[...] = jnp.zeros(cnt_scr.shape, F32)

    for sub in range(MRG_TM // MRG_SUB):
        _merge_rows(pl.ds(sub * MRG_SUB, MRG_SUB), seg,
                    ya_ref, yb_ref, ga0_ref, ga1_ref, gb0_ref, gb1_ref, x_ref, g1_ref, sc_ref, sh_ref,
                    n2_ref, wpa_ref, wpb_ref, wo_ref, wr_ref, br_ref, tri_ref,
                    xmid_ref, rec_ref, gid_ref, rank_ref, cnt_scr)
    cnt_ref[...] = jnp.broadcast_to(cnt_scr[...], cnt_ref.shape)


def _merge_rows(rows, seg, ya_ref, yb_ref, ga0_ref, ga1_ref, gb0_ref, gb1_ref, x_ref, g1_ref, sc_ref, sh_ref,
                n2_ref, wpa_ref, wpb_ref, wo_ref, wr_ref, br_ref, tri_ref,
                xmid_ref, rec_ref, gid_ref, rank_ref, cnt_scr):
    pa = jnp.dot(ya_ref[rows, :], wpa_ref[...], preferred_element_type=F32)
    pb = jnp.dot(yb_ref[rows, :], wpb_ref[...], preferred_element_type=F32)
    m_lo = ga0_ref[rows, :].astype(F32) * pa[:, :HALF_D] + gb0_ref[rows, :].astype(F32) * pb[:, :HALF_D]
    m_hi = ga1_ref[rows, :].astype(F32) * pa[:, HALF_D:] + gb1_ref[rows, :].astype(F32) * pb[:, HALF_D:]
    mix = jnp.dot(m_lo.astype(BF16), wo_ref[0:HALF_D, :], preferred_element_type=F32)
    mix = mix + jnp.dot(m_hi.astype(BF16), wo_ref[HALF_D:D_MODEL, :], preferred_element_type=F32)
    x = x_ref[rows, :] + g1_ref[pl.ds(seg, 1), :] * mix
    xmid_ref[rows, :] = x

    ms = jnp.mean(x * x, axis=-1, keepdims=True)
    y = x * lax.rsqrt(ms + EPS) * n2_ref[...]
    h2 = y * (1.0 + sc_ref[pl.ds(seg, 1), :]) + sh_ref[pl.ds(seg, 1), :]
    hi = h2.astype(BF16)
    hi_f = hi.astype(F32)
    lo = (h2 - hi_f).astype(BF16)
    bits = lax.bitcast_convert_type(hi_f, jnp.uint32)
    rec_ref[rows, 0:HALF_D] = bits[:, :HALF_D] | (bits[:, HALF_D:] >> 16)

    nt = (((1,), (1,)), ((), ()))
    lt = (lax.dot_general(wr_ref[...], hi, nt, preferred_element_type=F32)
          + lax.dot_general(wr_ref[...], lo, nt, preferred_element_type=F32))
    logits = lt[0:N_EXPERTS] + lt[N_EXPERTS:2 * N_EXPERTS]
    scores = jax.nn.sigmoid(logits)
    sel = scores + br_ref[...]
    ng = N_EXPERT_GROUPS
    s = [sel[j * ng:(j + 1) * ng] for j in range(EXPERTS_PER_GROUP)]
    p = [scores[j * ng:(j + 1) * ng] for j in range(EXPERTS_PER_GROUP)]
    a, b = jnp.maximum(s[0], s[1]), jnp.minimum(s[0], s[1])
    c, d = jnp.maximum(s[2], s[3]), jnp.minimum(s[2], s[3])
    grp_score = jnp.maximum(a, c) + jnp.maximum(jnp.minimum(a, c), jnp.maximum(b, d))
    best = jnp.max(grp_score, axis=0, keepdims=True)
    g_iota = lax.broadcasted_iota(jnp.int32, grp_score.shape, 0)
    g_idx = jnp.min(jnp.where(grp_score == best, g_iota, ng), axis=0, keepdims=True)
    onehot = g_iota == g_idx
    v = [jnp.sum(jnp.where(onehot, sj, 0.0), axis=0, keepdims=True) for sj in s]
    q = [jnp.sum(jnp.where(onehot, pj, 0.0), axis=0, keepdims=True) for pj in p]
    picked = []
    for j in range(EXPERTS_PER_GROUP):
        rank = jnp.zeros(v[j].shape, jnp.int32)
        for o in range(EXPERTS_PER_GROUP):
            if o == j:
                continue
            ahead = (v[o] > v[j]) | ((v[o] == v[j]) & (o < j))
            rank = rank + ahead.astype(jnp.int32)
        picked.append(jnp.where(rank < 2, q[j], 0.0))
    total = picked[0] + picked[1] + picked[2] + picked[3]
    gid_ref[:, rows] = g_idx
    cw_rows = jnp.concatenate([pj / total for pj in picked]
                              + [jnp.zeros((LANES - EXPERTS_PER_GROUP, MRG_SUB), F32)], axis=0)
    rec_ref[rows, HALF_D:REC_W] = lax.bitcast_convert_type(cw_rows.T, jnp.uint32)

    hot = onehot.astype(BF16)
    within = jnp.dot(hot, tri_ref[...], preferred_element_type=F32)
    before = jnp.sum(jnp.where(onehot, within + cnt_scr[...], 0.0), axis=0, keepdims=True) - 1.0
    rank_ref[:, rows] = before.astype(jnp.int32)
    cnt_scr[...] = cnt_scr[...] + within[:, MRG_SUB - 1:MRG_SUB]


def _merge(l, ya, yb, z, xres, mod, norm2_g, w_pa, w_pb, w_o, wr_t, br_col):
    n_t = N_TOK // MRG_TM
    tri = (jnp.arange(MRG_SUB)[:, None] <= jnp.arange(MRG_SUB)[None, :]).astype(BF16)
    zspec = lambda k: pl.BlockSpec((MRG_TM, HALF_D), lambda i: (i, k))
    once = pl.Buffered(1)
    lay = lambda *shape: pl.BlockSpec((None,) + shape, lambda i: (l,) + tuple(0 for _ in shape),
                                      pipeline_mode=once)
    const = lambda *shape: pl.BlockSpec(shape, lambda i: tuple(0 for _ in shape), pipeline_mode=once)
    return pl.pallas_call(
        _merge_kernel,
        grid=(n_t,),
        in_specs=[
            pl.BlockSpec((MRG_TM, SSM_WIDTH), lambda i: (i, 0)),
            pl.BlockSpec((MRG_TM, GM_WIDTH), lambda i: (i, 0)),
            zspec(2), zspec(3), zspec(4), zspec(5),
            pl.BlockSpec((MRG_TM, D_MODEL), lambda i: (i, 0)),
            _mod_spec(l, 2, 1), _mod_spec(l, 4, 1), _mod_spec(l, 3, 1),
            lay(1, D_MODEL),
            lay(SSM_WIDTH, D_MODEL),
            lay(GM_WIDTH, D_MODEL),
            lay(D_MODEL, D_MODEL),
            const(2 * N_EXPERTS, D_MODEL),
            const(N_EXPERTS, 1),
            const(MRG_SUB, MRG_SUB),
        ],
        out_specs=[
            pl.BlockSpec((MRG_TM, D_MODEL), lambda i: (i, 0)),
            pl.BlockSpec((MRG_TM, REC_W), lambda i: (i, 0)),
            pl.BlockSpec((None, 1, MRG_TM), lambda i: (i, 0, 0)),
            pl.BlockSpec((None, 1, MRG_TM), lambda i: (i, 0, 0)),
            pl.BlockSpec((N_EXPERT_GROUPS, LANES), lambda i: (0, 0)),
        ],
        out_shape=[
            jax.ShapeDtypeStruct((N_TOK, D_MODEL), F32),
            jax.ShapeDtypeStruct((N_TOK, REC_W), jnp.uint32),
            jax.ShapeDtypeStruct((n_t, 1, MRG_TM), jnp.int32),
            jax.ShapeDtypeStruct((n_t, 1, MRG_TM), jnp.int32),
            jax.ShapeDtypeStruct((N_EXPERT_GROUPS, LANES), F32),
        ],
        scratch_shapes=[pltpu.VMEM((N_EXPERT_GROUPS, 1), F32)],
        compiler_params=_cparams(("arbitrary",)),
        name="merge_router",
    )(ya, yb, z, z, z, z, xres, mod, mod, mod, norm2_g.reshape(DEPTH, 1, D_MODEL),
      w_pa, w_pb, w_o, wr_t, br_col, tri)


DSP_TM = 256


def _dispatch_kernel(pos_ref, rec_ref, init_ref, out_ref, sem):
    del init_ref
    base = pl.program_id(0) * DSP_TM

    def issue(r, carry):
        pltpu.make_async_copy(rec_ref.at[pl.ds(r, 1)], out_ref.at[pl.ds(pos_ref[base + r], 1)], sem).start()
        return carry

    lax.fori_loop(0, DSP_TM, issue, 0, unroll=8)
    pltpu.make_async_copy(rec_ref, out_ref.at[pl.ds(0, DSP_TM)], sem).wait()


def _dispatch(pos, rec):
    grid_spec = pltpu.PrefetchScalarGridSpec(
        num_scalar_prefetch=1,
        grid=(N_TOK // DSP_TM,),
        in_specs=[
            pl.BlockSpec((DSP_TM, REC_W), lambda i, pos: (i, 0)),
            pl.BlockSpec(memory_space=pl.ANY),
        ],
        out_specs=pl.BlockSpec(memory_space=pl.ANY),
        scratch_shapes=[pltpu.SemaphoreType.DMA(())],
    )
    return pl.pallas_call(
        _dispatch_kernel,
        grid_spec=grid_spec,
        out_shape=jax.ShapeDtypeStruct((MOE_SLOTS, REC_W), jnp.uint32),
        input_output_aliases={2: 0},
        compiler_params=_cparams(("arbitrary",)),
        name="moe_dispatch",
    )(pos, rec, jnp.zeros((MOE_SLOTS, REC_W), jnp.uint32))


def _expert_up_kernel(gid_ref, nblk_ref, rec_ref, wg_ref, wu_ref, h_ref, wg_scr, wu_scr):
    j = pl.program_id(0)
    b = pl.program_id(1)
    prev = gid_ref[jnp.maximum(b - 1, 0)]
    fresh = (b == 0) | (gid_ref[b] != prev)

    @pl.when(fresh)
    def _():
        wg_scr[...] = wg_ref[...].astype(BF16)
        wu_scr[...] = wu_ref[...].astype(BF16)

    @pl.when(b < nblk_ref[0])
    def _():
        packed = rec_ref[:, 0:HALF_D]
        x_lo = lax.bitcast_convert_type(packed & jnp.uint32(0xFFFF0000), F32).astype(BF16)
        x_hi = lax.bitcast_convert_type(packed << 16, F32).astype(BF16)
        cw = lax.bitcast_convert_type(rec_ref[:, HALF_D:REC_W], F32)
        lane = lax.broadcasted_iota(jnp.int32, cw.shape, 1)
        w_row = jnp.sum(jnp.where(lane == j, cw, 0.0), axis=1, keepdims=True)
        gate = (jnp.dot(x_lo, wg_scr[0:HALF_D, :], preferred_element_type=F32)
                + jnp.dot(x_hi, wg_scr[HALF_D:D_MODEL, :], preferred_element_type=F32))
        up = (jnp.dot(x_lo, wu_scr[0:HALF_D, :], preferred_element_type=F32)
              + jnp.dot(x_hi, wu_scr[HALF_D:D_MODEL, :], preferred_element_type=F32))
        h_ref[...] = (gate * jax.nn.sigmoid(gate) * up * w_row).astype(BF16)

    @pl.when(b >= nblk_ref[0])
    def _():
        h_ref[...] = jnp.zeros(h_ref.shape, BF16)


def _expert_up(l, blk_gid, nblk, rec_sorted, e_gate, e_up):
    wspec = pl.BlockSpec((None, None, D_MODEL, EXPERT_FF),
                         lambda j, b, gid, nb: (l, gid[b] * EXPERTS_PER_GROUP + j, 0, 0))
    grid_spec = pltpu.PrefetchScalarGridSpec(
        num_scalar_prefetch=2,
        grid=(EXPERTS_PER_GROUP, MOE_NBLK),
        in_specs=[
            pl.BlockSpec((MOE_BLK, REC_W), lambda j, b, gid, nb: (b, 0)),
            wspec, wspec,
        ],
        out_specs=pl.BlockSpec((MOE_BLK, EXPERT_FF), lambda j, b, gid, nb: (b, j)),
        scratch_shapes=[pltpu.VMEM((D_MODEL, EXPERT_FF), BF16), pltpu.VMEM((D_MODEL, EXPERT_FF), BF16)],
    )
    return pl.pallas_call(
        _expert_up_kernel,
        grid_spec=grid_spec,
        out_shape=jax.ShapeDtypeStruct((MOE_SLOTS, GROUP_FF), BF16),
        compiler_params=_cparams(("arbitrary", "arbitrary")),
        name="expert_up",
    )(blk_gid, nblk, rec_sorted, e_gate, e_up)


DOWN_CH = 512


def _expert_down_kernel(gid_ref, nblk_ref, h_ref, wd_ref, y_ref, wd_scr):
    b = pl.program_id(0)
    prev = gid_ref[jnp.maximum(b - 1, 0)]
    fresh = (b == 0) | (gid_ref[b] != prev)

    @pl.when(fresh)
    def _():
        for c in range(GROUP_FF // DOWN_CH):
            rows = pl.ds(c * DOWN_CH, DOWN_CH)
            wd_scr[rows, :] = wd_ref[rows, :].astype(BF16)

    for c in range(D_MODEL // DOWN_CH):
        cols = pl.ds(c * DOWN_CH, DOWN_CH)
        y_ref[:, cols] = jnp.dot(h_ref[...], wd_scr[:, cols], preferred_element_type=F32)


def _expert_down(l, blk_gid, nblk, h_sorted, e_down_grouped):
    grid_spec = pltpu.PrefetchScalarGridSpec(
        num_scalar_prefetch=2,
        grid=(MOE_NBLK,),
        in_specs=[
            pl.BlockSpec((MOE_BLK, GROUP_FF), lambda b, gid, nb: (b, 0)),
            pl.BlockSpec((None, None, GROUP_FF, D_MODEL), lambda b, gid, nb: (l, gid[b], 0, 0)),
        ],
        out_specs=pl.BlockSpec((MOE_BLK, D_MODEL), lambda b, gid, nb: (b, 0)),
        scratch_shapes=[pltpu.VMEM((GROUP_FF, D_MODEL), BF16)],
    )
    return pl.pallas_call(
        _expert_down_kernel,
        grid_spec=grid_spec,
        out_shape=jax.ShapeDtypeStruct((MOE_SLOTS, D_MODEL), F32),
        compiler_params=_cparams(("arbitrary",)),
        name="expert_down",
    )(blk_gid, nblk, h_sorted, e_down_grouped)


def _moe(l, rec, gid, rank, counts, e_gate, e_up, e_down_grouped):
    padded = (counts + MOE_BLK - 1) // MOE_BLK * MOE_BLK
    pend = jnp.cumsum(padded)
    pstart = pend - padded
    pos = (pstart[gid] + rank).astype(jnp.int32)
    blk_start = jnp.arange(MOE_NBLK, dtype=jnp.int32) * MOE_BLK
    blk_gid = jnp.minimum(jnp.sum((blk_start[:, None] >= pend[None, :]).astype(jnp.int32), axis=1),
                          N_EXPERT_GROUPS - 1)
    nblk = (pend[-1:] // MOE_BLK).astype(jnp.int32)
    rec_sorted = _dispatch(pos, rec)
    hid = _expert_up(l, blk_gid, nblk, rec_sorted, e_gate, e_up)
    y_sorted = _expert_down(l, blk_gid, nblk, hid, e_down_grouped)
    return y_sorted[pos]


FIN_TM = 512
FIN_PROMPT_TILES = N_PROMPT // FIN_TM


def _final_kernel(x_ref, y_ref, g2_ref, fg_ref, op_ref, os_ref):
    i = pl.program_id(0)
    seg = i // (SEG_TOK // FIN_TM)
    x = x_ref[...] + g2_ref[pl.ds(seg, 1), :] * y_ref[...]
    ms = jnp.mean(x * x, axis=-1, keepdims=True)
    out = x * lax.rsqrt(ms + EPS) * fg_ref[...]

    @pl.when(i < FIN_PROMPT_TILES)
    def _():
        op_ref[...] = out

    @pl.when(i >= FIN_PROMPT_TILES)
    def _():
        os_ref[...] = out


def _final_norm(xmid, moe_y, mod, final_g):
    return pl.pallas_call(
        _final_kernel,
        grid=(N_TOK // FIN_TM,),
        in_specs=[
            pl.BlockSpec((FIN_TM, D_MODEL), lambda i: (i, 0)),
            pl.BlockSpec((FIN_TM, D_MODEL), lambda i: (i, 0)),
            _mod_spec(DEPTH - 1, 5, 1),
            pl.BlockSpec((1, D_MODEL), lambda i: (0, 0)),
        ],
        out_specs=[
            pl.BlockSpec((FIN_TM, D_MODEL), lambda i: (jnp.minimum(i, FIN_PROMPT_TILES - 1), 0)),
            pl.BlockSpec((FIN_TM, D_MODEL), lambda i: (jnp.maximum(i - FIN_PROMPT_TILES, 0), 0)),
        ],
        out_shape=[
            jax.ShapeDtypeStruct((N_PROMPT, D_MODEL), F32),
            jax.ShapeDtypeStruct((N_SAMPLE, D_MODEL), F32),
        ],
        compiler_params=_cparams(("arbitrary",)),
        name="final_norm",
    )(xmid, moe_y, mod, final_g.reshape(1, D_MODEL))


def _grid_pos_embed(rows):
    quarter = D_MODEL // 4
    freqs = 1.0 / (POS_BASE ** (jnp.arange(quarter, dtype=F32) / quarter))
    er = jnp.arange(rows, dtype=F32)[:, None] * freqs
    ec = jnp.arange(GRID_W, dtype=F32)[:, None] * freqs
    row_emb = jnp.concatenate([jnp.sin(er), jnp.cos(er)], axis=-1)
    col_emb = jnp.concatenate([jnp.sin(ec), jnp.cos(ec)], axis=-1)
    pe = jnp.concatenate([
        jnp.broadcast_to(row_emb[:, None, :], (rows, GRID_W, D_MODEL // 2)),
        jnp.broadcast_to(col_emb[None, :, :], (rows, GRID_W, D_MODEL // 2))], axis=-1)
    return pe.reshape(rows * GRID_W, D_MODEL)


def kernel(x_prompt, x_sample, state_ssm_re, state_ssm_im, c, c_ctx, norm1_g, norm2_g, w_mod, b_mod,
           w_in, ssm_lam_re, ssm_lam_im, ssm_log_step, ssm_b_re, ssm_b_im, ssm_c_re, ssm_c_im, ssm_d,
           w_glu, b_glu, gm_ln_g, gm_w_s, gm_b_s, w_pa, w_pb, w_o, w_router, b_router,
           e_gate, e_up, e_down, final_g):
    cvec = jnp.concatenate([c_ctx[None], c, jnp.zeros((MOD_ROWS - 1 - DEC_BATCH, D_MODEL), F32)], axis=0)
    mod = _modulation(cvec, w_mod, b_mod)

    perm = (jnp.arange(N_EXPERT_GROUPS)[None, :] * EXPERTS_PER_GROUP
            + jnp.arange(EXPERTS_PER_GROUP)[:, None]).reshape(N_EXPERTS)
    wr = w_router.astype(F32).T[perm]
    wr_hi = wr.astype(BF16)
    wr_lo = (wr - wr_hi.astype(F32)).astype(BF16)
    wr_t = jnp.concatenate([wr_hi, wr_lo], axis=0)
    br_col = b_router.astype(F32)[perm][:, None]

    w_in_b, w_glu_b, w_s_b = w_in.astype(BF16), w_glu.astype(BF16), gm_w_s.astype(BF16)
    w_pa_b, w_pb_b, w_o_b = w_pa.astype(BF16), w_pb.astype(BF16), w_o.astype(BF16)
    b_s_full = jnp.repeat(jnp.transpose(gm_b_s.astype(F32), (0, 2, 1)), GM_GROUP_DIM, axis=2)
    e_down_grouped = e_down.reshape(DEPTH, N_EXPERT_GROUPS, GROUP_FF, D_MODEL)
    w1, w2, a16 = _s5_prep(ssm_lam_re, ssm_lam_im, ssm_log_step, ssm_b_re, ssm_b_im, ssm_c_re, ssm_c_im)
    h0_lat = jnp.concatenate([state_ssm_re[:, :, 0], state_ssm_re[:, :, 1],
                              state_ssm_im[:, :, 0], state_ssm_im[:, :, 1]], axis=-1).astype(F32)
    h0 = jnp.concatenate([jnp.zeros((DEPTH, 1, SSM_GROUPS, STATE_W), F32),
                          jnp.transpose(h0_lat, (1, 0, 2, 3))], axis=1)

    inproj_in = (x_prompt.reshape(N_PROMPT, D_MODEL), x_sample.reshape(N_SAMPLE, D_MODEL),
                 _grid_pos_embed(DEC_SEQ // GRID_W))

    new_re, new_im = [], []
    xmid = moe_y = None
    for l in range(DEPTH):
        xres, u, z = _inproj(l, *inproj_in, mod, norm1_g, gm_ln_g, w_in_b)
        ys, fs = _s5_scan(l, u, w1, w2, a16, h0)
        fin = fs[0]
        p = SSM_STATE
        new_re.append(jnp.stack([fin[:, :, 0:p], fin[::-1, :, p:2 * p]], axis=1))
        new_im.append(jnp.stack([fin[:, :, 2 * p:3 * p], fin[::-1, :, 3 * p:4 * p]], axis=1))

        ya, yb = _mix(l, ys, u, z, ssm_d.astype(F32), w_glu_b, b_glu.astype(F32), w_s_b, b_s_full)
        xmid, rec, gid, rank, cnt = _merge(l, ya, yb, z, xres, mod, norm2_g, w_pa_b, w_pb_b, w_o_b,
                                           wr_t, br_col)
        moe_y = _moe(l, rec, gid.reshape(N_TOK), rank.reshape(N_TOK), cnt[:, 0].astype(jnp.int32),
                     e_gate, e_up, e_down_grouped)
        inproj_in = (xmid, moe_y, mod)

    y_prompt, y_sample = _final_norm(xmid, moe_y, mod, final_g)
    new_state_re = jnp.stack(new_re, axis=1).astype(x_prompt.dtype)
    new_state_im = jnp.stack(new_im, axis=1).astype(x_prompt.dtype)
    return (y_prompt.reshape(BATCH, SEQ, D_MODEL), y_sample.reshape(DEC_BATCH, DEC_SEQ, D_MODEL),
            new_state_re, new_state_im)
```

```python
import functools

import jax
import jax.numpy as jnp
from jax import lax
from jax.experimental import pallas as pl
from jax.experimental.pallas import tpu as pltpu

F32 = jnp.float32
BF16 = jnp.bfloat16
HIGHEST = lax.Precision.HIGHEST

D_MODEL = 2048
BATCH = 16
SEQ = 256
DEPTH = 2
DEC_BATCH = 2
DEC_SEQ = 4096
GRID_W = 64
POS_BASE = 10000.0
EPS = 1e-6
SSM_WIDTH = D_MODEL // 2
SSM_GROUP = 16
SSM_GROUPS = SSM_WIDTH // SSM_GROUP
SSM_STATE = 64
GM_WIDTH = D_MODEL // 2
GM_CHUNK = 128
GM_GROUPS = 8
GM_GROUP_DIM = GM_WIDTH // GM_GROUPS
IN_WIDTH = SSM_WIDTH + 2 * GM_WIDTH + 2 * D_MODEL
N_EXPERTS = 32
N_EXPERT_GROUPS = 8
EXPERTS_PER_GROUP = N_EXPERTS // N_EXPERT_GROUPS
EXPERT_FF = D_MODEL // 4
N_MOD = 6

N_PROMPT = BATCH * SEQ
N_SAMPLE = DEC_BATCH * DEC_SEQ
N_TOK = N_PROMPT + N_SAMPLE
SEG_TOK = 4096
MOD_ROWS = 8
LANES = 128
SUBLANES = 8

SCAN_T = 16
SCAN_W = SCAN_T * SSM_GROUP
SCAN_ROWS = N_TOK // SCAN_T
SCAN_BLK = 256
SCAN_TOK = SCAN_BLK * SCAN_T
N_SCAN_BLK = SCAN_ROWS // SCAN_BLK
G_OCT = SUBLANES
STATE_W = 4 * SSM_STATE
HALF_W = 2 * SSM_STATE

MOE_BLK = 512
MOE_SLOTS = N_TOK + N_EXPERT_GROUPS * MOE_BLK
MOE_NBLK = MOE_SLOTS // MOE_BLK
GROUP_FF = EXPERTS_PER_GROUP * EXPERT_FF

VMEM_LIMIT = 56 * 1024 * 1024


def _cparams(sem):
    return pltpu.CompilerParams(dimension_semantics=sem, vmem_limit_bytes=VMEM_LIMIT)


MOD_TN = 1024


def _mod_kernel(c_ref, w_ref, b_ref, o_ref):
    c = c_ref[...]
    s = c * jax.nn.sigmoid(c)
    s_hi = s.astype(BF16)
    s_lo = (s - s_hi.astype(F32)).astype(BF16)
    w = w_ref[...]
    w_hi = w.astype(BF16)
    w_lo = (w - w_hi.astype(F32)).astype(BF16)
    both = jnp.dot(jnp.concatenate([s_hi, s_lo], axis=0), w_hi, preferred_element_type=F32)
    cross = jnp.dot(s_hi, w_lo, preferred_element_type=F32)
    o_ref[...] = both[0:MOD_ROWS] + both[MOD_ROWS:2 * MOD_ROWS] + cross + b_ref[...]


def _modulation(cvec, w_mod, b_mod):
    width = N_MOD * D_MODEL
    return pl.pallas_call(
        _mod_kernel,
        grid=(DEPTH, width // MOD_TN),
        in_specs=[
            pl.BlockSpec((MOD_ROWS, D_MODEL), lambda l, n: (0, 0)),
            pl.BlockSpec((None, D_MODEL, MOD_TN), lambda l, n: (l, 0, n)),
            pl.BlockSpec((None, 1, MOD_TN), lambda l, n: (l, 0, n)),
        ],
        out_specs=pl.BlockSpec((None, MOD_ROWS, MOD_TN), lambda l, n: (l, 0, n)),
        out_shape=jax.ShapeDtypeStruct((DEPTH, MOD_ROWS, width), F32),
        compiler_params=_cparams(("arbitrary", "arbitrary")),
        name="adaln_mod",
    )(cvec, w_mod, b_mod.reshape(DEPTH, 1, width))


def _pack_pairs(a, b):
    hi = lax.bitcast_convert_type(a.astype(BF16).astype(F32), jnp.uint32)
    lo = lax.bitcast_convert_type(b.astype(BF16).astype(F32), jnp.uint32)
    return hi | (lo >> 16)


def _unpack_pairs(packed):
    hi = lax.bitcast_convert_type(packed & jnp.uint32(0xFFFF0000), F32)
    lo = lax.bitcast_convert_type(packed << 16, F32)
    return jnp.concatenate([hi, lo], axis=-1)


def _mod_spec(l, k, nargs):
    if nargs == 1:
        return pl.BlockSpec((None, MOD_ROWS, D_MODEL), lambda i: (l, 0, k))
    return pl.BlockSpec((None, MOD_ROWS, D_MODEL), lambda i, j: (l, 0, k))


INP_TM = 256
INP_CH = 256
INP_PROMPT_TILES = N_PROMPT // INP_TM
Z_WIDTH = IN_WIDTH - SSM_WIDTH
INP_VMEM_LIMIT = 60 * 1024 * 1024


def _inproj_kernel(*refs, first):
    if first:
        xa_ref, xb_ref, add_ref = refs[:3]
    else:
        xa_ref, add_ref, gain_ref = refs[:3]
    sc_ref, sh_ref, g_ref, ln_ref, w_ref, xres_ref, u_ref, z_ref, h_scr, v_scr = refs[3:]
    i = pl.program_id(0)
    seg = i // (SEG_TOK // INP_TM)

    if first:
        latent = i >= INP_PROMPT_TILES
        x = jnp.where(latent, xb_ref[...] + add_ref[...], xa_ref[...])
    else:
        x = xa_ref[...] + gain_ref[pl.ds(seg, 1), :] * _unpack_pairs(add_ref[...])
    xres_ref[...] = x
    ms = jnp.mean(x * x, axis=-1, keepdims=True)
    y = x * lax.rsqrt(ms + EPS) * g_ref[...]
    h = y * (1.0 + sc_ref[pl.ds(seg, 1), :]) + sh_ref[pl.ds(seg, 1), :]
    h_scr[...] = h.astype(BF16)

    def proj(col):
        return jnp.dot(h_scr[...], w_ref[:, pl.ds(col, INP_CH)], preferred_element_type=F32)

    n_ch = SSM_WIDTH // INP_CH
    for c in range(n_ch):
        u_ref[:, pl.ds(c * INP_CH, INP_CH)] = proj(c * INP_CH)
    for c in range(n_ch):
        z_ref[:, pl.ds(c * INP_CH, INP_CH)] = jax.nn.gelu(proj(SSM_WIDTH + c * INP_CH)).astype(BF16)
    row_sum = jnp.zeros((INP_TM, 1), F32)
    for c in range(n_ch):
        v = jax.nn.gelu(proj(SSM_WIDTH + GM_WIDTH + c * INP_CH))
        v_scr[:, pl.ds(c * INP_CH, INP_CH)] = v
        row_sum = row_sum + jnp.sum(v, axis=-1, keepdims=True)
    mu = row_sum * (1.0 / GM_WIDTH)
    dev = v_scr[...] - mu
    var = jnp.mean(jnp.square(dev), axis=-1, keepdims=True)
    z_ref[:, pl.ds(GM_WIDTH, GM_WIDTH)] = (dev * lax.rsqrt(var + EPS) * ln_ref[...]).astype(BF16)
    gates = SSM_WIDTH + 2 * GM_WIDTH
    for c in range(2 * D_MODEL // INP_CH):
        z_ref[:, pl.ds(2 * GM_WIDTH + c * INP_CH, INP_CH)] = jax.nn.sigmoid(
            proj(gates + c * INP_CH)).astype(BF16)


def _inproj(l, xa, xb_or_add, add_or_gain, mod, norm1_g, gm_ln_g, w_in_bf16):
    first = l == 0
    row_tile = lambda m: pl.BlockSpec((INP_TM, D_MODEL), m)
    if first:
        lead = [row_tile(lambda i: (jnp.minimum(i, INP_PROMPT_TILES - 1), 0)),
                row_tile(lambda i: (jnp.maximum(i - INP_PROMPT_TILES, 0), 0)),
                row_tile(lambda i: (i % (DEC_SEQ // INP_TM), 0))]
    else:
        lead = [row_tile(lambda i: (i, 0)), pl.BlockSpec((INP_TM, HALF_D), lambda i: (i, 0)),
                _mod_spec(l - 1, 5, 1)]
    vec = lambda w: pl.BlockSpec((None, 1, w), lambda i: (l, 0, 0))
    return pl.pallas_call(
        functools.partial(_inproj_kernel, first=first),
        grid=(N_TOK // INP_TM,),
        in_specs=lead + [
            _mod_spec(l, 1, 1), _mod_spec(l, 0, 1),
            vec(D_MODEL), vec(GM_WIDTH),
            pl.BlockSpec((None, D_MODEL, IN_WIDTH), lambda i: (l, 0, 0), pipeline_mode=pl.Buffered(1)),
        ],
        out_specs=[
            pl.BlockSpec((INP_TM, D_MODEL), lambda i: (i, 0)),
            pl.BlockSpec((INP_TM, SSM_WIDTH), lambda i: (i, 0)),
            pl.BlockSpec((INP_TM, Z_WIDTH), lambda i: (i, 0)),
        ],
        out_shape=[
            jax.ShapeDtypeStruct((N_TOK, D_MODEL), F32),
            jax.ShapeDtypeStruct((N_TOK, SSM_WIDTH), F32),
            jax.ShapeDtypeStruct((N_TOK, Z_WIDTH), BF16),
        ],
        scratch_shapes=[pltpu.VMEM((INP_TM, D_MODEL), BF16), pltpu.VMEM((INP_TM, GM_WIDTH), F32)],
        compiler_params=pltpu.CompilerParams(dimension_semantics=("arbitrary",),
                                             vmem_limit_bytes=INP_VMEM_LIMIT),
        name="in_proj",
    )(xa, xb_or_add, add_or_gain, mod, mod, norm1_g.reshape(DEPTH, 1, D_MODEL),
      gm_ln_g.reshape(DEPTH, 1, GM_WIDTH), w_in_bf16)


PK_BRE, PK_BIM, PK_CRE, PK_CIM = 0, 16, 32, 48
PK_LR, PK_LI = 64, 65


def _split_bf16(x):
    hi = x.astype(BF16)
    return hi, (x - hi.astype(F32)).astype(BF16)


POW_ROWS = 24


def _prep_kernel(pk_ref, row_ref, pow_ref, tile_ref, w1_ref, w2_ref, a_ref):
    p = SSM_STATE
    k_sub = jnp.minimum(lax.broadcasted_iota(jnp.int32, (POW_ROWS, LANES), 0), SCAN_T).astype(F32)
    lane = lax.broadcasted_iota(jnp.int32, (p, LANES), 1)
    col = lax.broadcasted_iota(jnp.int32, (SSM_GROUP, SCAN_W), 1)

    def spread_pow(x_r, x_i, which):
        parts = jnp.concatenate(_split_bf16(x_r) + _split_bf16(x_i), axis=0)
        out = jnp.dot(parts, pow_ref[which], preferred_element_type=F32)
        return out[0:p] + out[p:2 * p], out[2 * p:3 * p] + out[3 * p:4 * p]

    def spread_tiles(x):
        out = jnp.dot(jnp.concatenate(_split_bf16(x), axis=0), tile_ref[...], preferred_element_type=F32)
        out = out[0:p] + out[p:2 * p]
        return [out[:, n * SCAN_W:(n + 1) * SCAN_W] for n in range(4)]

    def group(g, _):
        rows = row_ref[g]
        grow_r = rows[0:1] * rows[2:3]
        grow_i = rows[1:2] * rows[2:3]
        mag = jnp.exp(grow_r * k_sub)
        ang = grow_i * k_sub
        unused = jnp.zeros((LANES - POW_ROWS, LANES), F32)
        pw_t_r = jnp.concatenate([mag * jnp.cos(ang), unused], axis=0).T
        pw_t_i = jnp.concatenate([mag * jnp.sin(ang), unused], axis=0).T
        per_dir = []
        for d in range(2):
            pk = pk_ref[d, g]
            lr = pk[:, PK_LR:PK_LR + 1]
            li = pk[:, PK_LI:PK_LI + 1]
            p_r = pw_t_r[d * p:(d + 1) * p]
            p_i = pw_t_i[d * p:(d + 1) * p]
            a_r = p_r[:, 1:2]
            a_i = p_i[:, 1:2]
            den = lr * lr + li * li
            q_r = ((a_r - 1.0) * lr + a_i * li) / den
            q_i = (a_i * lr - (a_r - 1.0) * li) / den
            per_dir.append((pk, p_r, p_i, q_r, q_i))

        w1_rows, lag, carry = [], [], []
        for d in range(2):
            pk, p_r, p_i, q_r, q_i = per_dir[d]
            b_r, b_i, c_r, c_i = spread_tiles(pk)
            bb_r = q_r * b_r - q_i * b_i
            bb_i = q_r * b_i + q_i * b_r
            pw_r, pw_i = spread_pow(p_r, p_i, 1 if d == 0 else 0)
            w1_rows.append((pw_r * bb_r - pw_i * bb_i, pw_r * bb_i + pw_i * bb_r))
            pl_r, pl_i = spread_pow(p_r, p_i, 0 if d == 0 else 1)
            cl_r = c_r * pl_r - c_i * pl_i
            cl_i = c_r * pl_i + c_i * pl_r
            pk_im = pltpu.roll(pk, LANES - (PK_BIM - PK_BRE), 1)
            bt_r = (q_r * pk - q_i * pk_im).T[0:SSM_GROUP, :]
            bt_i = (q_r * pk_im + q_i * pk).T[0:SSM_GROUP, :]
            lag.append(jnp.dot(bt_r, cl_r, precision=HIGHEST, preferred_element_type=F32)
                       - jnp.dot(bt_i, cl_i, precision=HIGHEST, preferred_element_type=F32))
            pc_r, pc_i = spread_pow(p_r, p_i, 2 if d == 0 else 3)
            carry.append((c_r * pc_r - c_i * pc_i, -(c_r * pc_i + c_i * pc_r)))

        (f_re, f_im), (b_re, b_im) = w1_rows
        w1_ref[g] = jnp.concatenate([f_re, b_re, f_im, b_im], axis=0).T.astype(BF16)

        for s in range(SCAN_T):
            fwd = lag[0] if s == 0 else pltpu.roll(lag[0], SSM_GROUP * s, 1)
            fwd = jnp.where(col >= SSM_GROUP * s, fwd, 0.0)
            shift_b = SSM_GROUP * (SCAN_T - 1 - s)
            bwd = lag[1] if shift_b == 0 else pltpu.roll(lag[1], SCAN_W - shift_b, 1)
            bwd = jnp.where(col < SSM_GROUP * (s + 1), bwd, 0.0)
            w2_ref[g, pl.ds(SSM_GROUP * s, SSM_GROUP), :] = (fwd + bwd).astype(BF16)
        (x_re, x_im), (y_re, y_im) = carry
        for n, rows in enumerate((x_re, y_re, x_im, y_im)):
            w2_ref[g, pl.ds(SCAN_W + SSM_STATE * n, SSM_STATE), :] = rows.astype(BF16)

        cols = [per_dir[0][1], per_dir[1][1], per_dir[0][2], per_dir[1][2]]
        a_cols = jnp.zeros((SSM_STATE, LANES), F32)
        for n, c in enumerate(cols):
            a_cols = jnp.where(lane == n, c[:, SCAN_T:SCAN_T + 1], a_cols)
        a_ref[g] = a_cols
        return 0

    lax.fori_loop(0, G_OCT, group, 0, unroll=2)


def _s5_prep(lam_re, lam_im, log_step, b_re, b_im, c_re, c_im):
    shape = (DEPTH, 2, SSM_GROUPS, SSM_STATE)
    lr = lam_re.astype(F32)
    li = lam_im.astype(F32)
    dt = jnp.broadcast_to(jnp.exp(log_step.astype(F32))[..., None], shape)
    pk = jnp.concatenate([
        b_re.astype(F32), b_im.astype(F32),
        jnp.swapaxes(c_re.astype(F32), -1, -2), jnp.swapaxes(c_im.astype(F32), -1, -2),
        lr[..., None], li[..., None],
        jnp.zeros(shape + (LANES - PK_LI - 1,), F32)], axis=-1)
    both_dirs = lambda a: jnp.concatenate([a[:, 0], a[:, 1]], axis=-1)
    rows = jnp.stack([both_dirs(lr), both_dirs(li), both_dirs(dt)], axis=2)
    rows = jnp.concatenate([rows, jnp.zeros((DEPTH, SSM_GROUPS, SUBLANES - 3, LANES), F32)], axis=2)

    blk = jnp.arange(SCAN_W) // SSM_GROUP
    k = jnp.arange(LANES)[:, None]
    pows = [k == blk[None, :], k == (SCAN_T - 1 - blk)[None, :], k == (blk + 1)[None, :],
            k == (SCAN_T - blk)[None, :]]
    h = (jnp.arange(SCAN_W) % SSM_GROUP)[None, :]
    sel_pow = jnp.stack(pows).astype(BF16)
    sel_tile = jnp.concatenate([k == h + off for off in (PK_BRE, PK_BIM, PK_CRE, PK_CIM)],
                               axis=1).astype(BF16)

    n_oct = SSM_GROUPS // G_OCT
    w1, w2, a_cols = pl.pallas_call(
        _prep_kernel,
        grid=(DEPTH, n_oct),
        in_specs=[
            pl.BlockSpec((None, 2, G_OCT, SSM_STATE, LANES), lambda l, o: (l, 0, o, 0, 0)),
            pl.BlockSpec((None, G_OCT, SUBLANES, LANES), lambda l, o: (l, o, 0, 0)),
            pl.BlockSpec((4, LANES, SCAN_W), lambda l, o: (0, 0, 0)),
            pl.BlockSpec((LANES, 4 * SCAN_W), lambda l, o: (0, 0)),
        ],
        out_specs=[
            pl.BlockSpec((None, G_OCT, SCAN_W, STATE_W), lambda l, o: (l, o, 0, 0)),
            pl.BlockSpec((None, G_OCT, SCAN_W + STATE_W, SCAN_W), lambda l, o: (l, o, 0, 0)),
            pl.BlockSpec((None, G_OCT, SSM_STATE, LANES), lambda l, o: (l, o, 0, 0)),
        ],
        out_shape=[
            jax.ShapeDtypeStruct((DEPTH, SSM_GROUPS, SCAN_W, STATE_W), BF16),
            jax.ShapeDtypeStruct((DEPTH, SSM_GROUPS, SCAN_W + STATE_W, SCAN_W), BF16),
            jax.ShapeDtypeStruct((DEPTH, SSM_GROUPS, SSM_STATE, LANES), F32),
        ],
        compiler_params=_cparams(("arbitrary", "arbitrary")),
        name="s5_prep",
    )(pk, rows, sel_pow, sel_tile)
    a16 = jnp.swapaxes(a_cols[..., 0:4], -1, -2).reshape(DEPTH, SSM_GROUPS, STATE_W)
    return w1, w2, a16


def _s5_kernel(u_ref, w1_ref, w2_ref, a_ref, h0_ref, y_ref, fs_ref,
               t_scr, ug_scr, vr_scr, vi_scr, cr_scr, ci_scr, fr_scr, fi_scr):
    blk = pl.program_id(1)
    seq_rows = jnp.where(blk == 0, SEQ // SCAN_T, DEC_SEQ // SCAN_T)

    for s in range(SCAN_T):
        t_scr[s] = u_ref[pl.ds(s, SCAN_BLK, stride=SCAN_T), :].T
    for g in range(G_OCT):
        stacked = t_scr[:, pl.ds(g * SSM_GROUP, SSM_GROUP), :].reshape(SCAN_W, SCAN_BLK)
        ug_scr[g] = stacked.T.astype(BF16)

    for g in range(G_OCT):
        v = jnp.dot(ug_scr[g], w1_ref[g], preferred_element_type=F32)
        vr_scr[pl.ds(g, SCAN_BLK, stride=G_OCT), :] = v[:, 0:HALF_W]
        vi_scr[pl.ds(g, SCAN_BLK, stride=G_OCT), :] = v[:, HALF_W:STATE_W]

    a_r = a_ref[:, 0:HALF_W]
    a_i = a_ref[:, HALF_W:STATE_W]
    h0_r = h0_ref[:, 0:HALF_W]
    h0_i = h0_ref[:, HALF_W:STATE_W]
    fwd_lanes = lax.broadcasted_iota(jnp.int32, (G_OCT, HALF_W), 1) < SSM_STATE
    bwd_lanes = jnp.logical_not(fwd_lanes)

    def step(k, carry):
        s_r, s_i = carry
        rf = pl.ds(pl.multiple_of(k * G_OCT, G_OCT), G_OCT)
        rb = pl.ds(pl.multiple_of((SCAN_BLK - 1 - k) * G_OCT, G_OCT), G_OCT)
        restart = (k & (seq_rows - 1)) == 0
        s_r = jnp.where(restart, h0_r, s_r)
        s_i = jnp.where(restart, h0_i, s_i)
        pltpu.store(cr_scr.at[rf, :], s_r, mask=fwd_lanes)
        pltpu.store(cr_scr.at[rb, :], s_r, mask=bwd_lanes)
        pltpu.store(ci_scr.at[rf, :], s_i, mask=fwd_lanes)
        pltpu.store(ci_scr.at[rb, :], s_i, mask=bwd_lanes)
        v_r = jnp.where(fwd_lanes, vr_scr[rf, :], vr_scr[rb, :])
        v_i = jnp.where(fwd_lanes, vi_scr[rf, :], vi_scr[rb, :])
        n_r = a_r * s_r - a_i * s_i + v_r
        n_i = a_r * s_i + a_i * s_r + v_i
        fr_scr[rf, :] = n_r
        fi_scr[rf, :] = n_i
        return n_r, n_i

    zero = jnp.zeros((G_OCT, HALF_W), F32)
    lax.fori_loop(0, SCAN_BLK, step, (zero, zero), unroll=4)

    for g in range(G_OCT):
        c_r = cr_scr[pl.ds(g, SCAN_BLK, stride=G_OCT), :].astype(BF16)
        c_i = ci_scr[pl.ds(g, SCAN_BLK, stride=G_OCT), :].astype(BF16)
        y = jnp.dot(ug_scr[g], w2_ref[g, 0:SCAN_W, :], preferred_element_type=F32)
        y = y + jnp.dot(c_r, w2_ref[g, SCAN_W:SCAN_W + HALF_W, :], preferred_element_type=F32)
        y = y + jnp.dot(c_i, w2_ref[g, SCAN_W + HALF_W:SCAN_W + STATE_W, :], preferred_element_type=F32)
        t_scr[:, pl.ds(g * SSM_GROUP, SSM_GROUP), :] = y.T.reshape(SCAN_T, SSM_GROUP, SCAN_BLK)
    for s in range(SCAN_T):
        y_ref[pl.ds(s, SCAN_BLK, stride=SCAN_T), :] = t_scr[s].T

    rows_per_seq = SEQ // SCAN_T
    for q in range(SCAN_BLK // rows_per_seq):
        last = pl.ds((q * rows_per_seq + rows_per_seq - 1) * G_OCT, G_OCT)
        fs_ref[q, :, 0:HALF_W] = fr_scr[last, :]
        fs_ref[q, :, HALF_W:STATE_W] = fi_scr[last, :]


def _s5_scan(l, u, w1, w2, a16, h0):
    n_oct = SSM_GROUPS // G_OCT
    n_fin = SCAN_BLK // (SEQ // SCAN_T)
    return pl.pallas_call(
        _s5_kernel,
        grid=(n_oct, N_SCAN_BLK),
        in_specs=[
            pl.BlockSpec((SCAN_TOK, LANES), lambda o, b: (b, o)),
            pl.BlockSpec((None, G_OCT, SCAN_W, STATE_W), lambda o, b: (l, o, 0, 0)),
            pl.BlockSpec((None, G_OCT, SCAN_W + STATE_W, SCAN_W), lambda o, b: (l, o, 0, 0)),
            pl.BlockSpec((None, G_OCT, STATE_W), lambda o, b: (l, o, 0)),
            pl.BlockSpec((None, None, G_OCT, STATE_W), lambda o, b: (l, b, o, 0)),
        ],
        out_specs=[
            pl.BlockSpec((SCAN_TOK, LANES), lambda o, b: (b, o)),
            pl.BlockSpec((None, n_fin, G_OCT, STATE_W), lambda o, b: (b, 0, o, 0)),
        ],
        out_shape=[
            jax.ShapeDtypeStruct((N_TOK, SSM_WIDTH), F32),
            jax.ShapeDtypeStruct((N_SCAN_BLK, n_fin, SSM_GROUPS, STATE_W), F32),
        ],
        scratch_shapes=[
            pltpu.VMEM((SCAN_T, LANES, SCAN_BLK), F32),
            pltpu.VMEM((G_OCT, SCAN_BLK, SCAN_W), BF16),
        ] + [pltpu.VMEM((SCAN_BLK * G_OCT, HALF_W), F32) for _ in range(6)],
        compiler_params=_cparams(("arbitrary", "arbitrary")),
        name="s5_scan",
    )(u, w1, w2, a16, h0)


MIX_TM = 512


def _mix_kernel(ys_ref, u_ref, gu_ref, vn_ref, d_ref, wglu_ref, bglu_ref, ws_ref, bs_ref,
                ya_ref, yb_ref):
    y = ys_ref[...] + d_ref[...] * u_ref[...]
    y = jax.nn.gelu(y)
    gate = jnp.dot(y.astype(BF16), wglu_ref[...], preferred_element_type=F32) + bglu_ref[...]
    ya_ref[...] = (y * jax.nn.sigmoid(gate)).astype(BF16)
    for c in range(MIX_TM // GM_CHUNK):
        rows = pl.ds(c * GM_CHUNK, GM_CHUNK)
        for g in range(GM_GROUPS):
            cols = pl.ds(g * GM_GROUP_DIM, GM_GROUP_DIM)
            mixed = jnp.dot(ws_ref[g], vn_ref[rows, cols], preferred_element_type=F32) + bs_ref[:, cols]
            yb_ref[rows, cols] = (gu_ref[rows, cols].astype(F32) * mixed).astype(BF16)


def _mix(l, ys, u, z, d_skip, w_glu_bf16, b_glu, w_s_bf16, b_s_full):
    tile = lambda k: pl.BlockSpec((MIX_TM, SSM_WIDTH), lambda i: (i, k))
    lay = lambda *shape: pl.BlockSpec((None,) + shape, lambda i: (l,) + tuple(0 for _ in shape))
    return pl.pallas_call(
        _mix_kernel,
        grid=(N_TOK // MIX_TM,),
        in_specs=[
            tile(0), tile(0), tile(0), tile(1),
            lay(1, SSM_WIDTH),
            lay(SSM_WIDTH, SSM_WIDTH),
            lay(1, SSM_WIDTH),
            lay(GM_GROUPS, GM_CHUNK, GM_CHUNK),
            lay(GM_CHUNK, GM_WIDTH),
        ],
        out_specs=[tile(0), tile(0)],
        out_shape=[
            jax.ShapeDtypeStruct((N_TOK, SSM_WIDTH), BF16),
            jax.ShapeDtypeStruct((N_TOK, GM_WIDTH), BF16),
        ],
        compiler_params=_cparams(("arbitrary",)),
        name="mixers",
    )(ys, u, z, z, d_skip.reshape(DEPTH, 1, SSM_WIDTH), w_glu_bf16, b_glu.reshape(DEPTH, 1, SSM_WIDTH),
      w_s_bf16, b_s_full)


MRG_TM = 512
MRG_SUB = 256
HALF_D = D_MODEL // 2
REC_W = HALF_D + LANES


def _merge_kernel(ya_ref, yb_ref, ga0_ref, ga1_ref, gb0_ref, gb1_ref, x_ref, g1_ref, sc_ref, sh_ref,
                  n2_ref, wpa_ref, wpb_ref, wo_ref, wr_ref, br_ref, tri_ref,
                  xmid_ref, rec_ref, gid_ref, rank_ref, cnt_ref, cnt_scr):
    i = pl.program_id(0)
    seg = i // (SEG_TOK // MRG_TM)

    @pl.when(i == 0)
    def _():
        cnt_scr[...] = jnp.zeros(cnt_scr.shape, F32)

    passes = [pl.ds(sub * MRG_SUB, MRG_SUB) for sub in range(MRG_TM // MRG_SUB)]
    for rows in passes:
        _merge_mix(rows, seg, ya_ref, yb_ref, ga0_ref, ga1_ref, gb0_ref, gb1_ref, x_ref, g1_ref,
                   wpa_ref, wpb_ref, wo_ref, xmid_ref)
    for rows in passes:
        _merge_route(rows, seg, sc_ref, sh_ref, n2_ref, wr_ref, br_ref, tri_ref,
                     xmid_ref, rec_ref, gid_ref, rank_ref, cnt_scr)
    cnt_ref[...] = jnp.broadcast_to(cnt_scr[...], cnt_ref.shape)


def _merge_mix(rows, seg, ya_ref, yb_ref, ga0_ref, ga1_ref, gb0_ref, gb1_ref, x_ref, g1_ref,
               wpa_ref, wpb_ref, wo_ref, xmid_ref):
    pa = jnp.dot(ya_ref[rows, :], wpa_ref[...], preferred_element_type=F32)
    pb = jnp.dot(yb_ref[rows, :], wpb_ref[...], preferred_element_type=F32)
    m_lo = ga0_ref[rows, :].astype(F32) * pa[:, :HALF_D] + gb0_ref[rows, :].astype(F32) * pb[:, :HALF_D]
    m_hi = ga1_ref[rows, :].astype(F32) * pa[:, HALF_D:] + gb1_ref[rows, :].astype(F32) * pb[:, HALF_D:]
    mix = jnp.dot(m_lo.astype(BF16), wo_ref[0:HALF_D, :], preferred_element_type=F32)
    mix = mix + jnp.dot(m_hi.astype(BF16), wo_ref[HALF_D:D_MODEL, :], preferred_element_type=F32)
    xmid_ref[rows, :] = x_ref[rows, :] + g1_ref[pl.ds(seg, 1), :] * mix


def _merge_route(rows, seg, sc_ref, sh_ref, n2_ref, wr_ref, br_ref, tri_ref,
                 xmid_ref, rec_ref, gid_ref, rank_ref, cnt_scr):
    x = xmid_ref[rows, :]
    ms = jnp.mean(x * x, axis=-1, keepdims=True)
    y = x * lax.rsqrt(ms + EPS) * n2_ref[...]
    h2 = y * (1.0 + sc_ref[pl.ds(seg, 1), :]) + sh_ref[pl.ds(seg, 1), :]
    hi = h2.astype(BF16)
    hi_f = hi.astype(F32)
    lo = (h2 - hi_f).astype(BF16)
    bits = lax.bitcast_convert_type(hi_f, jnp.uint32)
    rec_ref[rows, 0:HALF_D] = bits[:, :HALF_D] | (bits[:, HALF_D:] >> 16)

    nt = (((1,), (1,)), ((), ()))
    lt = (lax.dot_general(wr_ref[...], hi, nt, preferred_element_type=F32)
          + lax.dot_general(wr_ref[...], lo, nt, preferred_element_type=F32))
    logits = lt[0:N_EXPERTS] + lt[N_EXPERTS:2 * N_EXPERTS]
    scores = jax.nn.sigmoid(logits)
    sel = scores + br_ref[...]
    ng = N_EXPERT_GROUPS
    s = [sel[j * ng:(j + 1) * ng] for j in range(EXPERTS_PER_GROUP)]
    p = [scores[j * ng:(j + 1) * ng] for j in range(EXPERTS_PER_GROUP)]
    a, b = jnp.maximum(s[0], s[1]), jnp.minimum(s[0], s[1])
    c, d = jnp.maximum(s[2], s[3]), jnp.minimum(s[2], s[3])
    grp_score = jnp.maximum(a, c) + jnp.maximum(jnp.minimum(a, c), jnp.maximum(b, d))
    best = jnp.max(grp_score, axis=0, keepdims=True)
    g_iota = lax.broadcasted_iota(jnp.int32, grp_score.shape, 0)
    g_idx = jnp.min(jnp.where(grp_score == best, g_iota, ng), axis=0, keepdims=True)
    onehot = g_iota == g_idx
    v = [jnp.sum(jnp.where(onehot, sj, 0.0), axis=0, keepdims=True) for sj in s]
    q = [jnp.sum(jnp.where(onehot, pj, 0.0), axis=0, keepdims=True) for pj in p]
    picked = []
    for j in range(EXPERTS_PER_GROUP):
        rank = jnp.zeros(v[j].shape, jnp.int32)
        for o in range(EXPERTS_PER_GROUP):
            if o == j:
                continue
            ahead = (v[o] > v[j]) | ((v[o] == v[j]) & (o < j))
            rank = rank + ahead.astype(jnp.int32)
        picked.append(jnp.where(rank < 2, q[j], 0.0))
    total = picked[0] + picked[1] + picked[2] + picked[3]
    gid_ref[:, rows] = g_idx
    cw_rows = jnp.concatenate([pj / total for pj in picked]
                              + [jnp.zeros((LANES - EXPERTS_PER_GROUP, MRG_SUB), F32)], axis=0)
    rec_ref[rows, HALF_D:REC_W] = lax.bitcast_convert_type(cw_rows.T, jnp.uint32)

    hot = onehot.astype(BF16)
    within = jnp.dot(hot, tri_ref[...], preferred_element_type=F32)
    before = jnp.sum(jnp.where(onehot, within + cnt_scr[...], 0.0), axis=0, keepdims=True) - 1.0
    rank_ref[:, rows] = before.astype(jnp.int32)
    cnt_scr[...] = cnt_scr[...] + within[:, MRG_SUB - 1:MRG_SUB]


def _merge(l, ya, yb, z, xres, mod, norm2_g, w_pa, w_pb, w_o, wr_t, br_col):
    n_t = N_TOK // MRG_TM
    tri = (jnp.arange(MRG_SUB)[:, None] <= jnp.arange(MRG_SUB)[None, :]).astype(BF16)
    zspec = lambda k: pl.BlockSpec((MRG_TM, HALF_D), lambda i: (i, k))
    once = pl.Buffered(1)
    lay = lambda *shape: pl.BlockSpec((None,) + shape, lambda i: (l,) + tuple(0 for _ in shape),
                                      pipeline_mode=once)
    const = lambda *shape: pl.BlockSpec(shape, lambda i: tuple(0 for _ in shape), pipeline_mode=once)
    return pl.pallas_call(
        _merge_kernel,
        grid=(n_t,),
        in_specs=[
            pl.BlockSpec((MRG_TM, SSM_WIDTH), lambda i: (i, 0)),
            pl.BlockSpec((MRG_TM, GM_WIDTH), lambda i: (i, 0)),
            zspec(2), zspec(3), zspec(4), zspec(5),
            pl.BlockSpec((MRG_TM, D_MODEL), lambda i: (i, 0)),
            _mod_spec(l, 2, 1), _mod_spec(l, 4, 1), _mod_spec(l, 3, 1),
            lay(1, D_MODEL),
            lay(SSM_WIDTH, D_MODEL),
            lay(GM_WIDTH, D_MODEL),
            lay(D_MODEL, D_MODEL),
            const(2 * N_EXPERTS, D_MODEL),
            const(N_EXPERTS, 1),
            const(MRG_SUB, MRG_SUB),
        ],
        out_specs=[
            pl.BlockSpec((MRG_TM, D_MODEL), lambda i: (i, 0)),
            pl.BlockSpec((MRG_TM, REC_W), lambda i: (i, 0)),
            pl.BlockSpec((None, 1, MRG_TM), lambda i: (i, 0, 0)),
            pl.BlockSpec((None, 1, MRG_TM), lambda i: (i, 0, 0)),
            pl.BlockSpec((N_EXPERT_GROUPS, LANES), lambda i: (0, 0)),
        ],
        out_shape=[
            jax.ShapeDtypeStruct((N_TOK, D_MODEL), F32),
            jax.ShapeDtypeStruct((N_TOK, REC_W), jnp.uint32),
            jax.ShapeDtypeStruct((n_t, 1, MRG_TM), jnp.int32),
            jax.ShapeDtypeStruct((n_t, 1, MRG_TM), jnp.int32),
            jax.ShapeDtypeStruct((N_EXPERT_GROUPS, LANES), F32),
        ],
        scratch_shapes=[pltpu.VMEM((N_EXPERT_GROUPS, 1), F32)],
        compiler_params=_cparams(("arbitrary",)),
        name="merge_router",
    )(ya, yb, z, z, z, z, xres, mod, mod, mod, norm2_g.reshape(DEPTH, 1, D_MODEL),
      w_pa, w_pb, w_o, wr_t, br_col, tri)


DSP_TM = 256


def _dispatch_kernel(pos_ref, rec_ref, init_ref, out_ref, sem):
    del init_ref
    base = pl.program_id(0) * DSP_TM

    def issue(r, carry):
        pltpu.make_async_copy(rec_ref.at[pl.ds(r, 1)], out_ref.at[pl.ds(pos_ref[base + r], 1)], sem).start()
        return carry

    lax.fori_loop(0, DSP_TM, issue, 0, unroll=8)
    pltpu.make_async_copy(rec_ref, out_ref.at[pl.ds(0, DSP_TM)], sem).wait()


def _dispatch(pos, rec):
    grid_spec = pltpu.PrefetchScalarGridSpec(
        num_scalar_prefetch=1,
        grid=(N_TOK // DSP_TM,),
        in_specs=[
            pl.BlockSpec((DSP_TM, REC_W), lambda i, pos: (i, 0)),
            pl.BlockSpec(memory_space=pl.ANY),
        ],
        out_specs=pl.BlockSpec(memory_space=pl.ANY),
        scratch_shapes=[pltpu.SemaphoreType.DMA(())],
    )
    return pl.pallas_call(
        _dispatch_kernel,
        grid_spec=grid_spec,
        out_shape=jax.ShapeDtypeStruct((MOE_SLOTS, REC_W), jnp.uint32),
        input_output_aliases={2: 0},
        compiler_params=_cparams(("arbitrary",)),
        name="moe_dispatch",
    )(pos, rec, jnp.zeros((MOE_SLOTS, REC_W), jnp.uint32))


UP_EXPERTS = 2


def _expert_up_kernel(gid_ref, nblk_ref, rec_ref, wg_ref, wu_ref, h_ref, wg_scr, wu_scr):
    half = pl.program_id(0)
    b = pl.program_id(1)
    prev = gid_ref[jnp.maximum(b - 1, 0)]
    fresh = (b == 0) | (gid_ref[b] != prev)

    @pl.when(fresh)
    def _():
        for e in range(UP_EXPERTS):
            wg_scr[e] = wg_ref[e].astype(BF16)
            wu_scr[e] = wu_ref[e].astype(BF16)

    @pl.when(b < nblk_ref[0])
    def _():
        packed = rec_ref[:, 0:HALF_D]
        x_lo = lax.bitcast_convert_type(packed & jnp.uint32(0xFFFF0000), F32).astype(BF16)
        x_hi = lax.bitcast_convert_type(packed << 16, F32).astype(BF16)
        cw = lax.bitcast_convert_type(rec_ref[:, HALF_D:REC_W], F32)
        lane = lax.broadcasted_iota(jnp.int32, cw.shape, 1)
        for e in range(UP_EXPERTS):
            w_row = jnp.sum(jnp.where(lane == half * UP_EXPERTS + e, cw, 0.0), axis=1, keepdims=True)
            gate = (jnp.dot(x_lo, wg_scr[e, 0:HALF_D, :], preferred_element_type=F32)
                    + jnp.dot(x_hi, wg_scr[e, HALF_D:D_MODEL, :], preferred_element_type=F32))
            up = (jnp.dot(x_lo, wu_scr[e, 0:HALF_D, :], preferred_element_type=F32)
                  + jnp.dot(x_hi, wu_scr[e, HALF_D:D_MODEL, :], preferred_element_type=F32))
            h_ref[:, pl.ds(e * EXPERT_FF, EXPERT_FF)] = (gate * jax.nn.sigmoid(gate) * up * w_row).astype(BF16)

    @pl.when(b >= nblk_ref[0])
    def _():
        h_ref[...] = jnp.zeros(h_ref.shape, BF16)


def _expert_up(l, blk_gid, nblk, rec_sorted, e_gate, e_up):
    halves = EXPERTS_PER_GROUP // UP_EXPERTS
    wspec = pl.BlockSpec((None, UP_EXPERTS, D_MODEL, EXPERT_FF),
                         lambda h, b, gid, nb: (l, gid[b] * halves + h, 0, 0))
    grid_spec = pltpu.PrefetchScalarGridSpec(
        num_scalar_prefetch=2,
        grid=(halves, MOE_NBLK),
        in_specs=[
            pl.BlockSpec((MOE_BLK, REC_W), lambda h, b, gid, nb: (b, 0)),
            wspec, wspec,
        ],
        out_specs=pl.BlockSpec((MOE_BLK, UP_EXPERTS * EXPERT_FF), lambda h, b, gid, nb: (b, h)),
        scratch_shapes=[pltpu.VMEM((UP_EXPERTS, D_MODEL, EXPERT_FF), BF16),
                        pltpu.VMEM((UP_EXPERTS, D_MODEL, EXPERT_FF), BF16)],
    )
    return pl.pallas_call(
        _expert_up_kernel,
        grid_spec=grid_spec,
        out_shape=jax.ShapeDtypeStruct((MOE_SLOTS, GROUP_FF), BF16),
        compiler_params=_cparams(("arbitrary", "arbitrary")),
        name="expert_up",
    )(blk_gid, nblk, rec_sorted, e_gate, e_up)


DOWN_CH = 512


def _expert_down_kernel(gid_ref, nblk_ref, h_ref, wd_ref, y_ref, wd_scr):
    b = pl.program_id(0)
    prev = gid_ref[jnp.maximum(b - 1, 0)]
    fresh = (b == 0) | (gid_ref[b] != prev)

    @pl.when(fresh)
    def _():
        for c in range(GROUP_FF // DOWN_CH):
            rows = pl.ds(c * DOWN_CH, DOWN_CH)
            wd_scr[rows, :] = wd_ref[rows, :].astype(BF16)

    for c in range(HALF_D // DOWN_CH):
        cols = pl.ds(c * DOWN_CH, DOWN_CH)
        pair = pl.ds(HALF_D + c * DOWN_CH, DOWN_CH)
        y_ref[:, cols] = _pack_pairs(jnp.dot(h_ref[...], wd_scr[:, cols], preferred_element_type=F32),
                                     jnp.dot(h_ref[...], wd_scr[:, pair], preferred_element_type=F32))


def _expert_down(l, blk_gid, nblk, h_sorted, e_down_grouped):
    grid_spec = pltpu.PrefetchScalarGridSpec(
        num_scalar_prefetch=2,
        grid=(MOE_NBLK,),
        in_specs=[
            pl.BlockSpec((MOE_BLK, GROUP_FF), lambda b, gid, nb: (b, 0)),
            pl.BlockSpec((None, None, GROUP_FF, D_MODEL), lambda b, gid, nb: (l, gid[b], 0, 0)),
        ],
        out_specs=pl.BlockSpec((MOE_BLK, HALF_D), lambda b, gid, nb: (b, 0)),
        scratch_shapes=[pltpu.VMEM((GROUP_FF, D_MODEL), BF16)],
    )
    return pl.pallas_call(
        _expert_down_kernel,
        grid_spec=grid_spec,
        out_shape=jax.ShapeDtypeStruct((MOE_SLOTS, HALF_D), jnp.uint32),
        compiler_params=_cparams(("arbitrary",)),
        name="expert_down",
    )(blk_gid, nblk, h_sorted, e_down_grouped)


def _moe(l, rec, gid, rank, counts, e_gate, e_up, e_down_grouped):
    padded = (counts + MOE_BLK - 1) // MOE_BLK * MOE_BLK
    pend = jnp.cumsum(padded)
    pstart = pend - padded
    pos = (pstart[gid] + rank).astype(jnp.int32)
    blk_start = jnp.arange(MOE_NBLK, dtype=jnp.int32) * MOE_BLK
    blk_gid = jnp.minimum(jnp.sum((blk_start[:, None] >= pend[None, :]).astype(jnp.int32), axis=1),
                          N_EXPERT_GROUPS - 1)
    nblk = (pend[-1:] // MOE_BLK).astype(jnp.int32)
    rec_sorted = _dispatch(pos, rec)
    hid = _expert_up(l, blk_gid, nblk, rec_sorted, e_gate, e_up)
    y_sorted = _expert_down(l, blk_gid, nblk, hid, e_down_grouped)
    return y_sorted[pos]


FIN_TM = 512
FIN_PROMPT_TILES = N_PROMPT // FIN_TM


def _final_kernel(x_ref, y_ref, g2_ref, fg_ref, op_ref, os_ref):
    i = pl.program_id(0)
    seg = i // (SEG_TOK // FIN_TM)
    x = x_ref[...] + g2_ref[pl.ds(seg, 1), :] * _unpack_pairs(y_ref[...])
    ms = jnp.mean(x * x, axis=-1, keepdims=True)
    out = x * lax.rsqrt(ms + EPS) * fg_ref[...]

    @pl.when(i < FIN_PROMPT_TILES)
    def _():
        op_ref[...] = out

    @pl.when(i >= FIN_PROMPT_TILES)
    def _():
        os_ref[...] = out


def _final_norm(xmid, moe_y, mod, final_g):
    return pl.pallas_call(
        _final_kernel,
        grid=(N_TOK // FIN_TM,),
        in_specs=[
            pl.BlockSpec((FIN_TM, D_MODEL), lambda i: (i, 0)),
            pl.BlockSpec((FIN_TM, HALF_D), lambda i: (i, 0)),
            _mod_spec(DEPTH - 1, 5, 1),
            pl.BlockSpec((1, D_MODEL), lambda i: (0, 0)),
        ],
        out_specs=[
            pl.BlockSpec((FIN_TM, D_MODEL), lambda i: (jnp.minimum(i, FIN_PROMPT_TILES - 1), 0)),
            pl.BlockSpec((FIN_TM, D_MODEL), lambda i: (jnp.maximum(i - FIN_PROMPT_TILES, 0), 0)),
        ],
        out_shape=[
            jax.ShapeDtypeStruct((N_PROMPT, D_MODEL), F32),
            jax.ShapeDtypeStruct((N_SAMPLE, D_MODEL), F32),
        ],
        compiler_params=_cparams(("arbitrary",)),
        name="final_norm",
    )(xmid, moe_y, mod, final_g.reshape(1, D_MODEL))


def _grid_pos_embed(rows):
    quarter = D_MODEL // 4
    freqs = 1.0 / (POS_BASE ** (jnp.arange(quarter, dtype=F32) / quarter))
    er = jnp.arange(rows, dtype=F32)[:, None] * freqs
    ec = jnp.arange(GRID_W, dtype=F32)[:, None] * freqs
    row_emb = jnp.concatenate([jnp.sin(er), jnp.cos(er)], axis=-1)
    col_emb = jnp.concatenate([jnp.sin(ec), jnp.cos(ec)], axis=-1)
    pe = jnp.concatenate([
        jnp.broadcast_to(row_emb[:, None, :], (rows, GRID_W, D_MODEL // 2)),
        jnp.broadcast_to(col_emb[None, :, :], (rows, GRID_W, D_MODEL // 2))], axis=-1)
    return pe.reshape(rows * GRID_W, D_MODEL)


def kernel(x_prompt, x_sample, state_ssm_re, state_ssm_im, c, c_ctx, norm1_g, norm2_g, w_mod, b_mod,
           w_in, ssm_lam_re, ssm_lam_im, ssm_log_step, ssm_b_re, ssm_b_im, ssm_c_re, ssm_c_im, ssm_d,
           w_glu, b_glu, gm_ln_g, gm_w_s, gm_b_s, w_pa, w_pb, w_o, w_router, b_router,
           e_gate, e_up, e_down, final_g):
    cvec = jnp.concatenate([c_ctx[None], c, jnp.zeros((MOD_ROWS - 1 - DEC_BATCH, D_MODEL), F32)], axis=0)
    mod = _modulation(cvec, w_mod, b_mod)

    perm = (jnp.arange(N_EXPERT_GROUPS)[None, :] * EXPERTS_PER_GROUP
            + jnp.arange(EXPERTS_PER_GROUP)[:, None]).reshape(N_EXPERTS)
    wr = w_router.astype(F32).T[perm]
    wr_hi = wr.astype(BF16)
    wr_lo = (wr - wr_hi.astype(F32)).astype(BF16)
    wr_t = jnp.concatenate([wr_hi, wr_lo], axis=0)
    br_col = b_router.astype(F32)[perm][:, None]

    w_in_b, w_glu_b, w_s_b = w_in.astype(BF16), w_glu.astype(BF16), gm_w_s.astype(BF16)
    w_pa_b, w_pb_b, w_o_b = w_pa.astype(BF16), w_pb.astype(BF16), w_o.astype(BF16)
    b_s_full = jnp.repeat(jnp.transpose(gm_b_s.astype(F32), (0, 2, 1)), GM_GROUP_DIM, axis=2)
    e_down_grouped = e_down.reshape(DEPTH, N_EXPERT_GROUPS, GROUP_FF, D_MODEL)
    w1, w2, a16 = _s5_prep(ssm_lam_re, ssm_lam_im, ssm_log_step, ssm_b_re, ssm_b_im, ssm_c_re, ssm_c_im)
    h0_lat = jnp.concatenate([state_ssm_re[:, :, 0], state_ssm_re[:, :, 1],
                              state_ssm_im[:, :, 0], state_ssm_im[:, :, 1]], axis=-1).astype(F32)
    h0 = jnp.concatenate([jnp.zeros((DEPTH, 1, SSM_GROUPS, STATE_W), F32),
                          jnp.transpose(h0_lat, (1, 0, 2, 3))], axis=1)

    inproj_in = (x_prompt.reshape(N_PROMPT, D_MODEL), x_sample.reshape(N_SAMPLE, D_MODEL),
                 _grid_pos_embed(DEC_SEQ // GRID_W))

    new_re, new_im = [], []
    xmid = moe_y = None
    for l in range(DEPTH):
        xres, u, z = _inproj(l, *inproj_in, mod, norm1_g, gm_ln_g, w_in_b)
        ys, fs = _s5_scan(l, u, w1, w2, a16, h0)
        fin = fs[0]
        p = SSM_STATE
        new_re.append(jnp.stack([fin[:, :, 0:p], fin[::-1, :, p:2 * p]], axis=1))
        new_im.append(jnp.stack([fin[:, :, 2 * p:3 * p], fin[::-1, :, 3 * p:4 * p]], axis=1))

        ya, yb = _mix(l, ys, u, z, ssm_d.astype(F32), w_glu_b, b_glu.astype(F32), w_s_b, b_s_full)
        xmid, rec, gid, rank, cnt = _merge(l, ya, yb, z, xres, mod, norm2_g, w_pa_b, w_pb_b, w_o_b,
                                           wr_t, br_col)
        moe_y = _moe(l, rec, gid.reshape(N_TOK), rank.reshape(N_TOK), cnt[:, 0].astype(jnp.int32),
                     e_gate, e_up, e_down_grouped)
        inproj_in = (xmid, moe_y, mod)

    y_prompt, y_sample = _final_norm(xmid, moe_y, mod, final_g)
    new_state_re = jnp.stack(new_re, axis=1).astype(x_prompt.dtype)
    new_state_im = jnp.stack(new_im, axis=1).astype(x_prompt.dtype)
    return (y_prompt.reshape(BATCH, SEQ, D_MODEL), y_sample.reshape(DEC_BATCH, DEC_SEQ, D_MODEL),
            new_state_re, new_state_im)
```

```python
import functools

import jax
import jax.numpy as jnp
from jax import lax
from jax.experimental import pallas as pl
from jax.experimental.pallas import tpu as pltpu

F32 = jnp.float32
BF16 = jnp.bfloat16
HIGHEST = lax.Precision.HIGHEST

D_MODEL = 2048
BATCH = 16
SEQ = 256
DEPTH = 2
DEC_BATCH = 2
DEC_SEQ = 4096
GRID_W = 64
POS_BASE = 10000.0
EPS = 1e-6
SSM_WIDTH = D_MODEL // 2
SSM_GROUP = 16
SSM_GROUPS = SSM_WIDTH // SSM_GROUP
SSM_STATE = 64
GM_WIDTH = D_MODEL // 2
GM_CHUNK = 128
GM_GROUPS = 8
GM_GROUP_DIM = GM_WIDTH // GM_GROUPS
IN_WIDTH = SSM_WIDTH + 2 * GM_WIDTH + 2 * D_MODEL
N_EXPERTS = 32
N_EXPERT_GROUPS = 8
EXPERTS_PER_GROUP = N_EXPERTS // N_EXPERT_GROUPS
EXPERT_FF = D_MODEL // 4
N_MOD = 6

N_PROMPT = BATCH * SEQ
N_SAMPLE = DEC_BATCH * DEC_SEQ
N_TOK = N_PROMPT + N_SAMPLE
SEG_TOK = 4096
MOD_ROWS = 8
LANES = 128
SUBLANES = 8

SCAN_T = 16
SCAN_W = SCAN_T * SSM_GROUP
SCAN_ROWS = N_TOK // SCAN_T
SCAN_BLK = 256
SCAN_TOK = SCAN_BLK * SCAN_T
N_SCAN_BLK = SCAN_ROWS // SCAN_BLK
G_OCT = SUBLANES
STATE_W = 4 * SSM_STATE
HALF_W = 2 * SSM_STATE

MOE_BLK = 512
MOE_SLOTS = N_TOK + N_EXPERT_GROUPS * MOE_BLK
MOE_NBLK = MOE_SLOTS // MOE_BLK
GROUP_FF = EXPERTS_PER_GROUP * EXPERT_FF

VMEM_LIMIT = 56 * 1024 * 1024


def _cparams(sem):
    return pltpu.CompilerParams(dimension_semantics=sem, vmem_limit_bytes=VMEM_LIMIT)


MOD_TN = 1024


def _mod_kernel(c_ref, w_ref, b_ref, o_ref):
    c = c_ref[...]
    s = c * jax.nn.sigmoid(c)
    s_hi = s.astype(BF16)
    s_lo = (s - s_hi.astype(F32)).astype(BF16)
    w = w_ref[...]
    w_hi = w.astype(BF16)
    w_lo = (w - w_hi.astype(F32)).astype(BF16)
    both = jnp.dot(jnp.concatenate([s_hi, s_lo], axis=0), w_hi, preferred_element_type=F32)
    cross = jnp.dot(s_hi, w_lo, preferred_element_type=F32)
    o_ref[...] = both[0:MOD_ROWS] + both[MOD_ROWS:2 * MOD_ROWS] + cross + b_ref[...]


def _modulation(cvec, w_mod, b_mod):
    width = N_MOD * D_MODEL
    return pl.pallas_call(
        _mod_kernel,
        grid=(DEPTH, width // MOD_TN),
        in_specs=[
            pl.BlockSpec((MOD_ROWS, D_MODEL), lambda l, n: (0, 0)),
            pl.BlockSpec((None, D_MODEL, MOD_TN), lambda l, n: (l, 0, n)),
            pl.BlockSpec((None, 1, MOD_TN), lambda l, n: (l, 0, n)),
        ],
        out_specs=pl.BlockSpec((None, MOD_ROWS, MOD_TN), lambda l, n: (l, 0, n)),
        out_shape=jax.ShapeDtypeStruct((DEPTH, MOD_ROWS, width), F32),
        compiler_params=_cparams(("arbitrary", "arbitrary")),
        name="adaln_mod",
    )(cvec, w_mod, b_mod.reshape(DEPTH, 1, width))


def _pack_pairs(a, b):
    hi = lax.bitcast_convert_type(a.astype(BF16).astype(F32), jnp.uint32)
    lo = lax.bitcast_convert_type(b.astype(BF16).astype(F32), jnp.uint32)
    return hi | (lo >> 16)


def _unpack_pairs(packed):
    hi = lax.bitcast_convert_type(packed & jnp.uint32(0xFFFF0000), F32)
    lo = lax.bitcast_convert_type(packed << 16, F32)
    return jnp.concatenate([hi, lo], axis=-1)


def _mod_spec(l, k, nargs):
    if nargs == 1:
        return pl.BlockSpec((None, MOD_ROWS, D_MODEL), lambda i: (l, 0, k))
    return pl.BlockSpec((None, MOD_ROWS, D_MODEL), lambda i, j: (l, 0, k))


INP_TM = 256
INP_CH = 256
INP_PROMPT_TILES = N_PROMPT // INP_TM
Z_WIDTH = IN_WIDTH - SSM_WIDTH
INP_VMEM_LIMIT = 60 * 1024 * 1024


def _inproj_kernel(*refs, first):
    if first:
        xa_ref, xb_ref, add_ref = refs[:3]
    else:
        xa_ref, add_ref, gain_ref = refs[:3]
    sc_ref, sh_ref, g_ref, ln_ref, w_ref, xres_ref, u_ref, z_ref, h_scr, v_scr = refs[3:]
    i = pl.program_id(0)
    seg = i // (SEG_TOK // INP_TM)

    if first:
        latent = i >= INP_PROMPT_TILES
        x = jnp.where(latent, xb_ref[...] + add_ref[...], xa_ref[...])
    else:
        x = xa_ref[...] + gain_ref[pl.ds(seg, 1), :] * _unpack_pairs(add_ref[...])
    xres_ref[...] = x
    ms = jnp.mean(x * x, axis=-1, keepdims=True)
    y = x * lax.rsqrt(ms + EPS) * g_ref[...]
    h = y * (1.0 + sc_ref[pl.ds(seg, 1), :]) + sh_ref[pl.ds(seg, 1), :]
    h_scr[...] = h.astype(BF16)

    def proj(col):
        return jnp.dot(h_scr[...], w_ref[:, pl.ds(col, INP_CH)], preferred_element_type=F32)

    n_ch = SSM_WIDTH // INP_CH
    for c in range(n_ch):
        u_ref[:, pl.ds(c * INP_CH, INP_CH)] = proj(c * INP_CH)
    for c in range(n_ch):
        z_ref[:, pl.ds(c * INP_CH, INP_CH)] = jax.nn.gelu(proj(SSM_WIDTH + c * INP_CH)).astype(BF16)
    row_sum = jnp.zeros((INP_TM, 1), F32)
    for c in range(n_ch):
        v = jax.nn.gelu(proj(SSM_WIDTH + GM_WIDTH + c * INP_CH))
        v_scr[:, pl.ds(c * INP_CH, INP_CH)] = v
        row_sum = row_sum + jnp.sum(v, axis=-1, keepdims=True)
    mu = row_sum * (1.0 / GM_WIDTH)
    dev = v_scr[...] - mu
    var = jnp.mean(jnp.square(dev), axis=-1, keepdims=True)
    z_ref[:, pl.ds(GM_WIDTH, GM_WIDTH)] = (dev * lax.rsqrt(var + EPS) * ln_ref[...]).astype(BF16)
    gates = SSM_WIDTH + 2 * GM_WIDTH
    for c in range(2 * D_MODEL // INP_CH):
        z_ref[:, pl.ds(2 * GM_WIDTH + c * INP_CH, INP_CH)] = jax.nn.sigmoid(
            proj(gates + c * INP_CH)).astype(BF16)


def _inproj(l, xa, xb_or_add, add_or_gain, mod, norm1_g, gm_ln_g, w_in_bf16):
    first = l == 0
    row_tile = lambda m: pl.BlockSpec((INP_TM, D_MODEL), m)
    if first:
        lead = [row_tile(lambda i: (jnp.minimum(i, INP_PROMPT_TILES - 1), 0)),
                row_tile(lambda i: (jnp.maximum(i - INP_PROMPT_TILES, 0), 0)),
                row_tile(lambda i: (i % (DEC_SEQ // INP_TM), 0))]
    else:
        lead = [row_tile(lambda i: (i, 0)), pl.BlockSpec((INP_TM, HALF_D), lambda i: (i, 0)),
                _mod_spec(l - 1, 5, 1)]
    vec = lambda w: pl.BlockSpec((None, 1, w), lambda i: (l, 0, 0))
    return pl.pallas_call(
        functools.partial(_inproj_kernel, first=first),
        grid=(N_TOK // INP_TM,),
        in_specs=lead + [
            _mod_spec(l, 1, 1), _mod_spec(l, 0, 1),
            vec(D_MODEL), vec(GM_WIDTH),
            pl.BlockSpec((None, D_MODEL, IN_WIDTH), lambda i: (l, 0, 0), pipeline_mode=pl.Buffered(1)),
        ],
        out_specs=[
            pl.BlockSpec((INP_TM, D_MODEL), lambda i: (i, 0)),
            pl.BlockSpec((INP_TM, SSM_WIDTH), lambda i: (i, 0)),
            pl.BlockSpec((INP_TM, Z_WIDTH), lambda i: (i, 0)),
        ],
        out_shape=[
            jax.ShapeDtypeStruct((N_TOK, D_MODEL), F32),
            jax.ShapeDtypeStruct((N_TOK, SSM_WIDTH), F32),
            jax.ShapeDtypeStruct((N_TOK, Z_WIDTH), BF16),
        ],
        scratch_shapes=[pltpu.VMEM((INP_TM, D_MODEL), BF16), pltpu.VMEM((INP_TM, GM_WIDTH), F32)],
        compiler_params=pltpu.CompilerParams(dimension_semantics=("arbitrary",),
                                             vmem_limit_bytes=INP_VMEM_LIMIT),
        name="in_proj",
    )(xa, xb_or_add, add_or_gain, mod, mod, norm1_g.reshape(DEPTH, 1, D_MODEL),
      gm_ln_g.reshape(DEPTH, 1, GM_WIDTH), w_in_bf16)


PK_BRE, PK_BIM, PK_CRE, PK_CIM = 0, 16, 32, 48
PK_LR, PK_LI = 64, 65


def _split_bf16(x):
    hi = x.astype(BF16)
    return hi, (x - hi.astype(F32)).astype(BF16)


POW_ROWS = 24


def _prep_kernel(pk_ref, row_ref, pow_ref, tile_ref, w1_ref, w2_ref, a_ref):
    p = SSM_STATE
    k_sub = jnp.minimum(lax.broadcasted_iota(jnp.int32, (POW_ROWS, LANES), 0), SCAN_T).astype(F32)
    lane = lax.broadcasted_iota(jnp.int32, (p, LANES), 1)
    col = lax.broadcasted_iota(jnp.int32, (SSM_GROUP, SCAN_W), 1)

    def spread_pow(x_r, x_i, which):
        parts = jnp.concatenate(_split_bf16(x_r) + _split_bf16(x_i), axis=0)
        out = jnp.dot(parts, pow_ref[which], preferred_element_type=F32)
        return out[0:p] + out[p:2 * p], out[2 * p:3 * p] + out[3 * p:4 * p]

    def spread_tiles(x):
        out = jnp.dot(jnp.concatenate(_split_bf16(x), axis=0), tile_ref[...], preferred_element_type=F32)
        out = out[0:p] + out[p:2 * p]
        return [out[:, n * SCAN_W:(n + 1) * SCAN_W] for n in range(4)]

    def group(g, _):
        rows = row_ref[g]
        grow_r = rows[0:1] * rows[2:3]
        grow_i = rows[1:2] * rows[2:3]
        mag = jnp.exp(grow_r * k_sub)
        ang = grow_i * k_sub
        unused = jnp.zeros((LANES - POW_ROWS, LANES), F32)
        pw_t_r = jnp.concatenate([mag * jnp.cos(ang), unused], axis=0).T
        pw_t_i = jnp.concatenate([mag * jnp.sin(ang), unused], axis=0).T
        per_dir = []
        for d in range(2):
            pk = pk_ref[d, g]
            lr = pk[:, PK_LR:PK_LR + 1]
            li = pk[:, PK_LI:PK_LI + 1]
            p_r = pw_t_r[d * p:(d + 1) * p]
            p_i = pw_t_i[d * p:(d + 1) * p]
            a_r = p_r[:, 1:2]
            a_i = p_i[:, 1:2]
            den = lr * lr + li * li
            q_r = ((a_r - 1.0) * lr + a_i * li) / den
            q_i = (a_i * lr - (a_r - 1.0) * li) / den
            per_dir.append((pk, p_r, p_i, q_r, q_i))

        w1_rows, lag, carry = [], [], []
        for d in range(2):
            pk, p_r, p_i, q_r, q_i = per_dir[d]
            b_r, b_i, c_r, c_i = spread_tiles(pk)
            bb_r = q_r * b_r - q_i * b_i
            bb_i = q_r * b_i + q_i * b_r
            pw_r, pw_i = spread_pow(p_r, p_i, 1 if d == 0 else 0)
            w1_rows.append((pw_r * bb_r - pw_i * bb_i, pw_r * bb_i + pw_i * bb_r))
            pl_r, pl_i = spread_pow(p_r, p_i, 0 if d == 0 else 1)
            cl_r = c_r * pl_r - c_i * pl_i
            cl_i = c_r * pl_i + c_i * pl_r
            pk_im = pltpu.roll(pk, LANES - (PK_BIM - PK_BRE), 1)
            bt_r = (q_r * pk - q_i * pk_im).T[0:SSM_GROUP, :]
            bt_i = (q_r * pk_im + q_i * pk).T[0:SSM_GROUP, :]
            lag.append(jnp.dot(bt_r, cl_r, precision=HIGHEST, preferred_element_type=F32)
                       - jnp.dot(bt_i, cl_i, precision=HIGHEST, preferred_element_type=F32))
            pc_r, pc_i = spread_pow(p_r, p_i, 2 if d == 0 else 3)
            carry.append((c_r * pc_r - c_i * pc_i, -(c_r * pc_i + c_i * pc_r)))

        (f_re, f_im), (b_re, b_im) = w1_rows
        w1_ref[g] = jnp.concatenate([f_re, b_re, f_im, b_im], axis=0).T.astype(BF16)

        for s in range(SCAN_T):
            fwd = lag[0] if s == 0 else pltpu.roll(lag[0], SSM_GROUP * s, 1)
            fwd = jnp.where(col >= SSM_GROUP * s, fwd, 0.0)
            shift_b = SSM_GROUP * (SCAN_T - 1 - s)
            bwd = lag[1] if shift_b == 0 else pltpu.roll(lag[1], SCAN_W - shift_b, 1)
            bwd = jnp.where(col < SSM_GROUP * (s + 1), bwd, 0.0)
            w2_ref[g, pl.ds(SSM_GROUP * s, SSM_GROUP), :] = (fwd + bwd).astype(BF16)
        (x_re, x_im), (y_re, y_im) = carry
        for n, rows in enumerate((x_re, y_re, x_im, y_im)):
            w2_ref[g, pl.ds(SCAN_W + SSM_STATE * n, SSM_STATE), :] = rows.astype(BF16)

        cols = [per_dir[0][1], per_dir[1][1], per_dir[0][2], per_dir[1][2]]
        a_cols = jnp.zeros((SSM_STATE, LANES), F32)
        for n, c in enumerate(cols):
            a_cols = jnp.where(lane == n, c[:, SCAN_T:SCAN_T + 1], a_cols)
        a_ref[g] = a_cols
        return 0

    lax.fori_loop(0, G_OCT, group, 0, unroll=2)


def _s5_prep(lam_re, lam_im, log_step, b_re, b_im, c_re, c_im):
    shape = (DEPTH, 2, SSM_GROUPS, SSM_STATE)
    lr = lam_re.astype(F32)
    li = lam_im.astype(F32)
    dt = jnp.broadcast_to(jnp.exp(log_step.astype(F32))[..., None], shape)
    pk = jnp.concatenate([
        b_re.astype(F32), b_im.astype(F32),
        jnp.swapaxes(c_re.astype(F32), -1, -2), jnp.swapaxes(c_im.astype(F32), -1, -2),
        lr[..., None], li[..., None],
        jnp.zeros(shape + (LANES - PK_LI - 1,), F32)], axis=-1)
    both_dirs = lambda a: jnp.concatenate([a[:, 0], a[:, 1]], axis=-1)
    rows = jnp.stack([both_dirs(lr), both_dirs(li), both_dirs(dt)], axis=2)
    rows = jnp.concatenate([rows, jnp.zeros((DEPTH, SSM_GROUPS, SUBLANES - 3, LANES), F32)], axis=2)

    blk = jnp.arange(SCAN_W) // SSM_GROUP
    k = jnp.arange(LANES)[:, None]
    pows = [k == blk[None, :], k == (SCAN_T - 1 - blk)[None, :], k == (blk + 1)[None, :],
            k == (SCAN_T - blk)[None, :]]
    h = (jnp.arange(SCAN_W) % SSM_GROUP)[None, :]
    sel_pow = jnp.stack(pows).astype(BF16)
    sel_tile = jnp.concatenate([k == h + off for off in (PK_BRE, PK_BIM, PK_CRE, PK_CIM)],
                               axis=1).astype(BF16)

    n_oct = SSM_GROUPS // G_OCT
    w1, w2, a_cols = pl.pallas_call(
        _prep_kernel,
        grid=(DEPTH, n_oct),
        in_specs=[
            pl.BlockSpec((None, 2, G_OCT, SSM_STATE, LANES), lambda l, o: (l, 0, o, 0, 0)),
            pl.BlockSpec((None, G_OCT, SUBLANES, LANES), lambda l, o: (l, o, 0, 0)),
            pl.BlockSpec((4, LANES, SCAN_W), lambda l, o: (0, 0, 0)),
            pl.BlockSpec((LANES, 4 * SCAN_W), lambda l, o: (0, 0)),
        ],
        out_specs=[
            pl.BlockSpec((None, G_OCT, SCAN_W, STATE_W), lambda l, o: (l, o, 0, 0)),
            pl.BlockSpec((None, G_OCT, SCAN_W + STATE_W, SCAN_W), lambda l, o: (l, o, 0, 0)),
            pl.BlockSpec((None, G_OCT, SSM_STATE, LANES), lambda l, o: (l, o, 0, 0)),
        ],
        out_shape=[
            jax.ShapeDtypeStruct((DEPTH, SSM_GROUPS, SCAN_W, STATE_W), BF16),
            jax.ShapeDtypeStruct((DEPTH, SSM_GROUPS, SCAN_W + STATE_W, SCAN_W), BF16),
            jax.ShapeDtypeStruct((DEPTH, SSM_GROUPS, SSM_STATE, LANES), F32),
        ],
        compiler_params=_cparams(("arbitrary", "arbitrary")),
        name="s5_prep",
    )(pk, rows, sel_pow, sel_tile)
    a16 = jnp.swapaxes(a_cols[..., 0:4], -1, -2).reshape(DEPTH, SSM_GROUPS, STATE_W)
    return w1, w2, a16


def _s5_kernel(u_ref, w1_ref, w2_ref, a_ref, h0_ref, y_ref, fs_ref,
               t_scr, ug_scr, vr_scr, vi_scr, cr_scr, ci_scr, fr_scr, fi_scr):
    blk = pl.program_id(1)
    seq_rows = jnp.where(blk == 0, SEQ // SCAN_T, DEC_SEQ // SCAN_T)

    for s in range(SCAN_T):
        t_scr[s] = u_ref[pl.ds(s, SCAN_BLK, stride=SCAN_T), :].T
    for g in range(G_OCT):
        stacked = t_scr[:, pl.ds(g * SSM_GROUP, SSM_GROUP), :].reshape(SCAN_W, SCAN_BLK)
        ug_scr[g] = stacked.T.astype(BF16)

    for g in range(G_OCT):
        v = jnp.dot(ug_scr[g], w1_ref[g], preferred_element_type=F32)
        vr_scr[pl.ds(g, SCAN_BLK, stride=G_OCT), :] = v[:, 0:HALF_W]
        vi_scr[pl.ds(g, SCAN_BLK, stride=G_OCT), :] = v[:, HALF_W:STATE_W]

    a_r = a_ref[:, 0:HALF_W]
    a_i = a_ref[:, HALF_W:STATE_W]
    h0_r = h0_ref[:, 0:HALF_W]
    h0_i = h0_ref[:, HALF_W:STATE_W]
    fwd_lanes = lax.broadcasted_iota(jnp.int32, (G_OCT, HALF_W), 1) < SSM_STATE
    bwd_lanes = jnp.logical_not(fwd_lanes)

    def step(k, carry):
        s_r, s_i = carry
        rf = pl.ds(pl.multiple_of(k * G_OCT, G_OCT), G_OCT)
        rb = pl.ds(pl.multiple_of((SCAN_BLK - 1 - k) * G_OCT, G_OCT), G_OCT)
        restart = (k & (seq_rows - 1)) == 0
        s_r = jnp.where(restart, h0_r, s_r)
        s_i = jnp.where(restart, h0_i, s_i)
        pltpu.store(cr_scr.at[rf, :], s_r, mask=fwd_lanes)
        pltpu.store(cr_scr.at[rb, :], s_r, mask=bwd_lanes)
        pltpu.store(ci_scr.at[rf, :], s_i, mask=fwd_lanes)
        pltpu.store(ci_scr.at[rb, :], s_i, mask=bwd_lanes)
        v_r = jnp.where(fwd_lanes, vr_scr[rf, :], vr_scr[rb, :])
        v_i = jnp.where(fwd_lanes, vi_scr[rf, :], vi_scr[rb, :])
        n_r = a_r * s_r - a_i * s_i + v_r
        n_i = a_r * s_i + a_i * s_r + v_i
        fr_scr[rf, :] = n_r
        fi_scr[rf, :] = n_i
        return n_r, n_i

    zero = jnp.zeros((G_OCT, HALF_W), F32)
    lax.fori_loop(0, SCAN_BLK, step, (zero, zero), unroll=4)

    for g in range(G_OCT):
        c_r = cr_scr[pl.ds(g, SCAN_BLK, stride=G_OCT), :].astype(BF16)
        c_i = ci_scr[pl.ds(g, SCAN_BLK, stride=G_OCT), :].astype(BF16)
        y = jnp.dot(ug_scr[g], w2_ref[g, 0:SCAN_W, :], preferred_element_type=F32)
        y = y + jnp.dot(c_r, w2_ref[g, SCAN_W:SCAN_W + HALF_W, :], preferred_element_type=F32)
        y = y + jnp.dot(c_i, w2_ref[g, SCAN_W + HALF_W:SCAN_W + STATE_W, :], preferred_element_type=F32)
        t_scr[:, pl.ds(g * SSM_GROUP, SSM_GROUP), :] = y.T.reshape(SCAN_T, SSM_GROUP, SCAN_BLK)
    for s in range(SCAN_T):
        y_ref[pl.ds(s, SCAN_BLK, stride=SCAN_T), :] = t_scr[s].T

    rows_per_seq = SEQ // SCAN_T
    for q in range(SCAN_BLK // rows_per_seq):
        last = pl.ds((q * rows_per_seq + rows_per_seq - 1) * G_OCT, G_OCT)
        fs_ref[q, :, 0:HALF_W] = fr_scr[last, :]
        fs_ref[q, :, HALF_W:STATE_W] = fi_scr[last, :]


def _s5_scan(l, u, w1, w2, a16, h0):
    n_oct = SSM_GROUPS // G_OCT
    n_fin = SCAN_BLK // (SEQ // SCAN_T)
    return pl.pallas_call(
        _s5_kernel,
        grid=(n_oct, N_SCAN_BLK),
        in_specs=[
            pl.BlockSpec((SCAN_TOK, LANES), lambda o, b: (b, o)),
            pl.BlockSpec((None, G_OCT, SCAN_W, STATE_W), lambda o, b: (l, o, 0, 0)),
            pl.BlockSpec((None, G_OCT, SCAN_W + STATE_W, SCAN_W), lambda o, b: (l, o, 0, 0)),
            pl.BlockSpec((None, G_OCT, STATE_W), lambda o, b: (l, o, 0)),
            pl.BlockSpec((None, None, G_OCT, STATE_W), lambda o, b: (l, b, o, 0)),
        ],
        out_specs=[
            pl.BlockSpec((SCAN_TOK, LANES), lambda o, b: (b, o)),
            pl.BlockSpec((None, n_fin, G_OCT, STATE_W), lambda o, b: (b, 0, o, 0)),
        ],
        out_shape=[
            jax.ShapeDtypeStruct((N_TOK, SSM_WIDTH), F32),
            jax.ShapeDtypeStruct((N_SCAN_BLK, n_fin, SSM_GROUPS, STATE_W), F32),
        ],
        scratch_shapes=[
            pltpu.VMEM((SCAN_T, LANES, SCAN_BLK), F32),
            pltpu.VMEM((G_OCT, SCAN_BLK, SCAN_W), BF16),
        ] + [pltpu.VMEM((SCAN_BLK * G_OCT, HALF_W), F32) for _ in range(6)],
        compiler_params=_cparams(("arbitrary", "arbitrary")),
        name="s5_scan",
    )(u, w1, w2, a16, h0)


MIX_TM = 512


def _mix_kernel(ys_ref, u_ref, gu_ref, vn_ref, d_ref, wglu_ref, bglu_ref, ws_ref, bs_ref,
                ya_ref, yb_ref):
    y = ys_ref[...] + d_ref[...] * u_ref[...]
    y = jax.nn.gelu(y)
    gate = jnp.dot(y.astype(BF16), wglu_ref[...], preferred_element_type=F32) + bglu_ref[...]
    ya_ref[...] = (y * jax.nn.sigmoid(gate)).astype(BF16)
    for c in range(MIX_TM // GM_CHUNK):
        rows = pl.ds(c * GM_CHUNK, GM_CHUNK)
        for g in range(GM_GROUPS):
            cols = pl.ds(g * GM_GROUP_DIM, GM_GROUP_DIM)
            mixed = jnp.dot(ws_ref[g], vn_ref[rows, cols], preferred_element_type=F32) + bs_ref[:, cols]
            yb_ref[rows, cols] = (gu_ref[rows, cols].astype(F32) * mixed).astype(BF16)


def _mix(l, ys, u, z, d_skip, w_glu_bf16, b_glu, w_s_bf16, b_s_full):
    tile = lambda k: pl.BlockSpec((MIX_TM, SSM_WIDTH), lambda i: (i, k))
    lay = lambda *shape: pl.BlockSpec((None,) + shape, lambda i: (l,) + tuple(0 for _ in shape))
    return pl.pallas_call(
        _mix_kernel,
        grid=(N_TOK // MIX_TM,),
        in_specs=[
            tile(0), tile(0), tile(0), tile(1),
            lay(1, SSM_WIDTH),
            lay(SSM_WIDTH, SSM_WIDTH),
            lay(1, SSM_WIDTH),
            lay(GM_GROUPS, GM_CHUNK, GM_CHUNK),
            lay(GM_CHUNK, GM_WIDTH),
        ],
        out_specs=[tile(0), tile(0)],
        out_shape=[
            jax.ShapeDtypeStruct((N_TOK, SSM_WIDTH), BF16),
            jax.ShapeDtypeStruct((N_TOK, GM_WIDTH), BF16),
        ],
        compiler_params=_cparams(("arbitrary",)),
        name="mixers",
    )(ys, u, z, z, d_skip.reshape(DEPTH, 1, SSM_WIDTH), w_glu_bf16, b_glu.reshape(DEPTH, 1, SSM_WIDTH),
      w_s_bf16, b_s_full)


MRG_TM = 512
MRG_SUB = 256
HALF_D = D_MODEL // 2
REC_W = HALF_D + LANES


def _merge_kernel(ya_ref, yb_ref, ga0_ref, ga1_ref, gb0_ref, gb1_ref, x_ref, g1_ref, sc_ref, sh_ref,
                  n2_ref, wpa_ref, wpb_ref, wo_ref, wr_ref, br_ref, tri_ref,
                  xmid_ref, rec_ref, gid_ref, rank_ref, cnt_ref, cnt_scr):
    i = pl.program_id(0)
    seg = i // (SEG_TOK // MRG_TM)

    @pl.when(i == 0)
    def _():
        cnt_scr[...] = jnp.zeros(cnt_scr.shape, F32)

    passes = [pl.ds(sub * MRG_SUB, MRG_SUB) for sub in range(MRG_TM // MRG_SUB)]
    for rows in passes:
        _merge_mix(rows, seg, ya_ref, yb_ref, ga0_ref, ga1_ref, gb0_ref, gb1_ref, x_ref, g1_ref,
                   wpa_ref, wpb_ref, wo_ref, xmid_ref)
    for rows in passes:
        _merge_route(rows, seg, sc_ref, sh_ref, n2_ref, wr_ref, br_ref, tri_ref,
                     xmid_ref, rec_ref, gid_ref, rank_ref, cnt_scr)
    cnt_ref[...] = jnp.broadcast_to(cnt_scr[...], cnt_ref.shape)


def _merge_mix(rows, seg, ya_ref, yb_ref, ga0_ref, ga1_ref, gb0_ref, gb1_ref, x_ref, g1_ref,
               wpa_ref, wpb_ref, wo_ref, xmid_ref):
    pa = jnp.dot(ya_ref[rows, :], wpa_ref[...], preferred_element_type=F32)
    pb = jnp.dot(yb_ref[rows, :], wpb_ref[...], preferred_element_type=F32)
    m_lo = ga0_ref[rows, :].astype(F32) * pa[:, :HALF_D] + gb0_ref[rows, :].astype(F32) * pb[:, :HALF_D]
    m_hi = ga1_ref[rows, :].astype(F32) * pa[:, HALF_D:] + gb1_ref[rows, :].astype(F32) * pb[:, HALF_D:]
    mix = jnp.dot(m_lo.astype(BF16), wo_ref[0:HALF_D, :], preferred_element_type=F32)
    mix = mix + jnp.dot(m_hi.astype(BF16), wo_ref[HALF_D:D_MODEL, :], preferred_element_type=F32)
    xmid_ref[rows, :] = x_ref[rows, :] + g1_ref[pl.ds(seg, 1), :] * mix


def _merge_route(rows, seg, sc_ref, sh_ref, n2_ref, wr_ref, br_ref, tri_ref,
                 xmid_ref, rec_ref, gid_ref, rank_ref, cnt_scr):
    x = xmid_ref[rows, :]
    ms = jnp.mean(x * x, axis=-1, keepdims=True)
    y = x * lax.rsqrt(ms + EPS) * n2_ref[...]
    h2 = y * (1.0 + sc_ref[pl.ds(seg, 1), :]) + sh_ref[pl.ds(seg, 1), :]
    hi = h2.astype(BF16)
    hi_f = hi.astype(F32)
    lo = (h2 - hi_f).astype(BF16)
    bits = lax.bitcast_convert_type(hi_f, jnp.uint32)
    rec_ref[rows, 0:HALF_D] = bits[:, :HALF_D] | (bits[:, HALF_D:] >> 16)

    nt = (((1,), (1,)), ((), ()))
    lt = (lax.dot_general(wr_ref[...], hi, nt, preferred_element_type=F32)
          + lax.dot_general(wr_ref[...], lo, nt, preferred_element_type=F32))
    logits = lt[0:N_EXPERTS] + lt[N_EXPERTS:2 * N_EXPERTS]
    scores = jax.nn.sigmoid(logits)
    sel = scores + br_ref[...]
    ng = N_EXPERT_GROUPS
    s = [sel[j * ng:(j + 1) * ng] for j in range(EXPERTS_PER_GROUP)]
    p = [scores[j * ng:(j + 1) * ng] for j in range(EXPERTS_PER_GROUP)]
    a, b = jnp.maximum(s[0], s[1]), jnp.minimum(s[0], s[1])
    c, d = jnp.maximum(s[2], s[3]), jnp.minimum(s[2], s[3])
    grp_score = jnp.maximum(a, c) + jnp.maximum(jnp.minimum(a, c), jnp.maximum(b, d))
    best = jnp.max(grp_score, axis=0, keepdims=True)
    g_iota = lax.broadcasted_iota(jnp.int32, grp_score.shape, 0)
    g_idx = jnp.min(jnp.where(grp_score == best, g_iota, ng), axis=0, keepdims=True)
    onehot = g_iota == g_idx
    v = [jnp.sum(jnp.where(onehot, sj, 0.0), axis=0, keepdims=True) for sj in s]
    q = [jnp.sum(jnp.where(onehot, pj, 0.0), axis=0, keepdims=True) for pj in p]
    picked = []
    for j in range(EXPERTS_PER_GROUP):
        rank = jnp.zeros(v[j].shape, jnp.int32)
        for o in range(EXPERTS_PER_GROUP):
            if o == j:
                continue
            ahead = (v[o] > v[j]) | ((v[o] == v[j]) & (o < j))
            rank = rank + ahead.astype(jnp.int32)
        picked.append(jnp.where(rank < 2, q[j], 0.0))
    total = picked[0] + picked[1] + picked[2] + picked[3]
    gid_ref[:, rows] = g_idx
    cw_rows = jnp.concatenate([pj / total for pj in picked]
                              + [jnp.zeros((LANES - EXPERTS_PER_GROUP, MRG_SUB), F32)], axis=0)
    rec_ref[rows, HALF_D:REC_W] = lax.bitcast_convert_type(cw_rows.T, jnp.uint32)

    hot = onehot.astype(BF16)
    within = jnp.dot(hot, tri_ref[...], preferred_element_type=F32)
    before = jnp.sum(jnp.where(onehot, within + cnt_scr[...], 0.0), axis=0, keepdims=True) - 1.0
    rank_ref[:, rows] = before.astype(jnp.int32)
    cnt_scr[...] = cnt_scr[...] + within[:, MRG_SUB - 1:MRG_SUB]


def _merge(l, ya, yb, z, xres, mod, norm2_g, w_pa, w_pb, w_o, wr_t, br_col):
    n_t = N_TOK // MRG_TM
    tri = (jnp.arange(MRG_SUB)[:, None] <= jnp.arange(MRG_SUB)[None, :]).astype(BF16)
    zspec = lambda k: pl.BlockSpec((MRG_TM, HALF_D), lambda i: (i, k))
    once = pl.Buffered(1)
    lay = lambda *shape: pl.BlockSpec((None,) + shape, lambda i: (l,) + tuple(0 for _ in shape),
                                      pipeline_mode=once)
    const = lambda *shape: pl.BlockSpec(shape, lambda i: tuple(0 for _ in shape), pipeline_mode=once)
    return pl.pallas_call(
        _merge_kernel,
        grid=(n_t,),
        in_specs=[
            pl.BlockSpec((MRG_TM, SSM_WIDTH), lambda i: (i, 0)),
            pl.BlockSpec((MRG_TM, GM_WIDTH), lambda i: (i, 0)),
            zspec(2), zspec(3), zspec(4), zspec(5),
            pl.BlockSpec((MRG_TM, D_MODEL), lambda i: (i, 0)),
            _mod_spec(l, 2, 1), _mod_spec(l, 4, 1), _mod_spec(l, 3, 1),
            lay(1, D_MODEL),
            lay(SSM_WIDTH, D_MODEL),
            lay(GM_WIDTH, D_MODEL),
            lay(D_MODEL, D_MODEL),
            const(2 * N_EXPERTS, D_MODEL),
            const(N_EXPERTS, 1),
            const(MRG_SUB, MRG_SUB),
        ],
        out_specs=[
            pl.BlockSpec((MRG_TM, D_MODEL), lambda i: (i, 0)),
            pl.BlockSpec((MRG_TM, REC_W), lambda i: (i, 0)),
            pl.BlockSpec((None, 1, MRG_TM), lambda i: (i, 0, 0)),
            pl.BlockSpec((None, 1, MRG_TM), lambda i: (i, 0, 0)),
            pl.BlockSpec((N_EXPERT_GROUPS, LANES), lambda i: (0, 0)),
        ],
        out_shape=[
            jax.ShapeDtypeStruct((N_TOK, D_MODEL), F32),
            jax.ShapeDtypeStruct((N_TOK, REC_W), jnp.uint32),
            jax.ShapeDtypeStruct((n_t, 1, MRG_TM), jnp.int32),
            jax.ShapeDtypeStruct((n_t, 1, MRG_TM), jnp.int32),
            jax.ShapeDtypeStruct((N_EXPERT_GROUPS, LANES), F32),
        ],
        scratch_shapes=[pltpu.VMEM((N_EXPERT_GROUPS, 1), F32)],
        compiler_params=_cparams(("arbitrary",)),
        name="merge_router",
    )(ya, yb, z, z, z, z, xres, mod, mod, mod, norm2_g.reshape(DEPTH, 1, D_MODEL),
      w_pa, w_pb, w_o, wr_t, br_col, tri)


DSP_TM = 512


def _dispatch_kernel(pos_ref, pend_ref, rec_ref, out_ref, zero_scr, sem):
    step = pl.program_id(0)

    @pl.when(step == 0)
    def _():
        zero_scr[...] = jnp.zeros(zero_scr.shape, jnp.uint32)
        for g in range(N_EXPERT_GROUPS):
            start = pl.multiple_of(jnp.maximum(pend_ref[g] - MOE_BLK, 0), MOE_BLK)
            fill = pltpu.make_async_copy(zero_scr, out_ref.at[pl.ds(start, MOE_BLK)], sem)
            fill.start()
            fill.wait()
        for blk in range(N_TOK // MOE_BLK, MOE_NBLK):
            @pl.when(blk * MOE_BLK >= pend_ref[N_EXPERT_GROUPS - 1])
            def _():
                fill = pltpu.make_async_copy(zero_scr, out_ref.at[pl.ds(blk * MOE_BLK, MOE_BLK)], sem)
                fill.start()
                fill.wait()

    base = step * DSP_TM

    def issue(r, carry):
        pltpu.make_async_copy(rec_ref.at[pl.ds(r, 1)], out_ref.at[pl.ds(pos_ref[base + r], 1)], sem).start()
        return carry

    lax.fori_loop(0, DSP_TM, issue, 0, unroll=8)
    pltpu.make_async_copy(rec_ref, out_ref.at[pl.ds(0, DSP_TM)], sem).wait()


def _dispatch(pos, pend, rec):
    grid_spec = pltpu.PrefetchScalarGridSpec(
        num_scalar_prefetch=2,
        grid=(N_TOK // DSP_TM,),
        in_specs=[pl.BlockSpec((DSP_TM, REC_W), lambda i, pos, pend: (i, 0))],
        out_specs=pl.BlockSpec(memory_space=pl.ANY),
        scratch_shapes=[pltpu.VMEM((MOE_BLK, REC_W), jnp.uint32), pltpu.SemaphoreType.DMA(())],
    )
    return pl.pallas_call(
        _dispatch_kernel,
        grid_spec=grid_spec,
        out_shape=jax.ShapeDtypeStruct((MOE_SLOTS, REC_W), jnp.uint32),
        compiler_params=_cparams(("arbitrary",)),
        name="moe_dispatch",
    )(pos, pend, rec)


UP_EXPERTS = 2


MOE_SUB = 256


def _expert_up_kernel(gid_ref, fill_ref, last_ref, rec_ref, wg_ref, wu_ref, h_ref, wg_scr, wu_scr):
    del last_ref
    half = pl.program_id(0)
    b = pl.program_id(1)
    prev = gid_ref[jnp.maximum(b - 1, 0)]
    fresh = (b == 0) | (gid_ref[b] != prev)

    @pl.when(fresh)
    def _():
        for e in range(UP_EXPERTS):
            wg_scr[e] = wg_ref[e].astype(BF16)
            wu_scr[e] = wu_ref[e].astype(BF16)

    for sub in range(MOE_BLK // MOE_SUB):
        rows = pl.ds(sub * MOE_SUB, MOE_SUB)

        @pl.when(fill_ref[b] > sub * MOE_SUB)
        def _():
            packed = rec_ref[rows, 0:HALF_D]
            x_lo = lax.bitcast_convert_type(packed & jnp.uint32(0xFFFF0000), F32).astype(BF16)
            x_hi = lax.bitcast_convert_type(packed << 16, F32).astype(BF16)
            cw = lax.bitcast_convert_type(rec_ref[rows, HALF_D:REC_W], F32)
            lane = lax.broadcasted_iota(jnp.int32, cw.shape, 1)
            for e in range(UP_EXPERTS):
                w_row = jnp.sum(jnp.where(lane == half * UP_EXPERTS + e, cw, 0.0), axis=1, keepdims=True)
                gate = (jnp.dot(x_lo, wg_scr[e, 0:HALF_D, :], preferred_element_type=F32)
                        + jnp.dot(x_hi, wg_scr[e, HALF_D:D_MODEL, :], preferred_element_type=F32))
                up = (jnp.dot(x_lo, wu_scr[e, 0:HALF_D, :], preferred_element_type=F32)
                      + jnp.dot(x_hi, wu_scr[e, HALF_D:D_MODEL, :], preferred_element_type=F32))
                h_ref[rows, pl.ds(e * EXPERT_FF, EXPERT_FF)] = (
                    gate * jax.nn.sigmoid(gate) * up * w_row).astype(BF16)

        @pl.when(fill_ref[b] <= sub * MOE_SUB)
        def _():
            h_ref[rows, :] = jnp.zeros((MOE_SUB, UP_EXPERTS * EXPERT_FF), BF16)


def _expert_up(l, blk_gid, blk_fill, blk_last, rec_sorted, e_gate, e_up):
    halves = EXPERTS_PER_GROUP // UP_EXPERTS
    wspec = pl.BlockSpec((None, UP_EXPERTS, D_MODEL, EXPERT_FF),
                         lambda h, b, gid, fill, last: (l, gid[b] * halves + h, 0, 0))
    grid_spec = pltpu.PrefetchScalarGridSpec(
        num_scalar_prefetch=3,
        grid=(halves, MOE_NBLK),
        in_specs=[
            pl.BlockSpec((MOE_BLK, REC_W), lambda h, b, gid, fill, last: (jnp.minimum(b, last[0]), 0)),
            wspec, wspec,
        ],
        out_specs=pl.BlockSpec((MOE_BLK, UP_EXPERTS * EXPERT_FF), lambda h, b, gid, fill, last: (b, h)),
        scratch_shapes=[pltpu.VMEM((UP_EXPERTS, D_MODEL, EXPERT_FF), BF16),
                        pltpu.VMEM((UP_EXPERTS, D_MODEL, EXPERT_FF), BF16)],
    )
    return pl.pallas_call(
        _expert_up_kernel,
        grid_spec=grid_spec,
        out_shape=jax.ShapeDtypeStruct((MOE_SLOTS, GROUP_FF), BF16),
        compiler_params=_cparams(("arbitrary", "arbitrary")),
        name="expert_up",
    )(blk_gid, blk_fill, blk_last, rec_sorted, e_gate, e_up)


DOWN_CH = 512


def _expert_down_kernel(gid_ref, fill_ref, h_ref, wd_ref, y_ref, wd_scr):
    b = pl.program_id(0)
    prev = gid_ref[jnp.maximum(b - 1, 0)]
    fresh = (b == 0) | (gid_ref[b] != prev)

    @pl.when(fresh)
    def _():
        for c in range(GROUP_FF // DOWN_CH):
            rows = pl.ds(c * DOWN_CH, DOWN_CH)
            wd_scr[rows, :] = wd_ref[rows, :].astype(BF16)

    for sub in range(MOE_BLK // MOE_SUB):
        rows = pl.ds(sub * MOE_SUB, MOE_SUB)

        @pl.when(fill_ref[b] > sub * MOE_SUB)
        def _():
            hid = h_ref[rows, :]
            for c in range(HALF_D // DOWN_CH):
                cols = pl.ds(c * DOWN_CH, DOWN_CH)
                pair = pl.ds(HALF_D + c * DOWN_CH, DOWN_CH)
                y_ref[rows, cols] = _pack_pairs(jnp.dot(hid, wd_scr[:, cols], preferred_element_type=F32),
                                                jnp.dot(hid, wd_scr[:, pair], preferred_element_type=F32))

        @pl.when(fill_ref[b] <= sub * MOE_SUB)
        def _():
            y_ref[rows, :] = jnp.zeros((MOE_SUB, HALF_D), jnp.uint32)


def _expert_down(l, blk_gid, blk_fill, h_sorted, e_down_grouped):
    grid_spec = pltpu.PrefetchScalarGridSpec(
        num_scalar_prefetch=2,
        grid=(MOE_NBLK,),
        in_specs=[
            pl.BlockSpec((MOE_BLK, GROUP_FF), lambda b, gid, fill: (b, 0)),
            pl.BlockSpec((None, None, GROUP_FF, D_MODEL), lambda b, gid, fill: (l, gid[b], 0, 0)),
        ],
        out_specs=pl.BlockSpec((MOE_BLK, HALF_D), lambda b, gid, fill: (b, 0)),
        scratch_shapes=[pltpu.VMEM((GROUP_FF, D_MODEL), BF16)],
    )
    return pl.pallas_call(
        _expert_down_kernel,
        grid_spec=grid_spec,
        out_shape=jax.ShapeDtypeStruct((MOE_SLOTS, HALF_D), jnp.uint32),
        compiler_params=_cparams(("arbitrary",)),
        name="expert_down",
    )(blk_gid, blk_fill, h_sorted, e_down_grouped)


def _moe(l, rec, gid, rank, counts, e_gate, e_up, e_down_grouped):
    padded = (counts + MOE_BLK - 1) // MOE_BLK * MOE_BLK
    pend = jnp.cumsum(padded)
    pstart = pend - padded
    pos = (pstart[gid] + rank).astype(jnp.int32)
    blk_start = jnp.arange(MOE_NBLK, dtype=jnp.int32) * MOE_BLK
    blk_gid = jnp.minimum(jnp.sum((blk_start[:, None] >= pend[None, :]).astype(jnp.int32), axis=1),
                          N_EXPERT_GROUPS - 1)
    blk_fill = jnp.clip(pstart[blk_gid] + counts[blk_gid] - blk_start, 0, MOE_BLK)
    blk_fill = jnp.where(blk_start < pend[-1], blk_fill, 0).astype(jnp.int32)
    blk_last = (pend[-1:] // MOE_BLK - 1).astype(jnp.int32)
    rec_sorted = _dispatch(pos, pend.astype(jnp.int32), rec)
    hid = _expert_up(l, blk_gid, blk_fill, blk_last, rec_sorted, e_gate, e_up)
    y_sorted = _expert_down(l, blk_gid, blk_fill, hid, e_down_grouped)
    return y_sorted[pos]


FIN_TM = 512
FIN_PROMPT_TILES = N_PROMPT // FIN_TM


def _final_kernel(x_ref, y_ref, g2_ref, fg_ref, op_ref, os_ref):
    i = pl.program_id(0)
    seg = i // (SEG_TOK // FIN_TM)
    x = x_ref[...] + g2_ref[pl.ds(seg, 1), :] * _unpack_pairs(y_ref[...])
    ms = jnp.mean(x * x, axis=-1, keepdims=True)
    out = x * lax.rsqrt(ms + EPS) * fg_ref[...]

    @pl.when(i < FIN_PROMPT_TILES)
    def _():
        op_ref[...] = out

    @pl.when(i >= FIN_PROMPT_TILES)
    def _():
        os_ref[...] = out


def _final_norm(xmid, moe_y, mod, final_g):
    return pl.pallas_call(
        _final_kernel,
        grid=(N_TOK // FIN_TM,),
        in_specs=[
            pl.BlockSpec((FIN_TM, D_MODEL), lambda i: (i, 0)),
            pl.BlockSpec((FIN_TM, HALF_D), lambda i: (i, 0)),
            _mod_spec(DEPTH - 1, 5, 1),
            pl.BlockSpec((1, D_MODEL), lambda i: (0, 0)),
        ],
        out_specs=[
            pl.BlockSpec((FIN_TM, D_MODEL), lambda i: (jnp.minimum(i, FIN_PROMPT_TILES - 1), 0)),
            pl.BlockSpec((FIN_TM, D_MODEL), lambda i: (jnp.maximum(i - FIN_PROMPT_TILES, 0), 0)),
        ],
        out_shape=[
            jax.ShapeDtypeStruct((N_PROMPT, D_MODEL), F32),
            jax.ShapeDtypeStruct((N_SAMPLE, D_MODEL), F32),
        ],
        compiler_params=_cparams(("arbitrary",)),
        name="final_norm",
    )(xmid, moe_y, mod, final_g.reshape(1, D_MODEL))


def _grid_pos_embed(rows):
    quarter = D_MODEL // 4
    freqs = 1.0 / (POS_BASE ** (jnp.arange(quarter, dtype=F32) / quarter))
    er = jnp.arange(rows, dtype=F32)[:, None] * freqs
    ec = jnp.arange(GRID_W, dtype=F32)[:, None] * freqs
    row_emb = jnp.concatenate([jnp.sin(er), jnp.cos(er)], axis=-1)
    col_emb = jnp.concatenate([jnp.sin(ec), jnp.cos(ec)], axis=-1)
    pe = jnp.concatenate([
        jnp.broadcast_to(row_emb[:, None, :], (rows, GRID_W, D_MODEL // 2)),
        jnp.broadcast_to(col_emb[None, :, :], (rows, GRID_W, D_MODEL // 2))], axis=-1)
    return pe.reshape(rows * GRID_W, D_MODEL)


def kernel(x_prompt, x_sample, state_ssm_re, state_ssm_im, c, c_ctx, norm1_g, norm2_g, w_mod, b_mod,
           w_in, ssm_lam_re, ssm_lam_im, ssm_log_step, ssm_b_re, ssm_b_im, ssm_c_re, ssm_c_im, ssm_d,
           w_glu, b_glu, gm_ln_g, gm_w_s, gm_b_s, w_pa, w_pb, w_o, w_router, b_router,
           e_gate, e_up, e_down, final_g):
    cvec = jnp.concatenate([c_ctx[None], c, jnp.zeros((MOD_ROWS - 1 - DEC_BATCH, D_MODEL), F32)], axis=0)
    mod = _modulation(cvec, w_mod, b_mod)

    perm = (jnp.arange(N_EXPERT_GROUPS)[None, :] * EXPERTS_PER_GROUP
            + jnp.arange(EXPERTS_PER_GROUP)[:, None]).reshape(N_EXPERTS)
    wr = w_router.astype(F32).T[perm]
    wr_hi = wr.astype(BF16)
    wr_lo = (wr - wr_hi.astype(F32)).astype(BF16)
    wr_t = jnp.concatenate([wr_hi, wr_lo], axis=0)
    br_col = b_router.astype(F32)[perm][:, None]

    w_in_b, w_glu_b, w_s_b = w_in.astype(BF16), w_glu.astype(BF16), gm_w_s.astype(BF16)
    w_pa_b, w_pb_b, w_o_b = w_pa.astype(BF16), w_pb.astype(BF16), w_o.astype(BF16)
    b_s_full = jnp.repeat(jnp.transpose(gm_b_s.astype(F32), (0, 2, 1)), GM_GROUP_DIM, axis=2)
    e_down_grouped = e_down.reshape(DEPTH, N_EXPERT_GROUPS, GROUP_FF, D_MODEL)
    w1, w2, a16 = _s5_prep(ssm_lam_re, ssm_lam_im, ssm_log_step, ssm_b_re, ssm_b_im, ssm_c_re, ssm_c_im)
    h0_lat = jnp.concatenate([state_ssm_re[:, :, 0], state_ssm_re[:, :, 1],
                              state_ssm_im[:, :, 0], state_ssm_im[:, :, 1]], axis=-1).astype(F32)
    h0 = jnp.concatenate([jnp.zeros((DEPTH, 1, SSM_GROUPS, STATE_W), F32),
                          jnp.transpose(h0_lat, (1, 0, 2, 3))], axis=1)

    inproj_in = (x_prompt.reshape(N_PROMPT, D_MODEL), x_sample.reshape(N_SAMPLE, D_MODEL),
                 _grid_pos_embed(DEC_SEQ // GRID_W))

    new_re, new_im = [], []
    xmid = moe_y = None
    for l in range(DEPTH):
        xres, u, z = _inproj(l, *inproj_in, mod, norm1_g, gm_ln_g, w_in_b)
        ys, fs = _s5_scan(l, u, w1, w2, a16, h0)
        fin = fs[0]
        p = SSM_STATE
        new_re.append(jnp.stack([fin[:, :, 0:p], fin[::-1, :, p:2 * p]], axis=1))
        new_im.append(jnp.stack([fin[:, :, 2 * p:3 * p], fin[::-1, :, 3 * p:4 * p]], axis=1))

        ya, yb = _mix(l, ys, u, z, ssm_d.astype(F32), w_glu_b, b_glu.astype(F32), w_s_b, b_s_full)
        xmid, rec, gid, rank, cnt = _merge(l, ya, yb, z, xres, mod, norm2_g, w_pa_b, w_pb_b, w_o_b,
                                           wr_t, br_col)
        moe_y = _moe(l, rec, gid.reshape(N_TOK), rank.reshape(N_TOK), cnt[:, 0].astype(jnp.int32),
                     e_gate, e_up, e_down_grouped)
        inproj_in = (xmid, moe_y, mod)

    y_prompt, y_sample = _final_norm(xmid, moe_y, mod, final_g)
    new_state_re = jnp.stack(new_re, axis=1).astype(x_prompt.dtype)
    new_state_im = jnp.stack(new_im, axis=1).astype(x_prompt.dtype)
    return (y_prompt.reshape(BATCH, SEQ, D_MODEL), y_sample.reshape(DEC_BATCH, DEC_SEQ, D_MODEL),
            new_state_re, new_state_im)
```

```python
import functools

import jax
import jax.numpy as jnp
from jax import lax
from jax.experimental import pallas as pl
from jax.experimental.pallas import tpu as pltpu

F32 = jnp.float32
BF16 = jnp.bfloat16
HIGHEST = lax.Precision.HIGHEST

D_MODEL = 2048
BATCH = 16
SEQ = 256
DEPTH = 2
DEC_BATCH = 2
DEC_SEQ = 4096
GRID_W = 64
POS_BASE = 10000.0
EPS = 1e-6
SSM_WIDTH = D_MODEL // 2
SSM_GROUP = 16
SSM_GROUPS = SSM_WIDTH // SSM_GROUP
SSM_STATE = 64
GM_WIDTH = D_MODEL // 2
GM_CHUNK = 128
GM_GROUPS = 8
GM_GROUP_DIM = GM_WIDTH // GM_GROUPS
IN_WIDTH = SSM_WIDTH + 2 * GM_WIDTH + 2 * D_MODEL
N_EXPERTS = 32
N_EXPERT_GROUPS = 8
EXPERTS_PER_GROUP = N_EXPERTS // N_EXPERT_GROUPS
EXPERT_FF = D_MODEL // 4
N_MOD = 6

N_PROMPT = BATCH * SEQ
N_SAMPLE = DEC_BATCH * DEC_SEQ
N_TOK = N_PROMPT + N_SAMPLE
SEG_TOK = 4096
MOD_ROWS = 8
LANES = 128
SUBLANES = 8

SCAN_T = 16
SCAN_W = SCAN_T * SSM_GROUP
SCAN_ROWS = N_TOK // SCAN_T
SCAN_BLK = 256
SCAN_TOK = SCAN_BLK * SCAN_T
N_SCAN_BLK = SCAN_ROWS // SCAN_BLK
G_OCT = SUBLANES
STATE_W = 4 * SSM_STATE
HALF_W = 2 * SSM_STATE

MOE_BLK = 512
MOE_SLOTS = N_TOK + N_EXPERT_GROUPS * MOE_BLK
MOE_NBLK = MOE_SLOTS // MOE_BLK
GROUP_FF = EXPERTS_PER_GROUP * EXPERT_FF

VMEM_LIMIT = 56 * 1024 * 1024


def _cparams(sem):
    return pltpu.CompilerParams(dimension_semantics=sem, vmem_limit_bytes=VMEM_LIMIT)


MOD_TN = 1024


def _mod_kernel(c_ref, w_ref, b_ref, o_ref):
    c = c_ref[...]
    s = c * jax.nn.sigmoid(c)
    s_hi = s.astype(BF16)
    s_lo = (s - s_hi.astype(F32)).astype(BF16)
    w = w_ref[...]
    w_hi = w.astype(BF16)
    w_lo = (w - w_hi.astype(F32)).astype(BF16)
    both = jnp.dot(jnp.concatenate([s_hi, s_lo], axis=0), w_hi, preferred_element_type=F32)
    cross = jnp.dot(s_hi, w_lo, preferred_element_type=F32)
    o_ref[...] = both[0:MOD_ROWS] + both[MOD_ROWS:2 * MOD_ROWS] + cross + b_ref[...]


def _modulation(cvec, w_mod, b_mod):
    width = N_MOD * D_MODEL
    return pl.pallas_call(
        _mod_kernel,
        grid=(DEPTH, width // MOD_TN),
        in_specs=[
            pl.BlockSpec((MOD_ROWS, D_MODEL), lambda l, n: (0, 0)),
            pl.BlockSpec((None, D_MODEL, MOD_TN), lambda l, n: (l, 0, n)),
            pl.BlockSpec((None, 1, MOD_TN), lambda l, n: (l, 0, n)),
        ],
        out_specs=pl.BlockSpec((None, MOD_ROWS, MOD_TN), lambda l, n: (l, 0, n)),
        out_shape=jax.ShapeDtypeStruct((DEPTH, MOD_ROWS, width), F32),
        compiler_params=_cparams(("arbitrary", "arbitrary")),
        name="adaln_mod",
    )(cvec, w_mod, b_mod.reshape(DEPTH, 1, width))


def _pack_pairs(a, b):
    hi = lax.bitcast_convert_type(a.astype(BF16).astype(F32), jnp.uint32)
    lo = lax.bitcast_convert_type(b.astype(BF16).astype(F32), jnp.uint32)
    return hi | (lo >> 16)


def _unpack_pairs(packed):
    hi = lax.bitcast_convert_type(packed & jnp.uint32(0xFFFF0000), F32)
    lo = lax.bitcast_convert_type(packed << 16, F32)
    return jnp.concatenate([hi, lo], axis=-1)


def _mod_spec(l, k, nargs):
    if nargs == 1:
        return pl.BlockSpec((None, MOD_ROWS, D_MODEL), lambda i: (l, 0, k))
    return pl.BlockSpec((None, MOD_ROWS, D_MODEL), lambda i, j: (l, 0, k))


INP_TM = 256
INP_CH = 256
INP_PROMPT_TILES = N_PROMPT // INP_TM
Z_WIDTH = IN_WIDTH - SSM_WIDTH
INP_VMEM_LIMIT = 60 * 1024 * 1024


def _inproj_kernel(*refs, first):
    if first:
        xa_ref, xb_ref, add_ref = refs[:3]
    else:
        xa_ref, add_ref, gain_ref = refs[:3]
    sc_ref, sh_ref, g_ref, ln_ref, w_ref, xres_ref, u_ref, z_ref, h_scr, v_scr = refs[3:]
    i = pl.program_id(0)
    seg = i // (SEG_TOK // INP_TM)

    if first:
        latent = i >= INP_PROMPT_TILES
        rows_per_tile = INP_TM // GRID_W
        row0 = (i % (DEC_SEQ // INP_TM)) * rows_per_tile
        row_part = jnp.concatenate(
            [jnp.broadcast_to(add_ref[0, pl.ds(row0 + q, 1), :], (GRID_W, D_MODEL // 2))
             for q in range(rows_per_tile)], axis=0)
        col_part = jnp.concatenate([add_ref[1]] * rows_per_tile, axis=0)
        x = jnp.where(latent, xb_ref[...] + jnp.concatenate([row_part, col_part], axis=1), xa_ref[...])
    else:
        x = xa_ref[...] + gain_ref[pl.ds(seg, 1), :] * _unpack_pairs(add_ref[...])
    xres_ref[...] = x
    ms = jnp.mean(x * x, axis=-1, keepdims=True)
    y = x * lax.rsqrt(ms + EPS) * g_ref[...]
    h = y * (1.0 + sc_ref[pl.ds(seg, 1), :]) + sh_ref[pl.ds(seg, 1), :]
    h_scr[...] = h.astype(BF16)

    def proj(col):
        return jnp.dot(h_scr[...], w_ref[:, pl.ds(col, INP_CH)], preferred_element_type=F32)

    n_ch = SSM_WIDTH // INP_CH
    for c in range(n_ch):
        u_ref[:, pl.ds(c * INP_CH, INP_CH)] = proj(c * INP_CH)
    for c in range(n_ch):
        z_ref[:, pl.ds(c * INP_CH, INP_CH)] = jax.nn.gelu(proj(SSM_WIDTH + c * INP_CH)).astype(BF16)
    row_sum = jnp.zeros((INP_TM, 1), F32)
    for c in range(n_ch):
        v = jax.nn.gelu(proj(SSM_WIDTH + GM_WIDTH + c * INP_CH))
        v_scr[:, pl.ds(c * INP_CH, INP_CH)] = v
        row_sum = row_sum + jnp.sum(v, axis=-1, keepdims=True)
    mu = row_sum * (1.0 / GM_WIDTH)
    dev = v_scr[...] - mu
    var = jnp.mean(jnp.square(dev), axis=-1, keepdims=True)
    z_ref[:, pl.ds(GM_WIDTH, GM_WIDTH)] = (dev * lax.rsqrt(var + EPS) * ln_ref[...]).astype(BF16)
    gates = SSM_WIDTH + 2 * GM_WIDTH
    for c in range(2 * D_MODEL // INP_CH):
        z_ref[:, pl.ds(2 * GM_WIDTH + c * INP_CH, INP_CH)] = jax.nn.sigmoid(
            proj(gates + c * INP_CH)).astype(BF16)


def _inproj(l, xa, xb_or_add, add_or_gain, mod, norm1_g, gm_ln_g, w_in_bf16):
    first = l == 0
    row_tile = lambda m: pl.BlockSpec((INP_TM, D_MODEL), m)
    if first:
        lead = [row_tile(lambda i: (jnp.minimum(i, INP_PROMPT_TILES - 1), 0)),
                row_tile(lambda i: (jnp.maximum(i - INP_PROMPT_TILES, 0), 0)),
                pl.BlockSpec((2, DEC_SEQ // GRID_W, D_MODEL // 2), lambda i: (0, 0, 0))]
    else:
        lead = [row_tile(lambda i: (i, 0)), pl.BlockSpec((INP_TM, HALF_D), lambda i: (i, 0)),
                _mod_spec(l - 1, 5, 1)]
    vec = lambda w: pl.BlockSpec((None, 1, w), lambda i: (l, 0, 0))
    return pl.pallas_call(
        functools.partial(_inproj_kernel, first=first),
        grid=(N_TOK // INP_TM,),
        in_specs=lead + [
            _mod_spec(l, 1, 1), _mod_spec(l, 0, 1),
            vec(D_MODEL), vec(GM_WIDTH),
            pl.BlockSpec((None, D_MODEL, IN_WIDTH), lambda i: (l, 0, 0), pipeline_mode=pl.Buffered(1)),
        ],
        out_specs=[
            pl.BlockSpec((INP_TM, D_MODEL), lambda i: (i, 0)),
            pl.BlockSpec((INP_TM, SSM_WIDTH), lambda i: (i, 0)),
            pl.BlockSpec((INP_TM, Z_WIDTH), lambda i: (i, 0)),
        ],
        out_shape=[
            jax.ShapeDtypeStruct((N_TOK, D_MODEL), F32),
            jax.ShapeDtypeStruct((N_TOK, SSM_WIDTH), F32),
            jax.ShapeDtypeStruct((N_TOK, Z_WIDTH), BF16),
        ],
        scratch_shapes=[pltpu.VMEM((INP_TM, D_MODEL), BF16), pltpu.VMEM((INP_TM, GM_WIDTH), F32)],
        compiler_params=pltpu.CompilerParams(dimension_semantics=("arbitrary",),
                                             vmem_limit_bytes=INP_VMEM_LIMIT),
        name="in_proj",
    )(xa, xb_or_add, add_or_gain, mod, mod, norm1_g.reshape(DEPTH, 1, D_MODEL),
      gm_ln_g.reshape(DEPTH, 1, GM_WIDTH), w_in_bf16)


PK_BRE, PK_BIM, PK_CRE, PK_CIM = 0, 16, 32, 48
PK_LR, PK_LI = 64, 65


def _split_bf16(x):
    hi = x.astype(BF16)
    return hi, (x - hi.astype(F32)).astype(BF16)


POW_ROWS = 24


def _prep_kernel(pk_ref, row_ref, pow_ref, tile_ref, w1_ref, w2_ref, a_ref):
    p = SSM_STATE
    k_sub = jnp.minimum(lax.broadcasted_iota(jnp.int32, (POW_ROWS, LANES), 0), SCAN_T).astype(F32)
    lane = lax.broadcasted_iota(jnp.int32, (p, LANES), 1)
    col = lax.broadcasted_iota(jnp.int32, (SSM_GROUP, SCAN_W), 1)

    def spread_pow(x_r, x_i, which):
        parts = jnp.concatenate(_split_bf16(x_r) + _split_bf16(x_i), axis=0)
        out = jnp.dot(parts, pow_ref[which], preferred_element_type=F32)
        return out[0:p] + out[p:2 * p], out[2 * p:3 * p] + out[3 * p:4 * p]

    def spread_tiles(x):
        out = jnp.dot(jnp.concatenate(_split_bf16(x), axis=0), tile_ref[...], preferred_element_type=F32)
        out = out[0:p] + out[p:2 * p]
        return [out[:, n * SCAN_W:(n + 1) * SCAN_W] for n in range(4)]

    def group(g, _):
        rows = row_ref[g]
        grow_r = rows[0:1] * rows[2:3]
        grow_i = rows[1:2] * rows[2:3]
        mag = jnp.exp(grow_r * k_sub)
        ang = grow_i * k_sub
        unused = jnp.zeros((LANES - POW_ROWS, LANES), F32)
        pw_t_r = jnp.concatenate([mag * jnp.cos(ang), unused], axis=0).T
        pw_t_i = jnp.concatenate([mag * jnp.sin(ang), unused], axis=0).T
        per_dir = []
        for d in range(2):
            pk = pk_ref[d, g]
            lr = pk[:, PK_LR:PK_LR + 1]
            li = pk[:, PK_LI:PK_LI + 1]
            p_r = pw_t_r[d * p:(d + 1) * p]
            p_i = pw_t_i[d * p:(d + 1) * p]
            a_r = p_r[:, 1:2]
            a_i = p_i[:, 1:2]
            den = lr * lr + li * li
            q_r = ((a_r - 1.0) * lr + a_i * li) / den
            q_i = (a_i * lr - (a_r - 1.0) * li) / den
            per_dir.append((pk, p_r, p_i, q_r, q_i))

        w1_rows, lag, carry = [], [], []
        for d in range(2):
            pk, p_r, p_i, q_r, q_i = per_dir[d]
            b_r, b_i, c_r, c_i = spread_tiles(pk)
            bb_r = q_r * b_r - q_i * b_i
            bb_i = q_r * b_i + q_i * b_r
            pw_r, pw_i = spread_pow(p_r, p_i, 1 if d == 0 else 0)
            w1_rows.append((pw_r * bb_r - pw_i * bb_i, pw_r * bb_i + pw_i * bb_r))
            pl_r, pl_i = spread_pow(p_r, p_i, 0 if d == 0 else 1)
            cl_r = c_r * pl_r - c_i * pl_i
            cl_i = c_r * pl_i + c_i * pl_r
            pk_im = pltpu.roll(pk, LANES - (PK_BIM - PK_BRE), 1)
            bt_r = (q_r * pk - q_i * pk_im).T[0:SSM_GROUP, :]
            bt_i = (q_r * pk_im + q_i * pk).T[0:SSM_GROUP, :]
            lag.append(jnp.dot(bt_r, cl_r, precision=HIGHEST, preferred_element_type=F32)
                       - jnp.dot(bt_i, cl_i, precision=HIGHEST, preferred_element_type=F32))
            pc_r, pc_i = spread_pow(p_r, p_i, 2 if d == 0 else 3)
            carry.append((c_r * pc_r - c_i * pc_i, -(c_r * pc_i + c_i * pc_r)))

        (f_re, f_im), (b_re, b_im) = w1_rows
        w1_ref[g] = jnp.concatenate([f_re, b_re, f_im, b_im], axis=0).T.astype(BF16)

        for s in range(SCAN_T):
            fwd = lag[0] if s == 0 else pltpu.roll(lag[0], SSM_GROUP * s, 1)
            fwd = jnp.where(col >= SSM_GROUP * s, fwd, 0.0)
            shift_b = SSM_GROUP * (SCAN_T - 1 - s)
            bwd = lag[1] if shift_b == 0 else pltpu.roll(lag[1], SCAN_W - shift_b, 1)
            bwd = jnp.where(col < SSM_GROUP * (s + 1), bwd, 0.0)
            w2_ref[g, pl.ds(SSM_GROUP * s, SSM_GROUP), :] = (fwd + bwd).astype(BF16)
        (x_re, x_im), (y_re, y_im) = carry
        for n, rows in enumerate((x_re, y_re, x_im, y_im)):
            w2_ref[g, pl.ds(SCAN_W + SSM_STATE * n, SSM_STATE), :] = rows.astype(BF16)

        cols = [per_dir[0][1], per_dir[1][1], per_dir[0][2], per_dir[1][2]]
        a_cols = jnp.zeros((SSM_STATE, LANES), F32)
        for n, c in enumerate(cols):
            a_cols = jnp.where(lane == n, c[:, SCAN_T:SCAN_T + 1], a_cols)
        a_ref[g] = a_cols
        return 0

    lax.fori_loop(0, G_OCT, group, 0, unroll=4)


def _s5_prep(lam_re, lam_im, log_step, b_re, b_im, c_re, c_im):
    shape = (DEPTH, 2, SSM_GROUPS, SSM_STATE)
    lr = lam_re.astype(F32)
    li = lam_im.astype(F32)
    dt = jnp.broadcast_to(jnp.exp(log_step.astype(F32))[..., None], shape)
    pk = jnp.concatenate([
        b_re.astype(F32), b_im.astype(F32),
        jnp.swapaxes(c_re.astype(F32), -1, -2), jnp.swapaxes(c_im.astype(F32), -1, -2),
        lr[..., None], li[..., None],
        jnp.zeros(shape + (LANES - PK_LI - 1,), F32)], axis=-1)
    both_dirs = lambda a: jnp.concatenate([a[:, 0], a[:, 1]], axis=-1)
    rows = jnp.stack([both_dirs(lr), both_dirs(li), both_dirs(dt)], axis=2)
    rows = jnp.concatenate([rows, jnp.zeros((DEPTH, SSM_GROUPS, SUBLANES - 3, LANES), F32)], axis=2)

    blk = jnp.arange(SCAN_W) // SSM_GROUP
    k = jnp.arange(LANES)[:, None]
    pows = [k == blk[None, :], k == (SCAN_T - 1 - blk)[None, :], k == (blk + 1)[None, :],
            k == (SCAN_T - blk)[None, :]]
    h = (jnp.arange(SCAN_W) % SSM_GROUP)[None, :]
    sel_pow = jnp.stack(pows).astype(BF16)
    sel_tile = jnp.concatenate([k == h + off for off in (PK_BRE, PK_BIM, PK_CRE, PK_CIM)],
                               axis=1).astype(BF16)

    n_oct = SSM_GROUPS // G_OCT
    w1, w2, a_cols = pl.pallas_call(
        _prep_kernel,
        grid=(DEPTH, n_oct),
        in_specs=[
            pl.BlockSpec((None, 2, G_OCT, SSM_STATE, LANES), lambda l, o: (l, 0, o, 0, 0)),
            pl.BlockSpec((None, G_OCT, SUBLANES, LANES), lambda l, o: (l, o, 0, 0)),
            pl.BlockSpec((4, LANES, SCAN_W), lambda l, o: (0, 0, 0)),
            pl.BlockSpec((LANES, 4 * SCAN_W), lambda l, o: (0, 0)),
        ],
        out_specs=[
            pl.BlockSpec((None, G_OCT, SCAN_W, STATE_W), lambda l, o: (l, o, 0, 0)),
            pl.BlockSpec((None, G_OCT, SCAN_W + STATE_W, SCAN_W), lambda l, o: (l, o, 0, 0)),
            pl.BlockSpec((None, G_OCT, SSM_STATE, LANES), lambda l, o: (l, o, 0, 0)),
        ],
        out_shape=[
            jax.ShapeDtypeStruct((DEPTH, SSM_GROUPS, SCAN_W, STATE_W), BF16),
            jax.ShapeDtypeStruct((DEPTH, SSM_GROUPS, SCAN_W + STATE_W, SCAN_W), BF16),
            jax.ShapeDtypeStruct((DEPTH, SSM_GROUPS, SSM_STATE, LANES), F32),
        ],
        compiler_params=_cparams(("arbitrary", "arbitrary")),
        name="s5_prep",
    )(pk, rows, sel_pow, sel_tile)
    a16 = jnp.swapaxes(a_cols[..., 0:4], -1, -2).reshape(DEPTH, SSM_GROUPS, STATE_W)
    return w1, w2, a16


def _s5_kernel(u_ref, w1_ref, w2_ref, a_ref, h0_ref, y_ref, fs_ref,
               t_scr, ug_scr, vr_scr, vi_scr, cr_scr, ci_scr, fr_scr, fi_scr):
    blk = pl.program_id(1)
    seq_rows = jnp.where(blk == 0, SEQ // SCAN_T, DEC_SEQ // SCAN_T)

    for s in range(SCAN_T):
        t_scr[s] = u_ref[pl.ds(s, SCAN_BLK, stride=SCAN_T), :].T
    for g in range(G_OCT):
        stacked = t_scr[:, pl.ds(g * SSM_GROUP, SSM_GROUP), :].reshape(SCAN_W, SCAN_BLK)
        ug_scr[g] = stacked.T.astype(BF16)

    for g in range(G_OCT):
        v = jnp.dot(ug_scr[g], w1_ref[g], preferred_element_type=F32)
        vr_scr[pl.ds(g, SCAN_BLK, stride=G_OCT), :] = v[:, 0:HALF_W]
        vi_scr[pl.ds(g, SCAN_BLK, stride=G_OCT), :] = v[:, HALF_W:STATE_W]

    a_r = a_ref[:, 0:HALF_W]
    a_i = a_ref[:, HALF_W:STATE_W]
    h0_r = h0_ref[:, 0:HALF_W]
    h0_i = h0_ref[:, HALF_W:STATE_W]
    fwd_lanes = lax.broadcasted_iota(jnp.int32, (G_OCT, HALF_W), 1) < SSM_STATE
    bwd_lanes = jnp.logical_not(fwd_lanes)

    def step(k, carry):
        s_r, s_i = carry
        rf = pl.ds(pl.multiple_of(k * G_OCT, G_OCT), G_OCT)
        rb = pl.ds(pl.multiple_of((SCAN_BLK - 1 - k) * G_OCT, G_OCT), G_OCT)
        restart = (k & (seq_rows - 1)) == 0
        s_r = jnp.where(restart, h0_r, s_r)
        s_i = jnp.where(restart, h0_i, s_i)
        pltpu.store(cr_scr.at[rf, :], s_r, mask=fwd_lanes)
        pltpu.store(cr_scr.at[rb, :], s_r, mask=bwd_lanes)
        pltpu.store(ci_scr.at[rf, :], s_i, mask=fwd_lanes)
        pltpu.store(ci_scr.at[rb, :], s_i, mask=bwd_lanes)
        v_r = jnp.where(fwd_lanes, vr_scr[rf, :], vr_scr[rb, :])
        v_i = jnp.where(fwd_lanes, vi_scr[rf, :], vi_scr[rb, :])
        n_r = a_r * s_r - a_i * s_i + v_r
        n_i = a_r * s_i + a_i * s_r + v_i
        fr_scr[rf, :] = n_r
        fi_scr[rf, :] = n_i
        return n_r, n_i

    zero = jnp.zeros((G_OCT, HALF_W), F32)
    lax.fori_loop(0, SCAN_BLK, step, (zero, zero), unroll=4)

    for g in range(G_OCT):
        c_r = cr_scr[pl.ds(g, SCAN_BLK, stride=G_OCT), :].astype(BF16)
        c_i = ci_scr[pl.ds(g, SCAN_BLK, stride=G_OCT), :].astype(BF16)
        y = jnp.dot(ug_scr[g], w2_ref[g, 0:SCAN_W, :], preferred_element_type=F32)
        y = y + jnp.dot(c_r, w2_ref[g, SCAN_W:SCAN_W + HALF_W, :], preferred_element_type=F32)
        y = y + jnp.dot(c_i, w2_ref[g, SCAN_W + HALF_W:SCAN_W + STATE_W, :], preferred_element_type=F32)
        t_scr[:, pl.ds(g * SSM_GROUP, SSM_GROUP), :] = y.T.reshape(SCAN_T, SSM_GROUP, SCAN_BLK)
    for s in range(SCAN_T):
        y_ref[pl.ds(s, SCAN_BLK, stride=SCAN_T), :] = t_scr[s].T

    rows_per_seq = SEQ // SCAN_T
    for q in range(SCAN_BLK // rows_per_seq):
        last = pl.ds((q * rows_per_seq + rows_per_seq - 1) * G_OCT, G_OCT)
        fs_ref[q, :, 0:HALF_W] = fr_scr[last, :]
        fs_ref[q, :, HALF_W:STATE_W] = fi_scr[last, :]


def _s5_scan(l, u, w1, w2, a16, h0):
    n_oct = SSM_GROUPS // G_OCT
    n_fin = SCAN_BLK // (SEQ // SCAN_T)
    return pl.pallas_call(
        _s5_kernel,
        grid=(n_oct, N_SCAN_BLK),
        in_specs=[
            pl.BlockSpec((SCAN_TOK, LANES), lambda o, b: (b, o)),
            pl.BlockSpec((None, G_OCT, SCAN_W, STATE_W), lambda o, b: (l, o, 0, 0)),
            pl.BlockSpec((None, G_OCT, SCAN_W + STATE_W, SCAN_W), lambda o, b: (l, o, 0, 0)),
            pl.BlockSpec((None, G_OCT, STATE_W), lambda o, b: (l, o, 0)),
            pl.BlockSpec((None, None, G_OCT, STATE_W), lambda o, b: (l, b, o, 0)),
        ],
        out_specs=[
            pl.BlockSpec((SCAN_TOK, LANES), lambda o, b: (b, o)),
            pl.BlockSpec((None, n_fin, G_OCT, STATE_W), lambda o, b: (b, 0, o, 0)),
        ],
        out_shape=[
            jax.ShapeDtypeStruct((N_TOK, SSM_WIDTH), F32),
            jax.ShapeDtypeStruct((N_SCAN_BLK, n_fin, SSM_GROUPS, STATE_W), F32),
        ],
        scratch_shapes=[
            pltpu.VMEM((SCAN_T, LANES, SCAN_BLK), F32),
            pltpu.VMEM((G_OCT, SCAN_BLK, SCAN_W), BF16),
        ] + [pltpu.VMEM((SCAN_BLK * G_OCT, HALF_W), F32) for _ in range(6)],
        compiler_params=_cparams(("arbitrary", "arbitrary")),
        name="s5_scan",
    )(u, w1, w2, a16, h0)


MIX_TM = 512


def _mix_kernel(ys_ref, u_ref, gu_ref, vn_ref, d_ref, wglu_ref, bglu_ref, ws_ref, bs_ref,
                ya_ref, yb_ref):
    y = ys_ref[...] + d_ref[...] * u_ref[...]
    y = jax.nn.gelu(y)
    gate = jnp.dot(y.astype(BF16), wglu_ref[...], preferred_element_type=F32) + bglu_ref[...]
    ya_ref[...] = (y * jax.nn.sigmoid(gate)).astype(BF16)
    for c in range(MIX_TM // GM_CHUNK):
        rows = pl.ds(c * GM_CHUNK, GM_CHUNK)
        for g in range(GM_GROUPS):
            cols = pl.ds(g * GM_GROUP_DIM, GM_GROUP_DIM)
            mixed = jnp.dot(ws_ref[g], vn_ref[rows, cols], preferred_element_type=F32) + bs_ref[:, cols]
            yb_ref[rows, cols] = (gu_ref[rows, cols].astype(F32) * mixed).astype(BF16)


def _mix(l, ys, u, z, d_skip, w_glu_bf16, b_glu, w_s_bf16, b_s_full):
    tile = lambda k: pl.BlockSpec((MIX_TM, SSM_WIDTH), lambda i: (i, k))
    lay = lambda *shape: pl.BlockSpec((None,) + shape, lambda i: (l,) + tuple(0 for _ in shape))
    return pl.pallas_call(
        _mix_kernel,
        grid=(N_TOK // MIX_TM,),
        in_specs=[
            tile(0), tile(0), tile(0), tile(1),
            lay(1, SSM_WIDTH),
            lay(SSM_WIDTH, SSM_WIDTH),
            lay(1, SSM_WIDTH),
            lay(GM_GROUPS, GM_CHUNK, GM_CHUNK),
            lay(GM_CHUNK, GM_WIDTH),
        ],
        out_specs=[tile(0), tile(0)],
        out_shape=[
            jax.ShapeDtypeStruct((N_TOK, SSM_WIDTH), BF16),
            jax.ShapeDtypeStruct((N_TOK, GM_WIDTH), BF16),
        ],
        compiler_params=_cparams(("arbitrary",)),
        name="mixers",
    )(ys, u, z, z, d_skip.reshape(DEPTH, 1, SSM_WIDTH), w_glu_bf16, b_glu.reshape(DEPTH, 1, SSM_WIDTH),
      w_s_bf16, b_s_full)


MRG_TM = 512
MRG_SUB = 256
HALF_D = D_MODEL // 2
REC_W = HALF_D + LANES


def _merge_kernel(ya_ref, yb_ref, ga0_ref, ga1_ref, gb0_ref, gb1_ref, x_ref, g1_ref, sc_ref, sh_ref,
                  n2_ref, wpa_ref, wpb_ref, wo_ref, wr_ref, br_ref, tri_ref,
                  xmid_ref, rec_ref, gid_ref, rank_ref, cnt_ref, cnt_scr):
    i = pl.program_id(0)
    seg = i // (SEG_TOK // MRG_TM)

    @pl.when(i == 0)
    def _():
        cnt_scr[...] = jnp.zeros(cnt_scr.shape, F32)

    passes = [pl.ds(sub * MRG_SUB, MRG_SUB) for sub in range(MRG_TM // MRG_SUB)]
    for rows in passes:
        _merge_mix(rows, seg, ya_ref, yb_ref, ga0_ref, ga1_ref, gb0_ref, gb1_ref, x_ref, g1_ref,
                   wpa_ref, wpb_ref, wo_ref, xmid_ref)
    for rows in passes:
        _merge_route(rows, seg, sc_ref, sh_ref, n2_ref, wr_ref, br_ref, tri_ref,
                     xmid_ref, rec_ref, gid_ref, rank_ref, cnt_scr)
    cnt_ref[...] = jnp.broadcast_to(cnt_scr[...], cnt_ref.shape)


def _merge_mix(rows, seg, ya_ref, yb_ref, ga0_ref, ga1_ref, gb0_ref, gb1_ref, x_ref, g1_ref,
               wpa_ref, wpb_ref, wo_ref, xmid_ref):
    pa = jnp.dot(ya_ref[rows, :], wpa_ref[...], preferred_element_type=F32)
    pb = jnp.dot(yb_ref[rows, :], wpb_ref[...], preferred_element_type=F32)
    m_lo = ga0_ref[rows, :].astype(F32) * pa[:, :HALF_D] + gb0_ref[rows, :].astype(F32) * pb[:, :HALF_D]
    m_hi = ga1_ref[rows, :].astype(F32) * pa[:, HALF_D:] + gb1_ref[rows, :].astype(F32) * pb[:, HALF_D:]
    mix = jnp.dot(m_lo.astype(BF16), wo_ref[0:HALF_D, :], preferred_element_type=F32)
    mix = mix + jnp.dot(m_hi.astype(BF16), wo_ref[HALF_D:D_MODEL, :], preferred_element_type=F32)
    xmid_ref[rows, :] = x_ref[rows, :] + g1_ref[pl.ds(seg, 1), :] * mix


def _merge_route(rows, seg, sc_ref, sh_ref, n2_ref, wr_ref, br_ref, tri_ref,
                 xmid_ref, rec_ref, gid_ref, rank_ref, cnt_scr):
    x = xmid_ref[rows, :]
    ms = jnp.mean(x * x, axis=-1, keepdims=True)
    y = x * lax.rsqrt(ms + EPS) * n2_ref[...]
    h2 = y * (1.0 + sc_ref[pl.ds(seg, 1), :]) + sh_ref[pl.ds(seg, 1), :]
    hi = h2.astype(BF16)
    hi_f = hi.astype(F32)
    lo = (h2 - hi_f).astype(BF16)
    bits = lax.bitcast_convert_type(hi_f, jnp.uint32)
    rec_ref[rows, 0:HALF_D] = bits[:, :HALF_D] | (bits[:, HALF_D:] >> 16)

    nt = (((1,), (1,)), ((), ()))
    lt = (lax.dot_general(wr_ref[...], hi, nt, preferred_element_type=F32)
          + lax.dot_general(wr_ref[...], lo, nt, preferred_element_type=F32))
    logits = lt[0:N_EXPERTS] + lt[N_EXPERTS:2 * N_EXPERTS]
    scores = jax.nn.sigmoid(logits)
    sel = scores + br_ref[...]
    ng = N_EXPERT_GROUPS
    s = [sel[j * ng:(j + 1) * ng] for j in range(EXPERTS_PER_GROUP)]
    p = [scores[j * ng:(j + 1) * ng] for j in range(EXPERTS_PER_GROUP)]
    a, b = jnp.maximum(s[0], s[1]), jnp.minimum(s[0], s[1])
    c, d = jnp.maximum(s[2], s[3]), jnp.minimum(s[2], s[3])
    grp_score = jnp.maximum(a, c) + jnp.maximum(jnp.minimum(a, c), jnp.maximum(b, d))
    best = jnp.max(grp_score, axis=0, keepdims=True)
    g_iota = lax.broadcasted_iota(jnp.int32, grp_score.shape, 0)
    g_idx = jnp.min(jnp.where(grp_score == best, g_iota, ng), axis=0, keepdims=True)
    onehot = g_iota == g_idx
    v = [jnp.sum(jnp.where(onehot, sj, 0.0), axis=0, keepdims=True) for sj in s]
    q = [jnp.sum(jnp.where(onehot, pj, 0.0), axis=0, keepdims=True) for pj in p]
    picked = []
    for j in range(EXPERTS_PER_GROUP):
        rank = jnp.zeros(v[j].shape, jnp.int32)
        for o in range(EXPERTS_PER_GROUP):
            if o == j:
                continue
            ahead = (v[o] > v[j]) | ((v[o] == v[j]) & (o < j))
            rank = rank + ahead.astype(jnp.int32)
        picked.append(jnp.where(rank < 2, q[j], 0.0))
    total = picked[0] + picked[1] + picked[2] + picked[3]
    gid_ref[:, rows] = g_idx
    cw_rows = jnp.concatenate([pj / total for pj in picked]
                              + [jnp.zeros((LANES - EXPERTS_PER_GROUP, MRG_SUB), F32)], axis=0)
    rec_ref[rows, HALF_D:REC_W] = lax.bitcast_convert_type(cw_rows.T, jnp.uint32)

    hot = onehot.astype(BF16)
    within = jnp.dot(hot, tri_ref[...], preferred_element_type=F32)
    before = jnp.sum(jnp.where(onehot, within + cnt_scr[...], 0.0), axis=0, keepdims=True) - 1.0
    rank_ref[:, rows] = before.astype(jnp.int32)
    cnt_scr[...] = cnt_scr[...] + within[:, MRG_SUB - 1:MRG_SUB]


def _merge(l, ya, yb, z, xres, mod, norm2_g, w_pa, w_pb, w_o, wr_t, br_col):
    n_t = N_TOK // MRG_TM
    tri = (jnp.arange(MRG_SUB)[:, None] <= jnp.arange(MRG_SUB)[None, :]).astype(BF16)
    zspec = lambda k: pl.BlockSpec((MRG_TM, HALF_D), lambda i: (i, k))
    once = pl.Buffered(1)
    lay = lambda *shape: pl.BlockSpec((None,) + shape, lambda i: (l,) + tuple(0 for _ in shape),
                                      pipeline_mode=once)
    const = lambda *shape: pl.BlockSpec(shape, lambda i: tuple(0 for _ in shape), pipeline_mode=once)
    return pl.pallas_call(
        _merge_kernel,
        grid=(n_t,),
        in_specs=[
            pl.BlockSpec((MRG_TM, SSM_WIDTH), lambda i: (i, 0)),
            pl.BlockSpec((MRG_TM, GM_WIDTH), lambda i: (i, 0)),
            zspec(2), zspec(3), zspec(4), zspec(5),
            pl.BlockSpec((MRG_TM, D_MODEL), lambda i: (i, 0)),
            _mod_spec(l, 2, 1), _mod_spec(l, 4, 1), _mod_spec(l, 3, 1),
            lay(1, D_MODEL),
            lay(SSM_WIDTH, D_MODEL),
            lay(GM_WIDTH, D_MODEL),
            lay(D_MODEL, D_MODEL),
            const(2 * N_EXPERTS, D_MODEL),
            const(N_EXPERTS, 1),
            const(MRG_SUB, MRG_SUB),
        ],
        out_specs=[
            pl.BlockSpec((MRG_TM, D_MODEL), lambda i: (i, 0)),
            pl.BlockSpec((MRG_TM, REC_W), lambda i: (i, 0)),
            pl.BlockSpec((None, 1, MRG_TM), lambda i: (i, 0, 0)),
            pl.BlockSpec((None, 1, MRG_TM), lambda i: (i, 0, 0)),
            pl.BlockSpec((N_EXPERT_GROUPS, LANES), lambda i: (0, 0)),
        ],
        out_shape=[
            jax.ShapeDtypeStruct((N_TOK, D_MODEL), F32),
            jax.ShapeDtypeStruct((N_TOK, REC_W), jnp.uint32),
            jax.ShapeDtypeStruct((n_t, 1, MRG_TM), jnp.int32),
            jax.ShapeDtypeStruct((n_t, 1, MRG_TM), jnp.int32),
            jax.ShapeDtypeStruct((N_EXPERT_GROUPS, LANES), F32),
        ],
        scratch_shapes=[pltpu.VMEM((N_EXPERT_GROUPS, 1), F32)],
        compiler_params=_cparams(("arbitrary",)),
        name="merge_router",
    )(ya, yb, z, z, z, z, xres, mod, mod, mod, norm2_g.reshape(DEPTH, 1, D_MODEL),
      w_pa, w_pb, w_o, wr_t, br_col, tri)


DSP_TM = 512


def _dispatch_kernel(pos_ref, pend_ref, rec_ref, out_ref, zero_scr, sem):
    step = pl.program_id(0)

    @pl.when(step == 0)
    def _():
        zero_scr[...] = jnp.zeros(zero_scr.shape, jnp.uint32)
        for g in range(N_EXPERT_GROUPS):
            start = pl.multiple_of(jnp.maximum(pend_ref[g] - MOE_BLK, 0), MOE_BLK)
            fill = pltpu.make_async_copy(zero_scr, out_ref.at[pl.ds(start, MOE_BLK)], sem)
            fill.start()
            fill.wait()
        for blk in range(N_TOK // MOE_BLK, MOE_NBLK):
            @pl.when(blk * MOE_BLK >= pend_ref[N_EXPERT_GROUPS - 1])
            def _():
                fill = pltpu.make_async_copy(zero_scr, out_ref.at[pl.ds(blk * MOE_BLK, MOE_BLK)], sem)
                fill.start()
                fill.wait()

    base = step * DSP_TM
    for r in range(DSP_TM):
        pltpu.make_async_copy(rec_ref.at[pl.ds(r, 1)], out_ref.at[pl.ds(pos_ref[base + r], 1)], sem).start()
    pltpu.make_async_copy(rec_ref, out_ref.at[pl.ds(0, DSP_TM)], sem).wait()


def _dispatch(pos, pend, rec):
    grid_spec = pltpu.PrefetchScalarGridSpec(
        num_scalar_prefetch=2,
        grid=(N_TOK // DSP_TM,),
        in_specs=[pl.BlockSpec((DSP_TM, REC_W), lambda i, pos, pend: (i, 0))],
        out_specs=pl.BlockSpec(memory_space=pl.ANY),
        scratch_shapes=[pltpu.VMEM((MOE_BLK, REC_W), jnp.uint32), pltpu.SemaphoreType.DMA(())],
    )
    return pl.pallas_call(
        _dispatch_kernel,
        grid_spec=grid_spec,
        out_shape=jax.ShapeDtypeStruct((MOE_SLOTS, REC_W), jnp.uint32),
        compiler_params=_cparams(("arbitrary",)),
        name="moe_dispatch",
    )(pos, pend, rec)


UP_EXPERTS = 2


MOE_SUB = 256


def _expert_up_kernel(gid_ref, fill_ref, last_ref, rec_ref, wg_ref, wu_ref, h_ref, wg_scr, wu_scr):
    del last_ref
    half = pl.program_id(0)
    b = pl.program_id(1)
    prev = gid_ref[jnp.maximum(b - 1, 0)]
    fresh = (b == 0) | (gid_ref[b] != prev)

    @pl.when(fresh)
    def _():
        for e in range(UP_EXPERTS):
            wg_scr[e] = wg_ref[e].astype(BF16)
            wu_scr[e] = wu_ref[e].astype(BF16)

    for sub in range(MOE_BLK // MOE_SUB):
        rows = pl.ds(sub * MOE_SUB, MOE_SUB)

        @pl.when(fill_ref[b] > sub * MOE_SUB)
        def _():
            packed = rec_ref[rows, 0:HALF_D]
            x_lo = lax.bitcast_convert_type(packed & jnp.uint32(0xFFFF0000), F32).astype(BF16)
            x_hi = lax.bitcast_convert_type(packed << 16, F32).astype(BF16)
            cw = lax.bitcast_convert_type(rec_ref[rows, HALF_D:REC_W], F32)
            lane = lax.broadcasted_iota(jnp.int32, cw.shape, 1)
            for e in range(UP_EXPERTS):
                w_row = jnp.sum(jnp.where(lane == half * UP_EXPERTS + e, cw, 0.0), axis=1, keepdims=True)
                gate = (jnp.dot(x_lo, wg_scr[e, 0:HALF_D, :], preferred_element_type=F32)
                        + jnp.dot(x_hi, wg_scr[e, HALF_D:D_MODEL, :], preferred_element_type=F32))
                up = (jnp.dot(x_lo, wu_scr[e, 0:HALF_D, :], preferred_element_type=F32)
                      + jnp.dot(x_hi, wu_scr[e, HALF_D:D_MODEL, :], preferred_element_type=F32))
                h_ref[rows, pl.ds(e * EXPERT_FF, EXPERT_FF)] = (
                    gate * jax.nn.sigmoid(gate) * up * w_row).astype(BF16)

        @pl.when(fill_ref[b] <= sub * MOE_SUB)
        def _():
            h_ref[rows, :] = jnp.zeros((MOE_SUB, UP_EXPERTS * EXPERT_FF), BF16)


def _expert_up(l, blk_gid, blk_fill, blk_last, rec_sorted, e_gate, e_up):
    halves = EXPERTS_PER_GROUP // UP_EXPERTS
    wspec = pl.BlockSpec((None, UP_EXPERTS, D_MODEL, EXPERT_FF),
                         lambda h, b, gid, fill, last: (l, gid[b] * halves + h, 0, 0))
    grid_spec = pltpu.PrefetchScalarGridSpec(
        num_scalar_prefetch=3,
        grid=(halves, MOE_NBLK),
        in_specs=[
            pl.BlockSpec((MOE_BLK, REC_W), lambda h, b, gid, fill, last: (jnp.minimum(b, last[0]), 0)),
            wspec, wspec,
        ],
        out_specs=pl.BlockSpec((MOE_BLK, UP_EXPERTS * EXPERT_FF), lambda h, b, gid, fill, last: (b, h)),
        scratch_shapes=[pltpu.VMEM((UP_EXPERTS, D_MODEL, EXPERT_FF), BF16),
                        pltpu.VMEM((UP_EXPERTS, D_MODEL, EXPERT_FF), BF16)],
    )
    return pl.pallas_call(
        _expert_up_kernel,
        grid_spec=grid_spec,
        out_shape=jax.ShapeDtypeStruct((MOE_SLOTS, GROUP_FF), BF16),
        compiler_params=_cparams(("arbitrary", "arbitrary")),
        name="expert_up",
    )(blk_gid, blk_fill, blk_last, rec_sorted, e_gate, e_up)


DOWN_CH = 512


def _expert_down_kernel(gid_ref, fill_ref, h_ref, wd_ref, y_ref, wd_scr):
    b = pl.program_id(0)
    prev = gid_ref[jnp.maximum(b - 1, 0)]
    fresh = (b == 0) | (gid_ref[b] != prev)

    @pl.when(fresh)
    def _():
        for c in range(GROUP_FF // DOWN_CH):
            rows = pl.ds(c * DOWN_CH, DOWN_CH)
            wd_scr[rows, :] = wd_ref[rows, :].astype(BF16)

    for sub in range(MOE_BLK // MOE_SUB):
        rows = pl.ds(sub * MOE_SUB, MOE_SUB)

        @pl.when(fill_ref[b] > sub * MOE_SUB)
        def _():
            hid = h_ref[rows, :]
            for c in range(HALF_D // DOWN_CH):
                cols = pl.ds(c * DOWN_CH, DOWN_CH)
                pair = pl.ds(HALF_D + c * DOWN_CH, DOWN_CH)
                y_ref[rows, cols] = _pack_pairs(jnp.dot(hid, wd_scr[:, cols], preferred_element_type=F32),
                                                jnp.dot(hid, wd_scr[:, pair], preferred_element_type=F32))

        @pl.when(fill_ref[b] <= sub * MOE_SUB)
        def _():
            y_ref[rows, :] = jnp.zeros((MOE_SUB, HALF_D), jnp.uint32)


def _expert_down(l, blk_gid, blk_fill, h_sorted, e_down_grouped):
    grid_spec = pltpu.PrefetchScalarGridSpec(
        num_scalar_prefetch=2,
        grid=(MOE_NBLK,),
        in_specs=[
            pl.BlockSpec((MOE_BLK, GROUP_FF), lambda b, gid, fill: (b, 0)),
            pl.BlockSpec((None, None, GROUP_FF, D_MODEL), lambda b, gid, fill: (l, gid[b], 0, 0)),
        ],
        out_specs=pl.BlockSpec((MOE_BLK, HALF_D), lambda b, gid, fill: (b, 0)),
        scratch_shapes=[pltpu.VMEM((GROUP_FF, D_MODEL), BF16)],
    )
    return pl.pallas_call(
        _expert_down_kernel,
        grid_spec=grid_spec,
        out_shape=jax.ShapeDtypeStruct((MOE_SLOTS, HALF_D), jnp.uint32),
        compiler_params=_cparams(("arbitrary",)),
        name="expert_down",
    )(blk_gid, blk_fill, h_sorted, e_down_grouped)


def _moe(l, rec, gid, rank, counts, e_gate, e_up, e_down_grouped):
    padded = (counts + MOE_BLK - 1) // MOE_BLK * MOE_BLK
    pend = jnp.cumsum(padded)
    pstart = pend - padded
    pos = (pstart[gid] + rank).astype(jnp.int32)
    blk_start = jnp.arange(MOE_NBLK, dtype=jnp.int32) * MOE_BLK
    blk_gid = jnp.minimum(jnp.sum((blk_start[:, None] >= pend[None, :]).astype(jnp.int32), axis=1),
                          N_EXPERT_GROUPS - 1)
    blk_fill = jnp.clip(pstart[blk_gid] + counts[blk_gid] - blk_start, 0, MOE_BLK)
    blk_fill = jnp.where(blk_start < pend[-1], blk_fill, 0).astype(jnp.int32)
    blk_last = (pend[-1:] // MOE_BLK - 1).astype(jnp.int32)
    rec_sorted = _dispatch(pos, pend.astype(jnp.int32), rec)
    hid = _expert_up(l, blk_gid, blk_fill, blk_last, rec_sorted, e_gate, e_up)
    y_sorted = _expert_down(l, blk_gid, blk_fill, hid, e_down_grouped)
    return y_sorted[pos]


FIN_TM = 512
FIN_PROMPT_TILES = N_PROMPT // FIN_TM


def _final_kernel(x_ref, y_ref, g2_ref, fg_ref, op_ref, os_ref):
    i = pl.program_id(0)
    seg = i // (SEG_TOK // FIN_TM)
    x = x_ref[...] + g2_ref[pl.ds(seg, 1), :] * _unpack_pairs(y_ref[...])
    ms = jnp.mean(x * x, axis=-1, keepdims=True)
    out = x * lax.rsqrt(ms + EPS) * fg_ref[...]

    @pl.when(i < FIN_PROMPT_TILES)
    def _():
        op_ref[...] = out

    @pl.when(i >= FIN_PROMPT_TILES)
    def _():
        os_ref[...] = out


def _final_norm(xmid, moe_y, mod, final_g):
    return pl.pallas_call(
        _final_kernel,
        grid=(N_TOK // FIN_TM,),
        in_specs=[
            pl.BlockSpec((FIN_TM, D_MODEL), lambda i: (i, 0)),
            pl.BlockSpec((FIN_TM, HALF_D), lambda i: (i, 0)),
            _mod_spec(DEPTH - 1, 5, 1),
            pl.BlockSpec((1, D_MODEL), lambda i: (0, 0)),
        ],
        out_specs=[
            pl.BlockSpec((FIN_TM, D_MODEL), lambda i: (jnp.minimum(i, FIN_PROMPT_TILES - 1), 0)),
            pl.BlockSpec((FIN_TM, D_MODEL), lambda i: (jnp.maximum(i - FIN_PROMPT_TILES, 0), 0)),
        ],
        out_shape=[
            jax.ShapeDtypeStruct((N_PROMPT, D_MODEL), F32),
            jax.ShapeDtypeStruct((N_SAMPLE, D_MODEL), F32),
        ],
        compiler_params=_cparams(("arbitrary",)),
        name="final_norm",
    )(xmid, moe_y, mod, final_g.reshape(1, D_MODEL))


def _grid_pos_embed(rows):
    quarter = D_MODEL // 4
    freqs = 1.0 / (POS_BASE ** (jnp.arange(quarter, dtype=F32) / quarter))
    er = jnp.arange(rows, dtype=F32)[:, None] * freqs
    ec = jnp.arange(GRID_W, dtype=F32)[:, None] * freqs
    row_emb = jnp.concatenate([jnp.sin(er), jnp.cos(er)], axis=-1)
    col_emb = jnp.concatenate([jnp.sin(ec), jnp.cos(ec)], axis=-1)
    return jnp.stack([row_emb, col_emb])


def kernel(x_prompt, x_sample, state_ssm_re, state_ssm_im, c, c_ctx, norm1_g, norm2_g, w_mod, b_mod,
           w_in, ssm_lam_re, ssm_lam_im, ssm_log_step, ssm_b_re, ssm_b_im, ssm_c_re, ssm_c_im, ssm_d,
           w_glu, b_glu, gm_ln_g, gm_w_s, gm_b_s, w_pa, w_pb, w_o, w_router, b_router,
           e_gate, e_up, e_down, final_g):
    cvec = jnp.concatenate([c_ctx[None], c, jnp.zeros((MOD_ROWS - 1 - DEC_BATCH, D_MODEL), F32)], axis=0)
    mod = _modulation(cvec, w_mod, b_mod)

    perm = (jnp.arange(N_EXPERT_GROUPS)[None, :] * EXPERTS_PER_GROUP
            + jnp.arange(EXPERTS_PER_GROUP)[:, None]).reshape(N_EXPERTS)
    wr = w_router.astype(F32).T[perm]
    wr_hi = wr.astype(BF16)
    wr_lo = (wr - wr_hi.astype(F32)).astype(BF16)
    wr_t = jnp.concatenate([wr_hi, wr_lo], axis=0)
    br_col = b_router.astype(F32)[perm][:, None]

    w_in_b, w_glu_b, w_s_b = w_in.astype(BF16), w_glu.astype(BF16), gm_w_s.astype(BF16)
    w_pa_b, w_pb_b, w_o_b = w_pa.astype(BF16), w_pb.astype(BF16), w_o.astype(BF16)
    b_s_full = jnp.repeat(jnp.transpose(gm_b_s.astype(F32), (0, 2, 1)), GM_GROUP_DIM, axis=2)
    e_down_grouped = e_down.reshape(DEPTH, N_EXPERT_GROUPS, GROUP_FF, D_MODEL)
    w1, w2, a16 = _s5_prep(ssm_lam_re, ssm_lam_im, ssm_log_step, ssm_b_re, ssm_b_im, ssm_c_re, ssm_c_im)
    h0_lat = jnp.concatenate([state_ssm_re[:, :, 0], state_ssm_re[:, :, 1],
                              state_ssm_im[:, :, 0], state_ssm_im[:, :, 1]], axis=-1).astype(F32)
    h0 = jnp.concatenate([jnp.zeros((DEPTH, 1, SSM_GROUPS, STATE_W), F32),
                          jnp.transpose(h0_lat, (1, 0, 2, 3))], axis=1)

    inproj_in = (x_prompt.reshape(N_PROMPT, D_MODEL), x_sample.reshape(N_SAMPLE, D_MODEL),
                 _grid_pos_embed(DEC_SEQ // GRID_W))

    new_re, new_im = [], []
    xmid = moe_y = None
    for l in range(DEPTH):
        xres, u, z = _inproj(l, *inproj_in, mod, norm1_g, gm_ln_g, w_in_b)
        ys, fs = _s5_scan(l, u, w1, w2, a16, h0)
        fin = fs[0]
        p = SSM_STATE
        new_re.append(jnp.stack([fin[:, :, 0:p], fin[::-1, :, p:2 * p]], axis=1))
        new_im.append(jnp.stack([fin[:, :, 2 * p:3 * p], fin[::-1, :, 3 * p:4 * p]], axis=1))

        ya, yb = _mix(l, ys, u, z, ssm_d.astype(F32), w_glu_b, b_glu.astype(F32), w_s_b, b_s_full)
        xmid, rec, gid, rank, cnt = _merge(l, ya, yb, z, xres, mod, norm2_g, w_pa_b, w_pb_b, w_o_b,
                                           wr_t, br_col)
        moe_y = _moe(l, rec, gid.reshape(N_TOK), rank.reshape(N_TOK), cnt[:, 0].astype(jnp.int32),
                     e_gate, e_up, e_down_grouped)
        inproj_in = (xmid, moe_y, mod)

    y_prompt, y_sample = _final_norm(xmid, moe_y, mod, final_g)
    new_state_re = jnp.stack(new_re, axis=1).astype(x_prompt.dtype)
    new_state_im = jnp.stack(new_im, axis=1).astype(x_prompt.dtype)
    return (y_prompt.reshape(BATCH, SEQ, D_MODEL), y_sample.reshape(DEC_BATCH, DEC_SEQ, D_MODEL),
            new_state_re, new_state_im)
```

```python
import functools

import jax
import jax.numpy as jnp
from jax import lax
from jax.experimental import pallas as pl
from jax.experimental.pallas import tpu as pltpu

F32 = jnp.float32
BF16 = jnp.bfloat16
HIGHEST = lax.Precision.HIGHEST

D_MODEL = 2048
BATCH = 16
SEQ = 256
DEPTH = 2
DEC_BATCH = 2
DEC_SEQ = 4096
GRID_W = 64
POS_BASE = 10000.0
EPS = 1e-6
SSM_WIDTH = D_MODEL // 2
SSM_GROUP = 16
SSM_GROUPS = SSM_WIDTH // SSM_GROUP
SSM_STATE = 64
GM_WIDTH = D_MODEL // 2
GM_CHUNK = 128
GM_GROUPS = 8
GM_GROUP_DIM = GM_WIDTH // GM_GROUPS
IN_WIDTH = SSM_WIDTH + 2 * GM_WIDTH + 2 * D_MODEL
N_EXPERTS = 32
N_EXPERT_GROUPS = 8
EXPERTS_PER_GROUP = N_EXPERTS // N_EXPERT_GROUPS
EXPERT_FF = D_MODEL // 4
N_MOD = 6

N_PROMPT = BATCH * SEQ
N_SAMPLE = DEC_BATCH * DEC_SEQ
N_TOK = N_PROMPT + N_SAMPLE
SEG_TOK = 4096
MOD_ROWS = 8
LANES = 128
SUBLANES = 8

SCAN_T = 16
SCAN_W = SCAN_T * SSM_GROUP
SCAN_ROWS = N_TOK // SCAN_T
SCAN_BLK = 256
SCAN_TOK = SCAN_BLK * SCAN_T
N_SCAN_BLK = SCAN_ROWS // SCAN_BLK
G_OCT = SUBLANES
STATE_W = 4 * SSM_STATE
HALF_W = 2 * SSM_STATE

MOE_BLK = 512
MOE_SLOTS = N_TOK + N_EXPERT_GROUPS * MOE_BLK
MOE_NBLK = MOE_SLOTS // MOE_BLK
GROUP_FF = EXPERTS_PER_GROUP * EXPERT_FF

VMEM_LIMIT = 56 * 1024 * 1024


def _cparams(sem):
    return pltpu.CompilerParams(dimension_semantics=sem, vmem_limit_bytes=VMEM_LIMIT)


MOD_TN = 1024


def _mod_kernel(c_ref, w_ref, b_ref, o_ref):
    c = c_ref[...]
    s = c * jax.nn.sigmoid(c)
    s_hi = s.astype(BF16)
    s_lo = (s - s_hi.astype(F32)).astype(BF16)
    w = w_ref[...]
    w_hi = w.astype(BF16)
    w_lo = (w - w_hi.astype(F32)).astype(BF16)
    both = jnp.dot(jnp.concatenate([s_hi, s_lo], axis=0), w_hi, preferred_element_type=F32)
    cross = jnp.dot(s_hi, w_lo, preferred_element_type=F32)
    o_ref[...] = both[0:MOD_ROWS] + both[MOD_ROWS:2 * MOD_ROWS] + cross + b_ref[...]


def _modulation(cvec, w_mod, b_mod):
    width = N_MOD * D_MODEL
    return pl.pallas_call(
        _mod_kernel,
        grid=(DEPTH, width // MOD_TN),
        in_specs=[
            pl.BlockSpec((MOD_ROWS, D_MODEL), lambda l, n: (0, 0)),
            pl.BlockSpec((None, D_MODEL, MOD_TN), lambda l, n: (l, 0, n)),
            pl.BlockSpec((None, 1, MOD_TN), lambda l, n: (l, 0, n)),
        ],
        out_specs=pl.BlockSpec((None, MOD_ROWS, MOD_TN), lambda l, n: (l, 0, n)),
        out_shape=jax.ShapeDtypeStruct((DEPTH, MOD_ROWS, width), F32),
        compiler_params=_cparams(("arbitrary", "arbitrary")),
        name="adaln_mod",
    )(cvec, w_mod, b_mod.reshape(DEPTH, 1, width))


def _pack_pairs(a, b):
    hi = lax.bitcast_convert_type(a.astype(BF16).astype(F32), jnp.uint32)
    lo = lax.bitcast_convert_type(b.astype(BF16).astype(F32), jnp.uint32)
    return hi | (lo >> 16)


def _unpack_pairs(packed):
    hi = lax.bitcast_convert_type(packed & jnp.uint32(0xFFFF0000), F32)
    lo = lax.bitcast_convert_type(packed << 16, F32)
    return jnp.concatenate([hi, lo], axis=-1)


def _mod_spec(l, k, nargs):
    if nargs == 1:
        return pl.BlockSpec((None, MOD_ROWS, D_MODEL), lambda i: (l, 0, k))
    return pl.BlockSpec((None, MOD_ROWS, D_MODEL), lambda i, j: (l, 0, k))


INP_TM = 256
INP_CH = 256
INP_PROMPT_TILES = N_PROMPT // INP_TM
Z_WIDTH = IN_WIDTH - SSM_WIDTH
INP_VMEM_LIMIT = 60 * 1024 * 1024


def _inproj_kernel(*refs, first):
    if first:
        xa_ref, xb_ref, add_ref = refs[:3]
    else:
        xa_ref, add_ref, gain_ref = refs[:3]
    sc_ref, sh_ref, g_ref, ln_ref, w_ref, xres_ref, u_ref, z_ref, h_scr, v_scr = refs[3:]
    i = pl.program_id(0)
    seg = i // (SEG_TOK // INP_TM)

    if first:
        latent = i >= INP_PROMPT_TILES
        rows_per_tile = INP_TM // GRID_W
        row0 = (i % (DEC_SEQ // INP_TM)) * rows_per_tile
        row_part = jnp.concatenate(
            [jnp.broadcast_to(add_ref[0, pl.ds(row0 + q, 1), :], (GRID_W, D_MODEL // 2))
             for q in range(rows_per_tile)], axis=0)
        col_part = jnp.concatenate([add_ref[1]] * rows_per_tile, axis=0)
        x = jnp.where(latent, xb_ref[...] + jnp.concatenate([row_part, col_part], axis=1), xa_ref[...])
    else:
        x = xa_ref[...] + gain_ref[pl.ds(seg, 1), :] * _unpack_pairs(add_ref[...])
    xres_ref[...] = x
    ms = jnp.mean(x * x, axis=-1, keepdims=True)
    y = x * lax.rsqrt(ms + EPS) * g_ref[...]
    h = y * (1.0 + sc_ref[pl.ds(seg, 1), :]) + sh_ref[pl.ds(seg, 1), :]
    h_scr[...] = h.astype(BF16)

    def proj(col):
        return jnp.dot(h_scr[...], w_ref[:, pl.ds(col, INP_CH)], preferred_element_type=F32)

    n_ch = SSM_WIDTH // INP_CH
    for c in range(n_ch):
        u_ref[:, pl.ds(c * INP_CH, INP_CH)] = proj(c * INP_CH)
    for c in range(n_ch):
        z_ref[:, pl.ds(c * INP_CH, INP_CH)] = jax.nn.gelu(proj(SSM_WIDTH + c * INP_CH)).astype(BF16)
    row_sum = jnp.zeros((INP_TM, 1), F32)
    for c in range(n_ch):
        v = jax.nn.gelu(proj(SSM_WIDTH + GM_WIDTH + c * INP_CH))
        v_scr[:, pl.ds(c * INP_CH, INP_CH)] = v
        row_sum = row_sum + jnp.sum(v, axis=-1, keepdims=True)
    mu = row_sum * (1.0 / GM_WIDTH)
    dev = v_scr[...] - mu
    var = jnp.mean(jnp.square(dev), axis=-1, keepdims=True)
    z_ref[:, pl.ds(GM_WIDTH, GM_WIDTH)] = (dev * lax.rsqrt(var + EPS) * ln_ref[...]).astype(BF16)
    gates = SSM_WIDTH + 2 * GM_WIDTH
    for c in range(2 * D_MODEL // INP_CH):
        z_ref[:, pl.ds(2 * GM_WIDTH + c * INP_CH, INP_CH)] = jax.nn.sigmoid(
            proj(gates + c * INP_CH)).astype(BF16)


def _inproj(l, xa, xb_or_add, add_or_gain, mod, norm1_g, gm_ln_g, w_in_bf16):
    first = l == 0
    row_tile = lambda m: pl.BlockSpec((INP_TM, D_MODEL), m)
    if first:
        lead = [row_tile(lambda i: (jnp.minimum(i, INP_PROMPT_TILES - 1), 0)),
                row_tile(lambda i: (jnp.maximum(i - INP_PROMPT_TILES, 0), 0)),
                pl.BlockSpec((2, DEC_SEQ // GRID_W, D_MODEL // 2), lambda i: (0, 0, 0))]
    else:
        lead = [row_tile(lambda i: (i, 0)), pl.BlockSpec((INP_TM, HALF_D), lambda i: (i, 0)),
                _mod_spec(l - 1, 5, 1)]
    vec = lambda w: pl.BlockSpec((None, 1, w), lambda i: (l, 0, 0))
    return pl.pallas_call(
        functools.partial(_inproj_kernel, first=first),
        grid=(N_TOK // INP_TM,),
        in_specs=lead + [
            _mod_spec(l, 1, 1), _mod_spec(l, 0, 1),
            vec(D_MODEL), vec(GM_WIDTH),
            pl.BlockSpec((None, D_MODEL, IN_WIDTH), lambda i: (l, 0, 0), pipeline_mode=pl.Buffered(1)),
        ],
        out_specs=[
            pl.BlockSpec((INP_TM, D_MODEL), lambda i: (i, 0)),
            pl.BlockSpec((INP_TM, SSM_WIDTH), lambda i: (i, 0)),
            pl.BlockSpec((INP_TM, Z_WIDTH), lambda i: (i, 0)),
        ],
        out_shape=[
            jax.ShapeDtypeStruct((N_TOK, D_MODEL), F32),
            jax.ShapeDtypeStruct((N_TOK, SSM_WIDTH), F32),
            jax.ShapeDtypeStruct((N_TOK, Z_WIDTH), BF16),
        ],
        scratch_shapes=[pltpu.VMEM((INP_TM, D_MODEL), BF16), pltpu.VMEM((INP_TM, GM_WIDTH), F32)],
        compiler_params=pltpu.CompilerParams(dimension_semantics=("arbitrary",),
                                             vmem_limit_bytes=INP_VMEM_LIMIT),
        name="in_proj",
    )(xa, xb_or_add, add_or_gain, mod, mod, norm1_g.reshape(DEPTH, 1, D_MODEL),
      gm_ln_g.reshape(DEPTH, 1, GM_WIDTH), w_in_bf16)


PK_BRE, PK_BIM, PK_CRE, PK_CIM = 0, 16, 32, 48
PK_LR, PK_LI = 64, 65


def _split_bf16(x):
    hi = x.astype(BF16)
    return hi, (x - hi.astype(F32)).astype(BF16)


POW_ROWS = 24


def _prep_kernel(pk_ref, row_ref, pow_ref, tile_ref, w1_ref, w2_ref, a_ref):
    p = SSM_STATE
    k_sub = jnp.minimum(lax.broadcasted_iota(jnp.int32, (POW_ROWS, LANES), 0), SCAN_T).astype(F32)
    lane = lax.broadcasted_iota(jnp.int32, (p, LANES), 1)
    col = lax.broadcasted_iota(jnp.int32, (SSM_GROUP, SCAN_W), 1)

    def spread_pow(x_r, x_i, which):
        parts = jnp.concatenate(_split_bf16(x_r) + _split_bf16(x_i), axis=0)
        out = jnp.dot(parts, pow_ref[which], preferred_element_type=F32)
        return out[0:p] + out[p:2 * p], out[2 * p:3 * p] + out[3 * p:4 * p]

    def spread_tiles(x):
        out = jnp.dot(jnp.concatenate(_split_bf16(x), axis=0), tile_ref[...], preferred_element_type=F32)
        out = out[0:p] + out[p:2 * p]
        return [out[:, n * SCAN_W:(n + 1) * SCAN_W] for n in range(4)]

    def group(g, _):
        rows = row_ref[g]
        grow_r = rows[0:1] * rows[2:3]
        grow_i = rows[1:2] * rows[2:3]
        mag = jnp.exp(grow_r * k_sub)
        ang = grow_i * k_sub
        unused = jnp.zeros((LANES - POW_ROWS, LANES), F32)
        pw_t_r = jnp.concatenate([mag * jnp.cos(ang), unused], axis=0).T
        pw_t_i = jnp.concatenate([mag * jnp.sin(ang), unused], axis=0).T
        per_dir = []
        for d in range(2):
            pk = pk_ref[d, g]
            lr = pk[:, PK_LR:PK_LR + 1]
            li = pk[:, PK_LI:PK_LI + 1]
            p_r = pw_t_r[d * p:(d + 1) * p]
            p_i = pw_t_i[d * p:(d + 1) * p]
            a_r = p_r[:, 1:2]
            a_i = p_i[:, 1:2]
            den = lr * lr + li * li
            q_r = ((a_r - 1.0) * lr + a_i * li) / den
            q_i = (a_i * lr - (a_r - 1.0) * li) / den
            per_dir.append((pk, p_r, p_i, q_r, q_i))

        w1_rows, lag, carry = [], [], []
        for d in range(2):
            pk, p_r, p_i, q_r, q_i = per_dir[d]
            b_r, b_i, c_r, c_i = spread_tiles(pk)
            bb_r = q_r * b_r - q_i * b_i
            bb_i = q_r * b_i + q_i * b_r
            pw_r, pw_i = spread_pow(p_r, p_i, 1 if d == 0 else 0)
            w1_rows.append((pw_r * bb_r - pw_i * bb_i, pw_r * bb_i + pw_i * bb_r))
            pl_r, pl_i = spread_pow(p_r, p_i, 0 if d == 0 else 1)
            cl_r = c_r * pl_r - c_i * pl_i
            cl_i = c_r * pl_i + c_i * pl_r
            pk_im = pltpu.roll(pk, LANES - (PK_BIM - PK_BRE), 1)
            bt_r = (q_r * pk - q_i * pk_im).T[0:SSM_GROUP, :]
            bt_i = (q_r * pk_im + q_i * pk).T[0:SSM_GROUP, :]
            lag.append(jnp.dot(bt_r, cl_r, precision=HIGHEST, preferred_element_type=F32)
                       - jnp.dot(bt_i, cl_i, precision=HIGHEST, preferred_element_type=F32))
            pc_r, pc_i = spread_pow(p_r, p_i, 2 if d == 0 else 3)
            carry.append((c_r * pc_r - c_i * pc_i, -(c_r * pc_i + c_i * pc_r)))

        (f_re, f_im), (b_re, b_im) = w1_rows
        w1_ref[g] = jnp.concatenate([f_re, b_re, f_im, b_im], axis=0).T.astype(BF16)

        for s in range(SCAN_T):
            fwd = lag[0] if s == 0 else pltpu.roll(lag[0], SSM_GROUP * s, 1)
            fwd = jnp.where(col >= SSM_GROUP * s, fwd, 0.0)
            shift_b = SSM_GROUP * (SCAN_T - 1 - s)
            bwd = lag[1] if shift_b == 0 else pltpu.roll(lag[1], SCAN_W - shift_b, 1)
            bwd = jnp.where(col < SSM_GROUP * (s + 1), bwd, 0.0)
            w2_ref[g, pl.ds(SSM_GROUP * s, SSM_GROUP), :] = (fwd + bwd).astype(BF16)
        (x_re, x_im), (y_re, y_im) = carry
        for n, rows in enumerate((x_re, y_re, x_im, y_im)):
            w2_ref[g, pl.ds(SCAN_W + SSM_STATE * n, SSM_STATE), :] = rows.astype(BF16)

        cols = [per_dir[0][1], per_dir[1][1], per_dir[0][2], per_dir[1][2]]
        a_cols = jnp.zeros((SSM_STATE, LANES), F32)
        for n, c in enumerate(cols):
            a_cols = jnp.where(lane == n, c[:, SCAN_T:SCAN_T + 1], a_cols)
        a_ref[g] = a_cols
        return 0

    lax.fori_loop(0, G_OCT, group, 0, unroll=4)


def _s5_prep(lam_re, lam_im, log_step, b_re, b_im, c_re, c_im):
    shape = (DEPTH, 2, SSM_GROUPS, SSM_STATE)
    lr = lam_re.astype(F32)
    li = lam_im.astype(F32)
    dt = jnp.broadcast_to(jnp.exp(log_step.astype(F32))[..., None], shape)
    pk = jnp.concatenate([
        b_re.astype(F32), b_im.astype(F32),
        jnp.swapaxes(c_re.astype(F32), -1, -2), jnp.swapaxes(c_im.astype(F32), -1, -2),
        lr[..., None], li[..., None],
        jnp.zeros(shape + (LANES - PK_LI - 1,), F32)], axis=-1)
    both_dirs = lambda a: jnp.concatenate([a[:, 0], a[:, 1]], axis=-1)
    rows = jnp.stack([both_dirs(lr), both_dirs(li), both_dirs(dt)], axis=2)
    rows = jnp.concatenate([rows, jnp.zeros((DEPTH, SSM_GROUPS, SUBLANES - 3, LANES), F32)], axis=2)

    blk = jnp.arange(SCAN_W) // SSM_GROUP
    k = jnp.arange(LANES)[:, None]
    pows = [k == blk[None, :], k == (SCAN_T - 1 - blk)[None, :], k == (blk + 1)[None, :],
            k == (SCAN_T - blk)[None, :]]
    h = (jnp.arange(SCAN_W) % SSM_GROUP)[None, :]
    sel_pow = jnp.stack(pows).astype(BF16)
    sel_tile = jnp.concatenate([k == h + off for off in (PK_BRE, PK_BIM, PK_CRE, PK_CIM)],
                               axis=1).astype(BF16)

    n_oct = SSM_GROUPS // G_OCT
    w1, w2, a_cols = pl.pallas_call(
        _prep_kernel,
        grid=(DEPTH, n_oct),
        in_specs=[
            pl.BlockSpec((None, 2, G_OCT, SSM_STATE, LANES), lambda l, o: (l, 0, o, 0, 0)),
            pl.BlockSpec((None, G_OCT, SUBLANES, LANES), lambda l, o: (l, o, 0, 0)),
            pl.BlockSpec((4, LANES, SCAN_W), lambda l, o: (0, 0, 0)),
            pl.BlockSpec((LANES, 4 * SCAN_W), lambda l, o: (0, 0)),
        ],
        out_specs=[
            pl.BlockSpec((None, G_OCT, SCAN_W, STATE_W), lambda l, o: (l, o, 0, 0)),
            pl.BlockSpec((None, G_OCT, SCAN_W + STATE_W, SCAN_W), lambda l, o: (l, o, 0, 0)),
            pl.BlockSpec((None, G_OCT, SSM_STATE, LANES), lambda l, o: (l, o, 0, 0)),
        ],
        out_shape=[
            jax.ShapeDtypeStruct((DEPTH, SSM_GROUPS, SCAN_W, STATE_W), BF16),
            jax.ShapeDtypeStruct((DEPTH, SSM_GROUPS, SCAN_W + STATE_W, SCAN_W), BF16),
            jax.ShapeDtypeStruct((DEPTH, SSM_GROUPS, SSM_STATE, LANES), F32),
        ],
        compiler_params=_cparams(("arbitrary", "arbitrary")),
        name="s5_prep",
    )(pk, rows, sel_pow, sel_tile)
    a16 = jnp.swapaxes(a_cols[..., 0:4], -1, -2).reshape(DEPTH, SSM_GROUPS, STATE_W)
    return w1, w2, a16


def _s5_kernel(u_ref, w1_ref, w2_ref, a_ref, h0_ref, y_ref, fs_ref,
               t_scr, ug_scr, vr_scr, vi_scr, cr_scr, ci_scr, fr_scr, fi_scr):
    blk = pl.program_id(1)
    seq_rows = jnp.where(blk == 0, SEQ // SCAN_T, DEC_SEQ // SCAN_T)

    for s in range(SCAN_T):
        t_scr[s] = u_ref[pl.ds(s, SCAN_BLK, stride=SCAN_T), :].T
    for g in range(G_OCT):
        stacked = t_scr[:, pl.ds(g * SSM_GROUP, SSM_GROUP), :].reshape(SCAN_W, SCAN_BLK)
        ug_scr[g] = stacked.T.astype(BF16)

    for g in range(G_OCT):
        v = jnp.dot(ug_scr[g], w1_ref[g], preferred_element_type=F32)
        vr_scr[pl.ds(g, SCAN_BLK, stride=G_OCT), :] = v[:, 0:HALF_W]
        vi_scr[pl.ds(g, SCAN_BLK, stride=G_OCT), :] = v[:, HALF_W:STATE_W]

    a_r = a_ref[:, 0:HALF_W]
    a_i = a_ref[:, HALF_W:STATE_W]
    h0_r = h0_ref[:, 0:HALF_W]
    h0_i = h0_ref[:, HALF_W:STATE_W]
    fwd_lanes = lax.broadcasted_iota(jnp.int32, (G_OCT, HALF_W), 1) < SSM_STATE
    bwd_lanes = jnp.logical_not(fwd_lanes)

    def step(k, carry):
        s_r, s_i = carry
        rf = pl.ds(pl.multiple_of(k * G_OCT, G_OCT), G_OCT)
        rb = pl.ds(pl.multiple_of((SCAN_BLK - 1 - k) * G_OCT, G_OCT), G_OCT)
        restart = (k & (seq_rows - 1)) == 0
        s_r = jnp.where(restart, h0_r, s_r)
        s_i = jnp.where(restart, h0_i, s_i)
        pltpu.store(cr_scr.at[rf, :], s_r, mask=fwd_lanes)
        pltpu.store(cr_scr.at[rb, :], s_r, mask=bwd_lanes)
        pltpu.store(ci_scr.at[rf, :], s_i, mask=fwd_lanes)
        pltpu.store(ci_scr.at[rb, :], s_i, mask=bwd_lanes)
        v_r = jnp.where(fwd_lanes, vr_scr[rf, :], vr_scr[rb, :])
        v_i = jnp.where(fwd_lanes, vi_scr[rf, :], vi_scr[rb, :])
        n_r = a_r * s_r - a_i * s_i + v_r
        n_i = a_r * s_i + a_i * s_r + v_i
        fr_scr[rf, :] = n_r
        fi_scr[rf, :] = n_i
        return n_r, n_i

    zero = jnp.zeros((G_OCT, HALF_W), F32)
    lax.fori_loop(0, SCAN_BLK, step, (zero, zero), unroll=4)

    for g in range(G_OCT):
        c_r = cr_scr[pl.ds(g, SCAN_BLK, stride=G_OCT), :].astype(BF16)
        c_i = ci_scr[pl.ds(g, SCAN_BLK, stride=G_OCT), :].astype(BF16)
        y = jnp.dot(ug_scr[g], w2_ref[g, 0:SCAN_W, :], preferred_element_type=F32)
        y = y + jnp.dot(c_r, w2_ref[g, SCAN_W:SCAN_W + HALF_W, :], preferred_element_type=F32)
        y = y + jnp.dot(c_i, w2_ref[g, SCAN_W + HALF_W:SCAN_W + STATE_W, :], preferred_element_type=F32)
        t_scr[:, pl.ds(g * SSM_GROUP, SSM_GROUP), :] = y.T.reshape(SCAN_T, SSM_GROUP, SCAN_BLK)
    for s in range(SCAN_T):
        y_ref[pl.ds(s, SCAN_BLK, stride=SCAN_T), :] = t_scr[s].T

    rows_per_seq = SEQ // SCAN_T
    for q in range(SCAN_BLK // rows_per_seq):
        last = pl.ds((q * rows_per_seq + rows_per_seq - 1) * G_OCT, G_OCT)
        fs_ref[q, :, 0:HALF_W] = fr_scr[last, :]
        fs_ref[q, :, HALF_W:STATE_W] = fi_scr[last, :]


def _s5_scan(l, u, w1, w2, a16, h0):
    n_oct = SSM_GROUPS // G_OCT
    n_fin = SCAN_BLK // (SEQ // SCAN_T)
    return pl.pallas_call(
        _s5_kernel,
        grid=(n_oct, N_SCAN_BLK),
        in_specs=[
            pl.BlockSpec((SCAN_TOK, LANES), lambda o, b: (b, o)),
            pl.BlockSpec((None, G_OCT, SCAN_W, STATE_W), lambda o, b: (l, o, 0, 0)),
            pl.BlockSpec((None, G_OCT, SCAN_W + STATE_W, SCAN_W), lambda o, b: (l, o, 0, 0)),
            pl.BlockSpec((None, G_OCT, STATE_W), lambda o, b: (l, o, 0)),
            pl.BlockSpec((None, None, G_OCT, STATE_W), lambda o, b: (l, b, o, 0)),
        ],
        out_specs=[
            pl.BlockSpec((SCAN_TOK, LANES), lambda o, b: (b, o)),
            pl.BlockSpec((None, n_fin, G_OCT, STATE_W), lambda o, b: (b, 0, o, 0)),
        ],
        out_shape=[
            jax.ShapeDtypeStruct((N_TOK, SSM_WIDTH), F32),
            jax.ShapeDtypeStruct((N_SCAN_BLK, n_fin, SSM_GROUPS, STATE_W), F32),
        ],
        scratch_shapes=[
            pltpu.VMEM((SCAN_T, LANES, SCAN_BLK), F32),
            pltpu.VMEM((G_OCT, SCAN_BLK, SCAN_W), BF16),
        ] + [pltpu.VMEM((SCAN_BLK * G_OCT, HALF_W), F32) for _ in range(6)],
        compiler_params=_cparams(("arbitrary", "arbitrary")),
        name="s5_scan",
    )(u, w1, w2, a16, h0)


MIX_TM = 512


def _mix_kernel(ys_ref, u_ref, gu_ref, vn_ref, d_ref, wglu_ref, bglu_ref, ws_ref, bs_ref,
                ya_ref, yb_ref):
    y = ys_ref[...] + d_ref[...] * u_ref[...]
    y = jax.nn.gelu(y)
    gate = jnp.dot(y.astype(BF16), wglu_ref[...], preferred_element_type=F32) + bglu_ref[...]
    ya_ref[...] = (y * jax.nn.sigmoid(gate)).astype(BF16)
    for c in range(MIX_TM // GM_CHUNK):
        rows = pl.ds(c * GM_CHUNK, GM_CHUNK)
        for g in range(GM_GROUPS):
            cols = pl.ds(g * GM_GROUP_DIM, GM_GROUP_DIM)
            mixed = jnp.dot(ws_ref[g], vn_ref[rows, cols], preferred_element_type=F32) + bs_ref[:, cols]
            yb_ref[rows, cols] = (gu_ref[rows, cols].astype(F32) * mixed).astype(BF16)


def _mix(l, ys, u, z, d_skip, w_glu_bf16, b_glu, w_s_bf16, b_s_full):
    tile = lambda k: pl.BlockSpec((MIX_TM, SSM_WIDTH), lambda i: (i, k))
    lay = lambda *shape: pl.BlockSpec((None,) + shape, lambda i: (l,) + tuple(0 for _ in shape))
    return pl.pallas_call(
        _mix_kernel,
        grid=(N_TOK // MIX_TM,),
        in_specs=[
            tile(0), tile(0), tile(0), tile(1),
            lay(1, SSM_WIDTH),
            lay(SSM_WIDTH, SSM_WIDTH),
            lay(1, SSM_WIDTH),
            lay(GM_GROUPS, GM_CHUNK, GM_CHUNK),
            lay(GM_CHUNK, GM_WIDTH),
        ],
        out_specs=[tile(0), tile(0)],
        out_shape=[
            jax.ShapeDtypeStruct((N_TOK, SSM_WIDTH), BF16),
            jax.ShapeDtypeStruct((N_TOK, GM_WIDTH), BF16),
        ],
        compiler_params=_cparams(("arbitrary",)),
        name="mixers",
    )(ys, u, z, z, d_skip.reshape(DEPTH, 1, SSM_WIDTH), w_glu_bf16, b_glu.reshape(DEPTH, 1, SSM_WIDTH),
      w_s_bf16, b_s_full)


MRG_TM = 512
MRG_SUB = 256
HALF_D = D_MODEL // 2
REC_W = HALF_D + LANES


def _merge_kernel(ya_ref, yb_ref, ga0_ref, ga1_ref, gb0_ref, gb1_ref, x_ref, g1_ref, sc_ref, sh_ref,
                  n2_ref, wpa_ref, wpb_ref, wo_ref, wr_ref, br_ref, tri_ref,
                  xmid_ref, rec_ref, gid_ref, rank_ref, cnt_ref, cnt_scr):
    i = pl.program_id(0)
    seg = i // (SEG_TOK // MRG_TM)

    @pl.when(i == 0)
    def _():
        cnt_scr[...] = jnp.zeros(cnt_scr.shape, F32)

    passes = [pl.ds(sub * MRG_SUB, MRG_SUB) for sub in range(MRG_TM // MRG_SUB)]
    for rows in passes:
        _merge_mix(rows, seg, ya_ref, yb_ref, ga0_ref, ga1_ref, gb0_ref, gb1_ref, x_ref, g1_ref,
                   wpa_ref, wpb_ref, wo_ref, xmid_ref)
    for rows in passes:
        _merge_route(rows, seg, sc_ref, sh_ref, n2_ref, wr_ref, br_ref, tri_ref,
                     xmid_ref, rec_ref, gid_ref, rank_ref, cnt_scr)
    cnt_ref[...] = jnp.broadcast_to(cnt_scr[...], cnt_ref.shape)


def _merge_mix(rows, seg, ya_ref, yb_ref, ga0_ref, ga1_ref, gb0_ref, gb1_ref, x_ref, g1_ref,
               wpa_ref, wpb_ref, wo_ref, xmid_ref):
    pa = jnp.dot(ya_ref[rows, :], wpa_ref[...], preferred_element_type=F32)
    pb = jnp.dot(yb_ref[rows, :], wpb_ref[...], preferred_element_type=F32)
    m_lo = ga0_ref[rows, :].astype(F32) * pa[:, :HALF_D] + gb0_ref[rows, :].astype(F32) * pb[:, :HALF_D]
    m_hi = ga1_ref[rows, :].astype(F32) * pa[:, HALF_D:] + gb1_ref[rows, :].astype(F32) * pb[:, HALF_D:]
    mix = jnp.dot(m_lo.astype(BF16), wo_ref[0:HALF_D, :], preferred_element_type=F32)
    mix = mix + jnp.dot(m_hi.astype(BF16), wo_ref[HALF_D:D_MODEL, :], preferred_element_type=F32)
    xmid_ref[rows, :] = x_ref[rows, :] + g1_ref[pl.ds(seg, 1), :] * mix


def _merge_route(rows, seg, sc_ref, sh_ref, n2_ref, wr_ref, br_ref, tri_ref,
                 xmid_ref, rec_ref, gid_ref, rank_ref, cnt_scr):
    x = xmid_ref[rows, :]
    ms = jnp.mean(x * x, axis=-1, keepdims=True)
    y = x * lax.rsqrt(ms + EPS) * n2_ref[...]
    h2 = y * (1.0 + sc_ref[pl.ds(seg, 1), :]) + sh_ref[pl.ds(seg, 1), :]
    hi = h2.astype(BF16)
    hi_f = hi.astype(F32)
    lo = (h2 - hi_f).astype(BF16)
    bits = lax.bitcast_convert_type(hi_f, jnp.uint32)
    rec_ref[rows, 0:HALF_D] = bits[:, :HALF_D] | (bits[:, HALF_D:] >> 16)

    nt = (((1,), (1,)), ((), ()))
    lt = (lax.dot_general(wr_ref[...], hi, nt, preferred_element_type=F32)
          + lax.dot_general(wr_ref[...], lo, nt, preferred_element_type=F32))
    logits = lt[0:N_EXPERTS] + lt[N_EXPERTS:2 * N_EXPERTS]
    scores = jax.nn.sigmoid(logits)
    sel = scores + br_ref[...]
    ng = N_EXPERT_GROUPS
    s = [sel[j * ng:(j + 1) * ng] for j in range(EXPERTS_PER_GROUP)]
    p = [scores[j * ng:(j + 1) * ng] for j in range(EXPERTS_PER_GROUP)]
    a, b = jnp.maximum(s[0], s[1]), jnp.minimum(s[0], s[1])
    c, d = jnp.maximum(s[2], s[3]), jnp.minimum(s[2], s[3])
    grp_score = jnp.maximum(a, c) + jnp.maximum(jnp.minimum(a, c), jnp.maximum(b, d))
    best = jnp.max(grp_score, axis=0, keepdims=True)
    g_iota = lax.broadcasted_iota(jnp.int32, grp_score.shape, 0)
    g_idx = jnp.min(jnp.where(grp_score == best, g_iota, ng), axis=0, keepdims=True)
    onehot = g_iota == g_idx
    v = [jnp.sum(jnp.where(onehot, sj, 0.0), axis=0, keepdims=True) for sj in s]
    q = [jnp.sum(jnp.where(onehot, pj, 0.0), axis=0, keepdims=True) for pj in p]
    picked = []
    for j in range(EXPERTS_PER_GROUP):
        rank = jnp.zeros(v[j].shape, jnp.int32)
        for o in range(EXPERTS_PER_GROUP):
            if o == j:
                continue
            ahead = (v[o] > v[j]) | ((v[o] == v[j]) & (o < j))
            rank = rank + ahead.astype(jnp.int32)
        picked.append(jnp.where(rank < 2, q[j], 0.0))
    total = picked[0] + picked[1] + picked[2] + picked[3]
    gid_ref[:, rows] = g_idx
    cw_rows = jnp.concatenate([pj / total for pj in picked]
                              + [jnp.zeros((LANES - EXPERTS_PER_GROUP, MRG_SUB), F32)], axis=0)
    rec_ref[rows, HALF_D:REC_W] = lax.bitcast_convert_type(cw_rows.T, jnp.uint32)

    hot = onehot.astype(BF16)
    within = jnp.dot(hot, tri_ref[...], preferred_element_type=F32)
    before = jnp.sum(jnp.where(onehot, within + cnt_scr[...], 0.0), axis=0, keepdims=True) - 1.0
    rank_ref[:, rows] = before.astype(jnp.int32)
    cnt_scr[...] = cnt_scr[...] + within[:, MRG_SUB - 1:MRG_SUB]


def _merge(l, ya, yb, z, xres, mod, norm2_g, w_pa, w_pb, w_o, wr_t, br_col):
    n_t = N_TOK // MRG_TM
    tri = (jnp.arange(MRG_SUB)[:, None] <= jnp.arange(MRG_SUB)[None, :]).astype(BF16)
    zspec = lambda k: pl.BlockSpec((MRG_TM, HALF_D), lambda i: (i, k))
    once = pl.Buffered(1)
    lay = lambda *shape: pl.BlockSpec((None,) + shape, lambda i: (l,) + tuple(0 for _ in shape),
                                      pipeline_mode=once)
    const = lambda *shape: pl.BlockSpec(shape, lambda i: tuple(0 for _ in shape), pipeline_mode=once)
    return pl.pallas_call(
        _merge_kernel,
        grid=(n_t,),
        in_specs=[
            pl.BlockSpec((MRG_TM, SSM_WIDTH), lambda i: (i, 0)),
            pl.BlockSpec((MRG_TM, GM_WIDTH), lambda i: (i, 0)),
            zspec(2), zspec(3), zspec(4), zspec(5),
            pl.BlockSpec((MRG_TM, D_MODEL), lambda i: (i, 0)),
            _mod_spec(l, 2, 1), _mod_spec(l, 4, 1), _mod_spec(l, 3, 1),
            lay(1, D_MODEL),
            lay(SSM_WIDTH, D_MODEL),
            lay(GM_WIDTH, D_MODEL),
            lay(D_MODEL, D_MODEL),
            const(2 * N_EXPERTS, D_MODEL),
            const(N_EXPERTS, 1),
            const(MRG_SUB, MRG_SUB),
        ],
        out_specs=[
            pl.BlockSpec((MRG_TM, D_MODEL), lambda i: (i, 0)),
            pl.BlockSpec((MRG_TM, REC_W), lambda i: (i, 0)),
            pl.BlockSpec((None, 1, MRG_TM), lambda i: (i, 0, 0)),
            pl.BlockSpec((None, 1, MRG_TM), lambda i: (i, 0, 0)),
            pl.BlockSpec((N_EXPERT_GROUPS, LANES), lambda i: (0, 0)),
        ],
        out_shape=[
            jax.ShapeDtypeStruct((N_TOK, D_MODEL), F32),
            jax.ShapeDtypeStruct((N_TOK, REC_W), jnp.uint32),
            jax.ShapeDtypeStruct((n_t, 1, MRG_TM), jnp.int32),
            jax.ShapeDtypeStruct((n_t, 1, MRG_TM), jnp.int32),
            jax.ShapeDtypeStruct((N_EXPERT_GROUPS, LANES), F32),
        ],
        scratch_shapes=[pltpu.VMEM((N_EXPERT_GROUPS, 1), F32)],
        compiler_params=_cparams(("arbitrary",)),
        name="merge_router",
    )(ya, yb, z, z, z, z, xres, mod, mod, mod, norm2_g.reshape(DEPTH, 1, D_MODEL),
      w_pa, w_pb, w_o, wr_t, br_col, tri)


DSP_TM = 1024


def _dispatch_kernel(pos_ref, pend_ref, rec_ref, out_ref, zero_scr, sem):
    step = pl.program_id(0)

    @pl.when(step == 0)
    def _():
        zero_scr[...] = jnp.zeros(zero_scr.shape, jnp.uint32)
        for g in range(N_EXPERT_GROUPS):
            start = pl.multiple_of(jnp.maximum(pend_ref[g] - MOE_BLK, 0), MOE_BLK)
            fill = pltpu.make_async_copy(zero_scr, out_ref.at[pl.ds(start, MOE_BLK)], sem)
            fill.start()
            fill.wait()
        for blk in range(N_TOK // MOE_BLK, MOE_NBLK):
            @pl.when(blk * MOE_BLK >= pend_ref[N_EXPERT_GROUPS - 1])
            def _():
                fill = pltpu.make_async_copy(zero_scr, out_ref.at[pl.ds(blk * MOE_BLK, MOE_BLK)], sem)
                fill.start()
                fill.wait()

    base = step * DSP_TM
    for r in range(DSP_TM):
        pltpu.make_async_copy(rec_ref.at[pl.ds(r, 1)], out_ref.at[pl.ds(pos_ref[base + r], 1)], sem).start()
    pltpu.make_async_copy(rec_ref, out_ref.at[pl.ds(0, DSP_TM)], sem).wait()


def _dispatch(pos, pend, rec):
    grid_spec = pltpu.PrefetchScalarGridSpec(
        num_scalar_prefetch=2,
        grid=(N_TOK // DSP_TM,),
        in_specs=[pl.BlockSpec((DSP_TM, REC_W), lambda i, pos, pend: (i, 0))],
        out_specs=pl.BlockSpec(memory_space=pl.ANY),
        scratch_shapes=[pltpu.VMEM((MOE_BLK, REC_W), jnp.uint32), pltpu.SemaphoreType.DMA(())],
    )
    return pl.pallas_call(
        _dispatch_kernel,
        grid_spec=grid_spec,
        out_shape=jax.ShapeDtypeStruct((MOE_SLOTS, REC_W), jnp.uint32),
        compiler_params=_cparams(("arbitrary",)),
        name="moe_dispatch",
    )(pos, pend, rec)


UP_EXPERTS = 2


MOE_SUB = 256


UP_CH = 256


def _up_inputs(rec_ref, rows, first_expert):
    packed = rec_ref[rows, 0:HALF_D]
    x_lo = lax.bitcast_convert_type(packed & jnp.uint32(0xFFFF0000), F32).astype(BF16)
    x_hi = lax.bitcast_convert_type(packed << 16, F32).astype(BF16)
    cw = lax.bitcast_convert_type(rec_ref[rows, HALF_D:REC_W], F32)
    lane = lax.broadcasted_iota(jnp.int32, cw.shape, 1)
    w_rows = [jnp.sum(jnp.where(lane == first_expert + e, cw, 0.0), axis=1, keepdims=True)
              for e in range(UP_EXPERTS)]
    return x_lo, x_hi, w_rows


def _up_hidden(x_lo, x_hi, w_row, wg, wu):
    gate = (jnp.dot(x_lo, wg[0:HALF_D, :], preferred_element_type=F32)
            + jnp.dot(x_hi, wg[HALF_D:D_MODEL, :], preferred_element_type=F32))
    up = (jnp.dot(x_lo, wu[0:HALF_D, :], preferred_element_type=F32)
          + jnp.dot(x_hi, wu[HALF_D:D_MODEL, :], preferred_element_type=F32))
    return (gate * jax.nn.sigmoid(gate) * up * w_row).astype(BF16)


def _expert_up_kernel(gid_ref, fill_ref, last_ref, rec_ref, wg_ref, wu_ref, h_ref, wg_scr, wu_scr):
    del last_ref
    first_expert = pl.program_id(0) * UP_EXPERTS
    b = pl.program_id(1)
    prev = gid_ref[jnp.maximum(b - 1, 0)]
    fresh = (b == 0) | (gid_ref[b] != prev)
    passes = [pl.ds(sub * MOE_SUB, MOE_SUB) for sub in range(MOE_BLK // MOE_SUB)]

    @pl.when(fresh)
    def _():
        inputs = [_up_inputs(rec_ref, rows, first_expert) for rows in passes]
        for e in range(UP_EXPERTS):
            for c in range(EXPERT_FF // UP_CH):
                cols = pl.ds(c * UP_CH, UP_CH)
                wg = wg_ref[e, :, cols].astype(BF16)
                wu = wu_ref[e, :, cols].astype(BF16)
                wg_scr[e, :, cols] = wg
                wu_scr[e, :, cols] = wu
                for rows, (x_lo, x_hi, w_rows) in zip(passes, inputs):
                    h_ref[rows, pl.ds(e * EXPERT_FF + c * UP_CH, UP_CH)] = _up_hidden(
                        x_lo, x_hi, w_rows[e], wg, wu)

    @pl.when(jnp.logical_not(fresh))
    def _():
        for sub, rows in enumerate(passes):
            @pl.when(fill_ref[b] > sub * MOE_SUB)
            def _():
                x_lo, x_hi, w_rows = _up_inputs(rec_ref, rows, first_expert)
                for e in range(UP_EXPERTS):
                    h_ref[rows, pl.ds(e * EXPERT_FF, EXPERT_FF)] = _up_hidden(
                        x_lo, x_hi, w_rows[e], wg_scr[e], wu_scr[e])

            @pl.when(fill_ref[b] <= sub * MOE_SUB)
            def _():
                h_ref[rows, :] = jnp.zeros((MOE_SUB, UP_EXPERTS * EXPERT_FF), BF16)


def _expert_up(l, blk_gid, blk_fill, blk_last, rec_sorted, e_gate, e_up):
    halves = EXPERTS_PER_GROUP // UP_EXPERTS
    wspec = pl.BlockSpec((None, UP_EXPERTS, D_MODEL, EXPERT_FF),
                         lambda h, b, gid, fill, last: (l, gid[b] * halves + h, 0, 0))
    grid_spec = pltpu.PrefetchScalarGridSpec(
        num_scalar_prefetch=3,
        grid=(halves, MOE_NBLK),
        in_specs=[
            pl.BlockSpec((MOE_BLK, REC_W), lambda h, b, gid, fill, last: (jnp.minimum(b, last[0]), 0)),
            wspec, wspec,
        ],
        out_specs=pl.BlockSpec((MOE_BLK, UP_EXPERTS * EXPERT_FF), lambda h, b, gid, fill, last: (b, h)),
        scratch_shapes=[pltpu.VMEM((UP_EXPERTS, D_MODEL, EXPERT_FF), BF16),
                        pltpu.VMEM((UP_EXPERTS, D_MODEL, EXPERT_FF), BF16)],
    )
    return pl.pallas_call(
        _expert_up_kernel,
        grid_spec=grid_spec,
        out_shape=jax.ShapeDtypeStruct((MOE_SLOTS, GROUP_FF), BF16),
        compiler_params=_cparams(("arbitrary", "arbitrary")),
        name="expert_up",
    )(blk_gid, blk_fill, blk_last, rec_sorted, e_gate, e_up)


DOWN_CH = 256


def _expert_down_kernel(gid_ref, fill_ref, h_ref, wd_ref, y_ref, wd_scr):
    b = pl.program_id(0)
    prev = gid_ref[jnp.maximum(b - 1, 0)]
    fresh = (b == 0) | (gid_ref[b] != prev)

    passes = [pl.ds(sub * MOE_SUB, MOE_SUB) for sub in range(MOE_BLK // MOE_SUB)]

    @pl.when(fresh)
    def _():
        for c in range(HALF_D // DOWN_CH):
            cols = pl.ds(c * DOWN_CH, DOWN_CH)
            pair = pl.ds(HALF_D + c * DOWN_CH, DOWN_CH)
            w_a = wd_ref[:, cols].astype(BF16)
            w_b = wd_ref[:, pair].astype(BF16)
            wd_scr[:, cols] = w_a
            wd_scr[:, pair] = w_b
            for rows in passes:
                hid = h_ref[rows, :]
                y_ref[rows, cols] = _pack_pairs(jnp.dot(hid, w_a, preferred_element_type=F32),
                                                jnp.dot(hid, w_b, preferred_element_type=F32))

    @pl.when(jnp.logical_not(fresh))
    def _():
        for sub, rows in enumerate(passes):
            @pl.when(fill_ref[b] > sub * MOE_SUB)
            def _():
                hid = h_ref[rows, :]
                for c in range(HALF_D // DOWN_CH):
                    cols = pl.ds(c * DOWN_CH, DOWN_CH)
                    pair = pl.ds(HALF_D + c * DOWN_CH, DOWN_CH)
                    y_ref[rows, cols] = _pack_pairs(jnp.dot(hid, wd_scr[:, cols], preferred_element_type=F32),
                                                    jnp.dot(hid, wd_scr[:, pair], preferred_element_type=F32))

            @pl.when(fill_ref[b] <= sub * MOE_SUB)
            def _():
                y_ref[rows, :] = jnp.zeros((MOE_SUB, HALF_D), jnp.uint32)


def _expert_down(l, blk_gid, blk_fill, h_sorted, e_down_grouped):
    grid_spec = pltpu.PrefetchScalarGridSpec(
        num_scalar_prefetch=2,
        grid=(MOE_NBLK,),
        in_specs=[
            pl.BlockSpec((MOE_BLK, GROUP_FF), lambda b, gid, fill: (b, 0)),
            pl.BlockSpec((None, None, GROUP_FF, D_MODEL), lambda b, gid, fill: (l, gid[b], 0, 0)),
        ],
        out_specs=pl.BlockSpec((MOE_BLK, HALF_D), lambda b, gid, fill: (b, 0)),
        scratch_shapes=[pltpu.VMEM((GROUP_FF, D_MODEL), BF16)],
    )
    return pl.pallas_call(
        _expert_down_kernel,
        grid_spec=grid_spec,
        out_shape=jax.ShapeDtypeStruct((MOE_SLOTS, HALF_D), jnp.uint32),
        compiler_params=_cparams(("arbitrary",)),
        name="expert_down",
    )(blk_gid, blk_fill, h_sorted, e_down_grouped)


def _moe(l, rec, gid, rank, counts, e_gate, e_up, e_down_grouped):
    padded = (counts + MOE_BLK - 1) // MOE_BLK * MOE_BLK
    pend = jnp.cumsum(padded)
    pstart = pend - padded
    pos = (pstart[gid] + rank).astype(jnp.int32)
    blk_start = jnp.arange(MOE_NBLK, dtype=jnp.int32) * MOE_BLK
    blk_gid = jnp.minimum(jnp.sum((blk_start[:, None] >= pend[None, :]).astype(jnp.int32), axis=1),
                          N_EXPERT_GROUPS - 1)
    blk_fill = jnp.clip(pstart[blk_gid] + counts[blk_gid] - blk_start, 0, MOE_BLK)
    blk_fill = jnp.where(blk_start < pend[-1], blk_fill, 0).astype(jnp.int32)
    blk_last = (pend[-1:] // MOE_BLK - 1).astype(jnp.int32)
    rec_sorted = _dispatch(pos, pend.astype(jnp.int32), rec)
    hid = _expert_up(l, blk_gid, blk_fill, blk_last, rec_sorted, e_gate, e_up)
    y_sorted = _expert_down(l, blk_gid, blk_fill, hid, e_down_grouped)
    return y_sorted[pos]


FIN_TM = 512
FIN_PROMPT_TILES = N_PROMPT // FIN_TM


def _final_kernel(x_ref, y_ref, g2_ref, fg_ref, op_ref, os_ref):
    i = pl.program_id(0)
    seg = i // (SEG_TOK // FIN_TM)
    x = x_ref[...] + g2_ref[pl.ds(seg, 1), :] * _unpack_pairs(y_ref[...])
    ms = jnp.mean(x * x, axis=-1, keepdims=True)
    out = x * lax.rsqrt(ms + EPS) * fg_ref[...]

    @pl.when(i < FIN_PROMPT_TILES)
    def _():
        op_ref[...] = out

    @pl.when(i >= FIN_PROMPT_TILES)
    def _():
        os_ref[...] = out


def _final_norm(xmid, moe_y, mod, final_g):
    return pl.pallas_call(
        _final_kernel,
        grid=(N_TOK // FIN_TM,),
        in_specs=[
            pl.BlockSpec((FIN_TM, D_MODEL), lambda i: (i, 0)),
            pl.BlockSpec((FIN_TM, HALF_D), lambda i: (i, 0)),
            _mod_spec(DEPTH - 1, 5, 1),
            pl.BlockSpec((1, D_MODEL), lambda i: (0, 0)),
        ],
        out_specs=[
            pl.BlockSpec((FIN_TM, D_MODEL), lambda i: (jnp.minimum(i, FIN_PROMPT_TILES - 1), 0)),
            pl.BlockSpec((FIN_TM, D_MODEL), lambda i: (jnp.maximum(i - FIN_PROMPT_TILES, 0), 0)),
        ],
        out_shape=[
            jax.ShapeDtypeStruct((N_PROMPT, D_MODEL), F32),
            jax.ShapeDtypeStruct((N_SAMPLE, D_MODEL), F32),
        ],
        compiler_params=_cparams(("arbitrary",)),
        name="final_norm",
    )(xmid, moe_y, mod, final_g.reshape(1, D_MODEL))


def _grid_pos_embed(rows):
    quarter = D_MODEL // 4
    freqs = 1.0 / (POS_BASE ** (jnp.arange(quarter, dtype=F32) / quarter))
    er = jnp.arange(rows, dtype=F32)[:, None] * freqs
    ec = jnp.arange(GRID_W, dtype=F32)[:, None] * freqs
    row_emb = jnp.concatenate([jnp.sin(er), jnp.cos(er)], axis=-1)
    col_emb = jnp.concatenate([jnp.sin(ec), jnp.cos(ec)], axis=-1)
    return jnp.stack([row_emb, col_emb])


def kernel(x_prompt, x_sample, state_ssm_re, state_ssm_im, c, c_ctx, norm1_g, norm2_g, w_mod, b_mod,
           w_in, ssm_lam_re, ssm_lam_im, ssm_log_step, ssm_b_re, ssm_b_im, ssm_c_re, ssm_c_im, ssm_d,
           w_glu, b_glu, gm_ln_g, gm_w_s, gm_b_s, w_pa, w_pb, w_o, w_router, b_router,
           e_gate, e_up, e_down, final_g):
    cvec = jnp.concatenate([c_ctx[None], c, jnp.zeros((MOD_ROWS - 1 - DEC_BATCH, D_MODEL), F32)], axis=0)
    mod = _modulation(cvec, w_mod, b_mod)

    perm = (jnp.arange(N_EXPERT_GROUPS)[None, :] * EXPERTS_PER_GROUP
            + jnp.arange(EXPERTS_PER_GROUP)[:, None]).reshape(N_EXPERTS)
    wr = w_router.astype(F32).T[perm]
    wr_hi = wr.astype(BF16)
    wr_lo = (wr - wr_hi.astype(F32)).astype(BF16)
    wr_t = jnp.concatenate([wr_hi, wr_lo], axis=0)
    br_col = b_router.astype(F32)[perm][:, None]

    w_in_b, w_glu_b, w_s_b = w_in.astype(BF16), w_glu.astype(BF16), gm_w_s.astype(BF16)
    w_pa_b, w_pb_b, w_o_b = w_pa.astype(BF16), w_pb.astype(BF16), w_o.astype(BF16)
    b_s_full = jnp.repeat(jnp.transpose(gm_b_s.astype(F32), (0, 2, 1)), GM_GROUP_DIM, axis=2)
    e_down_grouped = e_down.reshape(DEPTH, N_EXPERT_GROUPS, GROUP_FF, D_MODEL)
    w1, w2, a16 = _s5_prep(ssm_lam_re, ssm_lam_im, ssm_log_step, ssm_b_re, ssm_b_im, ssm_c_re, ssm_c_im)
    h0_lat = jnp.concatenate([state_ssm_re[:, :, 0], state_ssm_re[:, :, 1],
                              state_ssm_im[:, :, 0], state_ssm_im[:, :, 1]], axis=-1).astype(F32)
    h0 = jnp.concatenate([jnp.zeros((DEPTH, 1, SSM_GROUPS, STATE_W), F32),
                          jnp.transpose(h0_lat, (1, 0, 2, 3))], axis=1)

    inproj_in = (x_prompt.reshape(N_PROMPT, D_MODEL), x_sample.reshape(N_SAMPLE, D_MODEL),
                 _grid_pos_embed(DEC_SEQ // GRID_W))

    new_re, new_im = [], []
    xmid = moe_y = None
    for l in range(DEPTH):
        xres, u, z = _inproj(l, *inproj_in, mod, norm1_g, gm_ln_g, w_in_b)
        ys, fs = _s5_scan(l, u, w1, w2, a16, h0)
        fin = fs[0]
        p = SSM_STATE
        new_re.append(jnp.stack([fin[:, :, 0:p], fin[::-1, :, p:2 * p]], axis=1))
        new_im.append(jnp.stack([fin[:, :, 2 * p:3 * p], fin[::-1, :, 3 * p:4 * p]], axis=1))

        ya, yb = _mix(l, ys, u, z, ssm_d.astype(F32), w_glu_b, b_glu.astype(F32), w_s_b, b_s_full)
        xmid, rec, gid, rank, cnt = _merge(l, ya, yb, z, xres, mod, norm2_g, w_pa_b, w_pb_b, w_o_b,
                                           wr_t, br_col)
        moe_y = _moe(l, rec, gid.reshape(N_TOK), rank.reshape(N_TOK), cnt[:, 0].astype(jnp.int32),
                     e_gate, e_up, e_down_grouped)
        inproj_in = (xmid, moe_y, mod)

    y_prompt, y_sample = _final_norm(xmid, moe_y, mod, final_g)
    new_state_re = jnp.stack(new_re, axis=1).astype(x_prompt.dtype)
    new_state_im = jnp.stack(new_im, axis=1).astype(x_prompt.dtype)
    return (y_prompt.reshape(BATCH, SEQ, D_MODEL), y_sample.reshape(DEC_BATCH, DEC_SEQ, D_MODEL),
            new_state_re, new_state_im)
```

```python
import functools

import jax
import jax.numpy as jnp
from jax import lax
from jax.experimental import pallas as pl
from jax.experimental.pallas import tpu as pltpu

F32 = jnp.float32
BF16 = jnp.bfloat16
HIGHEST = lax.Precision.HIGHEST

D_MODEL = 2048
BATCH = 16
SEQ = 256
DEPTH = 2
DEC_BATCH = 2
DEC_SEQ = 4096
GRID_W = 64
POS_BASE = 10000.0
EPS = 1e-6
SSM_WIDTH = D_MODEL // 2
SSM_GROUP = 16
SSM_GROUPS = SSM_WIDTH // SSM_GROUP
SSM_STATE = 64
GM_WIDTH = D_MODEL // 2
GM_CHUNK = 128
GM_GROUPS = 8
GM_GROUP_DIM = GM_WIDTH // GM_GROUPS
IN_WIDTH = SSM_WIDTH + 2 * GM_WIDTH + 2 * D_MODEL
N_EXPERTS = 32
N_EXPERT_GROUPS = 8
EXPERTS_PER_GROUP = N_EXPERTS // N_EXPERT_GROUPS
EXPERT_FF = D_MODEL // 4
N_MOD = 6

N_PROMPT = BATCH * SEQ
N_SAMPLE = DEC_BATCH * DEC_SEQ
N_TOK = N_PROMPT + N_SAMPLE
SEG_TOK = 4096
MOD_ROWS = 8
LANES = 128
SUBLANES = 8

SCAN_T = 16
SCAN_W = SCAN_T * SSM_GROUP
SCAN_ROWS = N_TOK // SCAN_T
SCAN_BLK = 256
SCAN_TOK = SCAN_BLK * SCAN_T
N_SCAN_BLK = SCAN_ROWS // SCAN_BLK
G_OCT = SUBLANES
SCAN_G = 2 * SUBLANES
SCAN_GW = SCAN_G * SSM_GROUP
STATE_W = 4 * SSM_STATE
HALF_W = 2 * SSM_STATE

MOE_BLK = 512
MOE_SLOTS = N_TOK + N_EXPERT_GROUPS * MOE_BLK
MOE_NBLK = MOE_SLOTS // MOE_BLK
GROUP_FF = EXPERTS_PER_GROUP * EXPERT_FF

VMEM_LIMIT = 56 * 1024 * 1024


def _cparams(sem):
    return pltpu.CompilerParams(dimension_semantics=sem, vmem_limit_bytes=VMEM_LIMIT)


MOD_TN = 1024


def _mod_kernel(c_ref, w_ref, b_ref, o_ref):
    c = c_ref[...]
    s = c * jax.nn.sigmoid(c)
    s_hi = s.astype(BF16)
    s_lo = (s - s_hi.astype(F32)).astype(BF16)
    w = w_ref[...]
    w_hi = w.astype(BF16)
    w_lo = (w - w_hi.astype(F32)).astype(BF16)
    both = jnp.dot(jnp.concatenate([s_hi, s_lo], axis=0), w_hi, preferred_element_type=F32)
    cross = jnp.dot(s_hi, w_lo, preferred_element_type=F32)
    o_ref[...] = both[0:MOD_ROWS] + both[MOD_ROWS:2 * MOD_ROWS] + cross + b_ref[...]


def _modulation(cvec, w_mod, b_mod):
    width = N_MOD * D_MODEL
    return pl.pallas_call(
        _mod_kernel,
        grid=(DEPTH, width // MOD_TN),
        in_specs=[
            pl.BlockSpec((MOD_ROWS, D_MODEL), lambda l, n: (0, 0)),
            pl.BlockSpec((None, D_MODEL, MOD_TN), lambda l, n: (l, 0, n)),
            pl.BlockSpec((None, 1, MOD_TN), lambda l, n: (l, 0, n)),
        ],
        out_specs=pl.BlockSpec((None, MOD_ROWS, MOD_TN), lambda l, n: (l, 0, n)),
        out_shape=jax.ShapeDtypeStruct((DEPTH, MOD_ROWS, width), F32),
        compiler_params=_cparams(("arbitrary", "arbitrary")),
        name="adaln_mod",
    )(cvec, w_mod, b_mod.reshape(DEPTH, 1, width))


def _pack_pairs(a, b):
    hi = lax.bitcast_convert_type(a.astype(BF16).astype(F32), jnp.uint32)
    lo = lax.bitcast_convert_type(b.astype(BF16).astype(F32), jnp.uint32)
    return hi | (lo >> 16)


def _unpack_pairs(packed):
    hi = lax.bitcast_convert_type(packed & jnp.uint32(0xFFFF0000), F32)
    lo = lax.bitcast_convert_type(packed << 16, F32)
    return jnp.concatenate([hi, lo], axis=-1)


def _mod_spec(l, k, nargs):
    if nargs == 1:
        return pl.BlockSpec((None, MOD_ROWS, D_MODEL), lambda i: (l, 0, k))
    return pl.BlockSpec((None, MOD_ROWS, D_MODEL), lambda i, j: (l, 0, k))


INP_TM = 256
INP_CH = 256
INP_PROMPT_TILES = N_PROMPT // INP_TM
Z_WIDTH = IN_WIDTH - SSM_WIDTH
INP_VMEM_LIMIT = 60 * 1024 * 1024


def _inproj_kernel(*refs, first):
    if first:
        xa_ref, xb_ref, add_ref = refs[:3]
    else:
        xa_ref, add_ref, gain_ref = refs[:3]
    sc_ref, sh_ref, g_ref, ln_ref, w_ref, xres_ref, u_ref, z_ref, h_scr, v_scr = refs[3:]
    i = pl.program_id(0)
    seg = i // (SEG_TOK // INP_TM)

    if first:
        latent = i >= INP_PROMPT_TILES
        rows_per_tile = INP_TM // GRID_W
        row0 = (i % (DEC_SEQ // INP_TM)) * rows_per_tile
        row_part = jnp.concatenate(
            [jnp.broadcast_to(add_ref[0, pl.ds(row0 + q, 1), :], (GRID_W, D_MODEL // 2))
             for q in range(rows_per_tile)], axis=0)
        col_part = jnp.concatenate([add_ref[1]] * rows_per_tile, axis=0)
        x = jnp.where(latent, xb_ref[...] + jnp.concatenate([row_part, col_part], axis=1), xa_ref[...])
    else:
        x = xa_ref[...] + gain_ref[pl.ds(seg, 1), :] * _unpack_pairs(add_ref[...])
    xres_ref[...] = x
    ms = jnp.mean(x * x, axis=-1, keepdims=True)
    y = x * lax.rsqrt(ms + EPS) * g_ref[...]
    h = y * (1.0 + sc_ref[pl.ds(seg, 1), :]) + sh_ref[pl.ds(seg, 1), :]
    h_scr[...] = h.astype(BF16)

    def proj(col):
        return jnp.dot(h_scr[...], w_ref[:, pl.ds(col, INP_CH)], preferred_element_type=F32)

    n_ch = SSM_WIDTH // INP_CH
    for c in range(n_ch):
        acc = proj(c * INP_CH)
        for k in range(INP_CH // LANES):
            u_ref[c * (INP_CH // LANES) + k] = acc[:, k * LANES:(k + 1) * LANES]
    for c in range(n_ch):
        z_ref[:, pl.ds(c * INP_CH, INP_CH)] = jax.nn.gelu(proj(SSM_WIDTH + c * INP_CH)).astype(BF16)
    row_sum = jnp.zeros((INP_TM, 1), F32)
    for c in range(n_ch):
        v = jax.nn.gelu(proj(SSM_WIDTH + GM_WIDTH + c * INP_CH))
        v_scr[:, pl.ds(c * INP_CH, INP_CH)] = v
        row_sum = row_sum + jnp.sum(v, axis=-1, keepdims=True)
    mu = row_sum * (1.0 / GM_WIDTH)
    dev = v_scr[...] - mu
    var = jnp.mean(jnp.square(dev), axis=-1, keepdims=True)
    z_ref[:, pl.ds(GM_WIDTH, GM_WIDTH)] = (dev * lax.rsqrt(var + EPS) * ln_ref[...]).astype(BF16)
    gates = SSM_WIDTH + 2 * GM_WIDTH
    for c in range(2 * D_MODEL // INP_CH):
        z_ref[:, pl.ds(2 * GM_WIDTH + c * INP_CH, INP_CH)] = jax.nn.sigmoid(
            proj(gates + c * INP_CH)).astype(BF16)


def _inproj(l, xa, xb_or_add, add_or_gain, mod, norm1_g, gm_ln_g, w_in_bf16):
    first = l == 0
    row_tile = lambda m: pl.BlockSpec((INP_TM, D_MODEL), m)
    if first:
        lead = [row_tile(lambda i: (jnp.minimum(i, INP_PROMPT_TILES - 1), 0)),
                row_tile(lambda i: (jnp.maximum(i - INP_PROMPT_TILES, 0), 0)),
                pl.BlockSpec((2, DEC_SEQ // GRID_W, D_MODEL // 2), lambda i: (0, 0, 0))]
    else:
        lead = [row_tile(lambda i: (i, 0)), pl.BlockSpec((INP_TM, HALF_D), lambda i: (i, 0)),
                _mod_spec(l - 1, 5, 1)]
    vec = lambda w: pl.BlockSpec((None, 1, w), lambda i: (l, 0, 0))
    return pl.pallas_call(
        functools.partial(_inproj_kernel, first=first),
        grid=(N_TOK // INP_TM,),
        in_specs=lead + [
            _mod_spec(l, 1, 1), _mod_spec(l, 0, 1),
            vec(D_MODEL), vec(GM_WIDTH),
            pl.BlockSpec((None, D_MODEL, IN_WIDTH), lambda i: (l, 0, 0), pipeline_mode=pl.Buffered(1)),
        ],
        out_specs=[
            pl.BlockSpec((INP_TM, D_MODEL), lambda i: (i, 0)),
            pl.BlockSpec((SSM_WIDTH // LANES, INP_TM, LANES), lambda i: (0, i, 0)),
            pl.BlockSpec((INP_TM, Z_WIDTH), lambda i: (i, 0)),
        ],
        out_shape=[
            jax.ShapeDtypeStruct((N_TOK, D_MODEL), F32),
            jax.ShapeDtypeStruct((SSM_WIDTH // LANES, N_TOK, LANES), F32),
            jax.ShapeDtypeStruct((N_TOK, Z_WIDTH), BF16),
        ],
        scratch_shapes=[pltpu.VMEM((INP_TM, D_MODEL), BF16), pltpu.VMEM((INP_TM, GM_WIDTH), F32)],
        compiler_params=pltpu.CompilerParams(dimension_semantics=("arbitrary",),
                                             vmem_limit_bytes=INP_VMEM_LIMIT),
        name="in_proj",
    )(xa, xb_or_add, add_or_gain, mod, mod, norm1_g.reshape(DEPTH, 1, D_MODEL),
      gm_ln_g.reshape(DEPTH, 1, GM_WIDTH), w_in_bf16)


PK_BRE, PK_BIM, PK_CRE, PK_CIM = 0, 16, 32, 48
PK_LR, PK_LI = 64, 65


def _split_bf16(x):
    hi = x.astype(BF16)
    return hi, (x - hi.astype(F32)).astype(BF16)


POW_ROWS = 24


def _prep_kernel(pk_ref, row_ref, pow_ref, tile_ref, w1_ref, w2_ref, a_ref):
    p = SSM_STATE
    k_sub = jnp.minimum(lax.broadcasted_iota(jnp.int32, (POW_ROWS, LANES), 0), SCAN_T).astype(F32)
    lane = lax.broadcasted_iota(jnp.int32, (p, LANES), 1)
    col = lax.broadcasted_iota(jnp.int32, (SSM_GROUP, SCAN_W), 1)

    def spread_pow(x_r, x_i, which):
        parts = jnp.concatenate(_split_bf16(x_r) + _split_bf16(x_i), axis=0)
        out = jnp.dot(parts, pow_ref[which], preferred_element_type=F32)
        return out[0:p] + out[p:2 * p], out[2 * p:3 * p] + out[3 * p:4 * p]

    def spread_tiles(x):
        out = jnp.dot(jnp.concatenate(_split_bf16(x), axis=0), tile_ref[...], preferred_element_type=F32)
        out = out[0:p] + out[p:2 * p]
        return [out[:, n * SCAN_W:(n + 1) * SCAN_W] for n in range(4)]

    def group(g, _):
        rows = row_ref[g]
        grow_r = rows[0:1] * rows[2:3]
        grow_i = rows[1:2] * rows[2:3]
        mag = jnp.exp(grow_r * k_sub)
        ang = grow_i * k_sub
        unused = jnp.zeros((LANES - POW_ROWS, LANES), F32)
        pw_t_r = jnp.concatenate([mag * jnp.cos(ang), unused], axis=0).T
        pw_t_i = jnp.concatenate([mag * jnp.sin(ang), unused], axis=0).T
        per_dir = []
        for d in range(2):
            pk = pk_ref[d, g]
            lr = pk[:, PK_LR:PK_LR + 1]
            li = pk[:, PK_LI:PK_LI + 1]
            p_r = pw_t_r[d * p:(d + 1) * p]
            p_i = pw_t_i[d * p:(d + 1) * p]
            a_r = p_r[:, 1:2]
            a_i = p_i[:, 1:2]
            den = lr * lr + li * li
            q_r = ((a_r - 1.0) * lr + a_i * li) / den
            q_i = (a_i * lr - (a_r - 1.0) * li) / den
            per_dir.append((pk, p_r, p_i, q_r, q_i))

        w1_rows, lag, carry = [], [], []
        for d in range(2):
            pk, p_r, p_i, q_r, q_i = per_dir[d]
            b_r, b_i, c_r, c_i = spread_tiles(pk)
            bb_r = q_r * b_r - q_i * b_i
            bb_i = q_r * b_i + q_i * b_r
            pw_r, pw_i = spread_pow(p_r, p_i, 1 if d == 0 else 0)
            w1_rows.append((pw_r * bb_r - pw_i * bb_i, pw_r * bb_i + pw_i * bb_r))
            pl_r, pl_i = spread_pow(p_r, p_i, 0 if d == 0 else 1)
            cl_r = c_r * pl_r - c_i * pl_i
            cl_i = c_r * pl_i + c_i * pl_r
            pk_im = pltpu.roll(pk, LANES - (PK_BIM - PK_BRE), 1)
            bt_r = (q_r * pk - q_i * pk_im).T[0:SSM_GROUP, :]
            bt_i = (q_r * pk_im + q_i * pk).T[0:SSM_GROUP, :]
            lag.append(jnp.dot(bt_r, cl_r, precision=HIGHEST, preferred_element_type=F32)
                       - jnp.dot(bt_i, cl_i, precision=HIGHEST, preferred_element_type=F32))
            pc_r, pc_i = spread_pow(p_r, p_i, 2 if d == 0 else 3)
            carry.append((c_r * pc_r - c_i * pc_i, -(c_r * pc_i + c_i * pc_r)))

        (f_re, f_im), (b_re, b_im) = w1_rows
        w1_ref[g] = jnp.concatenate([f_re, b_re, f_im, b_im], axis=0).T.astype(BF16)

        for s in range(SCAN_T):
            fwd = lag[0] if s == 0 else pltpu.roll(lag[0], SSM_GROUP * s, 1)
            fwd = jnp.where(col >= SSM_GROUP * s, fwd, 0.0)
            shift_b = SSM_GROUP * (SCAN_T - 1 - s)
            bwd = lag[1] if shift_b == 0 else pltpu.roll(lag[1], SCAN_W - shift_b, 1)
            bwd = jnp.where(col < SSM_GROUP * (s + 1), bwd, 0.0)
            w2_ref[g, pl.ds(SSM_GROUP * s, SSM_GROUP), :] = (fwd + bwd).astype(BF16)
        (x_re, x_im), (y_re, y_im) = carry
        for n, rows in enumerate((x_re, y_re, x_im, y_im)):
            w2_ref[g, pl.ds(SCAN_W + SSM_STATE * n, SSM_STATE), :] = rows.astype(BF16)

        cols = [per_dir[0][1], per_dir[1][1], per_dir[0][2], per_dir[1][2]]
        a_cols = jnp.zeros((SSM_STATE, LANES), F32)
        for n, c in enumerate(cols):
            a_cols = jnp.where(lane == n, c[:, SCAN_T:SCAN_T + 1], a_cols)
        a_ref[g] = a_cols
        return 0

    lax.fori_loop(0, G_OCT, group, 0, unroll=4)


def _s5_prep(lam_re, lam_im, log_step, b_re, b_im, c_re, c_im):
    shape = (DEPTH, 2, SSM_GROUPS, SSM_STATE)
    lr = lam_re.astype(F32)
    li = lam_im.astype(F32)
    dt = jnp.broadcast_to(jnp.exp(log_step.astype(F32))[..., None], shape)
    pk = jnp.concatenate([
        b_re.astype(F32), b_im.astype(F32),
        jnp.swapaxes(c_re.astype(F32), -1, -2), jnp.swapaxes(c_im.astype(F32), -1, -2),
        lr[..., None], li[..., None],
        jnp.zeros(shape + (LANES - PK_LI - 1,), F32)], axis=-1)
    both_dirs = lambda a: jnp.concatenate([a[:, 0], a[:, 1]], axis=-1)
    rows = jnp.stack([both_dirs(lr), both_dirs(li), both_dirs(dt)], axis=2)
    rows = jnp.concatenate([rows, jnp.zeros((DEPTH, SSM_GROUPS, SUBLANES - 3, LANES), F32)], axis=2)

    blk = jnp.arange(SCAN_W) // SSM_GROUP
    k = jnp.arange(LANES)[:, None]
    pows = [k == blk[None, :], k == (SCAN_T - 1 - blk)[None, :], k == (blk + 1)[None, :],
            k == (SCAN_T - blk)[None, :]]
    h = (jnp.arange(SCAN_W) % SSM_GROUP)[None, :]
    sel_pow = jnp.stack(pows).astype(BF16)
    sel_tile = jnp.concatenate([k == h + off for off in (PK_BRE, PK_BIM, PK_CRE, PK_CIM)],
                               axis=1).astype(BF16)

    n_oct = SSM_GROUPS // G_OCT
    w1, w2, a_cols = pl.pallas_call(
        _prep_kernel,
        grid=(DEPTH, n_oct),
        in_specs=[
            pl.BlockSpec((None, 2, G_OCT, SSM_STATE, LANES), lambda l, o: (l, 0, o, 0, 0)),
            pl.BlockSpec((None, G_OCT, SUBLANES, LANES), lambda l, o: (l, o, 0, 0)),
            pl.BlockSpec((4, LANES, SCAN_W), lambda l, o: (0, 0, 0)),
            pl.BlockSpec((LANES, 4 * SCAN_W), lambda l, o: (0, 0)),
        ],
        out_specs=[
            pl.BlockSpec((None, G_OCT, SCAN_W, STATE_W), lambda l, o: (l, o, 0, 0)),
            pl.BlockSpec((None, G_OCT, SCAN_W + STATE_W, SCAN_W), lambda l, o: (l, o, 0, 0)),
            pl.BlockSpec((None, G_OCT, SSM_STATE, LANES), lambda l, o: (l, o, 0, 0)),
        ],
        out_shape=[
            jax.ShapeDtypeStruct((DEPTH, SSM_GROUPS, SCAN_W, STATE_W), BF16),
            jax.ShapeDtypeStruct((DEPTH, SSM_GROUPS, SCAN_W + STATE_W, SCAN_W), BF16),
            jax.ShapeDtypeStruct((DEPTH, SSM_GROUPS, SSM_STATE, LANES), F32),
        ],
        compiler_params=_cparams(("arbitrary", "arbitrary")),
        name="s5_prep",
    )(pk, rows, sel_pow, sel_tile)
    a16 = jnp.swapaxes(a_cols[..., 0:4], -1, -2).reshape(DEPTH, SSM_GROUPS, STATE_W)
    return w1, w2, a16


def _s5_kernel(u_ref, w1_ref, w2_ref, a_ref, h0_ref, y_ref, fs_ref,
               t_scr, ug_scr, vr_scr, vi_scr, cr_scr, ci_scr, fr_scr, fi_scr):
    blk = pl.program_id(1)
    seq_rows = jnp.where(blk == 0, SEQ // SCAN_T, DEC_SEQ // SCAN_T)

    for s in range(SCAN_T):
        for j in range(SCAN_GW // LANES):
            t_scr[s, pl.ds(j * LANES, LANES), :] = (
                u_ref[j, pl.ds(s, SCAN_BLK, stride=SCAN_T), :].astype(BF16).T)
    for g in range(SCAN_G):
        stacked = t_scr[:, pl.ds(g * SSM_GROUP, SSM_GROUP), :].reshape(SCAN_W, SCAN_BLK)
        ug_scr[g] = stacked.T

    for g in range(SCAN_G):
        v = jnp.dot(ug_scr[g], w1_ref[g], preferred_element_type=F32)
        vr_scr[pl.ds(g, SCAN_BLK, stride=SCAN_G), :] = v[:, 0:HALF_W]
        vi_scr[pl.ds(g, SCAN_BLK, stride=SCAN_G), :] = v[:, HALF_W:STATE_W]

    a_r = a_ref[:, 0:HALF_W]
    a_i = a_ref[:, HALF_W:STATE_W]
    h0_r = h0_ref[:, 0:HALF_W]
    h0_i = h0_ref[:, HALF_W:STATE_W]
    fwd_lanes = lax.broadcasted_iota(jnp.int32, (SCAN_G, HALF_W), 1) < SSM_STATE
    bwd_lanes = jnp.logical_not(fwd_lanes)

    def step(k, carry):
        s_r, s_i = carry
        rf = pl.ds(pl.multiple_of(k * SCAN_G, SCAN_G), SCAN_G)
        rb = pl.ds(pl.multiple_of((SCAN_BLK - 1 - k) * SCAN_G, SCAN_G), SCAN_G)
        restart = (k & (seq_rows - 1)) == 0
        s_r = jnp.where(restart, h0_r, s_r)
        s_i = jnp.where(restart, h0_i, s_i)
        pltpu.store(cr_scr.at[rf, :], s_r, mask=fwd_lanes)
        pltpu.store(cr_scr.at[rb, :], s_r, mask=bwd_lanes)
        pltpu.store(ci_scr.at[rf, :], s_i, mask=fwd_lanes)
        pltpu.store(ci_scr.at[rb, :], s_i, mask=bwd_lanes)
        v_r = jnp.where(fwd_lanes, vr_scr[rf, :], vr_scr[rb, :])
        v_i = jnp.where(fwd_lanes, vi_scr[rf, :], vi_scr[rb, :])
        n_r = a_r * s_r - a_i * s_i + v_r
        n_i = a_r * s_i + a_i * s_r + v_i
        fr_scr[rf, :] = n_r
        fi_scr[rf, :] = n_i
        return n_r, n_i

    zero = jnp.zeros((SCAN_G, HALF_W), F32)
    lax.fori_loop(0, SCAN_BLK, step, (zero, zero), unroll=4)

    for g in range(SCAN_G):
        c_r = cr_scr[pl.ds(g, SCAN_BLK, stride=SCAN_G), :].astype(BF16)
        c_i = ci_scr[pl.ds(g, SCAN_BLK, stride=SCAN_G), :].astype(BF16)
        y = jnp.dot(ug_scr[g], w2_ref[g, 0:SCAN_W, :], preferred_element_type=F32)
        y = y + jnp.dot(c_r, w2_ref[g, SCAN_W:SCAN_W + HALF_W, :], preferred_element_type=F32)
        y = y + jnp.dot(c_i, w2_ref[g, SCAN_W + HALF_W:SCAN_W + STATE_W, :], preferred_element_type=F32)
        t_scr[:, pl.ds(g * SSM_GROUP, SSM_GROUP), :] = y.astype(BF16).T.reshape(SCAN_T, SSM_GROUP, SCAN_BLK)
    for s in range(SCAN_T):
        for j in range(SCAN_GW // LANES):
            y_ref[j, pl.ds(s, SCAN_BLK, stride=SCAN_T), :] = t_scr[s, pl.ds(j * LANES, LANES), :].T.astype(F32)

    rows_per_seq = SEQ // SCAN_T
    for q in range(SCAN_BLK // rows_per_seq):
        last = pl.ds((q * rows_per_seq + rows_per_seq - 1) * SCAN_G, SCAN_G)
        fs_ref[q, :, 0:HALF_W] = fr_scr[last, :]
        fs_ref[q, :, HALF_W:STATE_W] = fi_scr[last, :]


def _s5_scan(l, u, w1, w2, a16, h0):
    n_oct = SSM_GROUPS // SCAN_G
    n_fin = SCAN_BLK // (SEQ // SCAN_T)
    return pl.pallas_call(
        _s5_kernel,
        grid=(n_oct, N_SCAN_BLK),
        in_specs=[
            pl.BlockSpec((SCAN_GW // LANES, SCAN_TOK, LANES), lambda o, b: (o, b, 0)),
            pl.BlockSpec((None, SCAN_G, SCAN_W, STATE_W), lambda o, b: (l, o, 0, 0)),
            pl.BlockSpec((None, SCAN_G, SCAN_W + STATE_W, SCAN_W), lambda o, b: (l, o, 0, 0)),
            pl.BlockSpec((None, SCAN_G, STATE_W), lambda o, b: (l, o, 0)),
            pl.BlockSpec((None, None, SCAN_G, STATE_W), lambda o, b: (l, b, o, 0)),
        ],
        out_specs=[
            pl.BlockSpec((SCAN_GW // LANES, SCAN_TOK, LANES), lambda o, b: (o, b, 0)),
            pl.BlockSpec((None, n_fin, SCAN_G, STATE_W), lambda o, b: (b, 0, o, 0)),
        ],
        out_shape=[
            jax.ShapeDtypeStruct((SSM_WIDTH // LANES, N_TOK, LANES), F32),
            jax.ShapeDtypeStruct((N_SCAN_BLK, n_fin, SSM_GROUPS, STATE_W), F32),
        ],
        scratch_shapes=[
            pltpu.VMEM((SCAN_T, SCAN_GW, SCAN_BLK), BF16),
            pltpu.VMEM((SCAN_G, SCAN_BLK, SCAN_W), BF16),
        ] + [pltpu.VMEM((SCAN_BLK * SCAN_G, HALF_W), F32) for _ in range(6)],
        compiler_params=_cparams(("arbitrary", "arbitrary")),
        name="s5_scan",
    )(u, w1, w2, a16, h0)


MIX_TM = 512


def _mix_kernel(ys_ref, u_ref, gu_ref, vn_ref, d_ref, wglu_ref, bglu_ref, ws_ref, bs_ref,
                ya_ref, yb_ref):
    slabs = SSM_WIDTH // LANES
    ys = jnp.concatenate([ys_ref[k] for k in range(slabs)], axis=1)
    u = jnp.concatenate([u_ref[k] for k in range(slabs)], axis=1)
    y = ys + d_ref[...] * u
    y = jax.nn.gelu(y)
    gate = jnp.dot(y.astype(BF16), wglu_ref[...], preferred_element_type=F32) + bglu_ref[...]
    ya_ref[...] = (y * jax.nn.sigmoid(gate)).astype(BF16)
    for c in range(MIX_TM // GM_CHUNK):
        rows = pl.ds(c * GM_CHUNK, GM_CHUNK)
        for g in range(GM_GROUPS):
            cols = pl.ds(g * GM_GROUP_DIM, GM_GROUP_DIM)
            mixed = jnp.dot(ws_ref[g], vn_ref[rows, cols], preferred_element_type=F32) + bs_ref[:, cols]
            yb_ref[rows, cols] = (gu_ref[rows, cols].astype(F32) * mixed).astype(BF16)


def _mix(l, ys, u, z, d_skip, w_glu_bf16, b_glu, w_s_bf16, b_s_full):
    tile = lambda k: pl.BlockSpec((MIX_TM, SSM_WIDTH), lambda i: (i, k))
    slab = pl.BlockSpec((SSM_WIDTH // LANES, MIX_TM, LANES), lambda i: (0, i, 0))
    lay = lambda *shape: pl.BlockSpec((None,) + shape, lambda i: (l,) + tuple(0 for _ in shape))
    return pl.pallas_call(
        _mix_kernel,
        grid=(N_TOK // MIX_TM,),
        in_specs=[
            slab, slab, tile(0), tile(1),
            lay(1, SSM_WIDTH),
            lay(SSM_WIDTH, SSM_WIDTH),
            lay(1, SSM_WIDTH),
            lay(GM_GROUPS, GM_CHUNK, GM_CHUNK),
            lay(GM_CHUNK, GM_WIDTH),
        ],
        out_specs=[tile(0), tile(0)],
        out_shape=[
            jax.ShapeDtypeStruct((N_TOK, SSM_WIDTH), BF16),
            jax.ShapeDtypeStruct((N_TOK, GM_WIDTH), BF16),
        ],
        compiler_params=_cparams(("arbitrary",)),
        name="mixers",
    )(ys, u, z, z, d_skip.reshape(DEPTH, 1, SSM_WIDTH), w_glu_bf16, b_glu.reshape(DEPTH, 1, SSM_WIDTH),
      w_s_bf16, b_s_full)


MRG_TM = 512
MRG_SUB = 256
HALF_D = D_MODEL // 2
REC_W = HALF_D + LANES


def _merge_kernel(ya_ref, yb_ref, ga0_ref, ga1_ref, gb0_ref, gb1_ref, x_ref, g1_ref, sc_ref, sh_ref,
                  n2_ref, wpa_ref, wpb_ref, wo_ref, wr_ref, br_ref, tri_ref,
                  xmid_ref, rec_ref, gid_ref, rank_ref, cnt_ref, cnt_scr):
    i = pl.program_id(0)
    seg = i // (SEG_TOK // MRG_TM)

    @pl.when(i == 0)
    def _():
        cnt_scr[...] = jnp.zeros(cnt_scr.shape, F32)

    passes = [pl.ds(sub * MRG_SUB, MRG_SUB) for sub in range(MRG_TM // MRG_SUB)]
    for rows in passes:
        _merge_mix(rows, seg, ya_ref, yb_ref, ga0_ref, ga1_ref, gb0_ref, gb1_ref, x_ref, g1_ref,
                   wpa_ref, wpb_ref, wo_ref, xmid_ref)
    for rows in passes:
        _merge_route(rows, seg, sc_ref, sh_ref, n2_ref, wr_ref, br_ref, tri_ref,
                     xmid_ref, rec_ref, gid_ref, rank_ref, cnt_scr)
    cnt_ref[...] = jnp.broadcast_to(cnt_scr[...], cnt_ref.shape)


def _merge_mix(rows, seg, ya_ref, yb_ref, ga0_ref, ga1_ref, gb0_ref, gb1_ref, x_ref, g1_ref,
               wpa_ref, wpb_ref, wo_ref, xmid_ref):
    pa = jnp.dot(ya_ref[rows, :], wpa_ref[...], preferred_element_type=F32)
    pb = jnp.dot(yb_ref[rows, :], wpb_ref[...], preferred_element_type=F32)
    m_lo = ga0_ref[rows, :].astype(F32) * pa[:, :HALF_D] + gb0_ref[rows, :].astype(F32) * pb[:, :HALF_D]
    m_hi = ga1_ref[rows, :].astype(F32) * pa[:, HALF_D:] + gb1_ref[rows, :].astype(F32) * pb[:, HALF_D:]
    mix = jnp.dot(m_lo.astype(BF16), wo_ref[0:HALF_D, :], preferred_element_type=F32)
    mix = mix + jnp.dot(m_hi.astype(BF16), wo_ref[HALF_D:D_MODEL, :], preferred_element_type=F32)
    xmid_ref[rows, :] = x_ref[rows, :] + g1_ref[pl.ds(seg, 1), :] * mix


def _merge_route(rows, seg, sc_ref, sh_ref, n2_ref, wr_ref, br_ref, tri_ref,
                 xmid_ref, rec_ref, gid_ref, rank_ref, cnt_scr):
    x = xmid_ref[rows, :]
    ms = jnp.mean(x * x, axis=-1, keepdims=True)
    y = x * lax.rsqrt(ms + EPS) * n2_ref[...]
    h2 = y * (1.0 + sc_ref[pl.ds(seg, 1), :]) + sh_ref[pl.ds(seg, 1), :]
    hi = h2.astype(BF16)
    hi_f = hi.astype(F32)
    lo = (h2 - hi_f).astype(BF16)
    bits = lax.bitcast_convert_type(hi_f, jnp.uint32)
    rec_ref[rows, 0:HALF_D] = bits[:, :HALF_D] | (bits[:, HALF_D:] >> 16)

    nt = (((1,), (1,)), ((), ()))
    lt = (lax.dot_general(wr_ref[...], hi, nt, preferred_element_type=F32)
          + lax.dot_general(wr_ref[...], lo, nt, preferred_element_type=F32))
    logits = lt[0:N_EXPERTS] + lt[N_EXPERTS:2 * N_EXPERTS]
    scores = jax.nn.sigmoid(logits)
    sel = scores + br_ref[...]
    ng = N_EXPERT_GROUPS
    s = [sel[j * ng:(j + 1) * ng] for j in range(EXPERTS_PER_GROUP)]
    p = [scores[j * ng:(j + 1) * ng] for j in range(EXPERTS_PER_GROUP)]
    a, b = jnp.maximum(s[0], s[1]), jnp.minimum(s[0], s[1])
    c, d = jnp.maximum(s[2], s[3]), jnp.minimum(s[2], s[3])
    grp_score = jnp.maximum(a, c) + jnp.maximum(jnp.minimum(a, c), jnp.maximum(b, d))
    best = jnp.max(grp_score, axis=0, keepdims=True)
    g_iota = lax.broadcasted_iota(jnp.int32, grp_score.shape, 0)
    g_idx = jnp.min(jnp.where(grp_score == best, g_iota, ng), axis=0, keepdims=True)
    onehot = g_iota == g_idx
    v = [jnp.sum(jnp.where(onehot, sj, 0.0), axis=0, keepdims=True) for sj in s]
    q = [jnp.sum(jnp.where(onehot, pj, 0.0), axis=0, keepdims=True) for pj in p]
    picked = []
    for j in range(EXPERTS_PER_GROUP):
        rank = jnp.zeros(v[j].shape, jnp.int32)
        for o in range(EXPERTS_PER_GROUP):
            if o == j:
                continue
            ahead = (v[o] > v[j]) | ((v[o] == v[j]) & (o < j))
            rank = rank + ahead.astype(jnp.int32)
        picked.append(jnp.where(rank < 2, q[j], 0.0))
    total = picked[0] + picked[1] + picked[2] + picked[3]
    gid_ref[:, rows] = g_idx
    cw_rows = jnp.concatenate([pj / total for pj in picked]
                              + [jnp.zeros((LANES - EXPERTS_PER_GROUP, MRG_SUB), F32)], axis=0)
    rec_ref[rows, HALF_D:REC_W] = lax.bitcast_convert_type(cw_rows.T, jnp.uint32)

    hot = onehot.astype(BF16)
    within = jnp.dot(hot, tri_ref[...], preferred_element_type=F32)
    before = jnp.sum(jnp.where(onehot, within + cnt_scr[...], 0.0), axis=0, keepdims=True) - 1.0
    rank_ref[:, rows] = before.astype(jnp.int32)
    cnt_scr[...] = cnt_scr[...] + within[:, MRG_SUB - 1:MRG_SUB]


def _merge(l, ya, yb, z, xres, mod, norm2_g, w_pa, w_pb, w_o, wr_t, br_col):
    n_t = N_TOK // MRG_TM
    tri = (jnp.arange(MRG_SUB)[:, None] <= jnp.arange(MRG_SUB)[None, :]).astype(BF16)
    zspec = lambda k: pl.BlockSpec((MRG_TM, HALF_D), lambda i: (i, k))
    once = pl.Buffered(1)
    lay = lambda *shape: pl.BlockSpec((None,) + shape, lambda i: (l,) + tuple(0 for _ in shape),
                                      pipeline_mode=once)
    const = lambda *shape: pl.BlockSpec(shape, lambda i: tuple(0 for _ in shape), pipeline_mode=once)
    return pl.pallas_call(
        _merge_kernel,
        grid=(n_t,),
        in_specs=[
            pl.BlockSpec((MRG_TM, SSM_WIDTH), lambda i: (i, 0)),
            pl.BlockSpec((MRG_TM, GM_WIDTH), lambda i: (i, 0)),
            zspec(2), zspec(3), zspec(4), zspec(5),
            pl.BlockSpec((MRG_TM, D_MODEL), lambda i: (i, 0)),
            _mod_spec(l, 2, 1), _mod_spec(l, 4, 1), _mod_spec(l, 3, 1),
            lay(1, D_MODEL),
            lay(SSM_WIDTH, D_MODEL),
            lay(GM_WIDTH, D_MODEL),
            lay(D_MODEL, D_MODEL),
            const(2 * N_EXPERTS, D_MODEL),
            const(N_EXPERTS, 1),
            const(MRG_SUB, MRG_SUB),
        ],
        out_specs=[
            pl.BlockSpec((MRG_TM, D_MODEL), lambda i: (i, 0)),
            pl.BlockSpec((MRG_TM, REC_W), lambda i: (i, 0)),
            pl.BlockSpec((None, 1, MRG_TM), lambda i: (i, 0, 0)),
            pl.BlockSpec((None, 1, MRG_TM), lambda i: (i, 0, 0)),
            pl.BlockSpec((N_EXPERT_GROUPS, LANES), lambda i: (0, 0)),
        ],
        out_shape=[
            jax.ShapeDtypeStruct((N_TOK, D_MODEL), F32),
            jax.ShapeDtypeStruct((N_TOK, REC_W), jnp.uint32),
            jax.ShapeDtypeStruct((n_t, 1, MRG_TM), jnp.int32),
            jax.ShapeDtypeStruct((n_t, 1, MRG_TM), jnp.int32),
            jax.ShapeDtypeStruct((N_EXPERT_GROUPS, LANES), F32),
        ],
        scratch_shapes=[pltpu.VMEM((N_EXPERT_GROUPS, 1), F32)],
        compiler_params=_cparams(("arbitrary",)),
        name="merge_router",
    )(ya, yb, z, z, z, z, xres, mod, mod, mod, norm2_g.reshape(DEPTH, 1, D_MODEL),
      w_pa, w_pb, w_o, wr_t, br_col, tri)


DSP_TM = 1024


def _dispatch_kernel(pos_ref, pend_ref, rec_ref, out_ref, zero_scr, sem):
    step = pl.program_id(0)

    @pl.when(step == 0)
    def _():
        zero_scr[...] = jnp.zeros(zero_scr.shape, jnp.uint32)
        for g in range(N_EXPERT_GROUPS):
            start = pl.multiple_of(jnp.maximum(pend_ref[g] - MOE_BLK, 0), MOE_BLK)
            fill = pltpu.make_async_copy(zero_scr, out_ref.at[pl.ds(start, MOE_BLK)], sem)
            fill.start()
            fill.wait()
        for blk in range(N_TOK // MOE_BLK, MOE_NBLK):
            @pl.when(blk * MOE_BLK >= pend_ref[N_EXPERT_GROUPS - 1])
            def _():
                fill = pltpu.make_async_copy(zero_scr, out_ref.at[pl.ds(blk * MOE_BLK, MOE_BLK)], sem)
                fill.start()
                fill.wait()

    base = step * DSP_TM
    for r in range(DSP_TM):
        pltpu.make_async_copy(rec_ref.at[pl.ds(r, 1)], out_ref.at[pl.ds(pos_ref[base + r], 1)], sem).start()
    pltpu.make_async_copy(rec_ref, out_ref.at[pl.ds(0, DSP_TM)], sem).wait()


def _dispatch(pos, pend, rec):
    grid_spec = pltpu.PrefetchScalarGridSpec(
        num_scalar_prefetch=2,
        grid=(N_TOK // DSP_TM,),
        in_specs=[pl.BlockSpec((DSP_TM, REC_W), lambda i, pos, pend: (i, 0))],
        out_specs=pl.BlockSpec(memory_space=pl.ANY),
        scratch_shapes=[pltpu.VMEM((MOE_BLK, REC_W), jnp.uint32), pltpu.SemaphoreType.DMA(())],
    )
    return pl.pallas_call(
        _dispatch_kernel,
        grid_spec=grid_spec,
        out_shape=jax.ShapeDtypeStruct((MOE_SLOTS, REC_W), jnp.uint32),
        compiler_params=_cparams(("arbitrary",)),
        name="moe_dispatch",
    )(pos, pend, rec)


UP_EXPERTS = 2


MOE_SUB = 256


UP_CH = 256


def _up_inputs(rec_ref, rows, first_expert):
    packed = rec_ref[rows, 0:HALF_D]
    x_lo = lax.bitcast_convert_type(packed & jnp.uint32(0xFFFF0000), F32).astype(BF16)
    x_hi = lax.bitcast_convert_type(packed << 16, F32).astype(BF16)
    cw = lax.bitcast_convert_type(rec_ref[rows, HALF_D:REC_W], F32)
    lane = lax.broadcasted_iota(jnp.int32, cw.shape, 1)
    w_rows = [jnp.sum(jnp.where(lane == first_expert + e, cw, 0.0), axis=1, keepdims=True)
              for e in range(UP_EXPERTS)]
    return x_lo, x_hi, w_rows


def _up_hidden(x_lo, x_hi, w_row, wg, wu):
    gate = (jnp.dot(x_lo, wg[0:HALF_D, :], preferred_element_type=F32)
            + jnp.dot(x_hi, wg[HALF_D:D_MODEL, :], preferred_element_type=F32))
    up = (jnp.dot(x_lo, wu[0:HALF_D, :], preferred_element_type=F32)
          + jnp.dot(x_hi, wu[HALF_D:D_MODEL, :], preferred_element_type=F32))
    return (gate * jax.nn.sigmoid(gate) * up * w_row).astype(BF16)


def _expert_up_kernel(gid_ref, fill_ref, last_ref, rec_ref, wg_ref, wu_ref, h_ref, wg_scr, wu_scr):
    del last_ref
    first_expert = pl.program_id(0) * UP_EXPERTS
    b = pl.program_id(1)
    prev = gid_ref[jnp.maximum(b - 1, 0)]
    fresh = (b == 0) | (gid_ref[b] != prev)
    passes = [pl.ds(sub * MOE_SUB, MOE_SUB) for sub in range(MOE_BLK // MOE_SUB)]

    @pl.when(fresh)
    def _():
        inputs = [_up_inputs(rec_ref, rows, first_expert) for rows in passes]
        for e in range(UP_EXPERTS):
            for c in range(EXPERT_FF // UP_CH):
                cols = pl.ds(c * UP_CH, UP_CH)
                wg = wg_ref[e, :, cols].astype(BF16)
                wu = wu_ref[e, :, cols].astype(BF16)
                wg_scr[e, :, cols] = wg
                wu_scr[e, :, cols] = wu
                for rows, (x_lo, x_hi, w_rows) in zip(passes, inputs):
                    h_ref[rows, pl.ds(e * EXPERT_FF + c * UP_CH, UP_CH)] = _up_hidden(
                        x_lo, x_hi, w_rows[e], wg, wu)

    @pl.when(jnp.logical_not(fresh))
    def _():
        for sub, rows in enumerate(passes):
            @pl.when(fill_ref[b] > sub * MOE_SUB)
            def _():
                x_lo, x_hi, w_rows = _up_inputs(rec_ref, rows, first_expert)
                for e in range(UP_EXPERTS):
                    h_ref[rows, pl.ds(e * EXPERT_FF, EXPERT_FF)] = _up_hidden(
                        x_lo, x_hi, w_rows[e], wg_scr[e], wu_scr[e])

            @pl.when(fill_ref[b] <= sub * MOE_SUB)
            def _():
                h_ref[rows, :] = jnp.zeros((MOE_SUB, UP_EXPERTS * EXPERT_FF), BF16)


def _expert_up(l, blk_gid, blk_fill, blk_last, rec_sorted, e_gate, e_up):
    halves = EXPERTS_PER_GROUP // UP_EXPERTS
    wspec = pl.BlockSpec((None, UP_EXPERTS, D_MODEL, EXPERT_FF),
                         lambda h, b, gid, fill, last: (l, gid[b] * halves + h, 0, 0))
    grid_spec = pltpu.PrefetchScalarGridSpec(
        num_scalar_prefetch=3,
        grid=(halves, MOE_NBLK),
        in_specs=[
            pl.BlockSpec((MOE_BLK, REC_W), lambda h, b, gid, fill, last: (jnp.minimum(b, last[0]), 0)),
            wspec, wspec,
        ],
        out_specs=pl.BlockSpec((MOE_BLK, UP_EXPERTS * EXPERT_FF), lambda h, b, gid, fill, last: (b, h)),
        scratch_shapes=[pltpu.VMEM((UP_EXPERTS, D_MODEL, EXPERT_FF), BF16),
                        pltpu.VMEM((UP_EXPERTS, D_MODEL, EXPERT_FF), BF16)],
    )
    return pl.pallas_call(
        _expert_up_kernel,
        grid_spec=grid_spec,
        out_shape=jax.ShapeDtypeStruct((MOE_SLOTS, GROUP_FF), BF16),
        compiler_params=_cparams(("arbitrary", "arbitrary")),
        name="expert_up",
    )(blk_gid, blk_fill, blk_last, rec_sorted, e_gate, e_up)


DOWN_CH = 256


def _expert_down_kernel(gid_ref, fill_ref, h_ref, wd_ref, y_ref, wd_scr):
    b = pl.program_id(0)
    prev = gid_ref[jnp.maximum(b - 1, 0)]
    fresh = (b == 0) | (gid_ref[b] != prev)

    passes = [pl.ds(sub * MOE_SUB, MOE_SUB) for sub in range(MOE_BLK // MOE_SUB)]

    @pl.when(fresh)
    def _():
        for c in range(HALF_D // DOWN_CH):
            cols = pl.ds(c * DOWN_CH, DOWN_CH)
            pair = pl.ds(HALF_D + c * DOWN_CH, DOWN_CH)
            w_a = wd_ref[:, cols].astype(BF16)
            w_b = wd_ref[:, pair].astype(BF16)
            wd_scr[:, cols] = w_a
            wd_scr[:, pair] = w_b
            for rows in passes:
                hid = h_ref[rows, :]
                y_ref[rows, cols] = _pack_pairs(jnp.dot(hid, w_a, preferred_element_type=F32),
                                                jnp.dot(hid, w_b, preferred_element_type=F32))

    @pl.when(jnp.logical_not(fresh))
    def _():
        for sub, rows in enumerate(passes):
            @pl.when(fill_ref[b] > sub * MOE_SUB)
            def _():
                hid = h_ref[rows, :]
                for c in range(HALF_D // DOWN_CH):
                    cols = pl.ds(c * DOWN_CH, DOWN_CH)
                    pair = pl.ds(HALF_D + c * DOWN_CH, DOWN_CH)
                    y_ref[rows, cols] = _pack_pairs(jnp.dot(hid, wd_scr[:, cols], preferred_element_type=F32),
                                                    jnp.dot(hid, wd_scr[:, pair], preferred_element_type=F32))

            @pl.when(fill_ref[b] <= sub * MOE_SUB)
            def _():
                y_ref[rows, :] = jnp.zeros((MOE_SUB, HALF_D), jnp.uint32)


def _expert_down(l, blk_gid, blk_fill, h_sorted, e_down_grouped):
    grid_spec = pltpu.PrefetchScalarGridSpec(
        num_scalar_prefetch=2,
        grid=(MOE_NBLK,),
        in_specs=[
            pl.BlockSpec((MOE_BLK, GROUP_FF), lambda b, gid, fill: (b, 0)),
            pl.BlockSpec((None, None, GROUP_FF, D_MODEL), lambda b, gid, fill: (l, gid[b], 0, 0)),
        ],
        out_specs=pl.BlockSpec((MOE_BLK, HALF_D), lambda b, gid, fill: (b, 0)),
        scratch_shapes=[pltpu.VMEM((GROUP_FF, D_MODEL), BF16)],
    )
    return pl.pallas_call(
        _expert_down_kernel,
        grid_spec=grid_spec,
        out_shape=jax.ShapeDtypeStruct((MOE_SLOTS, HALF_D), jnp.uint32),
        compiler_params=_cparams(("arbitrary",)),
        name="expert_down",
    )(blk_gid, blk_fill, h_sorted, e_down_grouped)


def _moe(l, rec, gid, rank, counts, e_gate, e_up, e_down_grouped):
    padded = (counts + MOE_BLK - 1) // MOE_BLK * MOE_BLK
    pend = jnp.cumsum(padded)
    pstart = pend - padded
    pos = (pstart[gid] + rank).astype(jnp.int32)
    blk_start = jnp.arange(MOE_NBLK, dtype=jnp.int32) * MOE_BLK
    blk_gid = jnp.minimum(jnp.sum((blk_start[:, None] >= pend[None, :]).astype(jnp.int32), axis=1),
                          N_EXPERT_GROUPS - 1)
    blk_fill = jnp.clip(pstart[blk_gid] + counts[blk_gid] - blk_start, 0, MOE_BLK)
    blk_fill = jnp.where(blk_start < pend[-1], blk_fill, 0).astype(jnp.int32)
    blk_last = (pend[-1:] // MOE_BLK - 1).astype(jnp.int32)
    rec_sorted = _dispatch(pos, pend.astype(jnp.int32), rec)
    hid = _expert_up(l, blk_gid, blk_fill, blk_last, rec_sorted, e_gate, e_up)
    y_sorted = _expert_down(l, blk_gid, blk_fill, hid, e_down_grouped)
    return y_sorted[pos]


FIN_TM = 512
FIN_PROMPT_TILES = N_PROMPT // FIN_TM


def _final_kernel(x_ref, y_ref, g2_ref, fg_ref, op_ref, os_ref):
    i = pl.program_id(0)
    seg = i // (SEG_TOK // FIN_TM)
    x = x_ref[...] + g2_ref[pl.ds(seg, 1), :] * _unpack_pairs(y_ref[...])
    ms = jnp.mean(x * x, axis=-1, keepdims=True)
    out = x * lax.rsqrt(ms + EPS) * fg_ref[...]

    @pl.when(i < FIN_PROMPT_TILES)
    def _():
        op_ref[...] = out

    @pl.when(i >= FIN_PROMPT_TILES)
    def _():
        os_ref[...] = out


def _final_norm(xmid, moe_y, mod, final_g):
    return pl.pallas_call(
        _final_kernel,
        grid=(N_TOK // FIN_TM,),
        in_specs=[
            pl.BlockSpec((FIN_TM, D_MODEL), lambda i: (i, 0)),
            pl.BlockSpec((FIN_TM, HALF_D), lambda i: (i, 0)),
            _mod_spec(DEPTH - 1, 5, 1),
            pl.BlockSpec((1, D_MODEL), lambda i: (0, 0)),
        ],
        out_specs=[
            pl.BlockSpec((FIN_TM, D_MODEL), lambda i: (jnp.minimum(i, FIN_PROMPT_TILES - 1), 0)),
            pl.BlockSpec((FIN_TM, D_MODEL), lambda i: (jnp.maximum(i - FIN_PROMPT_TILES, 0), 0)),
        ],
        out_shape=[
            jax.ShapeDtypeStruct((N_PROMPT, D_MODEL), F32),
            jax.ShapeDtypeStruct((N_SAMPLE, D_MODEL), F32),
        ],
        compiler_params=_cparams(("arbitrary",)),
        name="final_norm",
    )(xmid, moe_y, mod, final_g.reshape(1, D_MODEL))


def _grid_pos_embed(rows):
    quarter = D_MODEL // 4
    freqs = 1.0 / (POS_BASE ** (jnp.arange(quarter, dtype=F32) / quarter))
    er = jnp.arange(rows, dtype=F32)[:, None] * freqs
    ec = jnp.arange(GRID_W, dtype=F32)[:, None] * freqs
    row_emb = jnp.concatenate([jnp.sin(er), jnp.cos(er)], axis=-1)
    col_emb = jnp.concatenate([jnp.sin(ec), jnp.cos(ec)], axis=-1)
    return jnp.stack([row_emb, col_emb])


def kernel(x_prompt, x_sample, state_ssm_re, state_ssm_im, c, c_ctx, norm1_g, norm2_g, w_mod, b_mod,
           w_in, ssm_lam_re, ssm_lam_im, ssm_log_step, ssm_b_re, ssm_b_im, ssm_c_re, ssm_c_im, ssm_d,
           w_glu, b_glu, gm_ln_g, gm_w_s, gm_b_s, w_pa, w_pb, w_o, w_router, b_router,
           e_gate, e_up, e_down, final_g):
    cvec = jnp.concatenate([c_ctx[None], c, jnp.zeros((MOD_ROWS - 1 - DEC_BATCH, D_MODEL), F32)], axis=0)
    mod = _modulation(cvec, w_mod, b_mod)

    perm = (jnp.arange(N_EXPERT_GROUPS)[None, :] * EXPERTS_PER_GROUP
            + jnp.arange(EXPERTS_PER_GROUP)[:, None]).reshape(N_EXPERTS)
    wr = w_router.astype(F32).T[perm]
    wr_hi = wr.astype(BF16)
    wr_lo = (wr - wr_hi.astype(F32)).astype(BF16)
    wr_t = jnp.concatenate([wr_hi, wr_lo], axis=0)
    br_col = b_router.astype(F32)[perm][:, None]

    w_in_b, w_glu_b, w_s_b = w_in.astype(BF16), w_glu.astype(BF16), gm_w_s.astype(BF16)
    w_pa_b, w_pb_b, w_o_b = w_pa.astype(BF16), w_pb.astype(BF16), w_o.astype(BF16)
    b_s_full = jnp.repeat(jnp.transpose(gm_b_s.astype(F32), (0, 2, 1)), GM_GROUP_DIM, axis=2)
    e_down_grouped = e_down.reshape(DEPTH, N_EXPERT_GROUPS, GROUP_FF, D_MODEL)
    w1, w2, a16 = _s5_prep(ssm_lam_re, ssm_lam_im, ssm_log_step, ssm_b_re, ssm_b_im, ssm_c_re, ssm_c_im)
    h0_lat = jnp.concatenate([state_ssm_re[:, :, 0], state_ssm_re[:, :, 1],
                              state_ssm_im[:, :, 0], state_ssm_im[:, :, 1]], axis=-1).astype(F32)
    h0 = jnp.concatenate([jnp.zeros((DEPTH, 1, SSM_GROUPS, STATE_W), F32),
                          jnp.transpose(h0_lat, (1, 0, 2, 3))], axis=1)

    inproj_in = (x_prompt.reshape(N_PROMPT, D_MODEL), x_sample.reshape(N_SAMPLE, D_MODEL),
                 _grid_pos_embed(DEC_SEQ // GRID_W))

    new_re, new_im = [], []
    xmid = moe_y = None
    for l in range(DEPTH):
        xres, u, z = _inproj(l, *inproj_in, mod, norm1_g, gm_ln_g, w_in_b)
        ys, fs = _s5_scan(l, u, w1, w2, a16, h0)
        fin = fs[0]
        p = SSM_STATE
        new_re.append(jnp.stack([fin[:, :, 0:p], fin[::-1, :, p:2 * p]], axis=1))
        new_im.append(jnp.stack([fin[:, :, 2 * p:3 * p], fin[::-1, :, 3 * p:4 * p]], axis=1))

        ya, yb = _mix(l, ys, u, z, ssm_d.astype(F32), w_glu_b, b_glu.astype(F32), w_s_b, b_s_full)
        xmid, rec, gid, rank, cnt = _merge(l, ya, yb, z, xres, mod, norm2_g, w_pa_b, w_pb_b, w_o_b,
                                           wr_t, br_col)
        moe_y = _moe(l, rec, gid.reshape(N_TOK), rank.reshape(N_TOK), cnt[:, 0].astype(jnp.int32),
                     e_gate, e_up, e_down_grouped)
        inproj_in = (xmid, moe_y, mod)

    y_prompt, y_sample = _final_norm(xmid, moe_y, mod, final_g)
    new_state_re = jnp.stack(new_re, axis=1).astype(x_prompt.dtype)
    new_state_im = jnp.stack(new_im, axis=1).astype(x_prompt.dtype)
    return (y_prompt.reshape(BATCH, SEQ, D_MODEL), y_sample.reshape(DEC_BATCH, DEC_SEQ, D_MODEL),
            new_state_re, new_state_im)
```

```python
import functools

import jax
import jax.numpy as jnp
from jax import lax
from jax.experimental import pallas as pl
from jax.experimental.pallas import tpu as pltpu

F32 = jnp.float32
BF16 = jnp.bfloat16

D_MODEL = 2048
BATCH = 16
SEQ = 256
DEPTH = 2
DEC_BATCH = 2
DEC_SEQ = 4096
GRID_W = 64
POS_BASE = 10000.0
EPS = 1e-6
SSM_WIDTH = D_MODEL // 2
SSM_GROUP = 16
SSM_GROUPS = SSM_WIDTH // SSM_GROUP
SSM_STATE = 64
GM_WIDTH = D_MODEL // 2
GM_CHUNK = 128
GM_GROUPS = 8
GM_GROUP_DIM = GM_WIDTH // GM_GROUPS
IN_WIDTH = SSM_WIDTH + 2 * GM_WIDTH + 2 * D_MODEL
N_EXPERTS = 32
N_EXPERT_GROUPS = 8
EXPERTS_PER_GROUP = N_EXPERTS // N_EXPERT_GROUPS
EXPERT_FF = D_MODEL // 4
N_MOD = 6

N_PROMPT = BATCH * SEQ
N_SAMPLE = DEC_BATCH * DEC_SEQ
N_TOK = N_PROMPT + N_SAMPLE
SEG_TOK = 4096
MOD_ROWS = 8
LANES = 128
SUBLANES = 8

SCAN_T = 16
SCAN_W = SCAN_T * SSM_GROUP
SCAN_ROWS = N_TOK // SCAN_T
SCAN_BLK = 256
SCAN_TOK = SCAN_BLK * SCAN_T
N_SCAN_BLK = SCAN_ROWS // SCAN_BLK
G_OCT = SUBLANES
SCAN_G = 2 * SUBLANES
SCAN_GW = SCAN_G * SSM_GROUP
STATE_W = 4 * SSM_STATE
HALF_W = 2 * SSM_STATE

MOE_BLK = 512
MOE_SLOTS = N_TOK + N_EXPERT_GROUPS * MOE_BLK
MOE_NBLK = MOE_SLOTS // MOE_BLK
GROUP_FF = EXPERTS_PER_GROUP * EXPERT_FF

VMEM_LIMIT = 56 * 1024 * 1024


def _cparams(sem):
    return pltpu.CompilerParams(dimension_semantics=sem, vmem_limit_bytes=VMEM_LIMIT)


MOD_TN = 1024


def _mod_kernel(c_ref, w_ref, b_ref, o_ref):
    c = c_ref[...]
    s = c * jax.nn.sigmoid(c)
    s_hi = s.astype(BF16)
    s_lo = (s - s_hi.astype(F32)).astype(BF16)
    w = w_ref[...]
    w_hi = w.astype(BF16)
    w_lo = (w - w_hi.astype(F32)).astype(BF16)
    both = jnp.dot(jnp.concatenate([s_hi, s_lo], axis=0), w_hi, preferred_element_type=F32)
    cross = jnp.dot(s_hi, w_lo, preferred_element_type=F32)
    o_ref[...] = both[0:MOD_ROWS] + both[MOD_ROWS:2 * MOD_ROWS] + cross + b_ref[...]


def _modulation(cvec, w_mod, b_mod):
    width = N_MOD * D_MODEL
    return pl.pallas_call(
        _mod_kernel,
        grid=(DEPTH, width // MOD_TN),
        in_specs=[
            pl.BlockSpec((MOD_ROWS, D_MODEL), lambda l, n: (0, 0)),
            pl.BlockSpec((None, D_MODEL, MOD_TN), lambda l, n: (l, 0, n)),
            pl.BlockSpec((None, 1, MOD_TN), lambda l, n: (l, 0, n)),
        ],
        out_specs=pl.BlockSpec((None, MOD_ROWS, MOD_TN), lambda l, n: (l, 0, n)),
        out_shape=jax.ShapeDtypeStruct((DEPTH, MOD_ROWS, width), F32),
        compiler_params=_cparams(("arbitrary", "arbitrary")),
        name="adaln_mod",
    )(cvec, w_mod, b_mod.reshape(DEPTH, 1, width))


def _pack_pairs(a, b):
    hi = lax.bitcast_convert_type(a.astype(BF16).astype(F32), jnp.uint32)
    lo = lax.bitcast_convert_type(b.astype(BF16).astype(F32), jnp.uint32)
    return hi | (lo >> 16)


def _unpack_pairs(packed):
    hi = lax.bitcast_convert_type(packed & jnp.uint32(0xFFFF0000), F32)
    lo = lax.bitcast_convert_type(packed << 16, F32)
    return jnp.concatenate([hi, lo], axis=-1)


def _mod_spec(l, k, nargs):
    if nargs == 1:
        return pl.BlockSpec((None, MOD_ROWS, D_MODEL), lambda i: (l, 0, k))
    return pl.BlockSpec((None, MOD_ROWS, D_MODEL), lambda i, j: (l, 0, k))


INP_TM = 256
INP_CH = 256
INP_PROMPT_TILES = N_PROMPT // INP_TM
Z_WIDTH = IN_WIDTH - SSM_WIDTH
INP_VMEM_LIMIT = 60 * 1024 * 1024


def _inproj_kernel(*refs, first, layer):
    if first:
        xa_ref, xb_ref, add_ref = refs[:3]
    else:
        xa_ref, add_ref, gain_ref = refs[:3]
    sc_ref, sh_ref, g_ref, ln_ref, w_hbm, xres_ref, u_ref, z_ref, h_scr, v_scr, w_ref, stage, sem = refs[3:]
    i = pl.program_id(0)
    seg = i // (SEG_TOK // INP_TM)

    @pl.when(i == 0)
    def _():
        n_chunks = IN_WIDTH // INP_CH

        def chunk(c):
            return pltpu.make_async_copy(w_hbm.at[layer, :, pl.ds(c * INP_CH, INP_CH)],
                                         stage.at[c % 2], sem.at[c % 2])

        chunk(0).start()
        for c in range(n_chunks):
            if c + 1 < n_chunks:
                chunk(c + 1).start()
            chunk(c).wait()
            w_ref[:, pl.ds(c * INP_CH, INP_CH)] = stage[c % 2].astype(BF16)

    if first:
        latent = i >= INP_PROMPT_TILES
        rows_per_tile = INP_TM // GRID_W
        row0 = (i % (DEC_SEQ // INP_TM)) * rows_per_tile
        row_part = jnp.concatenate(
            [jnp.broadcast_to(add_ref[0, pl.ds(row0 + q, 1), :], (GRID_W, D_MODEL // 2))
             for q in range(rows_per_tile)], axis=0)
        col_part = jnp.concatenate([add_ref[1]] * rows_per_tile, axis=0)
        x = jnp.where(latent, xb_ref[...] + jnp.concatenate([row_part, col_part], axis=1), xa_ref[...])
    else:
        x = xa_ref[...] + gain_ref[pl.ds(seg, 1), :] * _unpack_pairs(add_ref[...])
    xres_ref[...] = x
    ms = jnp.mean(x * x, axis=-1, keepdims=True)
    y = x * lax.rsqrt(ms + EPS) * g_ref[...]
    h = y * (1.0 + sc_ref[pl.ds(seg, 1), :]) + sh_ref[pl.ds(seg, 1), :]
    h_scr[...] = h.astype(BF16)

    def proj(col):
        return jnp.dot(h_scr[...], w_ref[:, pl.ds(col, INP_CH)], preferred_element_type=F32)

    n_ch = SSM_WIDTH // INP_CH
    for c in range(n_ch):
        acc = proj(c * INP_CH)
        for k in range(INP_CH // LANES):
            u_ref[c * (INP_CH // LANES) + k] = acc[:, k * LANES:(k + 1) * LANES]
    for c in range(n_ch):
        z_ref[:, pl.ds(c * INP_CH, INP_CH)] = jax.nn.gelu(proj(SSM_WIDTH + c * INP_CH)).astype(BF16)
    row_sum = jnp.zeros((INP_TM, 1), F32)
    for c in range(n_ch):
        v = jax.nn.gelu(proj(SSM_WIDTH + GM_WIDTH + c * INP_CH))
        v_scr[:, pl.ds(c * INP_CH, INP_CH)] = v
        row_sum = row_sum + jnp.sum(v, axis=-1, keepdims=True)
    mu = row_sum * (1.0 / GM_WIDTH)
    dev = v_scr[...] - mu
    var = jnp.mean(jnp.square(dev), axis=-1, keepdims=True)
    z_ref[:, pl.ds(GM_WIDTH, GM_WIDTH)] = (dev * lax.rsqrt(var + EPS) * ln_ref[...]).astype(BF16)
    gates = SSM_WIDTH + 2 * GM_WIDTH
    for c in range(2 * D_MODEL // INP_CH):
        z_ref[:, pl.ds(2 * GM_WIDTH + c * INP_CH, INP_CH)] = jax.nn.sigmoid(
            proj(gates + c * INP_CH)).astype(BF16)


def _inproj(l, xa, xb_or_add, add_or_gain, mod, norm1_g, gm_ln_g, w_in):
    first = l == 0
    row_tile = lambda m: pl.BlockSpec((INP_TM, D_MODEL), m)
    if first:
        lead = [row_tile(lambda i: (jnp.minimum(i, INP_PROMPT_TILES - 1), 0)),
                row_tile(lambda i: (jnp.maximum(i - INP_PROMPT_TILES, 0), 0)),
                pl.BlockSpec((2, DEC_SEQ // GRID_W, D_MODEL // 2), lambda i: (0, 0, 0))]
    else:
        lead = [row_tile(lambda i: (i, 0)), pl.BlockSpec((INP_TM, HALF_D), lambda i: (i, 0)),
                _mod_spec(l - 1, 5, 1)]
    vec = lambda w: pl.BlockSpec((None, 1, w), lambda i: (l, 0, 0))
    return pl.pallas_call(
        functools.partial(_inproj_kernel, first=first, layer=l),
        grid=(N_TOK // INP_TM,),
        in_specs=lead + [
            _mod_spec(l, 1, 1), _mod_spec(l, 0, 1),
            vec(D_MODEL), vec(GM_WIDTH),
            pl.BlockSpec(memory_space=pl.ANY),
        ],
        out_specs=[
            pl.BlockSpec((INP_TM, D_MODEL), lambda i: (i, 0)),
            pl.BlockSpec((SSM_WIDTH // LANES, INP_TM, LANES), lambda i: (0, i, 0)),
            pl.BlockSpec((INP_TM, Z_WIDTH), lambda i: (i, 0)),
        ],
        out_shape=[
            jax.ShapeDtypeStruct((N_TOK, D_MODEL), F32),
            jax.ShapeDtypeStruct((SSM_WIDTH // LANES, N_TOK, LANES), F32),
            jax.ShapeDtypeStruct((N_TOK, Z_WIDTH), BF16),
        ],
        scratch_shapes=[pltpu.VMEM((INP_TM, D_MODEL), BF16), pltpu.VMEM((INP_TM, GM_WIDTH), F32),
                        pltpu.VMEM((D_MODEL, IN_WIDTH), BF16), pltpu.VMEM((2, D_MODEL, INP_CH), F32),
                        pltpu.SemaphoreType.DMA((2,))],
        compiler_params=pltpu.CompilerParams(dimension_semantics=("arbitrary",),
                                             vmem_limit_bytes=INP_VMEM_LIMIT),
        name="in_proj",
    )(xa, xb_or_add, add_or_gain, mod, mod, norm1_g.reshape(DEPTH, 1, D_MODEL),
      gm_ln_g.reshape(DEPTH, 1, GM_WIDTH), w_in)


PK_BRE, PK_BIM, PK_CRE, PK_CIM = 0, 16, 32, 48
PK_LR, PK_LI = 64, 65


def _split_bf16(x):
    hi = x.astype(BF16)
    return hi, (x - hi.astype(F32)).astype(BF16)


def _dot_split(a, b):
    a_hi, a_lo = _split_bf16(a)
    b_hi, b_lo = _split_bf16(b)
    m = a.shape[0]
    both = jnp.dot(jnp.concatenate([a_hi, a_lo], axis=0), b_hi, preferred_element_type=F32)
    return both[0:m] + both[m:2 * m] + jnp.dot(a_hi, b_lo, preferred_element_type=F32)


POW_ROWS = 24


def _prep_kernel(pk_ref, row_ref, pow_ref, tile_ref, w1_ref, w2_ref, a_ref):
    p = SSM_STATE
    k_sub = jnp.minimum(lax.broadcasted_iota(jnp.int32, (POW_ROWS, LANES), 0), SCAN_T).astype(F32)
    lane = lax.broadcasted_iota(jnp.int32, (p, LANES), 1)
    col = lax.broadcasted_iota(jnp.int32, (SSM_GROUP, SCAN_W), 1)

    def spread_pow(x_r, x_i, which):
        parts = jnp.concatenate(_split_bf16(x_r) + _split_bf16(x_i), axis=0)
        out = jnp.dot(parts, pow_ref[which], preferred_element_type=F32)
        return out[0:p] + out[p:2 * p], out[2 * p:3 * p] + out[3 * p:4 * p]

    def spread_tiles(x):
        out = jnp.dot(jnp.concatenate(_split_bf16(x), axis=0), tile_ref[...], preferred_element_type=F32)
        out = out[0:p] + out[p:2 * p]
        return [out[:, n * SCAN_W:(n + 1) * SCAN_W] for n in range(4)]

    def group(g, _):
        rows = row_ref[g]
        grow_r = rows[0:1] * rows[2:3]
        grow_i = rows[1:2] * rows[2:3]
        mag = jnp.exp(grow_r * k_sub)
        ang = grow_i * k_sub
        unused = jnp.zeros((LANES - POW_ROWS, LANES), F32)
        pw_t_r = jnp.concatenate([mag * jnp.cos(ang), unused], axis=0).T
        pw_t_i = jnp.concatenate([mag * jnp.sin(ang), unused], axis=0).T
        per_dir = []
        for d in range(2):
            pk = pk_ref[d, g]
            lr = pk[:, PK_LR:PK_LR + 1]
            li = pk[:, PK_LI:PK_LI + 1]
            p_r = pw_t_r[d * p:(d + 1) * p]
            p_i = pw_t_i[d * p:(d + 1) * p]
            a_r = p_r[:, 1:2]
            a_i = p_i[:, 1:2]
            den = lr * lr + li * li
            q_r = ((a_r - 1.0) * lr + a_i * li) / den
            q_i = (a_i * lr - (a_r - 1.0) * li) / den
            per_dir.append((pk, p_r, p_i, q_r, q_i))

        w1_rows, lag, carry = [], [], []
        for d in range(2):
            pk, p_r, p_i, q_r, q_i = per_dir[d]
            b_r, b_i, c_r, c_i = spread_tiles(pk)
            bb_r = q_r * b_r - q_i * b_i
            bb_i = q_r * b_i + q_i * b_r
            pw_r, pw_i = spread_pow(p_r, p_i, 1 if d == 0 else 0)
            w1_rows.append((pw_r * bb_r - pw_i * bb_i, pw_r * bb_i + pw_i * bb_r))
            pl_r, pl_i = spread_pow(p_r, p_i, 0 if d == 0 else 1)
            cl_r = c_r * pl_r - c_i * pl_i
            cl_i = c_r * pl_i + c_i * pl_r
            pk_im = pltpu.roll(pk, LANES - (PK_BIM - PK_BRE), 1)
            bt_r = (q_r * pk - q_i * pk_im).T[0:SSM_GROUP, :]
            bt_i = (q_r * pk_im + q_i * pk).T[0:SSM_GROUP, :]
            lag.append(_dot_split(bt_r, cl_r) - _dot_split(bt_i, cl_i))
            pc_r, pc_i = spread_pow(p_r, p_i, 2 if d == 0 else 3)
            carry.append((c_r * pc_r - c_i * pc_i, -(c_r * pc_i + c_i * pc_r)))

        (f_re, f_im), (b_re, b_im) = w1_rows
        w1_ref[g] = jnp.concatenate([f_re, b_re, f_im, b_im], axis=0).T.astype(BF16)

        for s in range(SCAN_T):
            fwd = lag[0] if s == 0 else pltpu.roll(lag[0], SSM_GROUP * s, 1)
            fwd = jnp.where(col >= SSM_GROUP * s, fwd, 0.0)
            shift_b = SSM_GROUP * (SCAN_T - 1 - s)
            bwd = lag[1] if shift_b == 0 else pltpu.roll(lag[1], SCAN_W - shift_b, 1)
            bwd = jnp.where(col < SSM_GROUP * (s + 1), bwd, 0.0)
            w2_ref[g, pl.ds(SSM_GROUP * s, SSM_GROUP), :] = (fwd + bwd).astype(BF16)
        (x_re, x_im), (y_re, y_im) = carry
        for n, rows in enumerate((x_re, y_re, x_im, y_im)):
            w2_ref[g, pl.ds(SCAN_W + SSM_STATE * n, SSM_STATE), :] = rows.astype(BF16)

        cols = [per_dir[0][1], per_dir[1][1], per_dir[0][2], per_dir[1][2]]
        a_cols = jnp.zeros((SSM_STATE, LANES), F32)
        for n, c in enumerate(cols):
            a_cols = jnp.where(lane == n, c[:, SCAN_T:SCAN_T + 1], a_cols)
        a_ref[g] = a_cols
        return 0

    lax.fori_loop(0, G_OCT, group, 0, unroll=4)


def _s5_prep(lam_re, lam_im, log_step, b_re, b_im, c_re, c_im):
    shape = (DEPTH, 2, SSM_GROUPS, SSM_STATE)
    lr = lam_re.astype(F32)
    li = lam_im.astype(F32)
    dt = jnp.broadcast_to(jnp.exp(log_step.astype(F32))[..., None], shape)
    pk = jnp.concatenate([
        b_re.astype(F32), b_im.astype(F32),
        jnp.swapaxes(c_re.astype(F32), -1, -2), jnp.swapaxes(c_im.astype(F32), -1, -2),
        lr[..., None], li[..., None],
        jnp.zeros(shape + (LANES - PK_LI - 1,), F32)], axis=-1)
    both_dirs = lambda a: jnp.concatenate([a[:, 0], a[:, 1]], axis=-1)
    rows = jnp.stack([both_dirs(lr), both_dirs(li), both_dirs(dt)], axis=2)
    rows = jnp.concatenate([rows, jnp.zeros((DEPTH, SSM_GROUPS, SUBLANES - 3, LANES), F32)], axis=2)

    blk = jnp.arange(SCAN_W) // SSM_GROUP
    k = jnp.arange(LANES)[:, None]
    pows = [k == blk[None, :], k == (SCAN_T - 1 - blk)[None, :], k == (blk + 1)[None, :],
            k == (SCAN_T - blk)[None, :]]
    h = (jnp.arange(SCAN_W) % SSM_GROUP)[None, :]
    sel_pow = jnp.stack(pows).astype(BF16)
    sel_tile = jnp.concatenate([k == h + off for off in (PK_BRE, PK_BIM, PK_CRE, PK_CIM)],
                               axis=1).astype(BF16)

    n_oct = SSM_GROUPS // G_OCT
    w1, w2, a_cols = pl.pallas_call(
        _prep_kernel,
        grid=(DEPTH, n_oct),
        in_specs=[
            pl.BlockSpec((None, 2, G_OCT, SSM_STATE, LANES), lambda l, o: (l, 0, o, 0, 0)),
            pl.BlockSpec((None, G_OCT, SUBLANES, LANES), lambda l, o: (l, o, 0, 0)),
            pl.BlockSpec((4, LANES, SCAN_W), lambda l, o: (0, 0, 0)),
            pl.BlockSpec((LANES, 4 * SCAN_W), lambda l, o: (0, 0)),
        ],
        out_specs=[
            pl.BlockSpec((None, G_OCT, SCAN_W, STATE_W), lambda l, o: (l, o, 0, 0)),
            pl.BlockSpec((None, G_OCT, SCAN_W + STATE_W, SCAN_W), lambda l, o: (l, o, 0, 0)),
            pl.BlockSpec((None, G_OCT, SSM_STATE, LANES), lambda l, o: (l, o, 0, 0)),
        ],
        out_shape=[
            jax.ShapeDtypeStruct((DEPTH, SSM_GROUPS, SCAN_W, STATE_W), BF16),
            jax.ShapeDtypeStruct((DEPTH, SSM_GROUPS, SCAN_W + STATE_W, SCAN_W), BF16),
            jax.ShapeDtypeStruct((DEPTH, SSM_GROUPS, SSM_STATE, LANES), F32),
        ],
        compiler_params=_cparams(("arbitrary", "arbitrary")),
        name="s5_prep",
    )(pk, rows, sel_pow, sel_tile)
    a16 = jnp.swapaxes(a_cols[..., 0:4], -1, -2).reshape(DEPTH, SSM_GROUPS, STATE_W)
    return w1, w2, a16


def _s5_kernel(u_ref, w1_ref, w2_ref, a_ref, h0_ref, d_ref, y_ref, fs_ref,
               t_scr, ug_scr, vr_scr, vi_scr, cr_scr, ci_scr, fr_scr, fi_scr):
    blk = pl.program_id(1)
    seq_rows = jnp.where(blk == 0, SEQ // SCAN_T, DEC_SEQ // SCAN_T)

    for s in range(SCAN_T):
        for j in range(SCAN_GW // LANES):
            t_scr[s, pl.ds(j * LANES, LANES), :] = (
                u_ref[j, pl.ds(s, SCAN_BLK, stride=SCAN_T), :].astype(BF16).T)
    for g in range(SCAN_G):
        stacked = t_scr[:, pl.ds(g * SSM_GROUP, SSM_GROUP), :].reshape(SCAN_W, SCAN_BLK)
        ug_scr[g] = stacked.T

    for g in range(SCAN_G):
        v = jnp.dot(ug_scr[g], w1_ref[g], preferred_element_type=F32)
        vr_scr[pl.ds(g, SCAN_BLK, stride=SCAN_G), :] = v[:, 0:HALF_W]
        vi_scr[pl.ds(g, SCAN_BLK, stride=SCAN_G), :] = v[:, HALF_W:STATE_W]

    a_r = a_ref[:, 0:HALF_W]
    a_i = a_ref[:, HALF_W:STATE_W]
    h0_r = h0_ref[:, 0:HALF_W]
    h0_i = h0_ref[:, HALF_W:STATE_W]
    fwd_lanes = lax.broadcasted_iota(jnp.int32, (SCAN_G, HALF_W), 1) < SSM_STATE
    bwd_lanes = jnp.logical_not(fwd_lanes)

    def step(k, carry):
        s_r, s_i = carry
        rf = pl.ds(pl.multiple_of(k * SCAN_G, SCAN_G), SCAN_G)
        rb = pl.ds(pl.multiple_of((SCAN_BLK - 1 - k) * SCAN_G, SCAN_G), SCAN_G)
        restart = (k & (seq_rows - 1)) == 0
        s_r = jnp.where(restart, h0_r, s_r)
        s_i = jnp.where(restart, h0_i, s_i)
        pltpu.store(cr_scr.at[rf, :], s_r, mask=fwd_lanes)
        pltpu.store(cr_scr.at[rb, :], s_r, mask=bwd_lanes)
        pltpu.store(ci_scr.at[rf, :], s_i, mask=fwd_lanes)
        pltpu.store(ci_scr.at[rb, :], s_i, mask=bwd_lanes)
        v_r = jnp.where(fwd_lanes, vr_scr[rf, :], vr_scr[rb, :])
        v_i = jnp.where(fwd_lanes, vi_scr[rf, :], vi_scr[rb, :])
        n_r = a_r * s_r - a_i * s_i + v_r
        n_i = a_r * s_i + a_i * s_r + v_i
        fr_scr[rf, :] = n_r
        fi_scr[rf, :] = n_i
        return n_r, n_i

    zero = jnp.zeros((SCAN_G, HALF_W), F32)
    lax.fori_loop(0, SCAN_BLK, step, (zero, zero), unroll=4)

    for g in range(SCAN_G):
        c_r = cr_scr[pl.ds(g, SCAN_BLK, stride=SCAN_G), :].astype(BF16)
        c_i = ci_scr[pl.ds(g, SCAN_BLK, stride=SCAN_G), :].astype(BF16)
        y = jnp.dot(ug_scr[g], w2_ref[g, 0:SCAN_W, :], preferred_element_type=F32)
        y = y + jnp.dot(c_r, w2_ref[g, SCAN_W:SCAN_W + HALF_W, :], preferred_element_type=F32)
        y = y + jnp.dot(c_i, w2_ref[g, SCAN_W + HALF_W:SCAN_W + STATE_W, :], preferred_element_type=F32)
        y = y + d_ref[pl.ds(g, 1), :] * ug_scr[g].astype(F32)
        t_scr[:, pl.ds(g * SSM_GROUP, SSM_GROUP), :] = y.astype(BF16).T.reshape(SCAN_T, SSM_GROUP, SCAN_BLK)
    for s in range(SCAN_T):
        for j in range(SCAN_GW // LANES):
            y_ref[j, pl.ds(s, SCAN_BLK, stride=SCAN_T), :] = t_scr[s, pl.ds(j * LANES, LANES), :].T.astype(F32)

    rows_per_seq = SEQ // SCAN_T
    for q in range(SCAN_BLK // rows_per_seq):
        last = pl.ds((q * rows_per_seq + rows_per_seq - 1) * SCAN_G, SCAN_G)
        fs_ref[q, :, 0:HALF_W] = fr_scr[last, :]
        fs_ref[q, :, HALF_W:STATE_W] = fi_scr[last, :]


def _s5_scan(l, u, w1, w2, a16, h0, d_lanes):
    n_oct = SSM_GROUPS // SCAN_G
    n_fin = SCAN_BLK // (SEQ // SCAN_T)
    return pl.pallas_call(
        _s5_kernel,
        grid=(n_oct, N_SCAN_BLK),
        in_specs=[
            pl.BlockSpec((SCAN_GW // LANES, SCAN_TOK, LANES), lambda o, b: (o, b, 0)),
            pl.BlockSpec((None, SCAN_G, SCAN_W, STATE_W), lambda o, b: (l, o, 0, 0)),
            pl.BlockSpec((None, SCAN_G, SCAN_W + STATE_W, SCAN_W), lambda o, b: (l, o, 0, 0)),
            pl.BlockSpec((None, SCAN_G, STATE_W), lambda o, b: (l, o, 0)),
            pl.BlockSpec((None, None, SCAN_G, STATE_W), lambda o, b: (l, b, o, 0)),
            pl.BlockSpec((None, SCAN_G, SCAN_W), lambda o, b: (l, o, 0)),
        ],
        out_specs=[
            pl.BlockSpec((SCAN_GW // LANES, SCAN_TOK, LANES), lambda o, b: (o, b, 0)),
            pl.BlockSpec((None, n_fin, SCAN_G, STATE_W), lambda o, b: (b, 0, o, 0)),
        ],
        out_shape=[
            jax.ShapeDtypeStruct((SSM_WIDTH // LANES, N_TOK, LANES), F32),
            jax.ShapeDtypeStruct((N_SCAN_BLK, n_fin, SSM_GROUPS, STATE_W), F32),
        ],
        scratch_shapes=[
            pltpu.VMEM((SCAN_T, SCAN_GW, SCAN_BLK), BF16),
            pltpu.VMEM((SCAN_G, SCAN_BLK, SCAN_W), BF16),
        ] + [pltpu.VMEM((SCAN_BLK * SCAN_G, HALF_W), F32) for _ in range(6)],
        compiler_params=_cparams(("arbitrary", "arbitrary")),
        name="s5_scan",
    )(u, w1, w2, a16, h0, d_lanes)


MIX_TM = 512


def _mix_kernel(ys_ref, gu_ref, vn_ref, wglu_ref, bglu_ref, ws_ref, bs_ref, ya_ref, yb_ref):
    y = jnp.concatenate([ys_ref[k] for k in range(SSM_WIDTH // LANES)], axis=1)
    y = jax.nn.gelu(y)
    gate = jnp.dot(y.astype(BF16), wglu_ref[...], preferred_element_type=F32) + bglu_ref[...]
    ya_ref[...] = (y * jax.nn.sigmoid(gate)).astype(BF16)
    for c in range(MIX_TM // GM_CHUNK):
        rows = pl.ds(c * GM_CHUNK, GM_CHUNK)
        for g in range(GM_GROUPS):
            cols = pl.ds(g * GM_GROUP_DIM, GM_GROUP_DIM)
            mixed = jnp.dot(ws_ref[g], vn_ref[rows, cols], preferred_element_type=F32) + bs_ref[:, cols]
            yb_ref[rows, cols] = (gu_ref[rows, cols].astype(F32) * mixed).astype(BF16)


def _mix(l, ys, z, w_glu_bf16, b_glu, w_s_bf16, b_s_full):
    tile = lambda k: pl.BlockSpec((MIX_TM, SSM_WIDTH), lambda i: (i, k))
    slab = pl.BlockSpec((SSM_WIDTH // LANES, MIX_TM, LANES), lambda i: (0, i, 0))
    lay = lambda *shape: pl.BlockSpec((None,) + shape, lambda i: (l,) + tuple(0 for _ in shape))
    return pl.pallas_call(
        _mix_kernel,
        grid=(N_TOK // MIX_TM,),
        in_specs=[
            slab, tile(0), tile(1),
            lay(SSM_WIDTH, SSM_WIDTH),
            lay(1, SSM_WIDTH),
            lay(GM_GROUPS, GM_CHUNK, GM_CHUNK),
            lay(GM_CHUNK, GM_WIDTH),
        ],
        out_specs=[tile(0), tile(0)],
        out_shape=[
            jax.ShapeDtypeStruct((N_TOK, SSM_WIDTH), BF16),
            jax.ShapeDtypeStruct((N_TOK, GM_WIDTH), BF16),
        ],
        compiler_params=_cparams(("arbitrary",)),
        name="mixers",
    )(ys, z, z, w_glu_bf16, b_glu.reshape(DEPTH, 1, SSM_WIDTH), w_s_bf16, b_s_full)


MRG_TM = 512
MRG_SUB = 256
HALF_D = D_MODEL // 2
REC_W = HALF_D + LANES


def _merge_kernel(ya_ref, yb_ref, ga0_ref, ga1_ref, gb0_ref, gb1_ref, x_ref, g1_ref, sc_ref, sh_ref,
                  n2_ref, wpa_ref, wpb_ref, wo_ref, wr_ref, br_ref, tri_ref,
                  xmid_ref, rec_ref, gid_ref, rank_ref, cnt_ref, cnt_scr):
    i = pl.program_id(0)
    seg = i // (SEG_TOK // MRG_TM)

    @pl.when(i == 0)
    def _():
        cnt_scr[...] = jnp.zeros(cnt_scr.shape, F32)

    passes = [pl.ds(sub * MRG_SUB, MRG_SUB) for sub in range(MRG_TM // MRG_SUB)]
    for rows in passes:
        _merge_mix(rows, seg, ya_ref, yb_ref, ga0_ref, ga1_ref, gb0_ref, gb1_ref, x_ref, g1_ref,
                   wpa_ref, wpb_ref, wo_ref, xmid_ref)
    for rows in passes:
        _merge_route(rows, seg, sc_ref, sh_ref, n2_ref, wr_ref, br_ref, tri_ref,
                     xmid_ref, rec_ref, gid_ref, rank_ref, cnt_scr)
    cnt_ref[...] = jnp.broadcast_to(cnt_scr[...], cnt_ref.shape)


def _merge_mix(rows, seg, ya_ref, yb_ref, ga0_ref, ga1_ref, gb0_ref, gb1_ref, x_ref, g1_ref,
               wpa_ref, wpb_ref, wo_ref, xmid_ref):
    pa = jnp.dot(ya_ref[rows, :], wpa_ref[...], preferred_element_type=F32)
    pb = jnp.dot(yb_ref[rows, :], wpb_ref[...], preferred_element_type=F32)
    m_lo = ga0_ref[rows, :].astype(F32) * pa[:, :HALF_D] + gb0_ref[rows, :].astype(F32) * pb[:, :HALF_D]
    m_hi = ga1_ref[rows, :].astype(F32) * pa[:, HALF_D:] + gb1_ref[rows, :].astype(F32) * pb[:, HALF_D:]
    mix = jnp.dot(m_lo.astype(BF16), wo_ref[0:HALF_D, :], preferred_element_type=F32)
    mix = mix + jnp.dot(m_hi.astype(BF16), wo_ref[HALF_D:D_MODEL, :], preferred_element_type=F32)
    xmid_ref[rows, :] = x_ref[rows, :] + g1_ref[pl.ds(seg, 1), :] * mix


def _merge_route(rows, seg, sc_ref, sh_ref, n2_ref, wr_ref, br_ref, tri_ref,
                 xmid_ref, rec_ref, gid_ref, rank_ref, cnt_scr):
    x = xmid_ref[rows, :]
    ms = jnp.mean(x * x, axis=-1, keepdims=True)
    y = x * lax.rsqrt(ms + EPS) * n2_ref[...]
    h2 = y * (1.0 + sc_ref[pl.ds(seg, 1), :]) + sh_ref[pl.ds(seg, 1), :]
    hi = h2.astype(BF16)
    hi_f = hi.astype(F32)
    lo = (h2 - hi_f).astype(BF16)
    bits = lax.bitcast_convert_type(hi_f, jnp.uint32)
    rec_ref[rows, 0:HALF_D] = bits[:, :HALF_D] | (bits[:, HALF_D:] >> 16)

    nt = (((1,), (1,)), ((), ()))
    lt = (lax.dot_general(wr_ref[...], hi, nt, preferred_element_type=F32)
          + lax.dot_general(wr_ref[...], lo, nt, preferred_element_type=F32))
    logits = lt[0:N_EXPERTS] + lt[N_EXPERTS:2 * N_EXPERTS]
    scores = jax.nn.sigmoid(logits)
    sel = scores + br_ref[...]
    ng = N_EXPERT_GROUPS
    s = [sel[j * ng:(j + 1) * ng] for j in range(EXPERTS_PER_GROUP)]
    p = [scores[j * ng:(j + 1) * ng] for j in range(EXPERTS_PER_GROUP)]
    a, b = jnp.maximum(s[0], s[1]), jnp.minimum(s[0], s[1])
    c, d = jnp.maximum(s[2], s[3]), jnp.minimum(s[2], s[3])
    grp_score = jnp.maximum(a, c) + jnp.maximum(jnp.minimum(a, c), jnp.maximum(b, d))
    best = jnp.max(grp_score, axis=0, keepdims=True)
    g_iota = lax.broadcasted_iota(jnp.int32, grp_score.shape, 0)
    g_idx = jnp.min(jnp.where(grp_score == best, g_iota, ng), axis=0, keepdims=True)
    onehot = g_iota == g_idx
    v = [jnp.sum(jnp.where(onehot, sj, 0.0), axis=0, keepdims=True) for sj in s]
    q = [jnp.sum(jnp.where(onehot, pj, 0.0), axis=0, keepdims=True) for pj in p]
    picked = []
    for j in range(EXPERTS_PER_GROUP):
        rank = jnp.zeros(v[j].shape, jnp.int32)
        for o in range(EXPERTS_PER_GROUP):
            if o == j:
                continue
            ahead = (v[o] > v[j]) | ((v[o] == v[j]) & (o < j))
            rank = rank + ahead.astype(jnp.int32)
        picked.append(jnp.where(rank < 2, q[j], 0.0))
    total = picked[0] + picked[1] + picked[2] + picked[3]
    gid_ref[:, rows] = g_idx
    cw_rows = jnp.concatenate([pj / total for pj in picked]
                              + [jnp.zeros((LANES - EXPERTS_PER_GROUP, MRG_SUB), F32)], axis=0)
    rec_ref[rows, HALF_D:REC_W] = lax.bitcast_convert_type(cw_rows.T, jnp.uint32)

    hot = onehot.astype(BF16)
    within = jnp.dot(hot, tri_ref[...], preferred_element_type=F32)
    before = jnp.sum(jnp.where(onehot, within + cnt_scr[...], 0.0), axis=0, keepdims=True) - 1.0
    rank_ref[:, rows] = before.astype(jnp.int32)
    cnt_scr[...] = cnt_scr[...] + within[:, MRG_SUB - 1:MRG_SUB]


def _merge(l, ya, yb, z, xres, mod, norm2_g, w_pa, w_pb, w_o, wr_t, br_col):
    n_t = N_TOK // MRG_TM
    tri = (jnp.arange(MRG_SUB)[:, None] <= jnp.arange(MRG_SUB)[None, :]).astype(BF16)
    zspec = lambda k: pl.BlockSpec((MRG_TM, HALF_D), lambda i: (i, k))
    once = pl.Buffered(1)
    lay = lambda *shape: pl.BlockSpec((None,) + shape, lambda i: (l,) + tuple(0 for _ in shape),
                                      pipeline_mode=once)
    const = lambda *shape: pl.BlockSpec(shape, lambda i: tuple(0 for _ in shape), pipeline_mode=once)
    return pl.pallas_call(
        _merge_kernel,
        grid=(n_t,),
        in_specs=[
            pl.BlockSpec((MRG_TM, SSM_WIDTH), lambda i: (i, 0)),
            pl.BlockSpec((MRG_TM, GM_WIDTH), lambda i: (i, 0)),
            zspec(2), zspec(3), zspec(4), zspec(5),
            pl.BlockSpec((MRG_TM, D_MODEL), lambda i: (i, 0)),
            _mod_spec(l, 2, 1), _mod_spec(l, 4, 1), _mod_spec(l, 3, 1),
            lay(1, D_MODEL),
            lay(SSM_WIDTH, D_MODEL),
            lay(GM_WIDTH, D_MODEL),
            lay(D_MODEL, D_MODEL),
            const(2 * N_EXPERTS, D_MODEL),
            const(N_EXPERTS, 1),
            const(MRG_SUB, MRG_SUB),
        ],
        out_specs=[
            pl.BlockSpec((MRG_TM, D_MODEL), lambda i: (i, 0)),
            pl.BlockSpec((MRG_TM, REC_W), lambda i: (i, 0)),
            pl.BlockSpec((None, 1, MRG_TM), lambda i: (i, 0, 0)),
            pl.BlockSpec((None, 1, MRG_TM), lambda i: (i, 0, 0)),
            pl.BlockSpec((N_EXPERT_GROUPS, LANES), lambda i: (0, 0)),
        ],
        out_shape=[
            jax.ShapeDtypeStruct((N_TOK, D_MODEL), F32),
            jax.ShapeDtypeStruct((N_TOK, REC_W), jnp.uint32),
            jax.ShapeDtypeStruct((n_t, 1, MRG_TM), jnp.int32),
            jax.ShapeDtypeStruct((n_t, 1, MRG_TM), jnp.int32),
            jax.ShapeDtypeStruct((N_EXPERT_GROUPS, LANES), F32),
        ],
        scratch_shapes=[pltpu.VMEM((N_EXPERT_GROUPS, 1), F32)],
        compiler_params=_cparams(("arbitrary",)),
        name="merge_router",
    )(ya, yb, z, z, z, z, xres, mod, mod, mod, norm2_g.reshape(DEPTH, 1, D_MODEL),
      w_pa, w_pb, w_o, wr_t, br_col, tri)


DSP_TM = 1024


def _dispatch_kernel(pos_ref, pend_ref, rec_ref, out_ref, zero_scr, sem):
    step = pl.program_id(0)

    @pl.when(step == 0)
    def _():
        zero_scr[...] = jnp.zeros(zero_scr.shape, jnp.uint32)
        for g in range(N_EXPERT_GROUPS):
            start = pl.multiple_of(jnp.maximum(pend_ref[g] - MOE_BLK, 0), MOE_BLK)
            fill = pltpu.make_async_copy(zero_scr, out_ref.at[pl.ds(start, MOE_BLK)], sem)
            fill.start()
            fill.wait()
        for blk in range(N_TOK // MOE_BLK, MOE_NBLK):
            @pl.when(blk * MOE_BLK >= pend_ref[N_EXPERT_GROUPS - 1])
            def _():
                fill = pltpu.make_async_copy(zero_scr, out_ref.at[pl.ds(blk * MOE_BLK, MOE_BLK)], sem)
                fill.start()
                fill.wait()

    base = step * DSP_TM
    for r in range(DSP_TM):
        pltpu.make_async_copy(rec_ref.at[pl.ds(r, 1)], out_ref.at[pl.ds(pos_ref[base + r], 1)], sem).start()
    pltpu.make_async_copy(rec_ref, out_ref.at[pl.ds(0, DSP_TM)], sem).wait()


def _dispatch(pos, pend, rec):
    grid_spec = pltpu.PrefetchScalarGridSpec(
        num_scalar_prefetch=2,
        grid=(N_TOK // DSP_TM,),
        in_specs=[pl.BlockSpec((DSP_TM, REC_W), lambda i, pos, pend: (i, 0))],
        out_specs=pl.BlockSpec(memory_space=pl.ANY),
        scratch_shapes=[pltpu.VMEM((MOE_BLK, REC_W), jnp.uint32), pltpu.SemaphoreType.DMA(())],
    )
    return pl.pallas_call(
        _dispatch_kernel,
        grid_spec=grid_spec,
        out_shape=jax.ShapeDtypeStruct((MOE_SLOTS, REC_W), jnp.uint32),
        compiler_params=_cparams(("arbitrary",)),
        name="moe_dispatch",
    )(pos, pend, rec)


UP_EXPERTS = 2


MOE_SUB = 256


UP_CH = 256


def _up_inputs(rec_ref, rows, first_expert):
    packed = rec_ref[rows, 0:HALF_D]
    x_lo = lax.bitcast_convert_type(packed & jnp.uint32(0xFFFF0000), F32).astype(BF16)
    x_hi = lax.bitcast_convert_type(packed << 16, F32).astype(BF16)
    cw = lax.bitcast_convert_type(rec_ref[rows, HALF_D:REC_W], F32)
    lane = lax.broadcasted_iota(jnp.int32, cw.shape, 1)
    w_rows = [jnp.sum(jnp.where(lane == first_expert + e, cw, 0.0), axis=1, keepdims=True)
              for e in range(UP_EXPERTS)]
    return x_lo, x_hi, w_rows


def _up_hidden(x_lo, x_hi, w_row, wg, wu):
    gate = (jnp.dot(x_lo, wg[0:HALF_D, :], preferred_element_type=F32)
            + jnp.dot(x_hi, wg[HALF_D:D_MODEL, :], preferred_element_type=F32))
    up = (jnp.dot(x_lo, wu[0:HALF_D, :], preferred_element_type=F32)
          + jnp.dot(x_hi, wu[HALF_D:D_MODEL, :], preferred_element_type=F32))
    return (gate * jax.nn.sigmoid(gate) * up * w_row).astype(BF16)


def _expert_up_kernel(gid_ref, fill_ref, last_ref, rec_ref, wg_ref, wu_ref, h_ref, wg_scr, wu_scr):
    del last_ref
    first_expert = pl.program_id(0) * UP_EXPERTS
    b = pl.program_id(1)
    prev = gid_ref[jnp.maximum(b - 1, 0)]
    fresh = (b == 0) | (gid_ref[b] != prev)
    passes = [pl.ds(sub * MOE_SUB, MOE_SUB) for sub in range(MOE_BLK // MOE_SUB)]

    @pl.when(fresh)
    def _():
        inputs = [_up_inputs(rec_ref, rows, first_expert) for rows in passes]
        for e in range(UP_EXPERTS):
            for c in range(EXPERT_FF // UP_CH):
                cols = pl.ds(c * UP_CH, UP_CH)
                wg = wg_ref[e, :, cols].astype(BF16)
                wu = wu_ref[e, :, cols].astype(BF16)
                wg_scr[e, :, cols] = wg
                wu_scr[e, :, cols] = wu
                for rows, (x_lo, x_hi, w_rows) in zip(passes, inputs):
                    h_ref[rows, pl.ds(e * EXPERT_FF + c * UP_CH, UP_CH)] = _up_hidden(
                        x_lo, x_hi, w_rows[e], wg, wu)

    @pl.when(jnp.logical_not(fresh))
    def _():
        for sub, rows in enumerate(passes):
            @pl.when(fill_ref[b] > sub * MOE_SUB)
            def _():
                x_lo, x_hi, w_rows = _up_inputs(rec_ref, rows, first_expert)
                for e in range(UP_EXPERTS):
                    h_ref[rows, pl.ds(e * EXPERT_FF, EXPERT_FF)] = _up_hidden(
                        x_lo, x_hi, w_rows[e], wg_scr[e], wu_scr[e])

            @pl.when(fill_ref[b] <= sub * MOE_SUB)
            def _():
                h_ref[rows, :] = jnp.zeros((MOE_SUB, UP_EXPERTS * EXPERT_FF), BF16)


def _expert_up(l, blk_gid, blk_fill, blk_last, rec_sorted, e_gate, e_up):
    halves = EXPERTS_PER_GROUP // UP_EXPERTS
    wspec = pl.BlockSpec((None, UP_EXPERTS, D_MODEL, EXPERT_FF),
                         lambda h, b, gid, fill, last: (l, gid[b] * halves + h, 0, 0))
    grid_spec = pltpu.PrefetchScalarGridSpec(
        num_scalar_prefetch=3,
        grid=(halves, MOE_NBLK),
        in_specs=[
            pl.BlockSpec((MOE_BLK, REC_W), lambda h, b, gid, fill, last: (jnp.minimum(b, last[0]), 0)),
            wspec, wspec,
        ],
        out_specs=pl.BlockSpec((MOE_BLK, UP_EXPERTS * EXPERT_FF), lambda h, b, gid, fill, last: (b, h)),
        scratch_shapes=[pltpu.VMEM((UP_EXPERTS, D_MODEL, EXPERT_FF), BF16),
                        pltpu.VMEM((UP_EXPERTS, D_MODEL, EXPERT_FF), BF16)],
    )
    return pl.pallas_call(
        _expert_up_kernel,
        grid_spec=grid_spec,
        out_shape=jax.ShapeDtypeStruct((MOE_SLOTS, GROUP_FF), BF16),
        compiler_params=_cparams(("arbitrary", "arbitrary")),
        name="expert_up",
    )(blk_gid, blk_fill, blk_last, rec_sorted, e_gate, e_up)


DOWN_CH = 256


def _expert_down_kernel(gid_ref, fill_ref, h_ref, wd_ref, y_ref, wd_scr):
    b = pl.program_id(0)
    prev = gid_ref[jnp.maximum(b - 1, 0)]
    fresh = (b == 0) | (gid_ref[b] != prev)

    passes = [pl.ds(sub * MOE_SUB, MOE_SUB) for sub in range(MOE_BLK // MOE_SUB)]

    @pl.when(fresh)
    def _():
        for c in range(HALF_D // DOWN_CH):
            cols = pl.ds(c * DOWN_CH, DOWN_CH)
            pair = pl.ds(HALF_D + c * DOWN_CH, DOWN_CH)
            w_a = wd_ref[:, cols].astype(BF16)
            w_b = wd_ref[:, pair].astype(BF16)
            wd_scr[:, cols] = w_a
            wd_scr[:, pair] = w_b
            for rows in passes:
                hid = h_ref[rows, :]
                y_ref[rows, cols] = _pack_pairs(jnp.dot(hid, w_a, preferred_element_type=F32),
                                                jnp.dot(hid, w_b, preferred_element_type=F32))

    @pl.when(jnp.logical_not(fresh))
    def _():
        for sub, rows in enumerate(passes):
            @pl.when(fill_ref[b] > sub * MOE_SUB)
            def _():
                hid = h_ref[rows, :]
                for c in range(HALF_D // DOWN_CH):
                    cols = pl.ds(c * DOWN_CH, DOWN_CH)
                    pair = pl.ds(HALF_D + c * DOWN_CH, DOWN_CH)
                    y_ref[rows, cols] = _pack_pairs(jnp.dot(hid, wd_scr[:, cols], preferred_element_type=F32),
                                                    jnp.dot(hid, wd_scr[:, pair], preferred_element_type=F32))

            @pl.when(fill_ref[b] <= sub * MOE_SUB)
            def _():
                y_ref[rows, :] = jnp.zeros((MOE_SUB, HALF_D), jnp.uint32)


def _expert_down(l, blk_gid, blk_fill, h_sorted, e_down_grouped):
    grid_spec = pltpu.PrefetchScalarGridSpec(
        num_scalar_prefetch=2,
        grid=(MOE_NBLK,),
        in_specs=[
            pl.BlockSpec((MOE_BLK, GROUP_FF), lambda b, gid, fill: (b, 0)),
            pl.BlockSpec((None, None, GROUP_FF, D_MODEL), lambda b, gid, fill: (l, gid[b], 0, 0)),
        ],
        out_specs=pl.BlockSpec((MOE_BLK, HALF_D), lambda b, gid, fill: (b, 0)),
        scratch_shapes=[pltpu.VMEM((GROUP_FF, D_MODEL), BF16)],
    )
    return pl.pallas_call(
        _expert_down_kernel,
        grid_spec=grid_spec,
        out_shape=jax.ShapeDtypeStruct((MOE_SLOTS, HALF_D), jnp.uint32),
        compiler_params=_cparams(("arbitrary",)),
        name="expert_down",
    )(blk_gid, blk_fill, h_sorted, e_down_grouped)


def _moe(l, rec, gid, rank, counts, e_gate, e_up, e_down_grouped):
    padded = (counts + MOE_BLK - 1) // MOE_BLK * MOE_BLK
    pend = jnp.cumsum(padded)
    pstart = pend - padded
    pos = (pstart[gid] + rank).astype(jnp.int32)
    blk_start = jnp.arange(MOE_NBLK, dtype=jnp.int32) * MOE_BLK
    blk_gid = jnp.minimum(jnp.sum((blk_start[:, None] >= pend[None, :]).astype(jnp.int32), axis=1),
                          N_EXPERT_GROUPS - 1)
    blk_fill = jnp.clip(pstart[blk_gid] + counts[blk_gid] - blk_start, 0, MOE_BLK)
    blk_fill = jnp.where(blk_start < pend[-1], blk_fill, 0).astype(jnp.int32)
    blk_last = (pend[-1:] // MOE_BLK - 1).astype(jnp.int32)
    rec_sorted = _dispatch(pos, pend.astype(jnp.int32), rec)
    hid = _expert_up(l, blk_gid, blk_fill, blk_last, rec_sorted, e_gate, e_up)
    y_sorted = _expert_down(l, blk_gid, blk_fill, hid, e_down_grouped)
    return y_sorted[pos]


FIN_TM = 512
FIN_PROMPT_TILES = N_PROMPT // FIN_TM


def _final_kernel(x_ref, y_ref, g2_ref, fg_ref, op_ref, os_ref):
    i = pl.program_id(0)
    seg = i // (SEG_TOK // FIN_TM)
    x = x_ref[...] + g2_ref[pl.ds(seg, 1), :] * _unpack_pairs(y_ref[...])
    ms = jnp.mean(x * x, axis=-1, keepdims=True)
    out = x * lax.rsqrt(ms + EPS) * fg_ref[...]

    @pl.when(i < FIN_PROMPT_TILES)
    def _():
        op_ref[...] = out

    @pl.when(i >= FIN_PROMPT_TILES)
    def _():
        os_ref[...] = out


def _final_norm(xmid, moe_y, mod, final_g):
    return pl.pallas_call(
        _final_kernel,
        grid=(N_TOK // FIN_TM,),
        in_specs=[
            pl.BlockSpec((FIN_TM, D_MODEL), lambda i: (i, 0)),
            pl.BlockSpec((FIN_TM, HALF_D), lambda i: (i, 0)),
            _mod_spec(DEPTH - 1, 5, 1),
            pl.BlockSpec((1, D_MODEL), lambda i: (0, 0)),
        ],
        out_specs=[
            pl.BlockSpec((FIN_TM, D_MODEL), lambda i: (jnp.minimum(i, FIN_PROMPT_TILES - 1), 0)),
            pl.BlockSpec((FIN_TM, D_MODEL), lambda i: (jnp.maximum(i - FIN_PROMPT_TILES, 0), 0)),
        ],
        out_shape=[
            jax.ShapeDtypeStruct((N_PROMPT, D_MODEL), F32),
            jax.ShapeDtypeStruct((N_SAMPLE, D_MODEL), F32),
        ],
        compiler_params=_cparams(("arbitrary",)),
        name="final_norm",
    )(xmid, moe_y, mod, final_g.reshape(1, D_MODEL))


def _grid_pos_embed(rows):
    quarter = D_MODEL // 4
    freqs = 1.0 / (POS_BASE ** (jnp.arange(quarter, dtype=F32) / quarter))
    er = jnp.arange(rows, dtype=F32)[:, None] * freqs
    ec = jnp.arange(GRID_W, dtype=F32)[:, None] * freqs
    row_emb = jnp.concatenate([jnp.sin(er), jnp.cos(er)], axis=-1)
    col_emb = jnp.concatenate([jnp.sin(ec), jnp.cos(ec)], axis=-1)
    return jnp.stack([row_emb, col_emb])


def kernel(x_prompt, x_sample, state_ssm_re, state_ssm_im, c, c_ctx, norm1_g, norm2_g, w_mod, b_mod,
           w_in, ssm_lam_re, ssm_lam_im, ssm_log_step, ssm_b_re, ssm_b_im, ssm_c_re, ssm_c_im, ssm_d,
           w_glu, b_glu, gm_ln_g, gm_w_s, gm_b_s, w_pa, w_pb, w_o, w_router, b_router,
           e_gate, e_up, e_down, final_g):
    cvec = jnp.concatenate([c_ctx[None], c, jnp.zeros((MOD_ROWS - 1 - DEC_BATCH, D_MODEL), F32)], axis=0)
    mod = _modulation(cvec, w_mod, b_mod)

    perm = (jnp.arange(N_EXPERT_GROUPS)[None, :] * EXPERTS_PER_GROUP
            + jnp.arange(EXPERTS_PER_GROUP)[:, None]).reshape(N_EXPERTS)
    wr = w_router.astype(F32).T[perm]
    wr_hi = wr.astype(BF16)
    wr_lo = (wr - wr_hi.astype(F32)).astype(BF16)
    wr_t = jnp.concatenate([wr_hi, wr_lo], axis=0)
    br_col = b_router.astype(F32)[perm][:, None]

    w_glu_b, w_s_b = w_glu.astype(BF16), gm_w_s.astype(BF16)
    w_pa_b, w_pb_b, w_o_b = w_pa.astype(BF16), w_pb.astype(BF16), w_o.astype(BF16)
    b_s_full = jnp.repeat(jnp.transpose(gm_b_s.astype(F32), (0, 2, 1)), GM_GROUP_DIM, axis=2)
    e_down_grouped = e_down.reshape(DEPTH, N_EXPERT_GROUPS, GROUP_FF, D_MODEL)
    w1, w2, a16 = _s5_prep(ssm_lam_re, ssm_lam_im, ssm_log_step, ssm_b_re, ssm_b_im, ssm_c_re, ssm_c_im)
    h0_lat = jnp.concatenate([state_ssm_re[:, :, 0], state_ssm_re[:, :, 1],
                              state_ssm_im[:, :, 0], state_ssm_im[:, :, 1]], axis=-1).astype(F32)
    h0 = jnp.concatenate([jnp.zeros((DEPTH, 1, SSM_GROUPS, STATE_W), F32),
                          jnp.transpose(h0_lat, (1, 0, 2, 3))], axis=1)

    d_lanes = jnp.tile(ssm_d.astype(F32).reshape(DEPTH, SSM_GROUPS, SSM_GROUP), (1, 1, SCAN_T))

    inproj_in = (x_prompt.reshape(N_PROMPT, D_MODEL), x_sample.reshape(N_SAMPLE, D_MODEL),
                 _grid_pos_embed(DEC_SEQ // GRID_W))

    new_re, new_im = [], []
    xmid = moe_y = None
    for l in range(DEPTH):
        xres, u, z = _inproj(l, *inproj_in, mod, norm1_g, gm_ln_g, w_in)
        ys, fs = _s5_scan(l, u, w1, w2, a16, h0, d_lanes)
        fin = fs[0]
        p = SSM_STATE
        new_re.append(jnp.stack([fin[:, :, 0:p], fin[::-1, :, p:2 * p]], axis=1))
        new_im.append(jnp.stack([fin[:, :, 2 * p:3 * p], fin[::-1, :, 3 * p:4 * p]], axis=1))

        ya, yb = _mix(l, ys, z, w_glu_b, b_glu.astype(F32), w_s_b, b_s_full)
        xmid, rec, gid, rank, cnt = _merge(l, ya, yb, z, xres, mod, norm2_g, w_pa_b, w_pb_b, w_o_b,
                                           wr_t, br_col)
        moe_y = _moe(l, rec, gid.reshape(N_TOK), rank.reshape(N_TOK), cnt[:, 0].astype(jnp.int32),
                     e_gate, e_up, e_down_grouped)
        inproj_in = (xmid, moe_y, mod)

    y_prompt, y_sample = _final_norm(xmid, moe_y, mod, final_g)
    new_state_re = jnp.stack(new_re, axis=1).astype(x_prompt.dtype)
    new_state_im = jnp.stack(new_im, axis=1).astype(x_prompt.dtype)
    return (y_prompt.reshape(BATCH, SEQ, D_MODEL), y_sample.reshape(DEC_BATCH, DEC_SEQ, D_MODEL),
            new_state_re, new_state_im)
```

```python
import functools

import jax
import jax.numpy as jnp
from jax import lax
from jax.experimental import pallas as pl
from jax.experimental.pallas import tpu as pltpu

F32 = jnp.float32
BF16 = jnp.bfloat16

D_MODEL = 2048
BATCH = 16
SEQ = 256
DEPTH = 2
DEC_BATCH = 2
DEC_SEQ = 4096
GRID_W = 64
POS_BASE = 10000.0
EPS = 1e-6
SSM_WIDTH = D_MODEL // 2
SSM_GROUP = 16
SSM_GROUPS = SSM_WIDTH // SSM_GROUP
SSM_STATE = 64
GM_WIDTH = D_MODEL // 2
GM_CHUNK = 128
GM_GROUPS = 8
GM_GROUP_DIM = GM_WIDTH // GM_GROUPS
IN_WIDTH = SSM_WIDTH + 2 * GM_WIDTH + 2 * D_MODEL
N_EXPERTS = 32
N_EXPERT_GROUPS = 8
EXPERTS_PER_GROUP = N_EXPERTS // N_EXPERT_GROUPS
EXPERT_FF = D_MODEL // 4
N_MOD = 6

N_PROMPT = BATCH * SEQ
N_SAMPLE = DEC_BATCH * DEC_SEQ
N_TOK = N_PROMPT + N_SAMPLE
SEG_TOK = 4096
MOD_ROWS = 8
LANES = 128
SUBLANES = 8

SCAN_T = 16
SCAN_W = SCAN_T * SSM_GROUP
SCAN_ROWS = N_TOK // SCAN_T
SCAN_BLK = 256
SCAN_TOK = SCAN_BLK * SCAN_T
N_SCAN_BLK = SCAN_ROWS // SCAN_BLK
G_OCT = SUBLANES
SCAN_G = 2 * SUBLANES
SCAN_GW = SCAN_G * SSM_GROUP
STATE_W = 4 * SSM_STATE
HALF_W = 2 * SSM_STATE

MOE_BLK = 512
MOE_SLOTS = N_TOK + N_EXPERT_GROUPS * MOE_BLK
MOE_NBLK = MOE_SLOTS // MOE_BLK
GROUP_FF = EXPERTS_PER_GROUP * EXPERT_FF

VMEM_LIMIT = 56 * 1024 * 1024


def _cparams(sem):
    return pltpu.CompilerParams(dimension_semantics=sem, vmem_limit_bytes=VMEM_LIMIT)


MOD_TN = 2048


def _mod_kernel(c_ref, w_ref, b_ref, o_ref):
    c = c_ref[...]
    s = c * jax.nn.sigmoid(c)
    s_hi = s.astype(BF16)
    s_lo = (s - s_hi.astype(F32)).astype(BF16)
    w = w_ref[...]
    w_hi = w.astype(BF16)
    w_lo = (w - w_hi.astype(F32)).astype(BF16)
    both = jnp.dot(jnp.concatenate([s_hi, s_lo], axis=0), w_hi, preferred_element_type=F32)
    cross = jnp.dot(s_hi, w_lo, preferred_element_type=F32)
    o_ref[...] = both[0:MOD_ROWS] + both[MOD_ROWS:2 * MOD_ROWS] + cross + b_ref[...]


def _modulation(cvec, w_mod, b_mod):
    width = N_MOD * D_MODEL
    return pl.pallas_call(
        _mod_kernel,
        grid=(DEPTH, width // MOD_TN),
        in_specs=[
            pl.BlockSpec((MOD_ROWS, D_MODEL), lambda l, n: (0, 0)),
            pl.BlockSpec((None, D_MODEL, MOD_TN), lambda l, n: (l, 0, n)),
            pl.BlockSpec((None, 1, MOD_TN), lambda l, n: (l, 0, n)),
        ],
        out_specs=pl.BlockSpec((None, MOD_ROWS, MOD_TN), lambda l, n: (l, 0, n)),
        out_shape=jax.ShapeDtypeStruct((DEPTH, MOD_ROWS, width), F32),
        compiler_params=_cparams(("arbitrary", "arbitrary")),
        name="adaln_mod",
    )(cvec, w_mod, b_mod.reshape(DEPTH, 1, width))


def _pack_pairs(a, b):
    hi = lax.bitcast_convert_type(a.astype(BF16).astype(F32), jnp.uint32)
    lo = lax.bitcast_convert_type(b.astype(BF16).astype(F32), jnp.uint32)
    return hi | (lo >> 16)


def _unpack_pairs(packed):
    hi = lax.bitcast_convert_type(packed & jnp.uint32(0xFFFF0000), F32)
    lo = lax.bitcast_convert_type(packed << 16, F32)
    return jnp.concatenate([hi, lo], axis=-1)


def _mod_spec(l, k):
    return pl.BlockSpec((None, MOD_ROWS, D_MODEL), lambda i: (l, 0, k))


INP_TM = 256
INP_CH = 256
INP_PROMPT_TILES = N_PROMPT // INP_TM
Z_WIDTH = IN_WIDTH - SSM_WIDTH
INP_VMEM_LIMIT = 60 * 1024 * 1024


def _inproj_kernel(*refs, first, layer):
    if first:
        xa_ref, xb_ref, add_ref = refs[:3]
    else:
        xa_ref, add_ref, gain_ref = refs[:3]
    sc_ref, sh_ref, g_ref, ln_ref, w_hbm, xres_ref, u_ref, z_ref, h_scr, v_scr, w_ref, stage, sem = refs[3:]
    i = pl.program_id(0)
    seg = i // (SEG_TOK // INP_TM)

    @pl.when(i == 0)
    def _():
        n_chunks = IN_WIDTH // INP_CH

        def chunk(c):
            return pltpu.make_async_copy(w_hbm.at[layer, :, pl.ds(c * INP_CH, INP_CH)],
                                         stage.at[c % 2], sem.at[c % 2])

        chunk(0).start()
        for c in range(n_chunks):
            if c + 1 < n_chunks:
                chunk(c + 1).start()
            chunk(c).wait()
            w_ref[:, pl.ds(c * INP_CH, INP_CH)] = stage[c % 2].astype(BF16)

    if first:
        latent = i >= INP_PROMPT_TILES
        rows_per_tile = INP_TM // GRID_W
        row0 = (i % (DEC_SEQ // INP_TM)) * rows_per_tile
        row_part = jnp.concatenate(
            [jnp.broadcast_to(add_ref[0, pl.ds(row0 + q, 1), :], (GRID_W, D_MODEL // 2))
             for q in range(rows_per_tile)], axis=0)
        col_part = jnp.concatenate([add_ref[1]] * rows_per_tile, axis=0)
        x = jnp.where(latent, xb_ref[...] + jnp.concatenate([row_part, col_part], axis=1), xa_ref[...])
    else:
        x = xa_ref[...] + gain_ref[pl.ds(seg, 1), :] * _unpack_pairs(add_ref[...])
    xres_ref[...] = x
    ms = jnp.mean(x * x, axis=-1, keepdims=True)
    y = x * lax.rsqrt(ms + EPS) * g_ref[...]
    h = y * (1.0 + sc_ref[pl.ds(seg, 1), :]) + sh_ref[pl.ds(seg, 1), :]
    h_scr[...] = h.astype(BF16)

    def proj(col):
        return jnp.dot(h_scr[...], w_ref[:, pl.ds(col, INP_CH)], preferred_element_type=F32)

    n_ch = SSM_WIDTH // INP_CH
    for c in range(n_ch):
        acc = proj(c * INP_CH)
        for k in range(INP_CH // LANES):
            u_ref[c * (INP_CH // LANES) + k] = acc[:, k * LANES:(k + 1) * LANES]
    for c in range(n_ch):
        z_ref[:, pl.ds(c * INP_CH, INP_CH)] = jax.nn.gelu(proj(SSM_WIDTH + c * INP_CH)).astype(BF16)
    row_sum = jnp.zeros((INP_TM, 1), F32)
    for c in range(n_ch):
        v = jax.nn.gelu(proj(SSM_WIDTH + GM_WIDTH + c * INP_CH))
        v_scr[:, pl.ds(c * INP_CH, INP_CH)] = v
        row_sum = row_sum + jnp.sum(v, axis=-1, keepdims=True)
    mu = row_sum * (1.0 / GM_WIDTH)
    dev = v_scr[...] - mu
    var = jnp.mean(jnp.square(dev), axis=-1, keepdims=True)
    z_ref[:, pl.ds(GM_WIDTH, GM_WIDTH)] = (dev * lax.rsqrt(var + EPS) * ln_ref[...]).astype(BF16)
    gates = SSM_WIDTH + 2 * GM_WIDTH
    for c in range(2 * D_MODEL // INP_CH):
        z_ref[:, pl.ds(2 * GM_WIDTH + c * INP_CH, INP_CH)] = jax.nn.sigmoid(
            proj(gates + c * INP_CH)).astype(BF16)


def _inproj(l, xa, xb_or_add, add_or_gain, mod, norm1_g, gm_ln_g, w_in):
    first = l == 0
    row_tile = lambda m: pl.BlockSpec((INP_TM, D_MODEL), m)
    if first:
        lead = [row_tile(lambda i: (jnp.minimum(i, INP_PROMPT_TILES - 1), 0)),
                row_tile(lambda i: (jnp.maximum(i - INP_PROMPT_TILES, 0), 0)),
                pl.BlockSpec((2, DEC_SEQ // GRID_W, D_MODEL // 2), lambda i: (0, 0, 0))]
    else:
        lead = [row_tile(lambda i: (i, 0)), pl.BlockSpec((INP_TM, HALF_D), lambda i: (i, 0)),
                _mod_spec(l - 1, 5)]
    vec = lambda w: pl.BlockSpec((None, 1, w), lambda i: (l, 0, 0))
    return pl.pallas_call(
        functools.partial(_inproj_kernel, first=first, layer=l),
        grid=(N_TOK // INP_TM,),
        in_specs=lead + [
            _mod_spec(l, 1), _mod_spec(l, 0),
            vec(D_MODEL), vec(GM_WIDTH),
            pl.BlockSpec(memory_space=pl.ANY),
        ],
        out_specs=[
            pl.BlockSpec((INP_TM, D_MODEL), lambda i: (i, 0)),
            pl.BlockSpec((SSM_WIDTH // LANES, INP_TM, LANES), lambda i: (0, i, 0)),
            pl.BlockSpec((INP_TM, Z_WIDTH), lambda i: (i, 0)),
        ],
        out_shape=[
            jax.ShapeDtypeStruct((N_TOK, D_MODEL), F32),
            jax.ShapeDtypeStruct((SSM_WIDTH // LANES, N_TOK, LANES), F32),
            jax.ShapeDtypeStruct((N_TOK, Z_WIDTH), BF16),
        ],
        scratch_shapes=[pltpu.VMEM((INP_TM, D_MODEL), BF16), pltpu.VMEM((INP_TM, GM_WIDTH), F32),
                        pltpu.VMEM((D_MODEL, IN_WIDTH), BF16), pltpu.VMEM((2, D_MODEL, INP_CH), F32),
                        pltpu.SemaphoreType.DMA((2,))],
        compiler_params=pltpu.CompilerParams(dimension_semantics=("arbitrary",),
                                             vmem_limit_bytes=INP_VMEM_LIMIT),
        name="in_proj",
    )(xa, xb_or_add, add_or_gain, mod, mod, norm1_g.reshape(DEPTH, 1, D_MODEL),
      gm_ln_g.reshape(DEPTH, 1, GM_WIDTH), w_in)


PK_BRE, PK_BIM, PK_CRE, PK_CIM = 0, 16, 32, 48
PK_LR, PK_LI = 64, 65


def _split_bf16(x):
    hi = x.astype(BF16)
    return hi, (x - hi.astype(F32)).astype(BF16)


def _dot_split(a, b):
    a_hi, a_lo = _split_bf16(a)
    b_hi, b_lo = _split_bf16(b)
    m = a.shape[0]
    both = jnp.dot(jnp.concatenate([a_hi, a_lo], axis=0), b_hi, preferred_element_type=F32)
    return both[0:m] + both[m:2 * m] + jnp.dot(a_hi, b_lo, preferred_element_type=F32)


POW_ROWS = 24


def _prep_kernel(pk_ref, row_ref, pow_ref, tile_ref, w1_ref, w2_ref, a_ref):
    p = SSM_STATE
    k_sub = jnp.minimum(lax.broadcasted_iota(jnp.int32, (POW_ROWS, LANES), 0), SCAN_T).astype(F32)
    lane = lax.broadcasted_iota(jnp.int32, (p, LANES), 1)
    col = lax.broadcasted_iota(jnp.int32, (SSM_GROUP, SCAN_W), 1)

    def spread_pow(x_r, x_i, which):
        parts = jnp.concatenate(_split_bf16(x_r) + _split_bf16(x_i), axis=0)
        out = jnp.dot(parts, pow_ref[which], preferred_element_type=F32)
        return out[0:p] + out[p:2 * p], out[2 * p:3 * p] + out[3 * p:4 * p]

    def spread_tiles(x):
        out = jnp.dot(jnp.concatenate(_split_bf16(x), axis=0), tile_ref[...], preferred_element_type=F32)
        out = out[0:p] + out[p:2 * p]
        return [out[:, n * SCAN_W:(n + 1) * SCAN_W] for n in range(4)]

    def group(g, _):
        rows = row_ref[g]
        grow_r = rows[0:1] * rows[2:3]
        grow_i = rows[1:2] * rows[2:3]
        mag = jnp.exp(grow_r * k_sub)
        ang = grow_i * k_sub
        unused = jnp.zeros((LANES - POW_ROWS, LANES), F32)
        pw_t_r = jnp.concatenate([mag * jnp.cos(ang), unused], axis=0).T
        pw_t_i = jnp.concatenate([mag * jnp.sin(ang), unused], axis=0).T
        per_dir = []
        for d in range(2):
            pk = pk_ref[d, g]
            lr = pk[:, PK_LR:PK_LR + 1]
            li = pk[:, PK_LI:PK_LI + 1]
            p_r = pw_t_r[d * p:(d + 1) * p]
            p_i = pw_t_i[d * p:(d + 1) * p]
            a_r = p_r[:, 1:2]
            a_i = p_i[:, 1:2]
            den = lr * lr + li * li
            q_r = ((a_r - 1.0) * lr + a_i * li) / den
            q_i = (a_i * lr - (a_r - 1.0) * li) / den
            per_dir.append((pk, p_r, p_i, q_r, q_i))

        w1_rows, lag, carry = [], [], []
        for d in range(2):
            pk, p_r, p_i, q_r, q_i = per_dir[d]
            b_r, b_i, c_r, c_i = spread_tiles(pk)
            bb_r = q_r * b_r - q_i * b_i
            bb_i = q_r * b_i + q_i * b_r
            pw_r, pw_i = spread_pow(p_r, p_i, 1 if d == 0 else 0)
            w1_rows.append((pw_r * bb_r - pw_i * bb_i, pw_r * bb_i + pw_i * bb_r))
            pl_r, pl_i = spread_pow(p_r, p_i, 0 if d == 0 else 1)
            cl_r = c_r * pl_r - c_i * pl_i
            cl_i = c_r * pl_i + c_i * pl_r
            pk_im = pltpu.roll(pk, LANES - (PK_BIM - PK_BRE), 1)
            bt_r = (q_r * pk - q_i * pk_im).T[0:SSM_GROUP, :]
            bt_i = (q_r * pk_im + q_i * pk).T[0:SSM_GROUP, :]
            lag.append(_dot_split(bt_r, cl_r) - _dot_split(bt_i, cl_i))
            pc_r, pc_i = spread_pow(p_r, p_i, 2 if d == 0 else 3)
            carry.append((c_r * pc_r - c_i * pc_i, -(c_r * pc_i + c_i * pc_r)))

        (f_re, f_im), (b_re, b_im) = w1_rows
        w1_ref[g] = jnp.concatenate([f_re, b_re, f_im, b_im], axis=0).T.astype(BF16)

        for s in range(SCAN_T):
            fwd = lag[0] if s == 0 else pltpu.roll(lag[0], SSM_GROUP * s, 1)
            fwd = jnp.where(col >= SSM_GROUP * s, fwd, 0.0)
            shift_b = SSM_GROUP * (SCAN_T - 1 - s)
            bwd = lag[1] if shift_b == 0 else pltpu.roll(lag[1], SCAN_W - shift_b, 1)
            bwd = jnp.where(col < SSM_GROUP * (s + 1), bwd, 0.0)
            w2_ref[g, pl.ds(SSM_GROUP * s, SSM_GROUP), :] = (fwd + bwd).astype(BF16)
        (x_re, x_im), (y_re, y_im) = carry
        for n, rows in enumerate((x_re, y_re, x_im, y_im)):
            w2_ref[g, pl.ds(SCAN_W + SSM_STATE * n, SSM_STATE), :] = rows.astype(BF16)

        cols = [per_dir[0][1], per_dir[1][1], per_dir[0][2], per_dir[1][2]]
        a_cols = jnp.zeros((SSM_STATE, LANES), F32)
        for n, c in enumerate(cols):
            a_cols = jnp.where(lane == n, c[:, SCAN_T:SCAN_T + 1], a_cols)
        a_ref[g] = a_cols
        return 0

    lax.fori_loop(0, G_OCT, group, 0, unroll=4)


def _s5_prep(lam_re, lam_im, log_step, b_re, b_im, c_re, c_im):
    shape = (DEPTH, 2, SSM_GROUPS, SSM_STATE)
    lr = lam_re.astype(F32)
    li = lam_im.astype(F32)
    dt = jnp.broadcast_to(jnp.exp(log_step.astype(F32))[..., None], shape)
    pk = jnp.concatenate([
        b_re.astype(F32), b_im.astype(F32),
        jnp.swapaxes(c_re.astype(F32), -1, -2), jnp.swapaxes(c_im.astype(F32), -1, -2),
        lr[..., None], li[..., None],
        jnp.zeros(shape + (LANES - PK_LI - 1,), F32)], axis=-1)
    both_dirs = lambda a: jnp.concatenate([a[:, 0], a[:, 1]], axis=-1)
    rows = jnp.stack([both_dirs(lr), both_dirs(li), both_dirs(dt)], axis=2)
    rows = jnp.concatenate([rows, jnp.zeros((DEPTH, SSM_GROUPS, SUBLANES - 3, LANES), F32)], axis=2)

    blk = jnp.arange(SCAN_W) // SSM_GROUP
    k = jnp.arange(LANES)[:, None]
    pows = [k == blk[None, :], k == (SCAN_T - 1 - blk)[None, :], k == (blk + 1)[None, :],
            k == (SCAN_T - blk)[None, :]]
    h = (jnp.arange(SCAN_W) % SSM_GROUP)[None, :]
    sel_pow = jnp.stack(pows).astype(BF16)
    sel_tile = jnp.concatenate([k == h + off for off in (PK_BRE, PK_BIM, PK_CRE, PK_CIM)],
                               axis=1).astype(BF16)

    n_oct = SSM_GROUPS // G_OCT
    w1, w2, a_cols = pl.pallas_call(
        _prep_kernel,
        grid=(DEPTH, n_oct),
        in_specs=[
            pl.BlockSpec((None, 2, G_OCT, SSM_STATE, LANES), lambda l, o: (l, 0, o, 0, 0)),
            pl.BlockSpec((None, G_OCT, SUBLANES, LANES), lambda l, o: (l, o, 0, 0)),
            pl.BlockSpec((4, LANES, SCAN_W), lambda l, o: (0, 0, 0)),
            pl.BlockSpec((LANES, 4 * SCAN_W), lambda l, o: (0, 0)),
        ],
        out_specs=[
            pl.BlockSpec((None, G_OCT, SCAN_W, STATE_W), lambda l, o: (l, o, 0, 0)),
            pl.BlockSpec((None, G_OCT, SCAN_W + STATE_W, SCAN_W), lambda l, o: (l, o, 0, 0)),
            pl.BlockSpec((None, G_OCT, SSM_STATE, LANES), lambda l, o: (l, o, 0, 0)),
        ],
        out_shape=[
            jax.ShapeDtypeStruct((DEPTH, SSM_GROUPS, SCAN_W, STATE_W), BF16),
            jax.ShapeDtypeStruct((DEPTH, SSM_GROUPS, SCAN_W + STATE_W, SCAN_W), BF16),
            jax.ShapeDtypeStruct((DEPTH, SSM_GROUPS, SSM_STATE, LANES), F32),
        ],
        compiler_params=_cparams(("arbitrary", "arbitrary")),
        name="s5_prep",
    )(pk, rows, sel_pow, sel_tile)
    a16 = jnp.swapaxes(a_cols[..., 0:4], -1, -2).reshape(DEPTH, SSM_GROUPS, STATE_W)
    return w1, w2, a16


def _s5_kernel(u_ref, w1_ref, w2_ref, a_ref, h0_ref, d_ref, y_ref, fs_ref,
               t_scr, ug_scr, vr_scr, vi_scr, cr_scr, ci_scr, fr_scr, fi_scr):
    blk = pl.program_id(1)
    seq_rows = jnp.where(blk == 0, SEQ // SCAN_T, DEC_SEQ // SCAN_T)

    for s in range(SCAN_T):
        for j in range(SCAN_GW // LANES):
            t_scr[s, pl.ds(j * LANES, LANES), :] = (
                u_ref[j, pl.ds(s, SCAN_BLK, stride=SCAN_T), :].astype(BF16).T)
    for g in range(SCAN_G):
        stacked = t_scr[:, pl.ds(g * SSM_GROUP, SSM_GROUP), :].reshape(SCAN_W, SCAN_BLK)
        ug_scr[g] = stacked.T

    for g in range(SCAN_G):
        v = jnp.dot(ug_scr[g], w1_ref[g], preferred_element_type=F32)
        vr_scr[pl.ds(g, SCAN_BLK, stride=SCAN_G), :] = v[:, 0:HALF_W]
        vi_scr[pl.ds(g, SCAN_BLK, stride=SCAN_G), :] = v[:, HALF_W:STATE_W]

    a_r = a_ref[:, 0:HALF_W]
    a_i = a_ref[:, HALF_W:STATE_W]
    h0_r = h0_ref[:, 0:HALF_W]
    h0_i = h0_ref[:, HALF_W:STATE_W]
    fwd_lanes = lax.broadcasted_iota(jnp.int32, (SCAN_G, HALF_W), 1) < SSM_STATE
    bwd_lanes = jnp.logical_not(fwd_lanes)

    def step(k, carry):
        s_r, s_i = carry
        rf = pl.ds(pl.multiple_of(k * SCAN_G, SCAN_G), SCAN_G)
        rb = pl.ds(pl.multiple_of((SCAN_BLK - 1 - k) * SCAN_G, SCAN_G), SCAN_G)
        restart = (k & (seq_rows - 1)) == 0
        s_r = jnp.where(restart, h0_r, s_r)
        s_i = jnp.where(restart, h0_i, s_i)
        pltpu.store(cr_scr.at[rf, :], s_r, mask=fwd_lanes)
        pltpu.store(cr_scr.at[rb, :], s_r, mask=bwd_lanes)
        pltpu.store(ci_scr.at[rf, :], s_i, mask=fwd_lanes)
        pltpu.store(ci_scr.at[rb, :], s_i, mask=bwd_lanes)
        v_r = jnp.where(fwd_lanes, vr_scr[rf, :], vr_scr[rb, :])
        v_i = jnp.where(fwd_lanes, vi_scr[rf, :], vi_scr[rb, :])
        n_r = a_r * s_r - a_i * s_i + v_r
        n_i = a_r * s_i + a_i * s_r + v_i
        fr_scr[rf, :] = n_r
        fi_scr[rf, :] = n_i
        return n_r, n_i

    zero = jnp.zeros((SCAN_G, HALF_W), F32)
    lax.fori_loop(0, SCAN_BLK, step, (zero, zero), unroll=4)

    for g in range(SCAN_G):
        c_r = cr_scr[pl.ds(g, SCAN_BLK, stride=SCAN_G), :].astype(BF16)
        c_i = ci_scr[pl.ds(g, SCAN_BLK, stride=SCAN_G), :].astype(BF16)
        y = jnp.dot(ug_scr[g], w2_ref[g, 0:SCAN_W, :], preferred_element_type=F32)
        y = y + jnp.dot(c_r, w2_ref[g, SCAN_W:SCAN_W + HALF_W, :], preferred_element_type=F32)
        y = y + jnp.dot(c_i, w2_ref[g, SCAN_W + HALF_W:SCAN_W + STATE_W, :], preferred_element_type=F32)
        y = y + d_ref[pl.ds(g, 1), :] * ug_scr[g].astype(F32)
        t_scr[:, pl.ds(g * SSM_GROUP, SSM_GROUP), :] = y.astype(BF16).T.reshape(SCAN_T, SSM_GROUP, SCAN_BLK)
    for s in range(SCAN_T):
        for j in range(SCAN_GW // LANES):
            y_ref[j, pl.ds(s, SCAN_BLK, stride=SCAN_T), :] = t_scr[s, pl.ds(j * LANES, LANES), :].T.astype(F32)

    rows_per_seq = SEQ // SCAN_T
    for q in range(SCAN_BLK // rows_per_seq):
        last = pl.ds((q * rows_per_seq + rows_per_seq - 1) * SCAN_G, SCAN_G)
        fs_ref[q, :, 0:HALF_W] = fr_scr[last, :]
        fs_ref[q, :, HALF_W:STATE_W] = fi_scr[last, :]


def _s5_scan(l, u, w1, w2, a16, h0, d_lanes):
    n_oct = SSM_GROUPS // SCAN_G
    n_fin = SCAN_BLK // (SEQ // SCAN_T)
    return pl.pallas_call(
        _s5_kernel,
        grid=(n_oct, N_SCAN_BLK),
        in_specs=[
            pl.BlockSpec((SCAN_GW // LANES, SCAN_TOK, LANES), lambda o, b: (o, b, 0)),
            pl.BlockSpec((None, SCAN_G, SCAN_W, STATE_W), lambda o, b: (l, o, 0, 0)),
            pl.BlockSpec((None, SCAN_G, SCAN_W + STATE_W, SCAN_W), lambda o, b: (l, o, 0, 0)),
            pl.BlockSpec((None, SCAN_G, STATE_W), lambda o, b: (l, o, 0)),
            pl.BlockSpec((None, None, SCAN_G, STATE_W), lambda o, b: (l, b, o, 0)),
            pl.BlockSpec((None, SCAN_G, SCAN_W), lambda o, b: (l, o, 0)),
        ],
        out_specs=[
            pl.BlockSpec((SCAN_GW // LANES, SCAN_TOK, LANES), lambda o, b: (o, b, 0)),
            pl.BlockSpec((None, n_fin, SCAN_G, STATE_W), lambda o, b: (b, 0, o, 0)),
        ],
        out_shape=[
            jax.ShapeDtypeStruct((SSM_WIDTH // LANES, N_TOK, LANES), F32),
            jax.ShapeDtypeStruct((N_SCAN_BLK, n_fin, SSM_GROUPS, STATE_W), F32),
        ],
        scratch_shapes=[
            pltpu.VMEM((SCAN_T, SCAN_GW, SCAN_BLK), BF16),
            pltpu.VMEM((SCAN_G, SCAN_BLK, SCAN_W), BF16),
        ] + [pltpu.VMEM((SCAN_BLK * SCAN_G, HALF_W), F32) for _ in range(6)],
        compiler_params=_cparams(("arbitrary", "arbitrary")),
        name="s5_scan",
    )(u, w1, w2, a16, h0, d_lanes)


MIX_TM = 512


def _mix_kernel(ys_ref, gu_ref, vn_ref, wglu_ref, bglu_ref, ws_ref, bs_ref, ya_ref, yb_ref):
    y = jnp.concatenate([ys_ref[k] for k in range(SSM_WIDTH // LANES)], axis=1)
    y = jax.nn.gelu(y)
    gate = jnp.dot(y.astype(BF16), wglu_ref[...], preferred_element_type=F32) + bglu_ref[...]
    ya_ref[...] = (y * jax.nn.sigmoid(gate)).astype(BF16)
    for c in range(MIX_TM // GM_CHUNK):
        rows = pl.ds(c * GM_CHUNK, GM_CHUNK)
        for g in range(GM_GROUPS):
            cols = pl.ds(g * GM_GROUP_DIM, GM_GROUP_DIM)
            mixed = jnp.dot(ws_ref[g], vn_ref[rows, cols], preferred_element_type=F32) + bs_ref[:, cols]
            yb_ref[rows, cols] = (gu_ref[rows, cols].astype(F32) * mixed).astype(BF16)


def _mix(l, ys, z, w_glu_bf16, b_glu, w_s_bf16, b_s_full):
    tile = lambda k: pl.BlockSpec((MIX_TM, SSM_WIDTH), lambda i: (i, k))
    slab = pl.BlockSpec((SSM_WIDTH // LANES, MIX_TM, LANES), lambda i: (0, i, 0))
    lay = lambda *shape: pl.BlockSpec((None,) + shape, lambda i: (l,) + tuple(0 for _ in shape))
    return pl.pallas_call(
        _mix_kernel,
        grid=(N_TOK // MIX_TM,),
        in_specs=[
            slab, tile(0), tile(1),
            lay(SSM_WIDTH, SSM_WIDTH),
            lay(1, SSM_WIDTH),
            lay(GM_GROUPS, GM_CHUNK, GM_CHUNK),
            lay(GM_CHUNK, GM_WIDTH),
        ],
        out_specs=[tile(0), tile(0)],
        out_shape=[
            jax.ShapeDtypeStruct((N_TOK, SSM_WIDTH), BF16),
            jax.ShapeDtypeStruct((N_TOK, GM_WIDTH), BF16),
        ],
        compiler_params=_cparams(("arbitrary",)),
        name="mixers",
    )(ys, z, z, w_glu_bf16, b_glu.reshape(DEPTH, 1, SSM_WIDTH), w_s_bf16, b_s_full)


MRG_TM = 512
MRG_SUB = 256
HALF_D = D_MODEL // 2
REC_W = HALF_D + LANES


def _merge_kernel(ya_ref, yb_ref, ga0_ref, ga1_ref, gb0_ref, gb1_ref, x_ref, g1_ref, sc_ref, sh_ref,
                  n2_ref, wpa_ref, wpb_ref, wo_ref, wr_ref, br_ref, tri_ref,
                  xmid_ref, rec_ref, gid_ref, rank_ref, cnt_ref, cnt_scr):
    i = pl.program_id(0)
    seg = i // (SEG_TOK // MRG_TM)

    @pl.when(i == 0)
    def _():
        cnt_scr[...] = jnp.zeros(cnt_scr.shape, F32)

    passes = [pl.ds(sub * MRG_SUB, MRG_SUB) for sub in range(MRG_TM // MRG_SUB)]
    for rows in passes:
        _merge_mix(rows, seg, ya_ref, yb_ref, ga0_ref, ga1_ref, gb0_ref, gb1_ref, x_ref, g1_ref,
                   wpa_ref, wpb_ref, wo_ref, xmid_ref)
    for rows in passes:
        _merge_route(rows, seg, sc_ref, sh_ref, n2_ref, wr_ref, br_ref, tri_ref,
                     xmid_ref, rec_ref, gid_ref, rank_ref, cnt_scr)
    cnt_ref[...] = jnp.broadcast_to(cnt_scr[...], cnt_ref.shape)


def _merge_mix(rows, seg, ya_ref, yb_ref, ga0_ref, ga1_ref, gb0_ref, gb1_ref, x_ref, g1_ref,
               wpa_ref, wpb_ref, wo_ref, xmid_ref):
    pa = jnp.dot(ya_ref[rows, :], wpa_ref[...], preferred_element_type=F32)
    pb = jnp.dot(yb_ref[rows, :], wpb_ref[...], preferred_element_type=F32)
    m_lo = ga0_ref[rows, :].astype(F32) * pa[:, :HALF_D] + gb0_ref[rows, :].astype(F32) * pb[:, :HALF_D]
    m_hi = ga1_ref[rows, :].astype(F32) * pa[:, HALF_D:] + gb1_ref[rows, :].astype(F32) * pb[:, HALF_D:]
    mix = jnp.dot(m_lo.astype(BF16), wo_ref[0:HALF_D, :], preferred_element_type=F32)
    mix = mix + jnp.dot(m_hi.astype(BF16), wo_ref[HALF_D:D_MODEL, :], preferred_element_type=F32)
    xmid_ref[rows, :] = x_ref[rows, :] + g1_ref[pl.ds(seg, 1), :] * mix


def _merge_route(rows, seg, sc_ref, sh_ref, n2_ref, wr_ref, br_ref, tri_ref,
                 xmid_ref, rec_ref, gid_ref, rank_ref, cnt_scr):
    x = xmid_ref[rows, :]
    ms = jnp.mean(x * x, axis=-1, keepdims=True)
    y = x * lax.rsqrt(ms + EPS) * n2_ref[...]
    h2 = y * (1.0 + sc_ref[pl.ds(seg, 1), :]) + sh_ref[pl.ds(seg, 1), :]
    hi = h2.astype(BF16)
    hi_f = hi.astype(F32)
    lo = (h2 - hi_f).astype(BF16)
    bits = lax.bitcast_convert_type(hi_f, jnp.uint32)
    rec_ref[rows, 0:HALF_D] = bits[:, :HALF_D] | (bits[:, HALF_D:] >> 16)

    nt = (((1,), (1,)), ((), ()))
    lt = (lax.dot_general(wr_ref[...], hi, nt, preferred_element_type=F32)
          + lax.dot_general(wr_ref[...], lo, nt, preferred_element_type=F32))
    logits = lt[0:N_EXPERTS] + lt[N_EXPERTS:2 * N_EXPERTS]
    scores = jax.nn.sigmoid(logits)
    sel = scores + br_ref[...]
    ng = N_EXPERT_GROUPS
    s = [sel[j * ng:(j + 1) * ng] for j in range(EXPERTS_PER_GROUP)]
    p = [scores[j * ng:(j + 1) * ng] for j in range(EXPERTS_PER_GROUP)]
    a, b = jnp.maximum(s[0], s[1]), jnp.minimum(s[0], s[1])
    c, d = jnp.maximum(s[2], s[3]), jnp.minimum(s[2], s[3])
    grp_score = jnp.maximum(a, c) + jnp.maximum(jnp.minimum(a, c), jnp.maximum(b, d))
    best = jnp.max(grp_score, axis=0, keepdims=True)
    g_iota = lax.broadcasted_iota(jnp.int32, grp_score.shape, 0)
    g_idx = jnp.min(jnp.where(grp_score == best, g_iota, ng), axis=0, keepdims=True)
    onehot = g_iota == g_idx
    v = [jnp.sum(jnp.where(onehot, sj, 0.0), axis=0, keepdims=True) for sj in s]
    q = [jnp.sum(jnp.where(onehot, pj, 0.0), axis=0, keepdims=True) for pj in p]
    picked = []
    for j in range(EXPERTS_PER_GROUP):
        rank = jnp.zeros(v[j].shape, jnp.int32)
        for o in range(EXPERTS_PER_GROUP):
            if o == j:
                continue
            ahead = (v[o] > v[j]) | ((v[o] == v[j]) & (o < j))
            rank = rank + ahead.astype(jnp.int32)
        picked.append(jnp.where(rank < 2, q[j], 0.0))
    total = picked[0] + picked[1] + picked[2] + picked[3]
    gid_ref[:, rows] = g_idx
    cw_rows = jnp.concatenate([pj / total for pj in picked]
                              + [jnp.zeros((LANES - EXPERTS_PER_GROUP, MRG_SUB), F32)], axis=0)
    rec_ref[rows, HALF_D:REC_W] = lax.bitcast_convert_type(cw_rows.T, jnp.uint32)

    hot = onehot.astype(BF16)
    within = jnp.dot(hot, tri_ref[...], preferred_element_type=F32)
    before = jnp.sum(jnp.where(onehot, within + cnt_scr[...], 0.0), axis=0, keepdims=True) - 1.0
    rank_ref[:, rows] = before.astype(jnp.int32)
    cnt_scr[...] = cnt_scr[...] + within[:, MRG_SUB - 1:MRG_SUB]


def _merge(l, ya, yb, z, xres, mod, norm2_g, w_pa, w_pb, w_o, wr_t, br_col):
    n_t = N_TOK // MRG_TM
    tri = (jnp.arange(MRG_SUB)[:, None] <= jnp.arange(MRG_SUB)[None, :]).astype(BF16)
    zspec = lambda k: pl.BlockSpec((MRG_TM, HALF_D), lambda i: (i, k))
    once = pl.Buffered(1)
    lay = lambda *shape: pl.BlockSpec((None,) + shape, lambda i: (l,) + tuple(0 for _ in shape),
                                      pipeline_mode=once)
    const = lambda *shape: pl.BlockSpec(shape, lambda i: tuple(0 for _ in shape), pipeline_mode=once)
    return pl.pallas_call(
        _merge_kernel,
        grid=(n_t,),
        in_specs=[
            pl.BlockSpec((MRG_TM, SSM_WIDTH), lambda i: (i, 0)),
            pl.BlockSpec((MRG_TM, GM_WIDTH), lambda i: (i, 0)),
            zspec(2), zspec(3), zspec(4), zspec(5),
            pl.BlockSpec((MRG_TM, D_MODEL), lambda i: (i, 0)),
            _mod_spec(l, 2), _mod_spec(l, 4), _mod_spec(l, 3),
            lay(1, D_MODEL),
            lay(SSM_WIDTH, D_MODEL),
            lay(GM_WIDTH, D_MODEL),
            lay(D_MODEL, D_MODEL),
            const(2 * N_EXPERTS, D_MODEL),
            const(N_EXPERTS, 1),
            const(MRG_SUB, MRG_SUB),
        ],
        out_specs=[
            pl.BlockSpec((MRG_TM, D_MODEL), lambda i: (i, 0)),
            pl.BlockSpec((MRG_TM, REC_W), lambda i: (i, 0)),
            pl.BlockSpec((None, 1, MRG_TM), lambda i: (i, 0, 0)),
            pl.BlockSpec((None, 1, MRG_TM), lambda i: (i, 0, 0)),
            pl.BlockSpec((N_EXPERT_GROUPS, LANES), lambda i: (0, 0)),
        ],
        out_shape=[
            jax.ShapeDtypeStruct((N_TOK, D_MODEL), F32),
            jax.ShapeDtypeStruct((N_TOK, REC_W), jnp.uint32),
            jax.ShapeDtypeStruct((n_t, 1, MRG_TM), jnp.int32),
            jax.ShapeDtypeStruct((n_t, 1, MRG_TM), jnp.int32),
            jax.ShapeDtypeStruct((N_EXPERT_GROUPS, LANES), F32),
        ],
        scratch_shapes=[pltpu.VMEM((N_EXPERT_GROUPS, 1), F32)],
        compiler_params=_cparams(("arbitrary",)),
        name="merge_router",
    )(ya, yb, z, z, z, z, xres, mod, mod, mod, norm2_g.reshape(DEPTH, 1, D_MODEL),
      w_pa, w_pb, w_o, wr_t, br_col, tri)


DSP_TM = 1024


def _dispatch_kernel(pos_ref, pend_ref, rec_ref, out_ref, zero_scr, sem):
    step = pl.program_id(0)

    @pl.when(step == 0)
    def _():
        zero_scr[...] = jnp.zeros(zero_scr.shape, jnp.uint32)
        for g in range(N_EXPERT_GROUPS):
            start = pl.multiple_of(jnp.maximum(pend_ref[g] - MOE_BLK, 0), MOE_BLK)
            fill = pltpu.make_async_copy(zero_scr, out_ref.at[pl.ds(start, MOE_BLK)], sem)
            fill.start()
            fill.wait()
        for blk in range(N_TOK // MOE_BLK, MOE_NBLK):
            @pl.when(blk * MOE_BLK >= pend_ref[N_EXPERT_GROUPS - 1])
            def _():
                fill = pltpu.make_async_copy(zero_scr, out_ref.at[pl.ds(blk * MOE_BLK, MOE_BLK)], sem)
                fill.start()
                fill.wait()

    base = step * DSP_TM
    for r in range(DSP_TM):
        pltpu.make_async_copy(rec_ref.at[pl.ds(r, 1)], out_ref.at[pl.ds(pos_ref[base + r], 1)], sem).start()
    pltpu.make_async_copy(rec_ref, out_ref.at[pl.ds(0, DSP_TM)], sem).wait()


def _dispatch(pos, pend, rec):
    grid_spec = pltpu.PrefetchScalarGridSpec(
        num_scalar_prefetch=2,
        grid=(N_TOK // DSP_TM,),
        in_specs=[pl.BlockSpec((DSP_TM, REC_W), lambda i, pos, pend: (i, 0))],
        out_specs=pl.BlockSpec(memory_space=pl.ANY),
        scratch_shapes=[pltpu.VMEM((MOE_BLK, REC_W), jnp.uint32), pltpu.SemaphoreType.DMA(())],
    )
    return pl.pallas_call(
        _dispatch_kernel,
        grid_spec=grid_spec,
        out_shape=jax.ShapeDtypeStruct((MOE_SLOTS, REC_W), jnp.uint32),
        compiler_params=_cparams(("arbitrary",)),
        name="moe_dispatch",
    )(pos, pend, rec)


UP_EXPERTS = 2
MOE_SUB = 256
UP_CH = 256


def _up_inputs(rec_ref, rows, first_expert):
    packed = rec_ref[rows, 0:HALF_D]
    x_lo = lax.bitcast_convert_type(packed & jnp.uint32(0xFFFF0000), F32).astype(BF16)
    x_hi = lax.bitcast_convert_type(packed << 16, F32).astype(BF16)
    cw = lax.bitcast_convert_type(rec_ref[rows, HALF_D:REC_W], F32)
    lane = lax.broadcasted_iota(jnp.int32, cw.shape, 1)
    w_rows = [jnp.sum(jnp.where(lane == first_expert + e, cw, 0.0), axis=1, keepdims=True)
              for e in range(UP_EXPERTS)]
    return x_lo, x_hi, w_rows


def _up_hidden(x_lo, x_hi, w_row, wg, wu):
    gate = (jnp.dot(x_lo, wg[0:HALF_D, :], preferred_element_type=F32)
            + jnp.dot(x_hi, wg[HALF_D:D_MODEL, :], preferred_element_type=F32))
    up = (jnp.dot(x_lo, wu[0:HALF_D, :], preferred_element_type=F32)
          + jnp.dot(x_hi, wu[HALF_D:D_MODEL, :], preferred_element_type=F32))
    return (gate * jax.nn.sigmoid(gate) * up * w_row).astype(BF16)


def _expert_up_kernel(gid_ref, fill_ref, last_ref, rec_ref, wg_ref, wu_ref, h_ref, wg_scr, wu_scr):
    del last_ref
    first_expert = pl.program_id(0) * UP_EXPERTS
    b = pl.program_id(1)
    prev = gid_ref[jnp.maximum(b - 1, 0)]
    fresh = (b == 0) | (gid_ref[b] != prev)
    passes = [pl.ds(sub * MOE_SUB, MOE_SUB) for sub in range(MOE_BLK // MOE_SUB)]

    @pl.when(fresh)
    def _():
        inputs = [_up_inputs(rec_ref, rows, first_expert) for rows in passes]
        for e in range(UP_EXPERTS):
            for c in range(EXPERT_FF // UP_CH):
                cols = pl.ds(c * UP_CH, UP_CH)
                wg = wg_ref[e, :, cols].astype(BF16)
                wu = wu_ref[e, :, cols].astype(BF16)
                wg_scr[e, :, cols] = wg
                wu_scr[e, :, cols] = wu
                for rows, (x_lo, x_hi, w_rows) in zip(passes, inputs):
                    h_ref[rows, pl.ds(e * EXPERT_FF + c * UP_CH, UP_CH)] = _up_hidden(
                        x_lo, x_hi, w_rows[e], wg, wu)

    @pl.when(jnp.logical_not(fresh))
    def _():
        for sub, rows in enumerate(passes):
            @pl.when(fill_ref[b] > sub * MOE_SUB)
            def _():
                x_lo, x_hi, w_rows = _up_inputs(rec_ref, rows, first_expert)
                for e in range(UP_EXPERTS):
                    h_ref[rows, pl.ds(e * EXPERT_FF, EXPERT_FF)] = _up_hidden(
                        x_lo, x_hi, w_rows[e], wg_scr[e], wu_scr[e])

            @pl.when(fill_ref[b] <= sub * MOE_SUB)
            def _():
                h_ref[rows, :] = jnp.zeros((MOE_SUB, UP_EXPERTS * EXPERT_FF), BF16)


def _expert_up(l, blk_gid, blk_fill, blk_last, rec_sorted, e_gate, e_up):
    halves = EXPERTS_PER_GROUP // UP_EXPERTS
    wspec = pl.BlockSpec((None, UP_EXPERTS, D_MODEL, EXPERT_FF),
                         lambda h, b, gid, fill, last: (l, gid[b] * halves + h, 0, 0))
    grid_spec = pltpu.PrefetchScalarGridSpec(
        num_scalar_prefetch=3,
        grid=(halves, MOE_NBLK),
        in_specs=[
            pl.BlockSpec((MOE_BLK, REC_W), lambda h, b, gid, fill, last: (jnp.minimum(b, last[0]), 0)),
            wspec, wspec,
        ],
        out_specs=pl.BlockSpec((MOE_BLK, UP_EXPERTS * EXPERT_FF), lambda h, b, gid, fill, last: (b, h)),
        scratch_shapes=[pltpu.VMEM((UP_EXPERTS, D_MODEL, EXPERT_FF), BF16),
                        pltpu.VMEM((UP_EXPERTS, D_MODEL, EXPERT_FF), BF16)],
    )
    return pl.pallas_call(
        _expert_up_kernel,
        grid_spec=grid_spec,
        out_shape=jax.ShapeDtypeStruct((MOE_SLOTS, GROUP_FF), BF16),
        compiler_params=_cparams(("arbitrary", "arbitrary")),
        name="expert_up",
    )(blk_gid, blk_fill, blk_last, rec_sorted, e_gate, e_up)


DOWN_CH = 256


def _expert_down_kernel(gid_ref, fill_ref, h_ref, wd_ref, y_ref, wd_scr):
    b = pl.program_id(0)
    prev = gid_ref[jnp.maximum(b - 1, 0)]
    fresh = (b == 0) | (gid_ref[b] != prev)

    passes = [pl.ds(sub * MOE_SUB, MOE_SUB) for sub in range(MOE_BLK // MOE_SUB)]

    @pl.when(fresh)
    def _():
        for c in range(HALF_D // DOWN_CH):
            cols = pl.ds(c * DOWN_CH, DOWN_CH)
            pair = pl.ds(HALF_D + c * DOWN_CH, DOWN_CH)
            w_a = wd_ref[:, cols].astype(BF16)
            w_b = wd_ref[:, pair].astype(BF16)
            wd_scr[:, cols] = w_a
            wd_scr[:, pair] = w_b
            for rows in passes:
                hid = h_ref[rows, :]
                y_ref[rows, cols] = _pack_pairs(jnp.dot(hid, w_a, preferred_element_type=F32),
                                                jnp.dot(hid, w_b, preferred_element_type=F32))

    @pl.when(jnp.logical_not(fresh))
    def _():
        for sub, rows in enumerate(passes):
            @pl.when(fill_ref[b] > sub * MOE_SUB)
            def _():
                hid = h_ref[rows, :]
                for c in range(HALF_D // DOWN_CH):
                    cols = pl.ds(c * DOWN_CH, DOWN_CH)
                    pair = pl.ds(HALF_D + c * DOWN_CH, DOWN_CH)
                    y_ref[rows, cols] = _pack_pairs(jnp.dot(hid, wd_scr[:, cols], preferred_element_type=F32),
                                                    jnp.dot(hid, wd_scr[:, pair], preferred_element_type=F32))

            @pl.when(fill_ref[b] <= sub * MOE_SUB)
            def _():
                y_ref[rows, :] = jnp.zeros((MOE_SUB, HALF_D), jnp.uint32)


def _expert_down(l, blk_gid, blk_fill, h_sorted, e_down_grouped):
    grid_spec = pltpu.PrefetchScalarGridSpec(
        num_scalar_prefetch=2,
        grid=(MOE_NBLK,),
        in_specs=[
            pl.BlockSpec((MOE_BLK, GROUP_FF), lambda b, gid, fill: (b, 0)),
            pl.BlockSpec((None, None, GROUP_FF, D_MODEL), lambda b, gid, fill: (l, gid[b], 0, 0)),
        ],
        out_specs=pl.BlockSpec((MOE_BLK, HALF_D), lambda b, gid, fill: (b, 0)),
        scratch_shapes=[pltpu.VMEM((GROUP_FF, D_MODEL), BF16)],
    )
    return pl.pallas_call(
        _expert_down_kernel,
        grid_spec=grid_spec,
        out_shape=jax.ShapeDtypeStruct((MOE_SLOTS, HALF_D), jnp.uint32),
        compiler_params=_cparams(("arbitrary",)),
        name="expert_down",
    )(blk_gid, blk_fill, h_sorted, e_down_grouped)


def _moe(l, rec, gid, rank, counts, e_gate, e_up, e_down_grouped):
    padded = (counts + MOE_BLK - 1) // MOE_BLK * MOE_BLK
    pend = jnp.cumsum(padded)
    pstart = pend - padded
    pos = (pstart[gid] + rank).astype(jnp.int32)
    blk_start = jnp.arange(MOE_NBLK, dtype=jnp.int32) * MOE_BLK
    blk_gid = jnp.minimum(jnp.sum((blk_start[:, None] >= pend[None, :]).astype(jnp.int32), axis=1),
                          N_EXPERT_GROUPS - 1)
    blk_fill = jnp.clip(pstart[blk_gid] + counts[blk_gid] - blk_start, 0, MOE_BLK)
    blk_fill = jnp.where(blk_start < pend[-1], blk_fill, 0).astype(jnp.int32)
    blk_last = (pend[-1:] // MOE_BLK - 1).astype(jnp.int32)
    rec_sorted = _dispatch(pos, pend.astype(jnp.int32), rec)
    hid = _expert_up(l, blk_gid, blk_fill, blk_last, rec_sorted, e_gate, e_up)
    y_sorted = _expert_down(l, blk_gid, blk_fill, hid, e_down_grouped)
    return y_sorted[pos]


FIN_TM = 512
FIN_PROMPT_TILES = N_PROMPT // FIN_TM


def _final_kernel(x_ref, y_ref, g2_ref, fg_ref, op_ref, os_ref):
    i = pl.program_id(0)
    seg = i // (SEG_TOK // FIN_TM)
    x = x_ref[...] + g2_ref[pl.ds(seg, 1), :] * _unpack_pairs(y_ref[...])
    ms = jnp.mean(x * x, axis=-1, keepdims=True)
    out = x * lax.rsqrt(ms + EPS) * fg_ref[...]

    @pl.when(i < FIN_PROMPT_TILES)
    def _():
        op_ref[...] = out

    @pl.when(i >= FIN_PROMPT_TILES)
    def _():
        os_ref[...] = out


def _final_norm(xmid, moe_y, mod, final_g):
    return pl.pallas_call(
        _final_kernel,
        grid=(N_TOK // FIN_TM,),
        in_specs=[
            pl.BlockSpec((FIN_TM, D_MODEL), lambda i: (i, 0)),
            pl.BlockSpec((FIN_TM, HALF_D), lambda i: (i, 0)),
            _mod_spec(DEPTH - 1, 5),
            pl.BlockSpec((1, D_MODEL), lambda i: (0, 0)),
        ],
        out_specs=[
            pl.BlockSpec((FIN_TM, D_MODEL), lambda i: (jnp.minimum(i, FIN_PROMPT_TILES - 1), 0)),
            pl.BlockSpec((FIN_TM, D_MODEL), lambda i: (jnp.maximum(i - FIN_PROMPT_TILES, 0), 0)),
        ],
        out_shape=[
            jax.ShapeDtypeStruct((N_PROMPT, D_MODEL), F32),
            jax.ShapeDtypeStruct((N_SAMPLE, D_MODEL), F32),
        ],
        compiler_params=_cparams(("arbitrary",)),
        name="final_norm",
    )(xmid, moe_y, mod, final_g.reshape(1, D_MODEL))


def _grid_pos_embed(rows):
    quarter = D_MODEL // 4
    freqs = 1.0 / (POS_BASE ** (jnp.arange(quarter, dtype=F32) / quarter))
    er = jnp.arange(rows, dtype=F32)[:, None] * freqs
    ec = jnp.arange(GRID_W, dtype=F32)[:, None] * freqs
    row_emb = jnp.concatenate([jnp.sin(er), jnp.cos(er)], axis=-1)
    col_emb = jnp.concatenate([jnp.sin(ec), jnp.cos(ec)], axis=-1)
    return jnp.stack([row_emb, col_emb])


def kernel(x_prompt, x_sample, state_ssm_re, state_ssm_im, c, c_ctx, norm1_g, norm2_g, w_mod, b_mod,
           w_in, ssm_lam_re, ssm_lam_im, ssm_log_step, ssm_b_re, ssm_b_im, ssm_c_re, ssm_c_im, ssm_d,
           w_glu, b_glu, gm_ln_g, gm_w_s, gm_b_s, w_pa, w_pb, w_o, w_router, b_router,
           e_gate, e_up, e_down, final_g):
    cvec = jnp.concatenate([c_ctx[None], c, jnp.zeros((MOD_ROWS - 1 - DEC_BATCH, D_MODEL), F32)], axis=0)
    mod = _modulation(cvec, w_mod, b_mod)

    perm = (jnp.arange(N_EXPERT_GROUPS)[None, :] * EXPERTS_PER_GROUP
            + jnp.arange(EXPERTS_PER_GROUP)[:, None]).reshape(N_EXPERTS)
    wr = w_router.astype(F32).T[perm]
    wr_hi = wr.astype(BF16)
    wr_lo = (wr - wr_hi.astype(F32)).astype(BF16)
    wr_t = jnp.concatenate([wr_hi, wr_lo], axis=0)
    br_col = b_router.astype(F32)[perm][:, None]

    w_glu_b, w_s_b = w_glu.astype(BF16), gm_w_s.astype(BF16)
    w_pa_b, w_pb_b, w_o_b = w_pa.astype(BF16), w_pb.astype(BF16), w_o.astype(BF16)
    b_s_full = jnp.repeat(jnp.transpose(gm_b_s.astype(F32), (0, 2, 1)), GM_GROUP_DIM, axis=2)
    e_down_grouped = e_down.reshape(DEPTH, N_EXPERT_GROUPS, GROUP_FF, D_MODEL)
    w1, w2, a16 = _s5_prep(ssm_lam_re, ssm_lam_im, ssm_log_step, ssm_b_re, ssm_b_im, ssm_c_re, ssm_c_im)
    h0_lat = jnp.concatenate([state_ssm_re[:, :, 0], state_ssm_re[:, :, 1],
                              state_ssm_im[:, :, 0], state_ssm_im[:, :, 1]], axis=-1).astype(F32)
    h0 = jnp.concatenate([jnp.zeros((DEPTH, 1, SSM_GROUPS, STATE_W), F32),
                          jnp.transpose(h0_lat, (1, 0, 2, 3))], axis=1)

    d_lanes = jnp.tile(ssm_d.astype(F32).reshape(DEPTH, SSM_GROUPS, SSM_GROUP), (1, 1, SCAN_T))

    inproj_in = (x_prompt.reshape(N_PROMPT, D_MODEL), x_sample.reshape(N_SAMPLE, D_MODEL),
                 _grid_pos_embed(DEC_SEQ // GRID_W))

    new_re, new_im = [], []
    xmid = moe_y = None
    for l in range(DEPTH):
        xres, u, z = _inproj(l, *inproj_in, mod, norm1_g, gm_ln_g, w_in)
        ys, fs = _s5_scan(l, u, w1, w2, a16, h0, d_lanes)
        fin = fs[0]
        p = SSM_STATE
        new_re.append(jnp.stack([fin[:, :, 0:p], fin[::-1, :, p:2 * p]], axis=1))
        new_im.append(jnp.stack([fin[:, :, 2 * p:3 * p], fin[::-1, :, 3 * p:4 * p]], axis=1))

        ya, yb = _mix(l, ys, z, w_glu_b, b_glu.astype(F32), w_s_b, b_s_full)
        xmid, rec, gid, rank, cnt = _merge(l, ya, yb, z, xres, mod, norm2_g, w_pa_b, w_pb_b, w_o_b,
                                           wr_t, br_col)
        moe_y = _moe(l, rec, gid.reshape(N_TOK), rank.reshape(N_TOK), cnt[:, 0].astype(jnp.int32),
                     e_gate, e_up, e_down_grouped)
        inproj_in = (xmid, moe_y, mod)

    y_prompt, y_sample = _final_norm(xmid, moe_y, mod, final_g)
    new_state_re = jnp.stack(new_re, axis=1).astype(x_prompt.dtype)
    new_state_im = jnp.stack(new_im, axis=1).astype(x_prompt.dtype)
    return (y_prompt.reshape(BATCH, SEQ, D_MODEL), y_sample.reshape(DEC_BATCH, DEC_SEQ, D_MODEL),
            new_state_re, new_state_im)
```

```python
import functools

import jax
import jax.numpy as jnp
from jax import lax
from jax.experimental import pallas as pl
from jax.experimental.pallas import tpu as pltpu

F32 = jnp.float32
BF16 = jnp.bfloat16

D_MODEL = 2048
BATCH = 16
SEQ = 256
DEPTH = 2
DEC_BATCH = 2
DEC_SEQ = 4096
GRID_W = 64
POS_BASE = 10000.0
EPS = 1e-6
SSM_WIDTH = D_MODEL // 2
SSM_GROUP = 16
SSM_GROUPS = SSM_WIDTH // SSM_GROUP
SSM_STATE = 64
GM_WIDTH = D_MODEL // 2
GM_CHUNK = 128
GM_GROUPS = 8
GM_GROUP_DIM = GM_WIDTH // GM_GROUPS
IN_WIDTH = SSM_WIDTH + 2 * GM_WIDTH + 2 * D_MODEL
N_EXPERTS = 32
N_EXPERT_GROUPS = 8
EXPERTS_PER_GROUP = N_EXPERTS // N_EXPERT_GROUPS
EXPERT_FF = D_MODEL // 4
N_MOD = 6

N_PROMPT = BATCH * SEQ
N_SAMPLE = DEC_BATCH * DEC_SEQ
N_TOK = N_PROMPT + N_SAMPLE
SEG_TOK = 4096
MOD_ROWS = 8
LANES = 128
SUBLANES = 8

SCAN_T = 16
SCAN_W = SCAN_T * SSM_GROUP
SCAN_ROWS = N_TOK // SCAN_T
SCAN_BLK = 256
SCAN_TOK = SCAN_BLK * SCAN_T
N_SCAN_BLK = SCAN_ROWS // SCAN_BLK
G_OCT = SUBLANES
SCAN_G = 2 * SUBLANES
SCAN_GW = SCAN_G * SSM_GROUP
STATE_W = 4 * SSM_STATE
HALF_W = 2 * SSM_STATE

MOE_BLK = 512
MOE_SLOTS = N_TOK + N_EXPERT_GROUPS * MOE_BLK
MOE_NBLK = MOE_SLOTS // MOE_BLK
GROUP_FF = EXPERTS_PER_GROUP * EXPERT_FF

VMEM_LIMIT = 56 * 1024 * 1024


def _cparams(sem):
    return pltpu.CompilerParams(dimension_semantics=sem, vmem_limit_bytes=VMEM_LIMIT)


MOD_TN = 1024


def _mod_kernel(c_ref, w_ref, b_ref, o_ref):
    c = c_ref[...]
    s = c * jax.nn.sigmoid(c)
    s_hi = s.astype(BF16)
    s_lo = (s - s_hi.astype(F32)).astype(BF16)
    w = w_ref[...]
    w_hi = w.astype(BF16)
    w_lo = (w - w_hi.astype(F32)).astype(BF16)
    both = jnp.dot(jnp.concatenate([s_hi, s_lo], axis=0), w_hi, preferred_element_type=F32)
    cross = jnp.dot(s_hi, w_lo, preferred_element_type=F32)
    o_ref[...] = both[0:MOD_ROWS] + both[MOD_ROWS:2 * MOD_ROWS] + cross + b_ref[...]


def _modulation(cvec, w_mod, b_mod):
    width = N_MOD * D_MODEL
    return pl.pallas_call(
        _mod_kernel,
        grid=(DEPTH, width // MOD_TN),
        in_specs=[
            pl.BlockSpec((MOD_ROWS, D_MODEL), lambda l, n: (0, 0)),
            pl.BlockSpec((None, D_MODEL, MOD_TN), lambda l, n: (l, 0, n)),
            pl.BlockSpec((None, 1, MOD_TN), lambda l, n: (l, 0, n)),
        ],
        out_specs=pl.BlockSpec((None, MOD_ROWS, MOD_TN), lambda l, n: (l, 0, n)),
        out_shape=jax.ShapeDtypeStruct((DEPTH, MOD_ROWS, width), F32),
        compiler_params=_cparams(("arbitrary", "arbitrary")),
        name="adaln_mod",
    )(cvec, w_mod, b_mod.reshape(DEPTH, 1, width))


def _pack_pairs(a, b):
    hi = lax.bitcast_convert_type(a.astype(BF16).astype(F32), jnp.uint32)
    lo = lax.bitcast_convert_type(b.astype(BF16).astype(F32), jnp.uint32)
    return hi | (lo >> 16)


def _unpack_pairs(packed):
    hi = lax.bitcast_convert_type(packed & jnp.uint32(0xFFFF0000), F32)
    lo = lax.bitcast_convert_type(packed << 16, F32)
    return jnp.concatenate([hi, lo], axis=-1)


def _mod_spec(l, k):
    return pl.BlockSpec((None, MOD_ROWS, D_MODEL), lambda i: (l, 0, k))


INP_TM = 256
INP_CH = 256
INP_PROMPT_TILES = N_PROMPT // INP_TM
Z_WIDTH = IN_WIDTH - SSM_WIDTH
INP_VMEM_LIMIT = 60 * 1024 * 1024


def _inproj_kernel(*refs, first, layer):
    if first:
        xa_ref, xb_ref, add_ref = refs[:3]
    else:
        xa_ref, add_ref, gain_ref = refs[:3]
    sc_ref, sh_ref, g_ref, ln_ref, w_hbm, xres_ref, u_ref, z_ref, h_scr, v_scr, w_ref, stage, sem = refs[3:]
    i = pl.program_id(0)
    seg = i // (SEG_TOK // INP_TM)

    @pl.when(i == 0)
    def _():
        n_chunks = IN_WIDTH // INP_CH

        def chunk(c):
            return pltpu.make_async_copy(w_hbm.at[layer, :, pl.ds(c * INP_CH, INP_CH)],
                                         stage.at[c % 2], sem.at[c % 2])

        chunk(0).start()
        for c in range(n_chunks):
            if c + 1 < n_chunks:
                chunk(c + 1).start()
            chunk(c).wait()
            w_ref[:, pl.ds(c * INP_CH, INP_CH)] = stage[c % 2].astype(BF16)

    if first:
        latent = i >= INP_PROMPT_TILES
        rows_per_tile = INP_TM // GRID_W
        row0 = (i % (DEC_SEQ // INP_TM)) * rows_per_tile
        row_part = jnp.concatenate(
            [jnp.broadcast_to(add_ref[0, pl.ds(row0 + q, 1), :], (GRID_W, D_MODEL // 2))
             for q in range(rows_per_tile)], axis=0)
        col_part = jnp.concatenate([add_ref[1]] * rows_per_tile, axis=0)
        x = jnp.where(latent, xb_ref[...] + jnp.concatenate([row_part, col_part], axis=1), xa_ref[...])
    else:
        x = xa_ref[...] + gain_ref[pl.ds(seg, 1), :] * _unpack_pairs(add_ref[...])
    xres_ref[...] = x
    ms = jnp.mean(x * x, axis=-1, keepdims=True)
    y = x * lax.rsqrt(ms + EPS) * g_ref[...]
    h = y * (1.0 + sc_ref[pl.ds(seg, 1), :]) + sh_ref[pl.ds(seg, 1), :]
    h_scr[...] = h.astype(BF16)

    def proj(col):
        return jnp.dot(h_scr[...], w_ref[:, pl.ds(col, INP_CH)], preferred_element_type=F32)

    n_ch = SSM_WIDTH // INP_CH
    for c in range(n_ch):
        acc = proj(c * INP_CH)
        for k in range(INP_CH // LANES):
            u_ref[c * (INP_CH // LANES) + k] = acc[:, k * LANES:(k + 1) * LANES]
    for c in range(n_ch):
        z_ref[:, pl.ds(c * INP_CH, INP_CH)] = jax.nn.gelu(proj(SSM_WIDTH + c * INP_CH)).astype(BF16)
    row_sum = jnp.zeros((INP_TM, 1), F32)
    for c in range(n_ch):
        v = jax.nn.gelu(proj(SSM_WIDTH + GM_WIDTH + c * INP_CH))
        v_scr[:, pl.ds(c * INP_CH, INP_CH)] = v
        row_sum = row_sum + jnp.sum(v, axis=-1, keepdims=True)
    mu = row_sum * (1.0 / GM_WIDTH)
    dev = v_scr[...] - mu
    var = jnp.mean(jnp.square(dev), axis=-1, keepdims=True)
    z_ref[:, pl.ds(GM_WIDTH, GM_WIDTH)] = (dev * lax.rsqrt(var + EPS) * ln_ref[...]).astype(BF16)
    gates = SSM_WIDTH + 2 * GM_WIDTH
    for c in range(2 * D_MODEL // INP_CH):
        z_ref[:, pl.ds(2 * GM_WIDTH + c * INP_CH, INP_CH)] = jax.nn.sigmoid(
            proj(gates + c * INP_CH)).astype(BF16)


def _inproj(l, xa, xb_or_add, add_or_gain, mod, norm1_g, gm_ln_g, w_in):
    first = l == 0
    row_tile = lambda m: pl.BlockSpec((INP_TM, D_MODEL), m)
    if first:
        lead = [row_tile(lambda i: (jnp.minimum(i, INP_PROMPT_TILES - 1), 0)),
                row_tile(lambda i: (jnp.maximum(i - INP_PROMPT_TILES, 0), 0)),
                pl.BlockSpec((2, DEC_SEQ // GRID_W, D_MODEL // 2), lambda i: (0, 0, 0))]
    else:
        lead = [row_tile(lambda i: (i, 0)), pl.BlockSpec((INP_TM, HALF_D), lambda i: (i, 0)),
                _mod_spec(l - 1, 5)]
    vec = lambda w: pl.BlockSpec((None, 1, w), lambda i: (l, 0, 0))
    return pl.pallas_call(
        functools.partial(_inproj_kernel, first=first, layer=l),
        grid=(N_TOK // INP_TM,),
        in_specs=lead + [
            _mod_spec(l, 1), _mod_spec(l, 0),
            vec(D_MODEL), vec(GM_WIDTH),
            pl.BlockSpec(memory_space=pl.ANY),
        ],
        out_specs=[
            pl.BlockSpec((INP_TM, D_MODEL), lambda i: (i, 0)),
            pl.BlockSpec((SSM_WIDTH // LANES, INP_TM, LANES), lambda i: (0, i, 0)),
            pl.BlockSpec((INP_TM, Z_WIDTH), lambda i: (i, 0)),
        ],
        out_shape=[
            jax.ShapeDtypeStruct((N_TOK, D_MODEL), F32),
            jax.ShapeDtypeStruct((SSM_WIDTH // LANES, N_TOK, LANES), F32),
            jax.ShapeDtypeStruct((N_TOK, Z_WIDTH), BF16),
        ],
        scratch_shapes=[pltpu.VMEM((INP_TM, D_MODEL), BF16), pltpu.VMEM((INP_TM, GM_WIDTH), F32),
                        pltpu.VMEM((D_MODEL, IN_WIDTH), BF16), pltpu.VMEM((2, D_MODEL, INP_CH), F32),
                        pltpu.SemaphoreType.DMA((2,))],
        compiler_params=pltpu.CompilerParams(dimension_semantics=("arbitrary",),
                                             vmem_limit_bytes=INP_VMEM_LIMIT),
        name="in_proj",
    )(xa, xb_or_add, add_or_gain, mod, mod, norm1_g.reshape(DEPTH, 1, D_MODEL),
      gm_ln_g.reshape(DEPTH, 1, GM_WIDTH), w_in)


PK_BRE, PK_BIM, PK_CRE, PK_CIM = 0, 16, 32, 48
PK_LR, PK_LI = 64, 65


def _split_bf16(x):
    hi = x.astype(BF16)
    return hi, (x - hi.astype(F32)).astype(BF16)


def _dot_split(a, b):
    a_hi, a_lo = _split_bf16(a)
    b_hi, b_lo = _split_bf16(b)
    m = a.shape[0]
    both = jnp.dot(jnp.concatenate([a_hi, a_lo], axis=0), b_hi, preferred_element_type=F32)
    return both[0:m] + both[m:2 * m] + jnp.dot(a_hi, b_lo, preferred_element_type=F32)


POW_ROWS = 24


def _prep_kernel(pk_ref, row_ref, pow_ref, tile_ref, w1_ref, w2_ref, a_ref):
    p = SSM_STATE
    k_sub = jnp.minimum(lax.broadcasted_iota(jnp.int32, (POW_ROWS, LANES), 0), SCAN_T).astype(F32)
    lane = lax.broadcasted_iota(jnp.int32, (p, LANES), 1)
    col = lax.broadcasted_iota(jnp.int32, (SSM_GROUP, SCAN_W), 1)

    def spread_pow(x_r, x_i, which):
        parts = jnp.concatenate(_split_bf16(x_r) + _split_bf16(x_i), axis=0)
        out = jnp.dot(parts, pow_ref[which], preferred_element_type=F32)
        return out[0:p] + out[p:2 * p], out[2 * p:3 * p] + out[3 * p:4 * p]

    def spread_tiles(x):
        out = jnp.dot(jnp.concatenate(_split_bf16(x), axis=0), tile_ref[...], preferred_element_type=F32)
        out = out[0:p] + out[p:2 * p]
        return [out[:, n * SCAN_W:(n + 1) * SCAN_W] for n in range(4)]

    def group(g, _):
        rows = row_ref[g]
        grow_r = rows[0:1] * rows[2:3]
        grow_i = rows[1:2] * rows[2:3]
        mag = jnp.exp(grow_r * k_sub)
        ang = grow_i * k_sub
        unused = jnp.zeros((LANES - POW_ROWS, LANES), F32)
        pw_t_r = jnp.concatenate([mag * jnp.cos(ang), unused], axis=0).T
        pw_t_i = jnp.concatenate([mag * jnp.sin(ang), unused], axis=0).T
        per_dir = []
        for d in range(2):
            pk = pk_ref[d, g]
            lr = pk[:, PK_LR:PK_LR + 1]
            li = pk[:, PK_LI:PK_LI + 1]
            p_r = pw_t_r[d * p:(d + 1) * p]
            p_i = pw_t_i[d * p:(d + 1) * p]
            a_r = p_r[:, 1:2]
            a_i = p_i[:, 1:2]
            den = lr * lr + li * li
            q_r = ((a_r - 1.0) * lr + a_i * li) / den
            q_i = (a_i * lr - (a_r - 1.0) * li) / den
            per_dir.append((pk, p_r, p_i, q_r, q_i))

        w1_rows, lag, carry = [], [], []
        for d in range(2):
            pk, p_r, p_i, q_r, q_i = per_dir[d]
            b_r, b_i, c_r, c_i = spread_tiles(pk)
            bb_r = q_r * b_r - q_i * b_i
            bb_i = q_r * b_i + q_i * b_r
            pw_r, pw_i = spread_pow(p_r, p_i, 1 if d == 0 else 0)
            w1_rows.append((pw_r * bb_r - pw_i * bb_i, pw_r * bb_i + pw_i * bb_r))
            pl_r, pl_i = spread_pow(p_r, p_i, 0 if d == 0 else 1)
            cl_r = c_r * pl_r - c_i * pl_i
            cl_i = c_r * pl_i + c_i * pl_r
            pk_im = pltpu.roll(pk, LANES - (PK_BIM - PK_BRE), 1)
            bt_r = (q_r * pk - q_i * pk_im).T[0:SSM_GROUP, :]
            bt_i = (q_r * pk_im + q_i * pk).T[0:SSM_GROUP, :]
            lag.append(_dot_split(bt_r, cl_r) - _dot_split(bt_i, cl_i))
            pc_r, pc_i = spread_pow(p_r, p_i, 2 if d == 0 else 3)
            carry.append((c_r * pc_r - c_i * pc_i, -(c_r * pc_i + c_i * pc_r)))

        (f_re, f_im), (b_re, b_im) = w1_rows
        w1_ref[g] = jnp.concatenate([f_re, b_re, f_im, b_im], axis=0).T.astype(BF16)

        for s in range(SCAN_T):
            fwd = lag[0] if s == 0 else pltpu.roll(lag[0], SSM_GROUP * s, 1)
            fwd = jnp.where(col >= SSM_GROUP * s, fwd, 0.0)
            shift_b = SSM_GROUP * (SCAN_T - 1 - s)
            bwd = lag[1] if shift_b == 0 else pltpu.roll(lag[1], SCAN_W - shift_b, 1)
            bwd = jnp.where(col < SSM_GROUP * (s + 1), bwd, 0.0)
            w2_ref[g, pl.ds(SSM_GROUP * s, SSM_GROUP), :] = (fwd + bwd).astype(BF16)
        (x_re, x_im), (y_re, y_im) = carry
        for n, rows in enumerate((x_re, y_re, x_im, y_im)):
            w2_ref[g, pl.ds(SCAN_W + SSM_STATE * n, SSM_STATE), :] = rows.astype(BF16)

        cols = [per_dir[0][1], per_dir[1][1], per_dir[0][2], per_dir[1][2]]
        a_cols = jnp.zeros((SSM_STATE, LANES), F32)
        for n, c in enumerate(cols):
            a_cols = jnp.where(lane == n, c[:, SCAN_T:SCAN_T + 1], a_cols)
        a_ref[g] = a_cols
        return 0

    lax.fori_loop(0, G_OCT, group, 0, unroll=4)


def _s5_prep(lam_re, lam_im, log_step, b_re, b_im, c_re, c_im):
    shape = (DEPTH, 2, SSM_GROUPS, SSM_STATE)
    lr = lam_re.astype(F32)
    li = lam_im.astype(F32)
    dt = jnp.broadcast_to(jnp.exp(log_step.astype(F32))[..., None], shape)
    pk = jnp.concatenate([
        b_re.astype(F32), b_im.astype(F32),
        jnp.swapaxes(c_re.astype(F32), -1, -2), jnp.swapaxes(c_im.astype(F32), -1, -2),
        lr[..., None], li[..., None],
        jnp.zeros(shape + (LANES - PK_LI - 1,), F32)], axis=-1)
    both_dirs = lambda a: jnp.concatenate([a[:, 0], a[:, 1]], axis=-1)
    rows = jnp.stack([both_dirs(lr), both_dirs(li), both_dirs(dt)], axis=2)
    rows = jnp.concatenate([rows, jnp.zeros((DEPTH, SSM_GROUPS, SUBLANES - 3, LANES), F32)], axis=2)

    blk = jnp.arange(SCAN_W) // SSM_GROUP
    k = jnp.arange(LANES)[:, None]
    pows = [k == blk[None, :], k == (SCAN_T - 1 - blk)[None, :], k == (blk + 1)[None, :],
            k == (SCAN_T - blk)[None, :]]
    h = (jnp.arange(SCAN_W) % SSM_GROUP)[None, :]
    sel_pow = jnp.stack(pows).astype(BF16)
    sel_tile = jnp.concatenate([k == h + off for off in (PK_BRE, PK_BIM, PK_CRE, PK_CIM)],
                               axis=1).astype(BF16)

    n_oct = SSM_GROUPS // G_OCT
    w1, w2, a_cols = pl.pallas_call(
        _prep_kernel,
        grid=(DEPTH, n_oct),
        in_specs=[
            pl.BlockSpec((None, 2, G_OCT, SSM_STATE, LANES), lambda l, o: (l, 0, o, 0, 0)),
            pl.BlockSpec((None, G_OCT, SUBLANES, LANES), lambda l, o: (l, o, 0, 0)),
            pl.BlockSpec((4, LANES, SCAN_W), lambda l, o: (0, 0, 0)),
            pl.BlockSpec((LANES, 4 * SCAN_W), lambda l, o: (0, 0)),
        ],
        out_specs=[
            pl.BlockSpec((None, G_OCT, SCAN_W, STATE_W), lambda l, o: (l, o, 0, 0)),
            pl.BlockSpec((None, G_OCT, SCAN_W + STATE_W, SCAN_W), lambda l, o: (l, o, 0, 0)),
            pl.BlockSpec((None, G_OCT, SSM_STATE, LANES), lambda l, o: (l, o, 0, 0)),
        ],
        out_shape=[
            jax.ShapeDtypeStruct((DEPTH, SSM_GROUPS, SCAN_W, STATE_W), BF16),
            jax.ShapeDtypeStruct((DEPTH, SSM_GROUPS, SCAN_W + STATE_W, SCAN_W), BF16),
            jax.ShapeDtypeStruct((DEPTH, SSM_GROUPS, SSM_STATE, LANES), F32),
        ],
        compiler_params=_cparams(("arbitrary", "arbitrary")),
        name="s5_prep",
    )(pk, rows, sel_pow, sel_tile)
    a16 = jnp.swapaxes(a_cols[..., 0:4], -1, -2).reshape(DEPTH, SSM_GROUPS, STATE_W)
    return w1, w2, a16


def _s5_kernel(u_ref, w1_ref, w2_ref, a_ref, h0_ref, d_ref, y_ref, fs_ref,
               t_scr, ug_scr, vr_scr, vi_scr, cr_scr, ci_scr, fr_scr, fi_scr):
    blk = pl.program_id(1)
    seq_rows = jnp.where(blk == 0, SEQ // SCAN_T, DEC_SEQ // SCAN_T)

    for s in range(SCAN_T):
        for j in range(SCAN_GW // LANES):
            t_scr[s, pl.ds(j * LANES, LANES), :] = (
                u_ref[j, pl.ds(s, SCAN_BLK, stride=SCAN_T), :].astype(BF16).T)
    for g in range(SCAN_G):
        stacked = t_scr[:, pl.ds(g * SSM_GROUP, SSM_GROUP), :].reshape(SCAN_W, SCAN_BLK)
        ug_scr[g] = stacked.T

    for g in range(SCAN_G):
        v = jnp.dot(ug_scr[g], w1_ref[g], preferred_element_type=F32)
        vr_scr[pl.ds(g, SCAN_BLK, stride=SCAN_G), :] = v[:, 0:HALF_W]
        vi_scr[pl.ds(g, SCAN_BLK, stride=SCAN_G), :] = v[:, HALF_W:STATE_W]

    a_r = a_ref[:, 0:HALF_W]
    a_i = a_ref[:, HALF_W:STATE_W]
    h0_r = h0_ref[:, 0:HALF_W]
    h0_i = h0_ref[:, HALF_W:STATE_W]
    fwd_lanes = lax.broadcasted_iota(jnp.int32, (SCAN_G, HALF_W), 1) < SSM_STATE
    bwd_lanes = jnp.logical_not(fwd_lanes)

    def step(k, carry):
        s_r, s_i = carry
        rf = pl.ds(pl.multiple_of(k * SCAN_G, SCAN_G), SCAN_G)
        rb = pl.ds(pl.multiple_of((SCAN_BLK - 1 - k) * SCAN_G, SCAN_G), SCAN_G)
        restart = (k & (seq_rows - 1)) == 0
        s_r = jnp.where(restart, h0_r, s_r)
        s_i = jnp.where(restart, h0_i, s_i)
        pltpu.store(cr_scr.at[rf, :], s_r, mask=fwd_lanes)
        pltpu.store(cr_scr.at[rb, :], s_r, mask=bwd_lanes)
        pltpu.store(ci_scr.at[rf, :], s_i, mask=fwd_lanes)
        pltpu.store(ci_scr.at[rb, :], s_i, mask=bwd_lanes)
        v_r = jnp.where(fwd_lanes, vr_scr[rf, :], vr_scr[rb, :])
        v_i = jnp.where(fwd_lanes, vi_scr[rf, :], vi_scr[rb, :])
        n_r = a_r * s_r - a_i * s_i + v_r
        n_i = a_r * s_i + a_i * s_r + v_i
        fr_scr[rf, :] = n_r
        fi_scr[rf, :] = n_i
        return n_r, n_i

    zero = jnp.zeros((SCAN_G, HALF_W), F32)
    lax.fori_loop(0, SCAN_BLK, step, (zero, zero), unroll=4)

    for g in range(SCAN_G):
        c_r = cr_scr[pl.ds(g, SCAN_BLK, stride=SCAN_G), :].astype(BF16)
        c_i = ci_scr[pl.ds(g, SCAN_BLK, stride=SCAN_G), :].astype(BF16)
        y = jnp.dot(ug_scr[g], w2_ref[g, 0:SCAN_W, :], preferred_element_type=F32)
        y = y + jnp.dot(c_r, w2_ref[g, SCAN_W:SCAN_W + HALF_W, :], preferred_element_type=F32)
        y = y + jnp.dot(c_i, w2_ref[g, SCAN_W + HALF_W:SCAN_W + STATE_W, :], preferred_element_type=F32)
        y = y + d_ref[pl.ds(g, 1), :] * ug_scr[g].astype(F32)
        t_scr[:, pl.ds(g * SSM_GROUP, SSM_GROUP), :] = y.astype(BF16).T.reshape(SCAN_T, SSM_GROUP, SCAN_BLK)
    for s in range(SCAN_T):
        for j in range(SCAN_GW // LANES):
            y_ref[j, pl.ds(s, SCAN_BLK, stride=SCAN_T), :] = t_scr[s, pl.ds(j * LANES, LANES), :].T.astype(F32)

    rows_per_seq = SEQ // SCAN_T
    for q in range(SCAN_BLK // rows_per_seq):
        last = pl.ds((q * rows_per_seq + rows_per_seq - 1) * SCAN_G, SCAN_G)
        fs_ref[q, :, 0:HALF_W] = fr_scr[last, :]
        fs_ref[q, :, HALF_W:STATE_W] = fi_scr[last, :]


def _s5_scan(l, u, w1, w2, a16, h0, d_lanes):
    n_oct = SSM_GROUPS // SCAN_G
    n_fin = SCAN_BLK // (SEQ // SCAN_T)
    return pl.pallas_call(
        _s5_kernel,
        grid=(n_oct, N_SCAN_BLK),
        in_specs=[
            pl.BlockSpec((SCAN_GW // LANES, SCAN_TOK, LANES), lambda o, b: (o, b, 0)),
            pl.BlockSpec((None, SCAN_G, SCAN_W, STATE_W), lambda o, b: (l, o, 0, 0)),
            pl.BlockSpec((None, SCAN_G, SCAN_W + STATE_W, SCAN_W), lambda o, b: (l, o, 0, 0)),
            pl.BlockSpec((None, SCAN_G, STATE_W), lambda o, b: (l, o, 0)),
            pl.BlockSpec((None, None, SCAN_G, STATE_W), lambda o, b: (l, b, o, 0)),
            pl.BlockSpec((None, SCAN_G, SCAN_W), lambda o, b: (l, o, 0)),
        ],
        out_specs=[
            pl.BlockSpec((SCAN_GW // LANES, SCAN_TOK, LANES), lambda o, b: (o, b, 0)),
            pl.BlockSpec((None, n_fin, SCAN_G, STATE_W), lambda o, b: (b, 0, o, 0)),
        ],
        out_shape=[
            jax.ShapeDtypeStruct((SSM_WIDTH // LANES, N_TOK, LANES), F32),
            jax.ShapeDtypeStruct((N_SCAN_BLK, n_fin, SSM_GROUPS, STATE_W), F32),
        ],
        scratch_shapes=[
            pltpu.VMEM((SCAN_T, SCAN_GW, SCAN_BLK), BF16),
            pltpu.VMEM((SCAN_G, SCAN_BLK, SCAN_W), BF16),
        ] + [pltpu.VMEM((SCAN_BLK * SCAN_G, HALF_W), F32) for _ in range(6)],
        compiler_params=_cparams(("arbitrary", "arbitrary")),
        name="s5_scan",
    )(u, w1, w2, a16, h0, d_lanes)


MIX_TM = 512


def _mix_kernel(ys_ref, gu_ref, vn_ref, wglu_ref, bglu_ref, ws_ref, bs_ref, ya_ref, yb_ref):
    y = jnp.concatenate([ys_ref[k] for k in range(SSM_WIDTH // LANES)], axis=1)
    y = jax.nn.gelu(y)
    gate = jnp.dot(y.astype(BF16), wglu_ref[...], preferred_element_type=F32) + bglu_ref[...]
    ya_ref[...] = (y * jax.nn.sigmoid(gate)).astype(BF16)
    for c in range(MIX_TM // GM_CHUNK):
        rows = pl.ds(c * GM_CHUNK, GM_CHUNK)
        for g in range(GM_GROUPS):
            cols = pl.ds(g * GM_GROUP_DIM, GM_GROUP_DIM)
            mixed = jnp.dot(ws_ref[g], vn_ref[rows, cols], preferred_element_type=F32) + bs_ref[:, cols]
            yb_ref[rows, cols] = (gu_ref[rows, cols].astype(F32) * mixed).astype(BF16)


def _mix(l, ys, z, w_glu_bf16, b_glu, w_s_bf16, b_s_full):
    tile = lambda k: pl.BlockSpec((MIX_TM, SSM_WIDTH), lambda i: (i, k))
    slab = pl.BlockSpec((SSM_WIDTH // LANES, MIX_TM, LANES), lambda i: (0, i, 0))
    lay = lambda *shape: pl.BlockSpec((None,) + shape, lambda i: (l,) + tuple(0 for _ in shape))
    return pl.pallas_call(
        _mix_kernel,
        grid=(N_TOK // MIX_TM,),
        in_specs=[
            slab, tile(0), tile(1),
            lay(SSM_WIDTH, SSM_WIDTH),
            lay(1, SSM_WIDTH),
            lay(GM_GROUPS, GM_CHUNK, GM_CHUNK),
            lay(GM_CHUNK, GM_WIDTH),
        ],
        out_specs=[tile(0), tile(0)],
        out_shape=[
            jax.ShapeDtypeStruct((N_TOK, SSM_WIDTH), BF16),
            jax.ShapeDtypeStruct((N_TOK, GM_WIDTH), BF16),
        ],
        compiler_params=_cparams(("arbitrary",)),
        name="mixers",
    )(ys, z, z, w_glu_bf16, b_glu.reshape(DEPTH, 1, SSM_WIDTH), w_s_bf16, b_s_full)


MRG_TM = 512
MRG_SUB = 256
HALF_D = D_MODEL // 2
REC_W = HALF_D + LANES


def _merge_kernel(ya_ref, yb_ref, ga0_ref, ga1_ref, gb0_ref, gb1_ref, x_ref, g1_ref, sc_ref, sh_ref,
                  n2_ref, wpa_ref, wpb_ref, wo_ref, wr_ref, br_ref, tri_ref,
                  xmid_ref, rec_ref, gid_ref, rank_ref, cnt_ref, cnt_scr):
    i = pl.program_id(0)
    seg = i // (SEG_TOK // MRG_TM)

    @pl.when(i == 0)
    def _():
        cnt_scr[...] = jnp.zeros(cnt_scr.shape, F32)

    passes = [pl.ds(sub * MRG_SUB, MRG_SUB) for sub in range(MRG_TM // MRG_SUB)]
    for rows in passes:
        _merge_mix(rows, seg, ya_ref, yb_ref, ga0_ref, ga1_ref, gb0_ref, gb1_ref, x_ref, g1_ref,
                   wpa_ref, wpb_ref, wo_ref, xmid_ref)
    for rows in passes:
        _merge_route(rows, seg, sc_ref, sh_ref, n2_ref, wr_ref, br_ref, tri_ref,
                     xmid_ref, rec_ref, gid_ref, rank_ref, cnt_scr)
    cnt_ref[...] = jnp.broadcast_to(cnt_scr[...], cnt_ref.shape)


def _merge_mix(rows, seg, ya_ref, yb_ref, ga0_ref, ga1_ref, gb0_ref, gb1_ref, x_ref, g1_ref,
               wpa_ref, wpb_ref, wo_ref, xmid_ref):
    pa = jnp.dot(ya_ref[rows, :], wpa_ref[...], preferred_element_type=F32)
    pb = jnp.dot(yb_ref[rows, :], wpb_ref[...], preferred_element_type=F32)
    m_lo = ga0_ref[rows, :].astype(F32) * pa[:, :HALF_D] + gb0_ref[rows, :].astype(F32) * pb[:, :HALF_D]
    m_hi = ga1_ref[rows, :].astype(F32) * pa[:, HALF_D:] + gb1_ref[rows, :].astype(F32) * pb[:, HALF_D:]
    mix = jnp.dot(m_lo.astype(BF16), wo_ref[0:HALF_D, :], preferred_element_type=F32)
    mix = mix + jnp.dot(m_hi.astype(BF16), wo_ref[HALF_D:D_MODEL, :], preferred_element_type=F32)
    xmid_ref[rows, :] = x_ref[rows, :] + g1_ref[pl.ds(seg, 1), :] * mix


def _merge_route(rows, seg, sc_ref, sh_ref, n2_ref, wr_ref, br_ref, tri_ref,
                 xmid_ref, rec_ref, gid_ref, rank_ref, cnt_scr):
    x = xmid_ref[rows, :]
    ms = jnp.mean(x * x, axis=-1, keepdims=True)
    y = x * lax.rsqrt(ms + EPS) * n2_ref[...]
    h2 = y * (1.0 + sc_ref[pl.ds(seg, 1), :]) + sh_ref[pl.ds(seg, 1), :]
    hi = h2.astype(BF16)
    hi_f = hi.astype(F32)
    lo = (h2 - hi_f).astype(BF16)
    bits = lax.bitcast_convert_type(hi_f, jnp.uint32)
    rec_ref[rows, 0:HALF_D] = bits[:, :HALF_D] | (bits[:, HALF_D:] >> 16)

    nt = (((1,), (1,)), ((), ()))
    lt = (lax.dot_general(wr_ref[...], hi, nt, preferred_element_type=F32)
          + lax.dot_general(wr_ref[...], lo, nt, preferred_element_type=F32))
    logits = lt[0:N_EXPERTS] + lt[N_EXPERTS:2 * N_EXPERTS]
    scores = jax.nn.sigmoid(logits)
    sel = scores + br_ref[...]
    ng = N_EXPERT_GROUPS
    s = [sel[j * ng:(j + 1) * ng] for j in range(EXPERTS_PER_GROUP)]
    p = [scores[j * ng:(j + 1) * ng] for j in range(EXPERTS_PER_GROUP)]
    a, b = jnp.maximum(s[0], s[1]), jnp.minimum(s[0], s[1])
    c, d = jnp.maximum(s[2], s[3]), jnp.minimum(s[2], s[3])
    grp_score = jnp.maximum(a, c) + jnp.maximum(jnp.minimum(a, c), jnp.maximum(b, d))
    best = jnp.max(grp_score, axis=0, keepdims=True)
    g_iota = lax.broadcasted_iota(jnp.int32, grp_score.shape, 0)
    g_idx = jnp.min(jnp.where(grp_score == best, g_iota, ng), axis=0, keepdims=True)
    onehot = g_iota == g_idx
    v = [jnp.sum(jnp.where(onehot, sj, 0.0), axis=0, keepdims=True) for sj in s]
    q = [jnp.sum(jnp.where(onehot, pj, 0.0), axis=0, keepdims=True) for pj in p]
    picked = []
    for j in range(EXPERTS_PER_GROUP):
        rank = jnp.zeros(v[j].shape, jnp.int32)
        for o in range(EXPERTS_PER_GROUP):
            if o == j:
                continue
            ahead = (v[o] > v[j]) | ((v[o] == v[j]) & (o < j))
            rank = rank + ahead.astype(jnp.int32)
        picked.append(jnp.where(rank < 2, q[j], 0.0))
    total = picked[0] + picked[1] + picked[2] + picked[3]
    gid_ref[:, rows] = g_idx
    cw_rows = jnp.concatenate([pj / total for pj in picked]
                              + [jnp.zeros((LANES - EXPERTS_PER_GROUP, MRG_SUB), F32)], axis=0)
    rec_ref[rows, HALF_D:REC_W] = lax.bitcast_convert_type(cw_rows.T, jnp.uint32)

    hot = onehot.astype(BF16)
    within = jnp.dot(hot, tri_ref[...], preferred_element_type=F32)
    before = jnp.sum(jnp.where(onehot, within + cnt_scr[...], 0.0), axis=0, keepdims=True) - 1.0
    rank_ref[:, rows] = before.astype(jnp.int32)
    cnt_scr[...] = cnt_scr[...] + within[:, MRG_SUB - 1:MRG_SUB]


def _merge(l, ya, yb, z, xres, mod, norm2_g, w_pa, w_pb, w_o, wr_t, br_col):
    n_t = N_TOK // MRG_TM
    tri = (jnp.arange(MRG_SUB)[:, None] <= jnp.arange(MRG_SUB)[None, :]).astype(BF16)
    zspec = lambda k: pl.BlockSpec((MRG_TM, HALF_D), lambda i: (i, k))
    once = pl.Buffered(1)
    lay = lambda *shape: pl.BlockSpec((None,) + shape, lambda i: (l,) + tuple(0 for _ in shape),
                                      pipeline_mode=once)
    const = lambda *shape: pl.BlockSpec(shape, lambda i: tuple(0 for _ in shape), pipeline_mode=once)
    return pl.pallas_call(
        _merge_kernel,
        grid=(n_t,),
        in_specs=[
            pl.BlockSpec((MRG_TM, SSM_WIDTH), lambda i: (i, 0)),
            pl.BlockSpec((MRG_TM, GM_WIDTH), lambda i: (i, 0)),
            zspec(2), zspec(3), zspec(4), zspec(5),
            pl.BlockSpec((MRG_TM, D_MODEL), lambda i: (i, 0)),
            _mod_spec(l, 2), _mod_spec(l, 4), _mod_spec(l, 3),
            lay(1, D_MODEL),
            lay(SSM_WIDTH, D_MODEL),
            lay(GM_WIDTH, D_MODEL),
            lay(D_MODEL, D_MODEL),
            const(2 * N_EXPERTS, D_MODEL),
            const(N_EXPERTS, 1),
            const(MRG_SUB, MRG_SUB),
        ],
        out_specs=[
            pl.BlockSpec((MRG_TM, D_MODEL), lambda i: (i, 0)),
            pl.BlockSpec((MRG_TM, REC_W), lambda i: (i, 0)),
            pl.BlockSpec((None, 1, MRG_TM), lambda i: (i, 0, 0)),
            pl.BlockSpec((None, 1, MRG_TM), lambda i: (i, 0, 0)),
            pl.BlockSpec((N_EXPERT_GROUPS, LANES), lambda i: (0, 0)),
        ],
        out_shape=[
            jax.ShapeDtypeStruct((N_TOK, D_MODEL), F32),
            jax.ShapeDtypeStruct((N_TOK, REC_W), jnp.uint32),
            jax.ShapeDtypeStruct((n_t, 1, MRG_TM), jnp.int32),
            jax.ShapeDtypeStruct((n_t, 1, MRG_TM), jnp.int32),
            jax.ShapeDtypeStruct((N_EXPERT_GROUPS, LANES), F32),
        ],
        scratch_shapes=[pltpu.VMEM((N_EXPERT_GROUPS, 1), F32)],
        compiler_params=_cparams(("arbitrary",)),
        name="merge_router",
    )(ya, yb, z, z, z, z, xres, mod, mod, mod, norm2_g.reshape(DEPTH, 1, D_MODEL),
      w_pa, w_pb, w_o, wr_t, br_col, tri)


DSP_TM = 1024


def _dispatch_kernel(pos_ref, pend_ref, rec_ref, out_ref, zero_scr, sem):
    step = pl.program_id(0)

    @pl.when(step == 0)
    def _():
        zero_scr[...] = jnp.zeros(zero_scr.shape, jnp.uint32)
        for g in range(N_EXPERT_GROUPS):
            start = pl.multiple_of(jnp.maximum(pend_ref[g] - MOE_BLK, 0), MOE_BLK)
            fill = pltpu.make_async_copy(zero_scr, out_ref.at[pl.ds(start, MOE_BLK)], sem)
            fill.start()
            fill.wait()
        for blk in range(N_TOK // MOE_BLK, MOE_NBLK):
            @pl.when(blk * MOE_BLK >= pend_ref[N_EXPERT_GROUPS - 1])
            def _():
                fill = pltpu.make_async_copy(zero_scr, out_ref.at[pl.ds(blk * MOE_BLK, MOE_BLK)], sem)
                fill.start()
                fill.wait()

    base = step * DSP_TM
    for r in range(DSP_TM):
        pltpu.make_async_copy(rec_ref.at[pl.ds(r, 1)], out_ref.at[pl.ds(pos_ref[base + r], 1)], sem).start()
    pltpu.make_async_copy(rec_ref, out_ref.at[pl.ds(0, DSP_TM)], sem).wait()


def _dispatch(pos, pend, rec):
    grid_spec = pltpu.PrefetchScalarGridSpec(
        num_scalar_prefetch=2,
        grid=(N_TOK // DSP_TM,),
        in_specs=[pl.BlockSpec((DSP_TM, REC_W), lambda i, pos, pend: (i, 0))],
        out_specs=pl.BlockSpec(memory_space=pl.ANY),
        scratch_shapes=[pltpu.VMEM((MOE_BLK, REC_W), jnp.uint32), pltpu.SemaphoreType.DMA(())],
    )
    return pl.pallas_call(
        _dispatch_kernel,
        grid_spec=grid_spec,
        out_shape=jax.ShapeDtypeStruct((MOE_SLOTS, REC_W), jnp.uint32),
        compiler_params=_cparams(("arbitrary",)),
        name="moe_dispatch",
    )(pos, pend, rec)


UP_EXPERTS = 2
MOE_SUB = 256
UP_CH = 256


def _up_inputs(rec_ref, rows, first_expert):
    packed = rec_ref[rows, 0:HALF_D]
    x_lo = lax.bitcast_convert_type(packed & jnp.uint32(0xFFFF0000), F32).astype(BF16)
    x_hi = lax.bitcast_convert_type(packed << 16, F32).astype(BF16)
    cw = lax.bitcast_convert_type(rec_ref[rows, HALF_D:REC_W], F32)
    lane = lax.broadcasted_iota(jnp.int32, cw.shape, 1)
    w_rows = [jnp.sum(jnp.where(lane == first_expert + e, cw, 0.0), axis=1, keepdims=True)
              for e in range(UP_EXPERTS)]
    return x_lo, x_hi, w_rows


def _up_hidden(x_lo, x_hi, w_row, wg, wu):
    gate = (jnp.dot(x_lo, wg[0:HALF_D, :], preferred_element_type=F32)
            + jnp.dot(x_hi, wg[HALF_D:D_MODEL, :], preferred_element_type=F32))
    up = (jnp.dot(x_lo, wu[0:HALF_D, :], preferred_element_type=F32)
          + jnp.dot(x_hi, wu[HALF_D:D_MODEL, :], preferred_element_type=F32))
    return (gate * jax.nn.sigmoid(gate) * up * w_row).astype(BF16)


def _expert_up_kernel(gid_ref, fill_ref, last_ref, rec_ref, wg_ref, wu_ref, h_ref, wg_scr, wu_scr):
    del last_ref
    first_expert = pl.program_id(0) * UP_EXPERTS
    b = pl.program_id(1)
    prev = gid_ref[jnp.maximum(b - 1, 0)]
    fresh = (b == 0) | (gid_ref[b] != prev)
    passes = [pl.ds(sub * MOE_SUB, MOE_SUB) for sub in range(MOE_BLK // MOE_SUB)]

    @pl.when(fresh)
    def _():
        inputs = [_up_inputs(rec_ref, rows, first_expert) for rows in passes]
        for e in range(UP_EXPERTS):
            for c in range(EXPERT_FF // UP_CH):
                cols = pl.ds(c * UP_CH, UP_CH)
                wg = wg_ref[e, :, cols].astype(BF16)
                wu = wu_ref[e, :, cols].astype(BF16)
                wg_scr[e, :, cols] = wg
                wu_scr[e, :, cols] = wu
                for rows, (x_lo, x_hi, w_rows) in zip(passes, inputs):
                    h_ref[rows, pl.ds(e * EXPERT_FF + c * UP_CH, UP_CH)] = _up_hidden(
                        x_lo, x_hi, w_rows[e], wg, wu)

    @pl.when(jnp.logical_not(fresh))
    def _():
        for sub, rows in enumerate(passes):
            @pl.when(fill_ref[b] > sub * MOE_SUB)
            def _():
                x_lo, x_hi, w_rows = _up_inputs(rec_ref, rows, first_expert)
                for e in range(UP_EXPERTS):
                    h_ref[rows, pl.ds(e * EXPERT_FF, EXPERT_FF)] = _up_hidden(
                        x_lo, x_hi, w_rows[e], wg_scr[e], wu_scr[e])

            @pl.when(fill_ref[b] <= sub * MOE_SUB)
            def _():
                h_ref[rows, :] = jnp.zeros((MOE_SUB, UP_EXPERTS * EXPERT_FF), BF16)


def _expert_up(l, blk_gid, blk_fill, blk_last, rec_sorted, e_gate, e_up):
    halves = EXPERTS_PER_GROUP // UP_EXPERTS
    wspec = pl.BlockSpec((None, UP_EXPERTS, D_MODEL, EXPERT_FF),
                         lambda h, b, gid, fill, last: (l, gid[b] * halves + h, 0, 0))
    grid_spec = pltpu.PrefetchScalarGridSpec(
        num_scalar_prefetch=3,
        grid=(halves, MOE_NBLK),
        in_specs=[
            pl.BlockSpec((MOE_BLK, REC_W), lambda h, b, gid, fill, last: (jnp.minimum(b, last[0]), 0)),
            wspec, wspec,
        ],
        out_specs=pl.BlockSpec((MOE_BLK, UP_EXPERTS * EXPERT_FF), lambda h, b, gid, fill, last: (b, h)),
        scratch_shapes=[pltpu.VMEM((UP_EXPERTS, D_MODEL, EXPERT_FF), BF16),
                        pltpu.VMEM((UP_EXPERTS, D_MODEL, EXPERT_FF), BF16)],
    )
    return pl.pallas_call(
        _expert_up_kernel,
        grid_spec=grid_spec,
        out_shape=jax.ShapeDtypeStruct((MOE_SLOTS, GROUP_FF), BF16),
        compiler_params=_cparams(("arbitrary", "arbitrary")),
        name="expert_up",
    )(blk_gid, blk_fill, blk_last, rec_sorted, e_gate, e_up)


DOWN_CH = 256


def _expert_down_kernel(gid_ref, fill_ref, h_ref, wd_ref, y_ref, wd_scr):
    b = pl.program_id(0)
    prev = gid_ref[jnp.maximum(b - 1, 0)]
    fresh = (b == 0) | (gid_ref[b] != prev)

    passes = [pl.ds(sub * MOE_SUB, MOE_SUB) for sub in range(MOE_BLK // MOE_SUB)]

    @pl.when(fresh)
    def _():
        for c in range(HALF_D // DOWN_CH):
            cols = pl.ds(c * DOWN_CH, DOWN_CH)
            pair = pl.ds(HALF_D + c * DOWN_CH, DOWN_CH)
            w_a = wd_ref[:, cols].astype(BF16)
            w_b = wd_ref[:, pair].astype(BF16)
            wd_scr[:, cols] = w_a
            wd_scr[:, pair] = w_b
            for rows in passes:
                hid = h_ref[rows, :]
                y_ref[rows, cols] = _pack_pairs(jnp.dot(hid, w_a, preferred_element_type=F32),
                                                jnp.dot(hid, w_b, preferred_element_type=F32))

    @pl.when(jnp.logical_not(fresh))
    def _():
        for sub, rows in enumerate(passes):
            @pl.when(fill_ref[b] > sub * MOE_SUB)
            def _():
                hid = h_ref[rows, :]
                for c in range(HALF_D // DOWN_CH):
                    cols = pl.ds(c * DOWN_CH, DOWN_CH)
                    pair = pl.ds(HALF_D + c * DOWN_CH, DOWN_CH)
                    y_ref[rows, cols] = _pack_pairs(jnp.dot(hid, wd_scr[:, cols], preferred_element_type=F32),
                                                    jnp.dot(hid, wd_scr[:, pair], preferred_element_type=F32))

            @pl.when(fill_ref[b] <= sub * MOE_SUB)
            def _():
                y_ref[rows, :] = jnp.zeros((MOE_SUB, HALF_D), jnp.uint32)


def _expert_down(l, blk_gid, blk_fill, h_sorted, e_down_grouped):
    grid_spec = pltpu.PrefetchScalarGridSpec(
        num_scalar_prefetch=2,
        grid=(MOE_NBLK,),
        in_specs=[
            pl.BlockSpec((MOE_BLK, GROUP_FF), lambda b, gid, fill: (b, 0)),
            pl.BlockSpec((None, None, GROUP_FF, D_MODEL), lambda b, gid, fill: (l, gid[b], 0, 0)),
        ],
        out_specs=pl.BlockSpec((MOE_BLK, HALF_D), lambda b, gid, fill: (b, 0)),
        scratch_shapes=[pltpu.VMEM((GROUP_FF, D_MODEL), BF16)],
    )
    return pl.pallas_call(
        _expert_down_kernel,
        grid_spec=grid_spec,
        out_shape=jax.ShapeDtypeStruct((MOE_SLOTS, HALF_D), jnp.uint32),
        compiler_params=_cparams(("arbitrary",)),
        name="expert_down",
    )(blk_gid, blk_fill, h_sorted, e_down_grouped)


def _moe(l, rec, gid, rank, counts, e_gate, e_up, e_down_grouped):
    padded = (counts + MOE_BLK - 1) // MOE_BLK * MOE_BLK
    pend = jnp.cumsum(padded)
    pstart = pend - padded
    pos = (pstart[gid] + rank).astype(jnp.int32)
    blk_start = jnp.arange(MOE_NBLK, dtype=jnp.int32) * MOE_BLK
    blk_gid = jnp.minimum(jnp.sum((blk_start[:, None] >= pend[None, :]).astype(jnp.int32), axis=1),
                          N_EXPERT_GROUPS - 1)
    blk_fill = jnp.clip(pstart[blk_gid] + counts[blk_gid] - blk_start, 0, MOE_BLK)
    blk_fill = jnp.where(blk_start < pend[-1], blk_fill, 0).astype(jnp.int32)
    blk_last = (pend[-1:] // MOE_BLK - 1).astype(jnp.int32)
    rec_sorted = _dispatch(pos, pend.astype(jnp.int32), rec)
    hid = _expert_up(l, blk_gid, blk_fill, blk_last, rec_sorted, e_gate, e_up)
    y_sorted = _expert_down(l, blk_gid, blk_fill, hid, e_down_grouped)
    return y_sorted[pos]


FIN_TM = 512
FIN_PROMPT_TILES = N_PROMPT // FIN_TM


def _final_kernel(x_ref, y_ref, g2_ref, fg_ref, op_ref, os_ref):
    i = pl.program_id(0)
    seg = i // (SEG_TOK // FIN_TM)
    x = x_ref[...] + g2_ref[pl.ds(seg, 1), :] * _unpack_pairs(y_ref[...])
    ms = jnp.mean(x * x, axis=-1, keepdims=True)
    out = x * lax.rsqrt(ms + EPS) * fg_ref[...]

    @pl.when(i < FIN_PROMPT_TILES)
    def _():
        op_ref[...] = out

    @pl.when(i >= FIN_PROMPT_TILES)
    def _():
        os_ref[...] = out


def _final_norm(xmid, moe_y, mod, final_g):
    return pl.pallas_call(
        _final_kernel,
        grid=(N_TOK // FIN_TM,),
        in_specs=[
            pl.BlockSpec((FIN_TM, D_MODEL), lambda i: (i, 0)),
            pl.BlockSpec((FIN_TM, HALF_D), lambda i: (i, 0)),
            _mod_spec(DEPTH - 1, 5),
            pl.BlockSpec((1, D_MODEL), lambda i: (0, 0)),
        ],
        out_specs=[
            pl.BlockSpec((FIN_TM, D_MODEL), lambda i: (jnp.minimum(i, FIN_PROMPT_TILES - 1), 0)),
            pl.BlockSpec((FIN_TM, D_MODEL), lambda i: (jnp.maximum(i - FIN_PROMPT_TILES, 0), 0)),
        ],
        out_shape=[
            jax.ShapeDtypeStruct((N_PROMPT, D_MODEL), F32),
            jax.ShapeDtypeStruct((N_SAMPLE, D_MODEL), F32),
        ],
        compiler_params=_cparams(("arbitrary",)),
        name="final_norm",
    )(xmid, moe_y, mod, final_g.reshape(1, D_MODEL))


def _grid_pos_embed(rows):
    quarter = D_MODEL // 4
    freqs = 1.0 / (POS_BASE ** (jnp.arange(quarter, dtype=F32) / quarter))
    er = jnp.arange(rows, dtype=F32)[:, None] * freqs
    ec = jnp.arange(GRID_W, dtype=F32)[:, None] * freqs
    row_emb = jnp.concatenate([jnp.sin(er), jnp.cos(er)], axis=-1)
    col_emb = jnp.concatenate([jnp.sin(ec), jnp.cos(ec)], axis=-1)
    return jnp.stack([row_emb, col_emb])


def kernel(x_prompt, x_sample, state_ssm_re, state_ssm_im, c, c_ctx, norm1_g, norm2_g, w_mod, b_mod,
           w_in, ssm_lam_re, ssm_lam_im, ssm_log_step, ssm_b_re, ssm_b_im, ssm_c_re, ssm_c_im, ssm_d,
           w_glu, b_glu, gm_ln_g, gm_w_s, gm_b_s, w_pa, w_pb, w_o, w_router, b_router,
           e_gate, e_up, e_down, final_g):
    cvec = jnp.concatenate([c_ctx[None], c, jnp.zeros((MOD_ROWS - 1 - DEC_BATCH, D_MODEL), F32)], axis=0)
    mod = _modulation(cvec, w_mod, b_mod)

    perm = (jnp.arange(N_EXPERT_GROUPS)[None, :] * EXPERTS_PER_GROUP
            + jnp.arange(EXPERTS_PER_GROUP)[:, None]).reshape(N_EXPERTS)
    wr = w_router.astype(F32).T[perm]
    wr_hi = wr.astype(BF16)
    wr_lo = (wr - wr_hi.astype(F32)).astype(BF16)
    wr_t = jnp.concatenate([wr_hi, wr_lo], axis=0)
    br_col = b_router.astype(F32)[perm][:, None]

    w_glu_b, w_s_b = w_glu.astype(BF16), gm_w_s.astype(BF16)
    w_pa_b, w_pb_b, w_o_b = w_pa.astype(BF16), w_pb.astype(BF16), w_o.astype(BF16)
    b_s_full = jnp.repeat(jnp.transpose(gm_b_s.astype(F32), (0, 2, 1)), GM_GROUP_DIM, axis=2)
    e_down_grouped = e_down.reshape(DEPTH, N_EXPERT_GROUPS, GROUP_FF, D_MODEL)
    w1, w2, a16 = _s5_prep(ssm_lam_re, ssm_lam_im, ssm_log_step, ssm_b_re, ssm_b_im, ssm_c_re, ssm_c_im)
    h0_lat = jnp.concatenate([state_ssm_re[:, :, 0], state_ssm_re[:, :, 1],
                              state_ssm_im[:, :, 0], state_ssm_im[:, :, 1]], axis=-1).astype(F32)
    h0 = jnp.concatenate([jnp.zeros((DEPTH, 1, SSM_GROUPS, STATE_W), F32),
                          jnp.transpose(h0_lat, (1, 0, 2, 3))], axis=1)

    d_lanes = jnp.tile(ssm_d.astype(F32).reshape(DEPTH, SSM_GROUPS, SSM_GROUP), (1, 1, SCAN_T))

    inproj_in = (x_prompt.reshape(N_PROMPT, D_MODEL), x_sample.reshape(N_SAMPLE, D_MODEL),
                 _grid_pos_embed(DEC_SEQ // GRID_W))

    new_re, new_im = [], []
    xmid = moe_y = None
    for l in range(DEPTH):
        xres, u, z = _inproj(l, *inproj_in, mod, norm1_g, gm_ln_g, w_in)
        ys, fs = _s5_scan(l, u, w1, w2, a16, h0, d_lanes)
        fin = fs[0]
        p = SSM_STATE
        new_re.append(jnp.stack([fin[:, :, 0:p], fin[::-1, :, p:2 * p]], axis=1))
        new_im.append(jnp.stack([fin[:, :, 2 * p:3 * p], fin[::-1, :, 3 * p:4 * p]], axis=1))

        ya, yb = _mix(l, ys, z, w_glu_b, b_glu.astype(F32), w_s_b, b_s_full)
        xmid, rec, gid, rank, cnt = _merge(l, ya, yb, z, xres, mod, norm2_g, w_pa_b, w_pb_b, w_o_b,
                                           wr_t, br_col)
        moe_y = _moe(l, rec, gid.reshape(N_TOK), rank.reshape(N_TOK), cnt[:, 0].astype(jnp.int32),
                     e_gate, e_up, e_down_grouped)
        inproj_in = (xmid, moe_y, mod)

    y_prompt, y_sample = _final_norm(xmid, moe_y, mod, final_g)
    new_state_re = jnp.stack(new_re, axis=1).astype(x_prompt.dtype)
    new_state_im = jnp.stack(new_im, axis=1).astype(x_prompt.dtype)
    return (y_prompt.reshape(BATCH, SEQ, D_MODEL), y_sample.reshape(DEC_BATCH, DEC_SEQ, D_MODEL),
            new_state_re, new_state_im)
```

```python
import functools

import jax
import jax.numpy as jnp
from jax import lax
from jax.experimental import pallas as pl
from jax.experimental.pallas import tpu as pltpu

F32 = jnp.float32
BF16 = jnp.bfloat16

D_MODEL = 2048
BATCH = 16
SEQ = 256
DEPTH = 2
DEC_BATCH = 2
DEC_SEQ = 4096
GRID_W = 64
POS_BASE = 10000.0
EPS = 1e-6
SSM_WIDTH = D_MODEL // 2
SSM_GROUP = 16
SSM_GROUPS = SSM_WIDTH // SSM_GROUP
SSM_STATE = 64
GM_WIDTH = D_MODEL // 2
GM_CHUNK = 128
GM_GROUPS = 8
GM_GROUP_DIM = GM_WIDTH // GM_GROUPS
IN_WIDTH = SSM_WIDTH + 2 * GM_WIDTH + 2 * D_MODEL
N_EXPERTS = 32
N_EXPERT_GROUPS = 8
EXPERTS_PER_GROUP = N_EXPERTS // N_EXPERT_GROUPS
EXPERT_FF = D_MODEL // 4
N_MOD = 6

N_PROMPT = BATCH * SEQ
N_SAMPLE = DEC_BATCH * DEC_SEQ
N_TOK = N_PROMPT + N_SAMPLE
SEG_TOK = 4096
MOD_ROWS = 8
LANES = 128
SUBLANES = 8

SCAN_T = 16
SCAN_W = SCAN_T * SSM_GROUP
SCAN_ROWS = N_TOK // SCAN_T
SCAN_BLK = 256
SCAN_TOK = SCAN_BLK * SCAN_T
N_SCAN_BLK = SCAN_ROWS // SCAN_BLK
G_OCT = SUBLANES
SCAN_G = 2 * SUBLANES
SCAN_GW = SCAN_G * SSM_GROUP
STATE_W = 4 * SSM_STATE
HALF_W = 2 * SSM_STATE

MOE_BLK = 512
MOE_SLOTS = N_TOK + N_EXPERT_GROUPS * MOE_BLK
MOE_NBLK = MOE_SLOTS // MOE_BLK
GROUP_FF = EXPERTS_PER_GROUP * EXPERT_FF

VMEM_LIMIT = 56 * 1024 * 1024


def _cparams(sem):
    return pltpu.CompilerParams(dimension_semantics=sem, vmem_limit_bytes=VMEM_LIMIT)


MOD_TN = 1024


def _mod_kernel(c_ref, w_ref, b_ref, o_ref):
    c = c_ref[...]
    s = c * jax.nn.sigmoid(c)
    s_hi = s.astype(BF16)
    s_lo = (s - s_hi.astype(F32)).astype(BF16)
    w = w_ref[...]
    w_hi = w.astype(BF16)
    w_lo = (w - w_hi.astype(F32)).astype(BF16)
    both = jnp.dot(jnp.concatenate([s_hi, s_lo], axis=0), w_hi, preferred_element_type=F32)
    cross = jnp.dot(s_hi, w_lo, preferred_element_type=F32)
    o_ref[...] = both[0:MOD_ROWS] + both[MOD_ROWS:2 * MOD_ROWS] + cross + b_ref[...]


def _modulation(cvec, w_mod, b_mod):
    width = N_MOD * D_MODEL
    return pl.pallas_call(
        _mod_kernel,
        grid=(DEPTH, width // MOD_TN),
        in_specs=[
            pl.BlockSpec((MOD_ROWS, D_MODEL), lambda l, n: (0, 0)),
            pl.BlockSpec((None, D_MODEL, MOD_TN), lambda l, n: (l, 0, n)),
            pl.BlockSpec((None, 1, MOD_TN), lambda l, n: (l, 0, n)),
        ],
        out_specs=pl.BlockSpec((None, MOD_ROWS, MOD_TN), lambda l, n: (l, 0, n)),
        out_shape=jax.ShapeDtypeStruct((DEPTH, MOD_ROWS, width), F32),
        compiler_params=_cparams(("arbitrary", "arbitrary")),
        name="adaln_mod",
    )(cvec, w_mod, b_mod.reshape(DEPTH, 1, width))


def _pack_pairs(a, b):
    hi = lax.bitcast_convert_type(a.astype(BF16).astype(F32), jnp.uint32)
    lo = lax.bitcast_convert_type(b.astype(BF16).astype(F32), jnp.uint32)
    return hi | (lo >> 16)


def _unpack_pairs(packed):
    hi = lax.bitcast_convert_type(packed & jnp.uint32(0xFFFF0000), F32)
    lo = lax.bitcast_convert_type(packed << 16, F32)
    return jnp.concatenate([hi, lo], axis=-1)


def _mod_spec(l, k):
    return pl.BlockSpec((None, MOD_ROWS, D_MODEL), lambda i: (l, 0, k))


INP_TM = 256
INP_CH = 256
INP_PROMPT_TILES = N_PROMPT // INP_TM
Z_WIDTH = IN_WIDTH - SSM_WIDTH
INP_VMEM_LIMIT = 60 * 1024 * 1024


def _inproj_kernel(*refs, first, layer):
    if first:
        xa_ref, xb_ref, add_ref = refs[:3]
    else:
        xa_ref, add_ref, gain_ref = refs[:3]
    sc_ref, sh_ref, g_ref, ln_ref, w_hbm, xres_ref, u_ref, z_ref, h_scr, v_scr, w_ref, stage, sem = refs[3:]
    i = pl.program_id(0)
    seg = i // (SEG_TOK // INP_TM)

    @pl.when(i == 0)
    def _():
        n_chunks = IN_WIDTH // INP_CH

        def chunk(c):
            return pltpu.make_async_copy(w_hbm.at[layer, :, pl.ds(c * INP_CH, INP_CH)],
                                         stage.at[c % 2], sem.at[c % 2])

        chunk(0).start()
        for c in range(n_chunks):
            if c + 1 < n_chunks:
                chunk(c + 1).start()
            chunk(c).wait()
            w_ref[:, pl.ds(c * INP_CH, INP_CH)] = stage[c % 2].astype(BF16)

    if first:
        latent = i >= INP_PROMPT_TILES
        rows_per_tile = INP_TM // GRID_W
        row0 = (i % (DEC_SEQ // INP_TM)) * rows_per_tile
        row_part = jnp.concatenate(
            [jnp.broadcast_to(add_ref[0, pl.ds(row0 + q, 1), :], (GRID_W, D_MODEL // 2))
             for q in range(rows_per_tile)], axis=0)
        col_part = jnp.concatenate([add_ref[1]] * rows_per_tile, axis=0)
        x = jnp.where(latent, xb_ref[...] + jnp.concatenate([row_part, col_part], axis=1), xa_ref[...])
    else:
        x = xa_ref[...] + gain_ref[pl.ds(seg, 1), :] * _unpack_pairs(add_ref[...])
    xres_ref[...] = x
    ms = jnp.mean(x * x, axis=-1, keepdims=True)
    y = x * lax.rsqrt(ms + EPS) * g_ref[...]
    h = y * (1.0 + sc_ref[pl.ds(seg, 1), :]) + sh_ref[pl.ds(seg, 1), :]
    h_scr[...] = h.astype(BF16)

    def proj(col):
        return jnp.dot(h_scr[...], w_ref[:, pl.ds(col, INP_CH)], preferred_element_type=F32)

    n_ch = SSM_WIDTH // INP_CH
    for c in range(n_ch):
        acc = proj(c * INP_CH)
        for k in range(INP_CH // LANES):
            u_ref[c * (INP_CH // LANES) + k] = acc[:, k * LANES:(k + 1) * LANES]
    for c in range(n_ch):
        z_ref[:, pl.ds(c * INP_CH, INP_CH)] = jax.nn.gelu(proj(SSM_WIDTH + c * INP_CH)).astype(BF16)
    row_sum = jnp.zeros((INP_TM, 1), F32)
    for c in range(n_ch):
        v = jax.nn.gelu(proj(SSM_WIDTH + GM_WIDTH + c * INP_CH))
        v_scr[:, pl.ds(c * INP_CH, INP_CH)] = v
        row_sum = row_sum + jnp.sum(v, axis=-1, keepdims=True)
    mu = row_sum * (1.0 / GM_WIDTH)
    dev = v_scr[...] - mu
    var = jnp.mean(jnp.square(dev), axis=-1, keepdims=True)
    z_ref[:, pl.ds(GM_WIDTH, GM_WIDTH)] = (dev * lax.rsqrt(var + EPS) * ln_ref[...]).astype(BF16)
    gates = SSM_WIDTH + 2 * GM_WIDTH
    for c in range(2 * D_MODEL // INP_CH):
        z_ref[:, pl.ds(2 * GM_WIDTH + c * INP_CH, INP_CH)] = jax.nn.sigmoid(
            proj(gates + c * INP_CH)).astype(BF16)


def _inproj(l, xa, xb_or_add, add_or_gain, mod, norm1_g, gm_ln_g, w_in):
    first = l == 0
    row_tile = lambda m: pl.BlockSpec((INP_TM, D_MODEL), m)
    if first:
        lead = [row_tile(lambda i: (jnp.minimum(i, INP_PROMPT_TILES - 1), 0)),
                row_tile(lambda i: (jnp.maximum(i - INP_PROMPT_TILES, 0), 0)),
                pl.BlockSpec((2, DEC_SEQ // GRID_W, D_MODEL // 2), lambda i: (0, 0, 0))]
    else:
        lead = [row_tile(lambda i: (i, 0)), pl.BlockSpec((INP_TM, HALF_D), lambda i: (i, 0)),
                _mod_spec(l - 1, 5)]
    vec = lambda w: pl.BlockSpec((None, 1, w), lambda i: (l, 0, 0))
    return pl.pallas_call(
        functools.partial(_inproj_kernel, first=first, layer=l),
        grid=(N_TOK // INP_TM,),
        in_specs=lead + [
            _mod_spec(l, 1), _mod_spec(l, 0),
            vec(D_MODEL), vec(GM_WIDTH),
            pl.BlockSpec(memory_space=pl.ANY),
        ],
        out_specs=[
            pl.BlockSpec((INP_TM, D_MODEL), lambda i: (i, 0)),
            pl.BlockSpec((SSM_WIDTH // LANES, INP_TM, LANES), lambda i: (0, i, 0)),
            pl.BlockSpec((INP_TM, Z_WIDTH), lambda i: (i, 0)),
        ],
        out_shape=[
            jax.ShapeDtypeStruct((N_TOK, D_MODEL), F32),
            jax.ShapeDtypeStruct((SSM_WIDTH // LANES, N_TOK, LANES), F32),
            jax.ShapeDtypeStruct((N_TOK, Z_WIDTH), BF16),
        ],
        scratch_shapes=[pltpu.VMEM((INP_TM, D_MODEL), BF16), pltpu.VMEM((INP_TM, GM_WIDTH), F32),
                        pltpu.VMEM((D_MODEL, IN_WIDTH), BF16), pltpu.VMEM((2, D_MODEL, INP_CH), F32),
                        pltpu.SemaphoreType.DMA((2,))],
        compiler_params=pltpu.CompilerParams(dimension_semantics=("arbitrary",),
                                             vmem_limit_bytes=INP_VMEM_LIMIT),
        name="in_proj",
    )(xa, xb_or_add, add_or_gain, mod, mod, norm1_g.reshape(DEPTH, 1, D_MODEL),
      gm_ln_g.reshape(DEPTH, 1, GM_WIDTH), w_in)


PK_BRE, PK_BIM, PK_CRE, PK_CIM = 0, 16, 32, 48
PK_LR, PK_LI = 64, 65


def _split_bf16(x):
    hi = x.astype(BF16)
    return hi, (x - hi.astype(F32)).astype(BF16)


def _dot_split(a, b):
    a_hi, a_lo = _split_bf16(a)
    b_hi, b_lo = _split_bf16(b)
    m = a.shape[0]
    both = jnp.dot(jnp.concatenate([a_hi, a_lo], axis=0), b_hi, preferred_element_type=F32)
    return both[0:m] + both[m:2 * m] + jnp.dot(a_hi, b_lo, preferred_element_type=F32)


POW_ROWS = 24


def _prep_kernel(pk_ref, row_ref, pow_ref, tile_ref, w1_ref, w2_ref, a_ref):
    p = SSM_STATE
    k_sub = jnp.minimum(lax.broadcasted_iota(jnp.int32, (POW_ROWS, LANES), 0), SCAN_T).astype(F32)
    lane = lax.broadcasted_iota(jnp.int32, (p, LANES), 1)
    col = lax.broadcasted_iota(jnp.int32, (SSM_GROUP, SCAN_W), 1)

    def spread_pow(x_r, x_i, which):
        parts = jnp.concatenate(_split_bf16(x_r) + _split_bf16(x_i), axis=0)
        out = jnp.dot(parts, pow_ref[which], preferred_element_type=F32)
        return out[0:p] + out[p:2 * p], out[2 * p:3 * p] + out[3 * p:4 * p]

    def spread_tiles(x):
        out = jnp.dot(jnp.concatenate(_split_bf16(x), axis=0), tile_ref[...], preferred_element_type=F32)
        out = out[0:p] + out[p:2 * p]
        return [out[:, n * SCAN_W:(n + 1) * SCAN_W] for n in range(4)]

    def group(g, _):
        rows = row_ref[g]
        grow_r = rows[0:1] * rows[2:3]
        grow_i = rows[1:2] * rows[2:3]
        mag = jnp.exp(grow_r * k_sub)
        ang = grow_i * k_sub
        unused = jnp.zeros((LANES - POW_ROWS, LANES), F32)
        pw_t_r = jnp.concatenate([mag * jnp.cos(ang), unused], axis=0).T
        pw_t_i = jnp.concatenate([mag * jnp.sin(ang), unused], axis=0).T
        per_dir = []
        for d in range(2):
            pk = pk_ref[d, g]
            lr = pk[:, PK_LR:PK_LR + 1]
            li = pk[:, PK_LI:PK_LI + 1]
            p_r = pw_t_r[d * p:(d + 1) * p]
            p_i = pw_t_i[d * p:(d + 1) * p]
            a_r = p_r[:, 1:2]
            a_i = p_i[:, 1:2]
            den = lr * lr + li * li
            q_r = ((a_r - 1.0) * lr + a_i * li) / den
            q_i = (a_i * lr - (a_r - 1.0) * li) / den
            per_dir.append((pk, p_r, p_i, q_r, q_i))

        w1_rows, lag, carry = [], [], []
        for d in range(2):
            pk, p_r, p_i, q_r, q_i = per_dir[d]
            b_r, b_i, c_r, c_i = spread_tiles(pk)
            bb_r = q_r * b_r - q_i * b_i
            bb_i = q_r * b_i + q_i * b_r
            pw_r, pw_i = spread_pow(p_r, p_i, 1 if d == 0 else 0)
            w1_rows.append((pw_r * bb_r - pw_i * bb_i, pw_r * bb_i + pw_i * bb_r))
            pl_r, pl_i = spread_pow(p_r, p_i, 0 if d == 0 else 1)
            cl_r = c_r * pl_r - c_i * pl_i
            cl_i = c_r * pl_i + c_i * pl_r
            pk_im = pltpu.roll(pk, LANES - (PK_BIM - PK_BRE), 1)
            bt_r = (q_r * pk - q_i * pk_im).T[0:SSM_GROUP, :]
            bt_i = (q_r * pk_im + q_i * pk).T[0:SSM_GROUP, :]
            lag.append(_dot_split(bt_r, cl_r) - _dot_split(bt_i, cl_i))
            pc_r, pc_i = spread_pow(p_r, p_i, 2 if d == 0 else 3)
            carry.append((c_r * pc_r - c_i * pc_i, -(c_r * pc_i + c_i * pc_r)))

        (f_re, f_im), (b_re, b_im) = w1_rows
        w1_ref[g] = jnp.concatenate([f_re, b_re, f_im, b_im], axis=0).T.astype(BF16)

        for s in range(SCAN_T):
            fwd = lag[0] if s == 0 else pltpu.roll(lag[0], SSM_GROUP * s, 1)
            fwd = jnp.where(col >= SSM_GROUP * s, fwd, 0.0)
            shift_b = SSM_GROUP * (SCAN_T - 1 - s)
            bwd = lag[1] if shift_b == 0 else pltpu.roll(lag[1], SCAN_W - shift_b, 1)
            bwd = jnp.where(col < SSM_GROUP * (s + 1), bwd, 0.0)
            w2_ref[g, pl.ds(SSM_GROUP * s, SSM_GROUP), :] = (fwd + bwd).astype(BF16)
        (x_re, x_im), (y_re, y_im) = carry
        for n, rows in enumerate((x_re, y_re, x_im, y_im)):
            w2_ref[g, pl.ds(SCAN_W + SSM_STATE * n, SSM_STATE), :] = rows.astype(BF16)

        cols = [per_dir[0][1], per_dir[1][1], per_dir[0][2], per_dir[1][2]]
        a_cols = jnp.zeros((SSM_STATE, LANES), F32)
        for n, c in enumerate(cols):
            a_cols = jnp.where(lane == n, c[:, SCAN_T:SCAN_T + 1], a_cols)
        a_ref[g] = a_cols
        return 0

    lax.fori_loop(0, G_OCT, group, 0, unroll=4)


def _s5_prep(lam_re, lam_im, log_step, b_re, b_im, c_re, c_im):
    shape = (DEPTH, 2, SSM_GROUPS, SSM_STATE)
    lr = lam_re.astype(F32)
    li = lam_im.astype(F32)
    dt = jnp.broadcast_to(jnp.exp(log_step.astype(F32))[..., None], shape)
    pk = jnp.concatenate([
        b_re.astype(F32), b_im.astype(F32),
        jnp.swapaxes(c_re.astype(F32), -1, -2), jnp.swapaxes(c_im.astype(F32), -1, -2),
        lr[..., None], li[..., None],
        jnp.zeros(shape + (LANES - PK_LI - 1,), F32)], axis=-1)
    both_dirs = lambda a: jnp.concatenate([a[:, 0], a[:, 1]], axis=-1)
    rows = jnp.stack([both_dirs(lr), both_dirs(li), both_dirs(dt)], axis=2)
    rows = jnp.concatenate([rows, jnp.zeros((DEPTH, SSM_GROUPS, SUBLANES - 3, LANES), F32)], axis=2)

    blk = jnp.arange(SCAN_W) // SSM_GROUP
    k = jnp.arange(LANES)[:, None]
    pows = [k == blk[None, :], k == (SCAN_T - 1 - blk)[None, :], k == (blk + 1)[None, :],
            k == (SCAN_T - blk)[None, :]]
    h = (jnp.arange(SCAN_W) % SSM_GROUP)[None, :]
    sel_pow = jnp.stack(pows).astype(BF16)
    sel_tile = jnp.concatenate([k == h + off for off in (PK_BRE, PK_BIM, PK_CRE, PK_CIM)],
                               axis=1).astype(BF16)

    n_oct = SSM_GROUPS // G_OCT
    w1, w2, a_cols = pl.pallas_call(
        _prep_kernel,
        grid=(DEPTH, n_oct),
        in_specs=[
            pl.BlockSpec((None, 2, G_OCT, SSM_STATE, LANES), lambda l, o: (l, 0, o, 0, 0)),
            pl.BlockSpec((None, G_OCT, SUBLANES, LANES), lambda l, o: (l, o, 0, 0)),
            pl.BlockSpec((4, LANES, SCAN_W), lambda l, o: (0, 0, 0)),
            pl.BlockSpec((LANES, 4 * SCAN_W), lambda l, o: (0, 0)),
        ],
        out_specs=[
            pl.BlockSpec((None, G_OCT, SCAN_W, STATE_W), lambda l, o: (l, o, 0, 0)),
            pl.BlockSpec((None, G_OCT, SCAN_W + STATE_W, SCAN_W), lambda l, o: (l, o, 0, 0)),
            pl.BlockSpec((None, G_OCT, SSM_STATE, LANES), lambda l, o: (l, o, 0, 0)),
        ],
        out_shape=[
            jax.ShapeDtypeStruct((DEPTH, SSM_GROUPS, SCAN_W, STATE_W), BF16),
            jax.ShapeDtypeStruct((DEPTH, SSM_GROUPS, SCAN_W + STATE_W, SCAN_W), BF16),
            jax.ShapeDtypeStruct((DEPTH, SSM_GROUPS, SSM_STATE, LANES), F32),
        ],
        compiler_params=_cparams(("arbitrary", "arbitrary")),
        name="s5_prep",
    )(pk, rows, sel_pow, sel_tile)
    a16 = jnp.swapaxes(a_cols[..., 0:4], -1, -2).reshape(DEPTH, SSM_GROUPS, STATE_W)
    return w1, w2, a16


def _s5_kernel(u_ref, w1_ref, w2_ref, a_ref, h0_ref, d_ref, y_ref, fs_ref,
               t_scr, ug_scr, vr_scr, vi_scr, cr_scr, ci_scr, fr_scr, fi_scr):
    blk = pl.program_id(1)
    seq_rows = jnp.where(blk == 0, SEQ // SCAN_T, DEC_SEQ // SCAN_T)

    for s in range(SCAN_T):
        for j in range(SCAN_GW // LANES):
            t_scr[s, pl.ds(j * LANES, LANES), :] = (
                u_ref[j, pl.ds(s, SCAN_BLK, stride=SCAN_T), :].astype(BF16).T)
    for g in range(SCAN_G):
        stacked = t_scr[:, pl.ds(g * SSM_GROUP, SSM_GROUP), :].reshape(SCAN_W, SCAN_BLK)
        ug_scr[g] = stacked.T

    for g in range(SCAN_G):
        v = jnp.dot(ug_scr[g], w1_ref[g], preferred_element_type=F32)
        vr_scr[pl.ds(g, SCAN_BLK, stride=SCAN_G), :] = v[:, 0:HALF_W]
        vi_scr[pl.ds(g, SCAN_BLK, stride=SCAN_G), :] = v[:, HALF_W:STATE_W]

    a_r = a_ref[:, 0:HALF_W]
    a_i = a_ref[:, HALF_W:STATE_W]
    h0_r = h0_ref[:, 0:HALF_W]
    h0_i = h0_ref[:, HALF_W:STATE_W]
    fwd_lanes = lax.broadcasted_iota(jnp.int32, (SCAN_G, HALF_W), 1) < SSM_STATE
    bwd_lanes = jnp.logical_not(fwd_lanes)

    def step(k, carry):
        s_r, s_i = carry
        rf = pl.ds(pl.multiple_of(k * SCAN_G, SCAN_G), SCAN_G)
        rb = pl.ds(pl.multiple_of((SCAN_BLK - 1 - k) * SCAN_G, SCAN_G), SCAN_G)
        restart = (k & (seq_rows - 1)) == 0
        s_r = jnp.where(restart, h0_r, s_r)
        s_i = jnp.where(restart, h0_i, s_i)
        pltpu.store(cr_scr.at[rf, :], s_r, mask=fwd_lanes)
        pltpu.store(cr_scr.at[rb, :], s_r, mask=bwd_lanes)
        pltpu.store(ci_scr.at[rf, :], s_i, mask=fwd_lanes)
        pltpu.store(ci_scr.at[rb, :], s_i, mask=bwd_lanes)
        v_r = jnp.where(fwd_lanes, vr_scr[rf, :], vr_scr[rb, :])
        v_i = jnp.where(fwd_lanes, vi_scr[rf, :], vi_scr[rb, :])
        n_r = a_r * s_r - a_i * s_i + v_r
        n_i = a_r * s_i + a_i * s_r + v_i
        fr_scr[rf, :] = n_r
        fi_scr[rf, :] = n_i
        return n_r, n_i

    zero = jnp.zeros((SCAN_G, HALF_W), F32)
    lax.fori_loop(0, SCAN_BLK, step, (zero, zero), unroll=4)

    for g in range(SCAN_G):
        c_r = cr_scr[pl.ds(g, SCAN_BLK, stride=SCAN_G), :].astype(BF16)
        c_i = ci_scr[pl.ds(g, SCAN_BLK, stride=SCAN_G), :].astype(BF16)
        y = jnp.dot(ug_scr[g], w2_ref[g, 0:SCAN_W, :], preferred_element_type=F32)
        y = y + jnp.dot(c_r, w2_ref[g, SCAN_W:SCAN_W + HALF_W, :], preferred_element_type=F32)
        y = y + jnp.dot(c_i, w2_ref[g, SCAN_W + HALF_W:SCAN_W + STATE_W, :], preferred_element_type=F32)
        y = y + d_ref[pl.ds(g, 1), :] * ug_scr[g].astype(F32)
        t_scr[:, pl.ds(g * SSM_GROUP, SSM_GROUP), :] = y.astype(BF16).T.reshape(SCAN_T, SSM_GROUP, SCAN_BLK)
    for s in range(SCAN_T):
        for j in range(SCAN_GW // LANES):
            y_ref[j, pl.ds(s, SCAN_BLK, stride=SCAN_T), :] = t_scr[s, pl.ds(j * LANES, LANES), :].T.astype(F32)

    rows_per_seq = SEQ // SCAN_T
    for q in range(SCAN_BLK // rows_per_seq):
        last = pl.ds((q * rows_per_seq + rows_per_seq - 1) * SCAN_G, SCAN_G)
        fs_ref[q, :, 0:HALF_W] = fr_scr[last, :]
        fs_ref[q, :, HALF_W:STATE_W] = fi_scr[last, :]


def _s5_scan(l, u, w1, w2, a16, h0, d_lanes):
    n_oct = SSM_GROUPS // SCAN_G
    n_fin = SCAN_BLK // (SEQ // SCAN_T)
    return pl.pallas_call(
        _s5_kernel,
        grid=(n_oct, N_SCAN_BLK),
        in_specs=[
            pl.BlockSpec((SCAN_GW // LANES, SCAN_TOK, LANES), lambda o, b: (o, b, 0)),
            pl.BlockSpec((None, SCAN_G, SCAN_W, STATE_W), lambda o, b: (l, o, 0, 0)),
            pl.BlockSpec((None, SCAN_G, SCAN_W + STATE_W, SCAN_W), lambda o, b: (l, o, 0, 0)),
            pl.BlockSpec((None, SCAN_G, STATE_W), lambda o, b: (l, o, 0)),
            pl.BlockSpec((None, None, SCAN_G, STATE_W), lambda o, b: (l, b, o, 0)),
            pl.BlockSpec((None, SCAN_G, SCAN_W), lambda o, b: (l, o, 0)),
        ],
        out_specs=[
            pl.BlockSpec((SCAN_GW // LANES, SCAN_TOK, LANES), lambda o, b: (o, b, 0)),
            pl.BlockSpec((None, n_fin, SCAN_G, STATE_W), lambda o, b: (b, 0, o, 0)),
        ],
        out_shape=[
            jax.ShapeDtypeStruct((SSM_WIDTH // LANES, N_TOK, LANES), F32),
            jax.ShapeDtypeStruct((N_SCAN_BLK, n_fin, SSM_GROUPS, STATE_W), F32),
        ],
        scratch_shapes=[
            pltpu.VMEM((SCAN_T, SCAN_GW, SCAN_BLK), BF16),
            pltpu.VMEM((SCAN_G, SCAN_BLK, SCAN_W), BF16),
        ] + [pltpu.VMEM((SCAN_BLK * SCAN_G, HALF_W), F32) for _ in range(6)],
        compiler_params=_cparams(("arbitrary", "arbitrary")),
        name="s5_scan",
    )(u, w1, w2, a16, h0, d_lanes)


MIX_TM = 1024


def _mix_kernel(ys_ref, gu_ref, vn_ref, wglu_ref, bglu_ref, ws_ref, bs_ref, ya_ref, yb_ref):
    y = jnp.concatenate([ys_ref[k] for k in range(SSM_WIDTH // LANES)], axis=1)
    y = jax.nn.gelu(y)
    gate = jnp.dot(y.astype(BF16), wglu_ref[...], preferred_element_type=F32) + bglu_ref[...]
    ya_ref[...] = (y * jax.nn.sigmoid(gate)).astype(BF16)
    for c in range(MIX_TM // GM_CHUNK):
        rows = pl.ds(c * GM_CHUNK, GM_CHUNK)
        for g in range(GM_GROUPS):
            cols = pl.ds(g * GM_GROUP_DIM, GM_GROUP_DIM)
            mixed = jnp.dot(ws_ref[g], vn_ref[rows, cols], preferred_element_type=F32) + bs_ref[:, cols]
            yb_ref[rows, cols] = (gu_ref[rows, cols].astype(F32) * mixed).astype(BF16)


def _mix(l, ys, z, w_glu_bf16, b_glu, w_s_bf16, b_s_full):
    tile = lambda k: pl.BlockSpec((MIX_TM, SSM_WIDTH), lambda i: (i, k))
    slab = pl.BlockSpec((SSM_WIDTH // LANES, MIX_TM, LANES), lambda i: (0, i, 0))
    lay = lambda *shape: pl.BlockSpec((None,) + shape, lambda i: (l,) + tuple(0 for _ in shape))
    return pl.pallas_call(
        _mix_kernel,
        grid=(N_TOK // MIX_TM,),
        in_specs=[
            slab, tile(0), tile(1),
            lay(SSM_WIDTH, SSM_WIDTH),
            lay(1, SSM_WIDTH),
            lay(GM_GROUPS, GM_CHUNK, GM_CHUNK),
            lay(GM_CHUNK, GM_WIDTH),
        ],
        out_specs=[tile(0), tile(0)],
        out_shape=[
            jax.ShapeDtypeStruct((N_TOK, SSM_WIDTH), BF16),
            jax.ShapeDtypeStruct((N_TOK, GM_WIDTH), BF16),
        ],
        compiler_params=_cparams(("arbitrary",)),
        name="mixers",
    )(ys, z, z, w_glu_bf16, b_glu.reshape(DEPTH, 1, SSM_WIDTH), w_s_bf16, b_s_full)


MRG_TM = 512
MRG_SUB = 256
HALF_D = D_MODEL // 2
REC_W = HALF_D + LANES


def _merge_kernel(ya_ref, yb_ref, ga0_ref, ga1_ref, gb0_ref, gb1_ref, x_ref, g1_ref, sc_ref, sh_ref,
                  n2_ref, wpa_ref, wpb_ref, wo_ref, wr_ref, br_ref, tri_ref,
                  xmid_ref, rec_ref, gid_ref, rank_ref, cnt_ref, cnt_scr):
    i = pl.program_id(0)
    seg = i // (SEG_TOK // MRG_TM)

    @pl.when(i == 0)
    def _():
        cnt_scr[...] = jnp.zeros(cnt_scr.shape, F32)

    passes = [pl.ds(sub * MRG_SUB, MRG_SUB) for sub in range(MRG_TM // MRG_SUB)]
    for rows in passes:
        _merge_mix(rows, seg, ya_ref, yb_ref, ga0_ref, ga1_ref, gb0_ref, gb1_ref, x_ref, g1_ref,
                   wpa_ref, wpb_ref, wo_ref, xmid_ref)
    for rows in passes:
        _merge_route(rows, seg, sc_ref, sh_ref, n2_ref, wr_ref, br_ref, tri_ref,
                     xmid_ref, rec_ref, gid_ref, rank_ref, cnt_scr)
    cnt_ref[...] = jnp.broadcast_to(cnt_scr[...], cnt_ref.shape)


def _merge_mix(rows, seg, ya_ref, yb_ref, ga0_ref, ga1_ref, gb0_ref, gb1_ref, x_ref, g1_ref,
               wpa_ref, wpb_ref, wo_ref, xmid_ref):
    pa = jnp.dot(ya_ref[rows, :], wpa_ref[...], preferred_element_type=F32)
    pb = jnp.dot(yb_ref[rows, :], wpb_ref[...], preferred_element_type=F32)
    m_lo = ga0_ref[rows, :].astype(F32) * pa[:, :HALF_D] + gb0_ref[rows, :].astype(F32) * pb[:, :HALF_D]
    m_hi = ga1_ref[rows, :].astype(F32) * pa[:, HALF_D:] + gb1_ref[rows, :].astype(F32) * pb[:, HALF_D:]
    mix = jnp.dot(m_lo.astype(BF16), wo_ref[0:HALF_D, :], preferred_element_type=F32)
    mix = mix + jnp.dot(m_hi.astype(BF16), wo_ref[HALF_D:D_MODEL, :], preferred_element_type=F32)
    xmid_ref[rows, :] = x_ref[rows, :] + g1_ref[pl.ds(seg, 1), :] * mix


def _merge_route(rows, seg, sc_ref, sh_ref, n2_ref, wr_ref, br_ref, tri_ref,
                 xmid_ref, rec_ref, gid_ref, rank_ref, cnt_scr):
    x = xmid_ref[rows, :]
    ms = jnp.mean(x * x, axis=-1, keepdims=True)
    y = x * lax.rsqrt(ms + EPS) * n2_ref[...]
    h2 = y * (1.0 + sc_ref[pl.ds(seg, 1), :]) + sh_ref[pl.ds(seg, 1), :]
    hi = h2.astype(BF16)
    hi_f = hi.astype(F32)
    lo = (h2 - hi_f).astype(BF16)
    bits = lax.bitcast_convert_type(hi_f, jnp.uint32)
    rec_ref[rows, 0:HALF_D] = bits[:, :HALF_D] | (bits[:, HALF_D:] >> 16)

    nt = (((1,), (1,)), ((), ()))
    lt = (lax.dot_general(wr_ref[...], hi, nt, preferred_element_type=F32)
          + lax.dot_general(wr_ref[...], lo, nt, preferred_element_type=F32))
    logits = lt[0:N_EXPERTS] + lt[N_EXPERTS:2 * N_EXPERTS]
    scores = jax.nn.sigmoid(logits)
    sel = scores + br_ref[...]
    ng = N_EXPERT_GROUPS
    s = [sel[j * ng:(j + 1) * ng] for j in range(EXPERTS_PER_GROUP)]
    p = [scores[j * ng:(j + 1) * ng] for j in range(EXPERTS_PER_GROUP)]
    a, b = jnp.maximum(s[0], s[1]), jnp.minimum(s[0], s[1])
    c, d = jnp.maximum(s[2], s[3]), jnp.minimum(s[2], s[3])
    grp_score = jnp.maximum(a, c) + jnp.maximum(jnp.minimum(a, c), jnp.maximum(b, d))
    best = jnp.max(grp_score, axis=0, keepdims=True)
    g_iota = lax.broadcasted_iota(jnp.int32, grp_score.shape, 0)
    g_idx = jnp.min(jnp.where(grp_score == best, g_iota, ng), axis=0, keepdims=True)
    onehot = g_iota == g_idx
    v = [jnp.sum(jnp.where(onehot, sj, 0.0), axis=0, keepdims=True) for sj in s]
    q = [jnp.sum(jnp.where(onehot, pj, 0.0), axis=0, keepdims=True) for pj in p]
    picked = []
    for j in range(EXPERTS_PER_GROUP):
        rank = jnp.zeros(v[j].shape, jnp.int32)
        for o in range(EXPERTS_PER_GROUP):
            if o == j:
                continue
            ahead = (v[o] > v[j]) | ((v[o] == v[j]) & (o < j))
            rank = rank + ahead.astype(jnp.int32)
        picked.append(jnp.where(rank < 2, q[j], 0.0))
    total = picked[0] + picked[1] + picked[2] + picked[3]
    gid_ref[:, rows] = g_idx
    cw_rows = jnp.concatenate([pj / total for pj in picked]
                              + [jnp.zeros((LANES - EXPERTS_PER_GROUP, MRG_SUB), F32)], axis=0)
    rec_ref[rows, HALF_D:REC_W] = lax.bitcast_convert_type(cw_rows.T, jnp.uint32)

    hot = onehot.astype(BF16)
    within = jnp.dot(hot, tri_ref[...], preferred_element_type=F32)
    before = jnp.sum(jnp.where(onehot, within + cnt_scr[...], 0.0), axis=0, keepdims=True) - 1.0
    rank_ref[:, rows] = before.astype(jnp.int32)
    cnt_scr[...] = cnt_scr[...] + within[:, MRG_SUB - 1:MRG_SUB]


def _merge(l, ya, yb, z, xres, mod, norm2_g, w_pa, w_pb, w_o, wr_t, br_col):
    n_t = N_TOK // MRG_TM
    tri = (jnp.arange(MRG_SUB)[:, None] <= jnp.arange(MRG_SUB)[None, :]).astype(BF16)
    zspec = lambda k: pl.BlockSpec((MRG_TM, HALF_D), lambda i: (i, k))
    once = pl.Buffered(1)
    lay = lambda *shape: pl.BlockSpec((None,) + shape, lambda i: (l,) + tuple(0 for _ in shape),
                                      pipeline_mode=once)
    const = lambda *shape: pl.BlockSpec(shape, lambda i: tuple(0 for _ in shape), pipeline_mode=once)
    return pl.pallas_call(
        _merge_kernel,
        grid=(n_t,),
        in_specs=[
            pl.BlockSpec((MRG_TM, SSM_WIDTH), lambda i: (i, 0)),
            pl.BlockSpec((MRG_TM, GM_WIDTH), lambda i: (i, 0)),
            zspec(2), zspec(3), zspec(4), zspec(5),
            pl.BlockSpec((MRG_TM, D_MODEL), lambda i: (i, 0)),
            _mod_spec(l, 2), _mod_spec(l, 4), _mod_spec(l, 3),
            lay(1, D_MODEL),
            lay(SSM_WIDTH, D_MODEL),
            lay(GM_WIDTH, D_MODEL),
            lay(D_MODEL, D_MODEL),
            const(2 * N_EXPERTS, D_MODEL),
            const(N_EXPERTS, 1),
            const(MRG_SUB, MRG_SUB),
        ],
        out_specs=[
            pl.BlockSpec((MRG_TM, D_MODEL), lambda i: (i, 0)),
            pl.BlockSpec((MRG_TM, REC_W), lambda i: (i, 0)),
            pl.BlockSpec((None, 1, MRG_TM), lambda i: (i, 0, 0)),
            pl.BlockSpec((None, 1, MRG_TM), lambda i: (i, 0, 0)),
            pl.BlockSpec((N_EXPERT_GROUPS, LANES), lambda i: (0, 0)),
        ],
        out_shape=[
            jax.ShapeDtypeStruct((N_TOK, D_MODEL), F32),
            jax.ShapeDtypeStruct((N_TOK, REC_W), jnp.uint32),
            jax.ShapeDtypeStruct((n_t, 1, MRG_TM), jnp.int32),
            jax.ShapeDtypeStruct((n_t, 1, MRG_TM), jnp.int32),
            jax.ShapeDtypeStruct((N_EXPERT_GROUPS, LANES), F32),
        ],
        scratch_shapes=[pltpu.VMEM((N_EXPERT_GROUPS, 1), F32)],
        compiler_params=_cparams(("arbitrary",)),
        name="merge_router",
    )(ya, yb, z, z, z, z, xres, mod, mod, mod, norm2_g.reshape(DEPTH, 1, D_MODEL),
      w_pa, w_pb, w_o, wr_t, br_col, tri)


DSP_TM = 2048


def _dispatch_kernel(pos_ref, pend_ref, rec_ref, out_ref, zero_scr, sem):
    step = pl.program_id(0)

    @pl.when(step == 0)
    def _():
        zero_scr[...] = jnp.zeros(zero_scr.shape, jnp.uint32)
        for g in range(N_EXPERT_GROUPS):
            start = pl.multiple_of(jnp.maximum(pend_ref[g] - MOE_BLK, 0), MOE_BLK)
            fill = pltpu.make_async_copy(zero_scr, out_ref.at[pl.ds(start, MOE_BLK)], sem)
            fill.start()
            fill.wait()
        for blk in range(N_TOK // MOE_BLK, MOE_NBLK):
            @pl.when(blk * MOE_BLK >= pend_ref[N_EXPERT_GROUPS - 1])
            def _():
                fill = pltpu.make_async_copy(zero_scr, out_ref.at[pl.ds(blk * MOE_BLK, MOE_BLK)], sem)
                fill.start()
                fill.wait()

    base = step * DSP_TM
    for r in range(DSP_TM):
        pltpu.make_async_copy(rec_ref.at[pl.ds(r, 1)], out_ref.at[pl.ds(pos_ref[base + r], 1)], sem).start()
    pltpu.make_async_copy(rec_ref, out_ref.at[pl.ds(0, DSP_TM)], sem).wait()


def _dispatch(pos, pend, rec):
    grid_spec = pltpu.PrefetchScalarGridSpec(
        num_scalar_prefetch=2,
        grid=(N_TOK // DSP_TM,),
        in_specs=[pl.BlockSpec((DSP_TM, REC_W), lambda i, pos, pend: (i, 0))],
        out_specs=pl.BlockSpec(memory_space=pl.ANY),
        scratch_shapes=[pltpu.VMEM((MOE_BLK, REC_W), jnp.uint32), pltpu.SemaphoreType.DMA(())],
    )
    return pl.pallas_call(
        _dispatch_kernel,
        grid_spec=grid_spec,
        out_shape=jax.ShapeDtypeStruct((MOE_SLOTS, REC_W), jnp.uint32),
        compiler_params=_cparams(("arbitrary",)),
        name="moe_dispatch",
    )(pos, pend, rec)


UP_EXPERTS = 2
MOE_SUB = 256
UP_CH = 256


def _up_inputs(rec_ref, rows, first_expert):
    packed = rec_ref[rows, 0:HALF_D]
    x_lo = lax.bitcast_convert_type(packed & jnp.uint32(0xFFFF0000), F32).astype(BF16)
    x_hi = lax.bitcast_convert_type(packed << 16, F32).astype(BF16)
    cw = lax.bitcast_convert_type(rec_ref[rows, HALF_D:REC_W], F32)
    lane = lax.broadcasted_iota(jnp.int32, cw.shape, 1)
    w_rows = [jnp.sum(jnp.where(lane == first_expert + e, cw, 0.0), axis=1, keepdims=True)
              for e in range(UP_EXPERTS)]
    return x_lo, x_hi, w_rows


def _up_hidden(x_lo, x_hi, w_row, wg, wu):
    gate = (jnp.dot(x_lo, wg[0:HALF_D, :], preferred_element_type=F32)
            + jnp.dot(x_hi, wg[HALF_D:D_MODEL, :], preferred_element_type=F32))
    up = (jnp.dot(x_lo, wu[0:HALF_D, :], preferred_element_type=F32)
          + jnp.dot(x_hi, wu[HALF_D:D_MODEL, :], preferred_element_type=F32))
    return (gate * jax.nn.sigmoid(gate) * up * w_row).astype(BF16)


def _expert_up_kernel(gid_ref, fill_ref, last_ref, rec_ref, wg_ref, wu_ref, h_ref, wg_scr, wu_scr):
    del last_ref
    first_expert = pl.program_id(0) * UP_EXPERTS
    b = pl.program_id(1)
    prev = gid_ref[jnp.maximum(b - 1, 0)]
    fresh = (b == 0) | (gid_ref[b] != prev)
    passes = [pl.ds(sub * MOE_SUB, MOE_SUB) for sub in range(MOE_BLK // MOE_SUB)]

    @pl.when(fresh)
    def _():
        inputs = [_up_inputs(rec_ref, rows, first_expert) for rows in passes]
        for e in range(UP_EXPERTS):
            for c in range(EXPERT_FF // UP_CH):
                cols = pl.ds(c * UP_CH, UP_CH)
                wg = wg_ref[e, :, cols].astype(BF16)
                wu = wu_ref[e, :, cols].astype(BF16)
                wg_scr[e, :, cols] = wg
                wu_scr[e, :, cols] = wu
                for rows, (x_lo, x_hi, w_rows) in zip(passes, inputs):
                    h_ref[rows, pl.ds(e * EXPERT_FF + c * UP_CH, UP_CH)] = _up_hidden(
                        x_lo, x_hi, w_rows[e], wg, wu)

    @pl.when(jnp.logical_not(fresh))
    def _():
        for sub, rows in enumerate(passes):
            @pl.when(fill_ref[b] > sub * MOE_SUB)
            def _():
                x_lo, x_hi, w_rows = _up_inputs(rec_ref, rows, first_expert)
                for e in range(UP_EXPERTS):
                    h_ref[rows, pl.ds(e * EXPERT_FF, EXPERT_FF)] = _up_hidden(
                        x_lo, x_hi, w_rows[e], wg_scr[e], wu_scr[e])

            @pl.when(fill_ref[b] <= sub * MOE_SUB)
            def _():
                h_ref[rows, :] = jnp.zeros((MOE_SUB, UP_EXPERTS * EXPERT_FF), BF16)


def _expert_up(l, blk_gid, blk_fill, blk_last, rec_sorted, e_gate, e_up):
    halves = EXPERTS_PER_GROUP // UP_EXPERTS
    wspec = pl.BlockSpec((None, UP_EXPERTS, D_MODEL, EXPERT_FF),
                         lambda h, b, gid, fill, last: (l, gid[b] * halves + h, 0, 0))
    grid_spec = pltpu.PrefetchScalarGridSpec(
        num_scalar_prefetch=3,
        grid=(halves, MOE_NBLK),
        in_specs=[
            pl.BlockSpec((MOE_BLK, REC_W), lambda h, b, gid, fill, last: (jnp.minimum(b, last[0]), 0)),
            wspec, wspec,
        ],
        out_specs=pl.BlockSpec((MOE_BLK, UP_EXPERTS * EXPERT_FF), lambda h, b, gid, fill, last: (b, h)),
        scratch_shapes=[pltpu.VMEM((UP_EXPERTS, D_MODEL, EXPERT_FF), BF16),
                        pltpu.VMEM((UP_EXPERTS, D_MODEL, EXPERT_FF), BF16)],
    )
    return pl.pallas_call(
        _expert_up_kernel,
        grid_spec=grid_spec,
        out_shape=jax.ShapeDtypeStruct((MOE_SLOTS, GROUP_FF), BF16),
        compiler_params=_cparams(("arbitrary", "arbitrary")),
        name="expert_up",
    )(blk_gid, blk_fill, blk_last, rec_sorted, e_gate, e_up)


DOWN_CH = 256


def _expert_down_kernel(gid_ref, fill_ref, h_ref, wd_ref, y_ref, wd_scr):
    b = pl.program_id(0)
    prev = gid_ref[jnp.maximum(b - 1, 0)]
    fresh = (b == 0) | (gid_ref[b] != prev)

    passes = [pl.ds(sub * MOE_SUB, MOE_SUB) for sub in range(MOE_BLK // MOE_SUB)]

    @pl.when(fresh)
    def _():
        for c in range(HALF_D // DOWN_CH):
            cols = pl.ds(c * DOWN_CH, DOWN_CH)
            pair = pl.ds(HALF_D + c * DOWN_CH, DOWN_CH)
            w_a = wd_ref[:, cols].astype(BF16)
            w_b = wd_ref[:, pair].astype(BF16)
            wd_scr[:, cols] = w_a
            wd_scr[:, pair] = w_b
            for rows in passes:
                hid = h_ref[rows, :]
                y_ref[rows, cols] = _pack_pairs(jnp.dot(hid, w_a, preferred_element_type=F32),
                                                jnp.dot(hid, w_b, preferred_element_type=F32))

    @pl.when(jnp.logical_not(fresh))
    def _():
        for sub, rows in enumerate(passes):
            @pl.when(fill_ref[b] > sub * MOE_SUB)
            def _():
                hid = h_ref[rows, :]
                for c in range(HALF_D // DOWN_CH):
                    cols = pl.ds(c * DOWN_CH, DOWN_CH)
                    pair = pl.ds(HALF_D + c * DOWN_CH, DOWN_CH)
                    y_ref[rows, cols] = _pack_pairs(jnp.dot(hid, wd_scr[:, cols], preferred_element_type=F32),
                                                    jnp.dot(hid, wd_scr[:, pair], preferred_element_type=F32))

            @pl.when(fill_ref[b] <= sub * MOE_SUB)
            def _():
                y_ref[rows, :] = jnp.zeros((MOE_SUB, HALF_D), jnp.uint32)


def _expert_down(l, blk_gid, blk_fill, h_sorted, e_down_grouped):
    grid_spec = pltpu.PrefetchScalarGridSpec(
        num_scalar_prefetch=2,
        grid=(MOE_NBLK,),
        in_specs=[
            pl.BlockSpec((MOE_BLK, GROUP_FF), lambda b, gid, fill: (b, 0)),
            pl.BlockSpec((None, None, GROUP_FF, D_MODEL), lambda b, gid, fill: (l, gid[b], 0, 0)),
        ],
        out_specs=pl.BlockSpec((MOE_BLK, HALF_D), lambda b, gid, fill: (b, 0)),
        scratch_shapes=[pltpu.VMEM((GROUP_FF, D_MODEL), BF16)],
    )
    return pl.pallas_call(
        _expert_down_kernel,
        grid_spec=grid_spec,
        out_shape=jax.ShapeDtypeStruct((MOE_SLOTS, HALF_D), jnp.uint32),
        compiler_params=_cparams(("arbitrary",)),
        name="expert_down",
    )(blk_gid, blk_fill, h_sorted, e_down_grouped)


def _moe(l, rec, gid, rank, counts, e_gate, e_up, e_down_grouped):
    padded = (counts + MOE_BLK - 1) // MOE_BLK * MOE_BLK
    pend = jnp.cumsum(padded)
    pstart = pend - padded
    pos = (pstart[gid] + rank).astype(jnp.int32)
    blk_start = jnp.arange(MOE_NBLK, dtype=jnp.int32) * MOE_BLK
    blk_gid = jnp.minimum(jnp.sum((blk_start[:, None] >= pend[None, :]).astype(jnp.int32), axis=1),
                          N_EXPERT_GROUPS - 1)
    blk_fill = jnp.clip(pstart[blk_gid] + counts[blk_gid] - blk_start, 0, MOE_BLK)
    blk_fill = jnp.where(blk_start < pend[-1], blk_fill, 0).astype(jnp.int32)
    blk_last = (pend[-1:] // MOE_BLK - 1).astype(jnp.int32)
    rec_sorted = _dispatch(pos, pend.astype(jnp.int32), rec)
    hid = _expert_up(l, blk_gid, blk_fill, blk_last, rec_sorted, e_gate, e_up)
    y_sorted = _expert_down(l, blk_gid, blk_fill, hid, e_down_grouped)
    return y_sorted[pos]


FIN_TM = 512
FIN_PROMPT_TILES = N_PROMPT // FIN_TM


def _final_kernel(x_ref, y_ref, g2_ref, fg_ref, op_ref, os_ref):
    i = pl.program_id(0)
    seg = i // (SEG_TOK // FIN_TM)
    x = x_ref[...] + g2_ref[pl.ds(seg, 1), :] * _unpack_pairs(y_ref[...])
    ms = jnp.mean(x * x, axis=-1, keepdims=True)
    out = x * lax.rsqrt(ms + EPS) * fg_ref[...]

    @pl.when(i < FIN_PROMPT_TILES)
    def _():
        op_ref[...] = out

    @pl.when(i >= FIN_PROMPT_TILES)
    def _():
        os_ref[...] = out


def _final_norm(xmid, moe_y, mod, final_g):
    return pl.pallas_call(
        _final_kernel,
        grid=(N_TOK // FIN_TM,),
        in_specs=[
            pl.BlockSpec((FIN_TM, D_MODEL), lambda i: (i, 0)),
            pl.BlockSpec((FIN_TM, HALF_D), lambda i: (i, 0)),
            _mod_spec(DEPTH - 1, 5),
            pl.BlockSpec((1, D_MODEL), lambda i: (0, 0)),
        ],
        out_specs=[
            pl.BlockSpec((FIN_TM, D_MODEL), lambda i: (jnp.minimum(i, FIN_PROMPT_TILES - 1), 0)),
            pl.BlockSpec((FIN_TM, D_MODEL), lambda i: (jnp.maximum(i - FIN_PROMPT_TILES, 0), 0)),
        ],
        out_shape=[
            jax.ShapeDtypeStruct((N_PROMPT, D_MODEL), F32),
            jax.ShapeDtypeStruct((N_SAMPLE, D_MODEL), F32),
        ],
        compiler_params=_cparams(("arbitrary",)),
        name="final_norm",
    )(xmid, moe_y, mod, final_g.reshape(1, D_MODEL))


def _grid_pos_embed(rows):
    quarter = D_MODEL // 4
    freqs = 1.0 / (POS_BASE ** (jnp.arange(quarter, dtype=F32) / quarter))
    er = jnp.arange(rows, dtype=F32)[:, None] * freqs
    ec = jnp.arange(GRID_W, dtype=F32)[:, None] * freqs
    row_emb = jnp.concatenate([jnp.sin(er), jnp.cos(er)], axis=-1)
    col_emb = jnp.concatenate([jnp.sin(ec), jnp.cos(ec)], axis=-1)
    return jnp.stack([row_emb, col_emb])


def kernel(x_prompt, x_sample, state_ssm_re, state_ssm_im, c, c_ctx, norm1_g, norm2_g, w_mod, b_mod,
           w_in, ssm_lam_re, ssm_lam_im, ssm_log_step, ssm_b_re, ssm_b_im, ssm_c_re, ssm_c_im, ssm_d,
           w_glu, b_glu, gm_ln_g, gm_w_s, gm_b_s, w_pa, w_pb, w_o, w_router, b_router,
           e_gate, e_up, e_down, final_g):
    cvec = jnp.concatenate([c_ctx[None], c, jnp.zeros((MOD_ROWS - 1 - DEC_BATCH, D_MODEL), F32)], axis=0)
    mod = _modulation(cvec, w_mod, b_mod)

    perm = (jnp.arange(N_EXPERT_GROUPS)[None, :] * EXPERTS_PER_GROUP
            + jnp.arange(EXPERTS_PER_GROUP)[:, None]).reshape(N_EXPERTS)
    wr = w_router.astype(F32).T[perm]
    wr_hi = wr.astype(BF16)
    wr_lo = (wr - wr_hi.astype(F32)).astype(BF16)
    wr_t = jnp.concatenate([wr_hi, wr_lo], axis=0)
    br_col = b_router.astype(F32)[perm][:, None]

    w_glu_b, w_s_b = w_glu.astype(BF16), gm_w_s.astype(BF16)
    w_pa_b, w_pb_b, w_o_b = w_pa.astype(BF16), w_pb.astype(BF16), w_o.astype(BF16)
    b_s_full = jnp.repeat(jnp.transpose(gm_b_s.astype(F32), (0, 2, 1)), GM_GROUP_DIM, axis=2)
    e_down_grouped = e_down.reshape(DEPTH, N_EXPERT_GROUPS, GROUP_FF, D_MODEL)
    w1, w2, a16 = _s5_prep(ssm_lam_re, ssm_lam_im, ssm_log_step, ssm_b_re, ssm_b_im, ssm_c_re, ssm_c_im)
    h0_lat = jnp.concatenate([state_ssm_re[:, :, 0], state_ssm_re[:, :, 1],
                              state_ssm_im[:, :, 0], state_ssm_im[:, :, 1]], axis=-1).astype(F32)
    h0 = jnp.concatenate([jnp.zeros((DEPTH, 1, SSM_GROUPS, STATE_W), F32),
                          jnp.transpose(h0_lat, (1, 0, 2, 3))], axis=1)

    d_lanes = jnp.tile(ssm_d.astype(F32).reshape(DEPTH, SSM_GROUPS, SSM_GROUP), (1, 1, SCAN_T))

    inproj_in = (x_prompt.reshape(N_PROMPT, D_MODEL), x_sample.reshape(N_SAMPLE, D_MODEL),
                 _grid_pos_embed(DEC_SEQ // GRID_W))

    new_re, new_im = [], []
    xmid = moe_y = None
    for l in range(DEPTH):
        xres, u, z = _inproj(l, *inproj_in, mod, norm1_g, gm_ln_g, w_in)
        ys, fs = _s5_scan(l, u, w1, w2, a16, h0, d_lanes)
        fin = fs[0]
        p = SSM_STATE
        new_re.append(jnp.stack([fin[:, :, 0:p], fin[::-1, :, p:2 * p]], axis=1))
        new_im.append(jnp.stack([fin[:, :, 2 * p:3 * p], fin[::-1, :, 3 * p:4 * p]], axis=1))

        ya, yb = _mix(l, ys, z, w_glu_b, b_glu.astype(F32), w_s_b, b_s_full)
        xmid, rec, gid, rank, cnt = _merge(l, ya, yb, z, xres, mod, norm2_g, w_pa_b, w_pb_b, w_o_b,
                                           wr_t, br_col)
        moe_y = _moe(l, rec, gid.reshape(N_TOK), rank.reshape(N_TOK), cnt[:, 0].astype(jnp.int32),
                     e_gate, e_up, e_down_grouped)
        inproj_in = (xmid, moe_y, mod)

    y_prompt, y_sample = _final_norm(xmid, moe_y, mod, final_g)
    new_state_re = jnp.stack(new_re, axis=1).astype(x_prompt.dtype)
    new_state_im = jnp.stack(new_im, axis=1).astype(x_prompt.dtype)
    return (y_prompt.reshape(BATCH, SEQ, D_MODEL), y_sample.reshape(DEC_BATCH, DEC_SEQ, D_MODEL),
            new_state_re, new_state_im)
```
